```python
import math
import jax, jax.numpy as jnp
from jax import lax
import numpy as np

D_MODEL = 2048
BATCH = 2
SEQ = 16384
DEPTH = 1

N_MEM = 256
XA_HEADS = 4
XA_DH = D_MODEL // XA_HEADS
RET_HEADS = 8
RET_DK = 128
RET_DV = 128
DN_HEADS = 8
DN_DK = 128
DN_DV = 128
CONV_W = 4
CHUNK = 128
ROPE_BASE = 10000.0
MAX_POS_OFFSET = 4096
EPS = 1e-6
N_GROUPS = 4
EXP_PER_GROUP = 8
N_EXPERTS = N_GROUPS * EXP_PER_GROUP
TOP_K = 2
D_FF_EXPERT = 1024
MOE_BLOCK = 128
RET_QK_W = RET_HEADS * RET_DK
RET_V_W = RET_HEADS * RET_DV
DN_QK_W = DN_HEADS * DN_DK
DN_V_W = DN_HEADS * DN_DV
DN_QKV_W = 2 * DN_QK_W + DN_V_W
MIX_W = RET_V_W + DN_V_W
IN_DIM = 2 * RET_QK_W + 2 * RET_V_W + DN_QKV_W + DN_V_W + 2 * DN_HEADS

kernel_name = "hymba_retention_gdn_hmoe_block"


def rmsnorm(x, w):
    xf = x.astype(jnp.float32)
    y = xf * lax.rsqrt(jnp.mean(xf * xf, axis=-1, keepdims=True) + EPS)
    return (y * w.astype(jnp.float32)).astype(x.dtype)


def l2norm(x):
    return x * lax.rsqrt(jnp.sum(x * x, axis=-1, keepdims=True) + EPS)


def rotary(x, positions):
    half = x.shape[-1] // 2
    inv = ROPE_BASE ** (-jnp.arange(half, dtype=jnp.float32) / half)
    ang = positions.astype(jnp.float32)[..., None] * inv
    cos = jnp.cos(ang)[:, :, None, :]
    sin = jnp.sin(ang)[:, :, None, :]
    x1, x2 = x[..., :half], x[..., half:]
    return jnp.concatenate([x1 * cos - x2 * sin, x1 * sin + x2 * cos], axis=-1)


def to_chunks(x):
    B, S, H, d = x.shape
    return x.reshape(B, S // CHUNK, CHUNK, H, d).transpose(1, 0, 3, 2, 4)


def scalars_to_chunks(x):
    B, S, H = x.shape
    return x.reshape(B, S // CHUNK, CHUNK, H).transpose(1, 0, 3, 2)


def from_chunks(o):
    N, B, H, C, d = o.shape
    return o.transpose(1, 0, 3, 2, 4).reshape(B, N * C, H, d)


def causal_depthwise_conv(x, w):
    return lax.conv_general_dilated(
        x, w[:, None, :].astype(x.dtype), window_strides=(1,), padding=[(CONV_W - 1, 0)],
        dimension_numbers=("NWC", "WIO", "NWC"), feature_group_count=x.shape[-1])


def retention(q, k, v, positions):
    B = q.shape[0]
    q = rotary(q, positions)
    k = rotary(k, positions) * (RET_DK ** -0.5)
    log_gamma = jnp.log1p(-jnp.exp2(-5.0 - jnp.arange(RET_HEADS, dtype=jnp.float32)))
    idx = jnp.arange(CHUNK, dtype=jnp.float32)
    rel = idx[:, None] - idx[None, :]
    causal = rel >= 0
    d_intra = jnp.where(causal, jnp.exp(log_gamma[:, None, None] * jnp.where(causal, rel, 0.0)), 0.0)
    q_dec = jnp.exp(log_gamma[:, None] * (idx + 1.0))[None, :, :, None]
    k_dec = jnp.exp(log_gamma[:, None] * (CHUNK - 1.0 - idx))[None, :, :, None]
    chunk_dec = jnp.exp(log_gamma * CHUNK)[None, :, None, None]

    def step(R, inp):
        qi, ki, vi = inp
        s = jnp.einsum("bhid,bhjd->bhij", qi, ki) * d_intra
        o = jnp.einsum("bhij,bhjv->bhiv", s, vi) + jnp.einsum("bhid,bhdv->bhiv", qi, R) * q_dec
        R = R * chunk_dec + jnp.einsum("bhjd,bhjv->bhdv", ki * k_dec, vi)
        return R, o

    R0 = jnp.zeros((B, RET_HEADS, RET_DK, RET_DV), jnp.float32)
    _, o = lax.scan(step, R0, (to_chunks(q), to_chunks(k), to_chunks(v)))
    return from_chunks(o)


def gated_delta_net(q, k, v, g_log, beta):
    B = q.shape[0]
    q = l2norm(q) * (DN_DK ** -0.5)
    k = l2norm(k)
    qc, kc, vc = to_chunks(q), to_chunks(k), to_chunks(v)
    gc = jnp.cumsum(scalars_to_chunks(g_log), axis=-1)
    bc = scalars_to_chunks(beta)
    idx = jnp.arange(CHUNK)
    causal = idx[:, None] >= idx[None, :]
    strict = idx[:, None] > idx[None, :]
    diff = gc[..., :, None] - gc[..., None, :]
    gam = jnp.where(causal, jnp.exp(jnp.where(causal, diff, 0.0)), 0.0)
    a = jnp.where(strict, jnp.einsum("nbhid,nbhjd->nbhij", kc, kc) * gam * bc[..., :, None], 0.0)
    eye = jnp.eye(CHUNK, dtype=jnp.float32)
    rhs = jnp.concatenate([vc * bc[..., None], kc * (bc * jnp.exp(gc))[..., None]], axis=-1)
    sol = lax.linalg.triangular_solve(a + eye, rhs, left_side=True, lower=True, unit_diagonal=True)
    u, w = sol[..., :DN_DV], sol[..., DN_DV:]
    qk = jnp.einsum("nbhid,nbhjd->nbhij", qc, kc) * gam
    q_dec = qc * jnp.exp(gc)[..., None]
    g_last = gc[..., -1]
    k_dec = kc * jnp.exp(g_last[..., None] - gc)[..., None]
    state_dec = jnp.exp(g_last)

    def step(S, inp):
        u_i, w_i, qk_i, q_i, k_i, sd = inp
        v_new = u_i - jnp.einsum("bhik,bhkv->bhiv", w_i, S)
        o = jnp.einsum("bhik,bhkv->bhiv", q_i, S) + jnp.einsum("bhij,bhjv->bhiv", qk_i, v_new)
        S = S * sd[..., None, None] + jnp.einsum("bhjk,bhjv->bhkv", k_i, v_new)
        return S, o

    S0 = jnp.zeros((B, DN_HEADS, DN_DK, DN_DV), jnp.float32)
    _, o = lax.scan(step, S0, (u, w, qk, q_dec, k_dec, state_dec))
    return from_chunks(o)


def token_mixer(xn, positions, w_in, conv_w, a_log, dt_bias, ret_gn_w, dn_norm_w, w_out):
    B, S, _ = xn.shape
    f32 = jnp.float32
    proj = xn @ w_in
    sizes = (RET_QK_W, RET_QK_W, RET_V_W, RET_V_W, DN_QKV_W, DN_V_W, DN_HEADS, DN_HEADS)
    splits = np.cumsum(sizes)[:-1].tolist()
    r_q, r_k, r_v, r_g, d_qkv, d_z, d_b, d_a = jnp.split(proj, splits, axis=-1)
    rq = r_q.astype(f32).reshape(B, S, RET_HEADS, RET_DK)
    rk = r_k.astype(f32).reshape(B, S, RET_HEADS, RET_DK)
    rv = r_v.astype(f32).reshape(B, S, RET_HEADS, RET_DV)
    ro = retention(rq, rk, rv, positions)
    mu = jnp.mean(ro, axis=-1, keepdims=True)
    var = jnp.mean(jnp.square(ro - mu), axis=-1, keepdims=True)
    ro = (ro - mu) * lax.rsqrt(var + EPS) * ret_gn_w.astype(f32).reshape(RET_HEADS, RET_DV)
    ro = ro.reshape(B, S, RET_V_W) * jax.nn.silu(r_g.astype(f32))
    d_qkv = jax.nn.silu(causal_depthwise_conv(d_qkv, conv_w)).astype(f32)
    dq, dk, dv = jnp.split(d_qkv, [DN_QK_W, 2 * DN_QK_W], axis=-1)
    beta = jax.nn.sigmoid(d_b.astype(f32))
    g_log = -jnp.exp(a_log.astype(f32)) * jax.nn.softplus(d_a.astype(f32) + dt_bias.astype(f32))
    do = gated_delta_net(dq.reshape(B, S, DN_HEADS, DN_DK), dk.reshape(B, S, DN_HEADS, DN_DK),
                         dv.reshape(B, S, DN_HEADS, DN_DV), g_log, beta)
    do = do * lax.rsqrt(jnp.mean(do * do, axis=-1, keepdims=True) + EPS) * dn_norm_w.astype(f32)
    do = do.reshape(B, S, DN_V_W) * jax.nn.silu(d_z.astype(f32))
    mix = jnp.concatenate([ro, do], axis=-1).astype(xn.dtype)
    return mix @ w_out


def memory_cross_attention(hn, memn, w_q, w_kv, w_o):
    B, S, D = hn.shape
    M = memn.shape[1]
    q = (hn @ w_q).reshape(B, S, XA_HEADS, XA_DH)
    k, v = jnp.split(memn @ w_kv, 2, axis=-1)
    k = k.reshape(B, M, XA_HEADS, XA_DH)
    v = v.reshape(B, M, XA_HEADS, XA_DH)
    s = jnp.einsum("bshd,bmhd->bhsm", q, k).astype(jnp.float32) * (XA_DH ** -0.5)
    p = jax.nn.softmax(s, axis=-1).astype(v.dtype)
    o = jnp.einsum("bhsm,bmhd->bshd", p, v).reshape(B, S, D)
    return o @ w_o


def hierarchical_moe(hn, w_group_router, b_group_router, w_expert_router, b_expert_router, w_gate, w_up, w_down):
    B, S, D = hn.shape
    T = B * S
    xt = hn.reshape(T, D)
    g_logits = (xt @ w_group_router).astype(jnp.float32) + b_group_router.astype(jnp.float32)
    g_prob = jax.nn.softmax(g_logits, axis=-1)
    g_sel = jnp.argmax(g_logits, axis=-1)
    g_w = jnp.take_along_axis(g_prob, g_sel[:, None], axis=-1)[:, 0]
    e_all = (xt @ w_expert_router).astype(jnp.float32) + b_expert_router.astype(jnp.float32)
    e_all = e_all.reshape(T, N_GROUPS, EXP_PER_GROUP)
    e_logits = jnp.take_along_axis(e_all, g_sel[:, None, None], axis=1)[:, 0]
    top_p, top_e = lax.top_k(jax.nn.softmax(e_logits, axis=-1), TOP_K)
    gate = g_w[:, None] * top_p / jnp.sum(top_p, axis=-1, keepdims=True)
    expert = g_sel[:, None].astype(jnp.int32) * EXP_PER_GROUP + top_e.astype(jnp.int32)
    A = T * TOP_K
    flat_e = expert.reshape(A)
    flat_tok = jnp.arange(A, dtype=jnp.int32) // TOP_K
    flat_gate = gate.reshape(A)
    order = jnp.argsort(flat_e)
    e_sorted, tok_sorted, gate_sorted = flat_e[order], flat_tok[order], flat_gate[order]
    counts = jax.ops.segment_sum(jnp.ones_like(flat_e), flat_e, num_segments=N_EXPERTS)
    start = jnp.cumsum(counts) - counts
    padded = (counts + MOE_BLOCK - 1) // MOE_BLOCK * MOE_BLOCK
    pend = jnp.cumsum(padded)
    pstart = pend - padded
    dest = pstart[e_sorted] + (jnp.arange(A, dtype=jnp.int32) - start[e_sorted])
    NB = -(-A // MOE_BLOCK) + N_EXPERTS
    P = NB * MOE_BLOCK
    buf = jnp.zeros((P, D), xt.dtype).at[dest].set(xt[tok_sorted])
    block_expert = jnp.minimum(
        jnp.searchsorted(pend, jnp.arange(NB, dtype=jnp.int32) * MOE_BLOCK, side="right"), N_EXPERTS - 1)

    def expert_block(args):
        xb, e = args
        hb = jax.nn.silu(xb @ w_gate[e]) * (xb @ w_up[e])
        return hb @ w_down[e]

    yb = lax.map(expert_block, (buf.reshape(NB, MOE_BLOCK, D), block_expert))
    y = yb.reshape(P, D)[dest] * gate_sorted[:, None].astype(xt.dtype)
    return jax.ops.segment_sum(y, tok_sorted, num_segments=T).reshape(B, S, D)


def setup_inputs(seed: int = 0) -> dict:
    key = jax.random.key(seed)
    ks = jax.random.split(key, 25)
    f32 = jnp.float32
    L = DEPTH

    def nrm(k, shape, scale):
        return jax.random.normal(k, shape, f32) * scale

    def gain(k, shape):
        return 1.0 + 0.02 * jax.random.normal(k, shape, f32)

    x = nrm(ks[0], (BATCH, SEQ, D_MODEL), 1.0)
    mem = nrm(ks[1], (BATCH, N_MEM, D_MODEL), 1.0)
    offset = jax.random.randint(ks[2], (BATCH, 1), 0, MAX_POS_OFFSET, dtype=jnp.int32)
    positions = (offset + jnp.arange(SEQ, dtype=jnp.int32)[None, :]).astype(jnp.int32)
    dt = jnp.exp(jax.random.uniform(ks[7], (L, DN_HEADS), f32, math.log(1e-3), math.log(1e-1)))
    return {
        "x": x,
        "mem": mem,
        "positions": positions,
        "norm_mix_w": gain(ks[3], (L, D_MODEL)),
        "w_in": nrm(ks[4], (L, D_MODEL, IN_DIM), D_MODEL ** -0.5),
        "dn_conv_w": nrm(ks[5], (L, CONV_W, DN_QKV_W), CONV_W ** -0.5),
        "dn_a_log": jnp.log(jax.random.uniform(ks[6], (L, DN_HEADS), f32, 1.0, 16.0)),
        "dn_dt_bias": dt + jnp.log(-jnp.expm1(-dt)),
        "ret_gn_w": gain(ks[8], (L, RET_V_W)),
        "dn_norm_w": gain(ks[9], (L, DN_DV)),
        "w_out": nrm(ks[10], (L, MIX_W, D_MODEL), MIX_W ** -0.5),
        "norm_xq_w": gain(ks[11], (L, D_MODEL)),
        "norm_mem_w": gain(ks[12], (L, D_MODEL)),
        "w_xq": nrm(ks[13], (L, D_MODEL, D_MODEL), D_MODEL ** -0.5),
        "w_xkv": nrm(ks[14], (L, D_MODEL, 2 * D_MODEL), D_MODEL ** -0.5),
        "w_xo": nrm(ks[15], (L, D_MODEL, D_MODEL), D_MODEL ** -0.5),
        "norm_moe_w": gain(ks[16], (L, D_MODEL)),
        "w_group_router": nrm(ks[17], (L, D_MODEL, N_GROUPS), D_MODEL ** -0.5),
        "b_group_router": nrm(ks[18], (L, N_GROUPS), 0.01),
        "w_expert_router": nrm(ks[19], (L, D_MODEL, N_EXPERTS), D_MODEL ** -0.5),
        "b_expert_router": nrm(ks[20], (L, N_EXPERTS), 0.01),
        "w_gate": nrm(ks[21], (L, N_EXPERTS, D_MODEL, D_FF_EXPERT), D_MODEL ** -0.5),
        "w_up": nrm(ks[22], (L, N_EXPERTS, D_MODEL, D_FF_EXPERT), D_MODEL ** -0.5),
        "w_down": nrm(ks[23], (L, N_EXPERTS, D_FF_EXPERT, D_MODEL), D_FF_EXPERT ** -0.5),
        "norm_final_w": gain(ks[24], (D_MODEL,)),
    }


def reference(x, mem, positions, norm_mix_w, w_in, dn_conv_w, dn_a_log, dn_dt_bias, ret_gn_w, dn_norm_w,
              w_out, norm_xq_w, norm_mem_w, w_xq, w_xkv, w_xo, norm_moe_w, w_group_router, b_group_router,
              w_expert_router, b_expert_router, w_gate, w_up, w_down, norm_final_w):
    h = x
    for l in range(DEPTH):
        h = h + token_mixer(rmsnorm(h, norm_mix_w[l]), positions, w_in[l], dn_conv_w[l], dn_a_log[l],
                            dn_dt_bias[l], ret_gn_w[l], dn_norm_w[l], w_out[l])
        h = h + memory_cross_attention(rmsnorm(h, norm_xq_w[l]), rmsnorm(mem, norm_mem_w[l]),
                                       w_xq[l], w_xkv[l], w_xo[l])
        h = h + hierarchical_moe(rmsnorm(h, norm_moe_w[l]), w_group_router[l], b_group_router[l],
                                 w_expert_router[l], b_expert_router[l], w_gate[l], w_up[l], w_down[l])
    return rmsnorm(h, norm_final_w)
```

```python
import functools

import numpy as np
import jax
import jax.numpy as jnp
from jax import lax
from jax.experimental import pallas as pl
from jax.experimental.pallas import tpu as pltpu

f32 = jnp.float32
bf16 = jnp.bfloat16

EPS = 1e-6
HEADS = 8
HD = 128
LANES = 128
XA_HEADS = 4
CONV_W = 4
ROPE_BASE = 10000.0
N_GROUPS = 4
EXP_PER_GROUP = 8
N_EXPERTS = N_GROUPS * EXP_PER_GROUP
RET_CHUNK = 256
DN_CHUNK = 128
VMEM_LIMIT = 56 * 1024 * 1024


def _params(sem, vmem=VMEM_LIMIT):
    return pltpu.CompilerParams(dimension_semantics=sem, vmem_limit_bytes=vmem)


def _dot(a, b):
    return jnp.dot(a.astype(bf16), b.astype(bf16), preferred_element_type=f32)


def _dot_nt(a, b):
    return lax.dot_general(a.astype(bf16), b.astype(bf16), (((1,), (1,)), ((), ())),
                           preferred_element_type=f32)


def _dot_tn(a, b):
    return lax.dot_general(a.astype(bf16), b.astype(bf16), (((0,), (0,)), ((), ())),
                           preferred_element_type=f32)


def _split3(a):
    hi = a.astype(bf16)
    r = a - hi.astype(f32)
    mid = r.astype(bf16)
    lo = (r - mid.astype(f32)).astype(bf16)
    return hi, mid, lo


def _sigmoid(x):
    return 1.0 / (1.0 + jnp.exp(-x))


def _silu(x):
    return x * _sigmoid(x)


def _rms_scale(x):
    return lax.rsqrt(jnp.mean(x * x, axis=-1, keepdims=True) + EPS)


def _in_proj_kernel(x_ref, nw_ref, w_ref, wba_ref, o_ref, ba_ref, xn_ref):
    @pl.when(pl.program_id(1) == 0)
    def _():
        x = x_ref[...]
        xn = (x * _rms_scale(x) * nw_ref[...]).astype(bf16)
        xn_ref[...] = xn
        ba_ref[...] = jnp.dot(xn, wba_ref[...], preferred_element_type=f32)

    acc = jnp.dot(xn_ref[...], w_ref[...], preferred_element_type=f32)
    for c in range(o_ref.shape[0]):
        o_ref[c] = acc[:, c * LANES:(c + 1) * LANES].astype(bf16)


def _in_proj(x2, norm_w, w_main, w_ba, tm, tn):
    T, D = x2.shape
    N = w_main.shape[1]
    return pl.pallas_call(
        _in_proj_kernel,
        grid=(T // tm, N // tn),
        in_specs=[
            pl.BlockSpec((tm, D), lambda i, j: (i, 0)),
            pl.BlockSpec((1, D), lambda i, j: (0, 0)),
            pl.BlockSpec((D, tn), lambda i, j: (0, j)),
            pl.BlockSpec((D, LANES), lambda i, j: (0, 0)),
        ],
        out_specs=[
            pl.BlockSpec((tn // LANES, tm, LANES), lambda i, j: (j, i, 0)),
            pl.BlockSpec((tm, LANES), lambda i, j: (i, 0)),
        ],
        out_shape=[
            jax.ShapeDtypeStruct((N // LANES, T, LANES), bf16),
            jax.ShapeDtypeStruct((T, LANES), f32),
        ],
        scratch_shapes=[pltpu.VMEM((tm, D), bf16)],
        compiler_params=_params(("parallel", "arbitrary")),
        name="in_proj",
    )(x2, norm_w.reshape(1, D), w_main, w_ba)


def _retention_kernel(q_ref, k_ref, v_ref, g_ref, pos_ref, inv_ref, sgn_ref, dmask_ref, qdec_ref,
                      kdec_ref, cdec_ref, gnw_ref, o_ref, state_ref):
    @pl.when(pl.program_id(2) == 0)
    def _():
        state_ref[...] = jnp.zeros_like(state_ref)

    ang = pos_ref[...] * inv_ref[...]
    cos2 = jnp.cos(ang)
    sin2 = jnp.sin(ang) * sgn_ref[...]
    for i in range(q_ref.shape[0]):
        q = q_ref[i].astype(f32)
        k = k_ref[i].astype(f32)
        qr = q * cos2 + pltpu.roll(q, HD // 2, 1) * sin2
        kr = (k * cos2 + pltpu.roll(k, HD // 2, 1) * sin2) * (HD ** -0.5)
        v = v_ref[i]
        s = _dot_nt(qr, kr) * dmask_ref[i]
        state = state_ref[i]
        o = _dot(s, v) + _dot(qr, state) * qdec_ref[i]
        state_ref[i] = state * cdec_ref[i] + _dot_tn(kr * kdec_ref[i], v)
        d = o - jnp.mean(o, axis=-1, keepdims=True)
        y = d * lax.rsqrt(jnp.mean(d * d, axis=-1, keepdims=True) + EPS) * gnw_ref[i]
        y = y * _silu(g_ref[i].astype(f32))
        o_ref[:, i * HD:(i + 1) * HD] = y.astype(bf16)


def _retention_tables(C):
    h = np.arange(HEADS, dtype=np.float64)
    log_gamma = np.log1p(-np.exp2(-5.0 - h))
    idx = np.arange(C, dtype=np.float64)
    rel = idx[:, None] - idx[None, :]
    dmask = np.where(rel >= 0, np.exp(log_gamma[:, None, None] * np.where(rel >= 0, rel, 0.0)), 0.0)
    qdec = np.exp(log_gamma[:, None] * (idx + 1.0))
    kdec = np.exp(log_gamma[:, None] * (C - 1.0 - idx))
    cdec = np.exp(log_gamma * C)
    rep = lambda a: np.broadcast_to(a[..., None], a.shape + (LANES,))
    return (jnp.asarray(dmask, f32), jnp.asarray(rep(qdec), f32), jnp.asarray(rep(kdec), f32),
            jnp.asarray(np.broadcast_to(cdec[:, None, None], (HEADS, 1, LANES)), f32))


def _retention(proj, pos_col, gn_w, B, S, hg):
    C = RET_CHUNK
    NC = S // C
    T = B * S
    G = HEADS // hg
    half = HD // 2
    inv = ROPE_BASE ** (-np.arange(half, dtype=np.float32) / half)
    inv2 = jnp.asarray(np.concatenate([inv, inv]).reshape(1, HD), f32)
    sgn = jnp.asarray(np.concatenate([-np.ones(half), np.ones(half)]).reshape(1, HD), f32)
    dmask, qdec, kdec, cdec = _retention_tables(C)

    def slab(off):
        return pl.BlockSpec((hg, C, HD), lambda b, g, n: (off // hg + g, b * NC + n, 0))

    def table(shape):
        return pl.BlockSpec((hg,) + shape, lambda b, g, n: (g,) + (0,) * len(shape))

    return pl.pallas_call(
        _retention_kernel,
        grid=(B, G, NC),
        in_specs=[
            slab(0), slab(HEADS), slab(2 * HEADS), slab(3 * HEADS),
            pl.BlockSpec((C, 1), lambda b, g, n: (b * NC + n, 0)),
            pl.BlockSpec((1, HD), lambda b, g, n: (0, 0)),
            pl.BlockSpec((1, HD), lambda b, g, n: (0, 0)),
            table((C, C)), table((C, LANES)), table((C, LANES)), table((1, LANES)), table((1, HD)),
        ],
        out_specs=pl.BlockSpec((C, hg * HD), lambda b, g, n: (b * NC + n, g)),
        out_shape=jax.ShapeDtypeStruct((T, HEADS * HD), bf16),
        scratch_shapes=[pltpu.VMEM((hg, HD, HD), f32)],
        compiler_params=_params(("parallel", "parallel", "arbitrary")),
        name="retention",
    )(proj, proj, proj, proj, pos_col, inv2, sgn, dmask, qdec, kdec, cdec,
      gn_w.reshape(HEADS, 1, HD))


def _unit_lower_inverse(a, same8, level_masks, eye):
    d = jnp.where(same8, a, 0.0)
    d2 = _dot(d, d)
    d3 = _dot(d, d2)
    d4 = _dot(d2, d2)
    x = eye - d + d2 - d3
    x = x + _dot(x, d4)
    for m in level_masks:
        x = x - _dot(_dot(x, jnp.where(m, a, 0.0)), x)
    return x


def _gdn_kernel(q_ref, k_ref, v_ref, z_ref, ba_ref, cw_ref, arow_ref, dtrow_ref, nw_ref, o_ref,
                state_ref, tail_ref, win_ref, *, hg):
    C = q_ref.shape[1]
    first = pl.program_id(2) == 0

    @pl.when(first)
    def _():
        state_ref[...] = jnp.zeros_like(state_ref)
        tail_ref[...] = jnp.zeros_like(tail_ref)

    row = lax.broadcasted_iota(jnp.int32, (C, C), 0)
    col = lax.broadcasted_iota(jnp.int32, (C, C), 1)
    causal = row >= col
    strict = row > col
    same8 = (row // 8) == (col // 8)
    level_masks = []
    s = 8
    while s < C:
        level_masks.append(((row // (2 * s)) == (col // (2 * s))) & ((row // s) != (col // s)))
        s *= 2
    eye = jnp.where(row == col, 1.0, 0.0).astype(f32)
    tri = jnp.where(causal, 1.0, 0.0).astype(bf16)

    ba = ba_ref[...]
    beta_all = _sigmoid(ba)
    xa = ba + dtrow_ref[...]
    softplus = jnp.maximum(xa, 0.0) + jnp.log(1.0 + jnp.exp(-jnp.abs(xa)))
    glog = -jnp.exp(arow_ref[...]) * softplus
    g_hi, g_mid, g_lo = _split3(glog)
    gcum = (jnp.dot(tri, g_hi, preferred_element_type=f32)
            + jnp.dot(tri, g_mid, preferred_element_type=f32)
            + jnp.dot(tri, g_lo, preferred_element_type=f32))
    gcum_t = gcum.T
    lane = lax.broadcasted_iota(jnp.int32, (C, LANES), 1)
    sub = lax.broadcasted_iota(jnp.int32, (LANES, C), 0)

    def conv_silu(x_ref, kind, i):
        slot = kind * hg + i
        win_ref[slot, 0:8, :] = tail_ref[slot]
        win_ref[slot, 8:8 + C, :] = x_ref[i].astype(f32)
        w = cw_ref[kind, i]
        y = win_ref[slot, 8:8 + C, :] * w[CONV_W - 1:CONV_W, :]
        for t in range(1, CONV_W):
            y = y + win_ref[slot, 8 - t:8 - t + C, :] * w[CONV_W - 1 - t:CONV_W - t, :]
        tail_ref[slot] = win_ref[slot, C:C + 8, :]
        return _silu(y)

    for i in range(hg):
        hd = pl.program_id(1) * hg + i
        q = conv_silu(q_ref, 0, i)
        k = conv_silu(k_ref, 1, i)
        v = conv_silu(v_ref, 2, i)
        q = q * lax.rsqrt(jnp.sum(q * q, axis=-1, keepdims=True) + EPS) * (HD ** -0.5)
        k = k * lax.rsqrt(jnp.sum(k * k, axis=-1, keepdims=True) + EPS)

        bcol = jnp.sum(jnp.where(lane == hd, beta_all, 0.0), axis=1, keepdims=True)
        gcol = jnp.sum(jnp.where(lane == hd + HEADS, gcum, 0.0), axis=1, keepdims=True)
        grow = jnp.sum(jnp.where(sub == hd + HEADS, gcum_t, 0.0), axis=0, keepdims=True)
        glast = gcol[C - 1:C, :]

        gam = jnp.where(causal, jnp.exp(jnp.where(causal, gcol - grow, 0.0)), 0.0)
        a = jnp.where(strict, _dot_nt(k, k) * gam * bcol, 0.0)
        tinv = _unit_lower_inverse(a, same8, level_masks, eye)
        egc = jnp.exp(gcol)
        rhs = jnp.concatenate([v * bcol, k * (bcol * egc)], axis=1)
        sol = _dot(tinv, rhs)
        u = sol[:, :HD]
        w = sol[:, HD:]
        qk = _dot_nt(q, k) * gam
        q_dec = q * egc
        k_dec = k * jnp.exp(glast - gcol)

        state = state_ref[i]
        ws = _dot(jnp.concatenate([w, q_dec], axis=0), state)
        v_new = u - ws[:C]
        o = ws[C:] + _dot(qk, v_new)
        state_ref[i] = state * jnp.exp(glast) + _dot_tn(k_dec, v_new)

        o = o * _rms_scale(o) * nw_ref[...]
        o = o * _silu(z_ref[i].astype(f32))
        o_ref[:, i * HD:(i + 1) * HD] = o.astype(bf16)


def _gdn(proj, ba, conv_w, a_log, dt_bias, norm_w, B, S, hg):
    C = DN_CHUNK
    NC = S // C
    T = B * S
    G = HEADS // hg
    cw = conv_w.reshape(CONV_W, 3, HEADS, HD).transpose(1, 2, 0, 3)
    pad = jnp.zeros((LANES - 2 * HEADS,), f32)
    arow = jnp.concatenate([jnp.zeros((HEADS,), f32), a_log.astype(f32), pad]).reshape(1, LANES)
    dtrow = jnp.concatenate([jnp.zeros((HEADS,), f32), dt_bias.astype(f32), pad]).reshape(1, LANES)

    def slab(off):
        return pl.BlockSpec((hg, C, HD), lambda b, g, n: (off // hg + g, b * NC + n, 0))

    row_spec = pl.BlockSpec((1, LANES), lambda b, g, n: (0, 0))
    return pl.pallas_call(
        functools.partial(_gdn_kernel, hg=hg),
        grid=(B, G, NC),
        in_specs=[
            slab(4 * HEADS), slab(5 * HEADS), slab(6 * HEADS), slab(7 * HEADS),
            pl.BlockSpec((C, LANES), lambda b, g, n: (b * NC + n, 0)),
            pl.BlockSpec((3, hg, CONV_W, HD), lambda b, g, n: (0, g, 0, 0)),
            row_spec, row_spec, row_spec,
        ],
        out_specs=pl.BlockSpec((C, hg * HD), lambda b, g, n: (b * NC + n, g)),
        out_shape=jax.ShapeDtypeStruct((T, HEADS * HD), bf16),
        scratch_shapes=[
            pltpu.VMEM((hg, HD, HD), f32),
            pltpu.VMEM((3 * hg, 8, HD), f32),
            pltpu.VMEM((3 * hg, C + 8, HD), f32),
        ],
        compiler_params=_params(("parallel", "parallel", "arbitrary")),
        name="gdn",
    )(proj, proj, proj, proj, ba, cw, arow, dtrow, norm_w.reshape(1, HD))


def _mm_res_kernel(a1_ref, a2_ref, w_ref, r_ref, o_ref):
    k1 = a1_ref.shape[1]
    acc = jnp.dot(a1_ref[...], w_ref[0:k1, :], preferred_element_type=f32)
    acc = acc + jnp.dot(a2_ref[...], w_ref[k1:, :], preferred_element_type=f32)
    o_ref[...] = r_ref[...] + acc


def _mm_res(a1, a2, w, res, tm, tn):
    T, K1 = a1.shape
    K2 = a2.shape[1]
    N = w.shape[1]
    return pl.pallas_call(
        _mm_res_kernel,
        grid=(T // tm, N // tn),
        in_specs=[
            pl.BlockSpec((tm, K1), lambda i, j: (i, 0)),
            pl.BlockSpec((tm, K2), lambda i, j: (i, 0)),
            pl.BlockSpec((K1 + K2, tn), lambda i, j: (0, j)),
            pl.BlockSpec((tm, tn), lambda i, j: (i, j)),
        ],
        out_specs=pl.BlockSpec((tm, tn), lambda i, j: (i, j)),
        out_shape=jax.ShapeDtypeStruct((T, N), f32),
        compiler_params=_params(("parallel", "parallel")),
        name="out_proj",
    )(a1, a2, w, res)


def _norm_mm_kernel(x_ref, nw_ref, w_ref, o_ref, xn_ref):
    @pl.when(pl.program_id(1) == 0)
    def _():
        x = x_ref[...]
        xn_ref[...] = (x * _rms_scale(x) * nw_ref[...]).astype(bf16)

    o_ref[...] = jnp.dot(xn_ref[...], w_ref[...], preferred_element_type=f32).astype(o_ref.dtype)


def _norm_mm(x2, norm_w, w, tm, tn, name):
    T, D = x2.shape
    N = w.shape[1]
    return pl.pallas_call(
        _norm_mm_kernel,
        grid=(T // tm, N // tn),
        in_specs=[
            pl.BlockSpec((tm, D), lambda i, j: (i, 0)),
            pl.BlockSpec((1, D), lambda i, j: (0, 0)),
            pl.BlockSpec((D, tn), lambda i, j: (0, j)),
        ],
        out_specs=pl.BlockSpec((tm, tn), lambda i, j: (i, j)),
        out_shape=jax.ShapeDtypeStruct((T, N), bf16),
        scratch_shapes=[pltpu.VMEM((tm, D), bf16)],
        compiler_params=_params(("parallel", "arbitrary")),
        name=name,
    )(x2, norm_w.reshape(1, D), w)


def _attn_kernel(q_ref, k_ref, v_ref, wo_ref, r_ref, o_ref):
    D = q_ref.shape[1]
    dh = D // XA_HEADS
    acc = r_ref[...]
    for h in range(XA_HEADS):
        sl = slice(h * dh, (h + 1) * dh)
        s = _dot_nt(q_ref[:, sl], k_ref[:, sl]) * (dh ** -0.5)
        p = jnp.exp(s - jnp.max(s, axis=-1, keepdims=True))
        p = p / jnp.sum(p, axis=-1, keepdims=True)
        oh = _dot(p, v_ref[:, sl])
        acc = acc + _dot(oh, wo_ref[sl, :])
    o_ref[...] = acc


def _attn(q, kv, w_o, res, B, S, M, tm):
    T, D = q.shape
    nt = S // tm
    return pl.pallas_call(
        _attn_kernel,
        grid=(B, nt),
        in_specs=[
            pl.BlockSpec((tm, D), lambda b, i: (b * nt + i, 0)),
            pl.BlockSpec((M, D), lambda b, i: (b, 0)),
            pl.BlockSpec((M, D), lambda b, i: (b, 1)),
            pl.BlockSpec((D, D), lambda b, i: (0, 0)),
            pl.BlockSpec((tm, D), lambda b, i: (b * nt + i, 0)),
        ],
        out_specs=pl.BlockSpec((tm, D), lambda b, i: (b * nt + i, 0)),
        out_shape=jax.ShapeDtypeStruct((T, D), f32),
        compiler_params=_params(("parallel", "parallel")),
        name="attn",
    )(q, kv, kv, w_o, res)


def _router_kernel(x_ref, nw_ref, whi_ref, wlo_ref, b_ref, o_ref):
    x = x_ref[...]
    xn = x * _rms_scale(x) * nw_ref[...]
    xhi = xn.astype(bf16)
    xlo = (xn - xhi.astype(f32)).astype(bf16)
    whi = whi_ref[...]
    logits = (jnp.dot(xhi, whi, preferred_element_type=f32)
              + jnp.dot(xlo, whi, preferred_element_type=f32)
              + jnp.dot(xhi, wlo_ref[...], preferred_element_type=f32)) + b_ref[...]
    lane = lax.broadcasted_iota(jnp.int32, logits.shape, 1).astype(f32)
    neg = -jnp.inf
    big = float(LANES)

    def first_max(vals):
        m = jnp.max(vals, axis=-1, keepdims=True)
        return m, jnp.min(jnp.where(vals == m, lane, big), axis=-1, keepdims=True)

    is_group = lane < N_GROUPS
    gmax, gsel = first_max(jnp.where(is_group, logits, neg))
    gsum = jnp.sum(jnp.where(is_group, jnp.exp(logits - gmax), 0.0), axis=-1, keepdims=True)
    g_w = 1.0 / gsum
    lo = N_GROUPS + EXP_PER_GROUP * gsel
    in_group = (lane >= lo) & (lane < lo + EXP_PER_GROUP)
    el = jnp.where(in_group, logits, neg)
    l1, i1 = first_max(el)
    l2, i2 = first_max(jnp.where(lane == i1, neg, el))
    esum = jnp.sum(jnp.where(in_group, jnp.exp(logits - l1), 0.0), axis=-1, keepdims=True)
    p1 = 1.0 / esum
    p2 = jnp.exp(l2 - l1) / esum
    gate1 = g_w * p1 / (p1 + p2)
    gate2 = g_w * p2 / (p1 + p2)
    out = jnp.where(lane == 0, i1 - N_GROUPS,
                    jnp.where(lane == 1, i2 - N_GROUPS,
                              jnp.where(lane == 2, gate1, jnp.where(lane == 3, gate2, 0.0))))
    o_ref[...] = out


def _router(h2, norm_w, w_hi, w_lo, bias_row, tm):
    T, D = h2.shape
    return pl.pallas_call(
        _router_kernel,
        grid=(T // tm,),
        in_specs=[
            pl.BlockSpec((tm, D), lambda i: (i, 0)),
            pl.BlockSpec((1, D), lambda i: (0, 0)),
            pl.BlockSpec((D, LANES), lambda i: (0, 0)),
            pl.BlockSpec((D, LANES), lambda i: (0, 0)),
            pl.BlockSpec((1, LANES), lambda i: (0, 0)),
        ],
        out_specs=pl.BlockSpec((tm, LANES), lambda i: (i, 0)),
        out_shape=jax.ShapeDtypeStruct((T, LANES), f32),
        compiler_params=_params(("parallel",)),
        name="router",
    )(h2, norm_w.reshape(1, D), w_hi, w_lo, bias_row)


def _ffn_kernel(blk_e_ref, src_ref, nused_ref, h_hbm, nw_ref, wg_ref, wu_ref, wd_ref, y_ref,
                xbuf, sem):
    i = pl.program_id(0)
    bm = xbuf.shape[1]
    nused = nused_ref[0]
    slot = i % 2

    def row_copy(tok, slot_, r):
        return pltpu.make_async_copy(h_hbm.at[pl.ds(tok, 1)], xbuf.at[slot_, pl.ds(r, 1)],
                                     sem.at[slot_])

    def issue(blk, slot_):
        def body(r, carry):
            row_copy(src_ref[blk * bm + r], slot_, r).start()
            return carry
        lax.fori_loop(0, bm, body, 0)

    @pl.when((i == 0) & (nused > 0))
    def _():
        issue(0, 0)

    @pl.when(i + 1 < nused)
    def _():
        issue(i + 1, 1 - slot)

    @pl.when(i < nused)
    def _():
        pltpu.make_async_copy(xbuf.at[slot], xbuf.at[slot], sem.at[slot]).wait()
        x = xbuf[slot]
        xn = (x * _rms_scale(x) * nw_ref[...]).astype(bf16)
        g = jnp.dot(xn, wg_ref[...], preferred_element_type=f32)
        u = jnp.dot(xn, wu_ref[...], preferred_element_type=f32)
        hmid = (_silu(g) * u).astype(bf16)
        y_ref[...] = jnp.dot(hmid, wd_ref[...], preferred_element_type=f32)

    @pl.when(i >= nused)
    def _():
        y_ref[...] = jnp.zeros_like(y_ref)


def _ffn(h2, norm_w, wg, wu, wd, blk_e, src, nused, bm):
    T, D = h2.shape
    F = wg.shape[2]
    NB = blk_e.shape[0]
    grid_spec = pltpu.PrefetchScalarGridSpec(
        num_scalar_prefetch=3,
        grid=(NB,),
        in_specs=[
            pl.BlockSpec(memory_space=pl.ANY),
            pl.BlockSpec((1, D), lambda i, be, s, n: (0, 0)),
            pl.BlockSpec((None, D, F), lambda i, be, s, n: (be[i], 0, 0)),
            pl.BlockSpec((None, D, F), lambda i, be, s, n: (be[i], 0, 0)),
            pl.BlockSpec((None, F, D), lambda i, be, s, n: (be[i], 0, 0)),
        ],
        out_specs=pl.BlockSpec((bm, D), lambda i, be, s, n: (i, 0)),
        scratch_shapes=[pltpu.VMEM((2, bm, D), f32), pltpu.SemaphoreType.DMA((2,))],
    )
    return pl.pallas_call(
        _ffn_kernel,
        grid_spec=grid_spec,
        out_shape=jax.ShapeDtypeStruct((NB * bm, D), f32),
        compiler_params=_params(("arbitrary",)),
        name="ffn",
    )(blk_e, src, nused, h2, norm_w.reshape(1, D), wg, wu, wd)


def _combine_kernel(dest_ref, y_hbm, h_ref, rt_ref, nw_ref, o_ref, ybuf, sem):
    i = pl.program_id(0)
    n = pl.num_programs(0)
    tm = h_ref.shape[0]
    slot = i % 2

    def issue(blk, slot_):
        def body(r, carry):
            a = 2 * (blk * tm + r)
            for kk in range(2):
                pltpu.make_async_copy(y_hbm.at[pl.ds(dest_ref[a + kk], 1)],
                                      ybuf.at[slot_, kk, pl.ds(r, 1)], sem.at[slot_]).start()
            return carry
        lax.fori_loop(0, tm, body, 0)

    @pl.when(i == 0)
    def _():
        issue(0, 0)

    @pl.when(i + 1 < n)
    def _():
        issue(i + 1, 1 - slot)

    pltpu.make_async_copy(ybuf.at[slot], ybuf.at[slot], sem.at[slot]).wait()
    rt = rt_ref[...]
    h = h_ref[...] + rt[:, 2:3] * ybuf[slot, 0] + rt[:, 3:4] * ybuf[slot, 1]
    o_ref[...] = h * _rms_scale(h) * nw_ref[...]


def _combine(y, h2, rout, dest, norm_w, tm):
    T, D = h2.shape
    grid_spec = pltpu.PrefetchScalarGridSpec(
        num_scalar_prefetch=1,
        grid=(T // tm,),
        in_specs=[
            pl.BlockSpec(memory_space=pl.ANY),
            pl.BlockSpec((tm, D), lambda i, d: (i, 0)),
            pl.BlockSpec((tm, LANES), lambda i, d: (i, 0)),
            pl.BlockSpec((1, D), lambda i, d: (0, 0)),
        ],
        out_specs=pl.BlockSpec((tm, D), lambda i, d: (i, 0)),
        scratch_shapes=[pltpu.VMEM((2, 2, tm, D), f32), pltpu.SemaphoreType.DMA((2,))],
    )
    return pl.pallas_call(
        _combine_kernel,
        grid_spec=grid_spec,
        out_shape=jax.ShapeDtypeStruct((T, D), f32),
        compiler_params=_params(("arbitrary",)),
        name="combine",
    )(dest, y, h2, rout, norm_w.reshape(1, D))


def _dispatch_plan(e_tk, bm):
    T = e_tk.shape[0]
    A = 2 * T
    NB = A // bm + N_EXPERTS
    flat_e = e_tk.reshape(A)
    order = jnp.argsort(flat_e, stable=True).astype(jnp.int32)
    e_sorted = flat_e[order]
    experts = jnp.arange(N_EXPERTS, dtype=jnp.int32)
    start = jnp.searchsorted(e_sorted, experts, side="left").astype(jnp.int32)
    counts = jnp.searchsorted(e_sorted, experts, side="right").astype(jnp.int32) - start
    padded = (counts + bm - 1) // bm * bm
    pend = jnp.cumsum(padded)
    pstart = pend - padded
    blk_e = jnp.minimum(jnp.searchsorted(pend, jnp.arange(NB, dtype=jnp.int32) * bm, side="right"),
                        N_EXPERTS - 1).astype(jnp.int32)
    nused = (pend[-1] // bm).astype(jnp.int32).reshape(1)
    p = jnp.arange(NB * bm, dtype=jnp.int32)
    pe = blk_e[p // bm]
    off = p - pstart[pe]
    valid = (off >= 0) & (off < counts[pe])
    src = jnp.where(valid, order[jnp.clip(start[pe] + off, 0, A - 1)] // 2, 0).astype(jnp.int32)
    rank = jnp.zeros((A,), jnp.int32).at[order].set(jnp.arange(A, dtype=jnp.int32))
    dest = (pstart[flat_e] + rank - start[flat_e]).astype(jnp.int32)
    return blk_e, src, nused, dest


def _tile(n, want):
    t = min(n, want)
    while n % t:
        t //= 2
    return t


def kernel(x, mem, positions, norm_mix_w, w_in, dn_conv_w, dn_a_log, dn_dt_bias, ret_gn_w, dn_norm_w,
           w_out, norm_xq_w, norm_mem_w, w_xq, w_xkv, w_xo, norm_moe_w, w_group_router, b_group_router,
           w_expert_router, b_expert_router, w_gate, w_up, w_down, norm_final_w):
    B, S, D = x.shape
    M = mem.shape[1]
    T = B * S
    depth = w_in.shape[0]
    n_main = 8 * HEADS * HD
    h = x.reshape(T, D)
    pos_col = positions.astype(f32).reshape(T, 1)
    mem2 = mem.reshape(B * M, D)
    hg = 4
    bm = 256
    for l in range(depth):
        w_main = w_in[l][:, :n_main].astype(bf16)
        w_ba = jnp.pad(w_in[l][:, n_main:], ((0, 0), (0, LANES - 2 * HEADS))).astype(bf16)
        proj, ba = _in_proj(h, norm_mix_w[l], w_main, w_ba, _tile(T, 1024), 512)
        mix_r = _retention(proj, pos_col, ret_gn_w[l], B, S, hg)
        mix_d = _gdn(proj, ba, dn_conv_w[l], dn_a_log[l], dn_dt_bias[l], dn_norm_w[l], B, S, hg)
        h = _mm_res(mix_r, mix_d, w_out[l].astype(bf16), h, _tile(T, 1024), _tile(D, 1024))
        q = _norm_mm(h, norm_xq_w[l], w_xq[l].astype(bf16), _tile(T, 1024), _tile(D, 1024), "xq")
        kv = _norm_mm(mem2, norm_mem_w[l], w_xkv[l].astype(bf16), _tile(B * M, 512), _tile(2 * D, 1024), "xkv")
        h = _attn(q, kv, w_xo[l].astype(bf16), h, B, S, M, _tile(S, 256))
        w_r = jnp.pad(jnp.concatenate([w_group_router[l], w_expert_router[l]], axis=1),
                      ((0, 0), (0, LANES - N_GROUPS - N_EXPERTS)))
        w_r_hi = w_r.astype(bf16)
        w_r_lo = (w_r - w_r_hi.astype(f32)).astype(bf16)
        b_r = jnp.pad(jnp.concatenate([b_group_router[l], b_expert_router[l]]),
                      (0, LANES - N_GROUPS - N_EXPERTS)).reshape(1, LANES).astype(f32)
        rout = _router(h, norm_moe_w[l], w_r_hi, w_r_lo, b_r, _tile(T, 512))
        blk_e, src, nused, dest = _dispatch_plan(rout[:, :2].astype(jnp.int32), bm)
        y = _ffn(h, norm_moe_w[l], w_gate[l].astype(bf16), w_up[l].astype(bf16), w_down[l].astype(bf16),
                 blk_e, src, nused, bm)
        if l + 1 < depth:
            raise NotImplementedError("the fused MoE combine applies the final norm: depth 1 only")
        h = _combine(y, h, rout, dest, norm_final_w, _tile(T, 256))
    return h.reshape(B, S, D)
```

```python
import functools

import numpy as np
import jax
import jax.numpy as jnp
from jax import lax
from jax.experimental import pallas as pl
from jax.experimental.pallas import tpu as pltpu

f32 = jnp.float32
bf16 = jnp.bfloat16

EPS = 1e-6
HEADS = 8
HD = 128
LANES = 128
XA_HEADS = 4
CONV_W = 4
ROPE_BASE = 10000.0
N_GROUPS = 4
EXP_PER_GROUP = 8
N_EXPERTS = N_GROUPS * EXP_PER_GROUP
RET_CHUNK = 256
DN_CHUNK = 128
VMEM_LIMIT = 56 * 1024 * 1024


def _params(sem, vmem=VMEM_LIMIT):
    return pltpu.CompilerParams(dimension_semantics=sem, vmem_limit_bytes=vmem)


def _dot(a, b):
    return jnp.dot(a.astype(bf16), b.astype(bf16), preferred_element_type=f32)


def _dot_nt(a, b):
    return lax.dot_general(a.astype(bf16), b.astype(bf16), (((1,), (1,)), ((), ())),
                           preferred_element_type=f32)


def _dot_tn(a, b):
    return lax.dot_general(a.astype(bf16), b.astype(bf16), (((0,), (0,)), ((), ())),
                           preferred_element_type=f32)


def _split3(a):
    hi = a.astype(bf16)
    r = a - hi.astype(f32)
    mid = r.astype(bf16)
    lo = (r - mid.astype(f32)).astype(bf16)
    return hi, mid, lo


def _sigmoid(x):
    return 1.0 / (1.0 + jnp.exp(-x))


def _silu(x):
    return x * _sigmoid(x)


def _rms_scale(x):
    return lax.rsqrt(jnp.mean(x * x, axis=-1, keepdims=True) + EPS)


def _in_proj_kernel(x_ref, nw_ref, w_ref, wba_ref, o_ref, ba_ref, xn_ref):
    @pl.when(pl.program_id(1) == 0)
    def _():
        x = x_ref[...]
        xn = (x * _rms_scale(x) * nw_ref[...]).astype(bf16)
        xn_ref[...] = xn
        ba_ref[...] = jnp.dot(xn, wba_ref[...], preferred_element_type=f32)

    acc = jnp.dot(xn_ref[...], w_ref[...], preferred_element_type=f32)
    for c in range(o_ref.shape[0]):
        o_ref[c] = acc[:, c * LANES:(c + 1) * LANES].astype(bf16)


def _in_proj(x2, norm_w, w_main, w_ba, tm, tn):
    T, D = x2.shape
    N = w_main.shape[1]
    return pl.pallas_call(
        _in_proj_kernel,
        grid=(T // tm, N // tn),
        in_specs=[
            pl.BlockSpec((tm, D), lambda i, j: (i, 0)),
            pl.BlockSpec((1, D), lambda i, j: (0, 0)),
            pl.BlockSpec((D, tn), lambda i, j: (0, j)),
            pl.BlockSpec((D, LANES), lambda i, j: (0, 0)),
        ],
        out_specs=[
            pl.BlockSpec((tn // LANES, tm, LANES), lambda i, j: (j, i, 0)),
            pl.BlockSpec((tm, LANES), lambda i, j: (i, 0)),
        ],
        out_shape=[
            jax.ShapeDtypeStruct((N // LANES, T, LANES), bf16),
            jax.ShapeDtypeStruct((T, LANES), f32),
        ],
        scratch_shapes=[pltpu.VMEM((tm, D), bf16)],
        compiler_params=_params(("parallel", "arbitrary")),
        name="in_proj",
    )(x2, norm_w.reshape(1, D), w_main, w_ba)


def _retention_kernel(q_ref, k_ref, v_ref, g_ref, pos_ref, inv_ref, sgn_ref, dmask_ref, qdec_ref,
                      kdec_ref, cdec_ref, gnw_ref, o_ref, state_ref):
    @pl.when(pl.program_id(2) == 0)
    def _():
        state_ref[...] = jnp.zeros_like(state_ref)

    ang = pos_ref[...] * inv_ref[...]
    cos2 = jnp.cos(ang)
    sin2 = jnp.sin(ang) * sgn_ref[...]
    heads = range(q_ref.shape[0])

    def rope(x):
        return x * cos2 + pltpu.roll(x, HD // 2, 1) * sin2

    qr = [rope(q_ref[i].astype(f32)) for i in heads]
    kr = [rope(k_ref[i].astype(f32)) * (HD ** -0.5) for i in heads]
    state = [state_ref[i] for i in heads]
    s = [_dot_nt(qr[i], kr[i]) * dmask_ref[i] for i in heads]
    cross = [_dot(qr[i], state[i]) * qdec_ref[i] for i in heads]
    upd = [_dot_tn(kr[i] * kdec_ref[i], v_ref[i]) for i in heads]
    o = [_dot(s[i], v_ref[i]) + cross[i] for i in heads]
    for i in heads:
        state_ref[i] = state[i] * cdec_ref[i] + upd[i]
        d = o[i] - jnp.mean(o[i], axis=-1, keepdims=True)
        y = d * lax.rsqrt(jnp.mean(d * d, axis=-1, keepdims=True) + EPS) * gnw_ref[i]
        y = y * _silu(g_ref[i].astype(f32))
        o_ref[:, i * HD:(i + 1) * HD] = y.astype(bf16)


def _retention_tables(C):
    h = np.arange(HEADS, dtype=np.float64)
    log_gamma = np.log1p(-np.exp2(-5.0 - h))
    idx = np.arange(C, dtype=np.float64)
    rel = idx[:, None] - idx[None, :]
    dmask = np.where(rel >= 0, np.exp(log_gamma[:, None, None] * np.where(rel >= 0, rel, 0.0)), 0.0)
    qdec = np.exp(log_gamma[:, None] * (idx + 1.0))
    kdec = np.exp(log_gamma[:, None] * (C - 1.0 - idx))
    cdec = np.exp(log_gamma * C)
    rep = lambda a: np.broadcast_to(a[..., None], a.shape + (LANES,))
    return (jnp.asarray(dmask, f32), jnp.asarray(rep(qdec), f32), jnp.asarray(rep(kdec), f32),
            jnp.asarray(np.broadcast_to(cdec[:, None, None], (HEADS, 1, LANES)), f32))


def _retention(proj, pos_col, gn_w, B, S, hg):
    C = RET_CHUNK
    NC = S // C
    T = B * S
    G = HEADS // hg
    half = HD // 2
    inv = ROPE_BASE ** (-np.arange(half, dtype=np.float32) / half)
    inv2 = jnp.asarray(np.concatenate([inv, inv]).reshape(1, HD), f32)
    sgn = jnp.asarray(np.concatenate([-np.ones(half), np.ones(half)]).reshape(1, HD), f32)
    dmask, qdec, kdec, cdec = _retention_tables(C)

    def slab(off):
        return pl.BlockSpec((hg, C, HD), lambda b, g, n: (off // hg + g, b * NC + n, 0))

    def table(shape):
        return pl.BlockSpec((hg,) + shape, lambda b, g, n: (g,) + (0,) * len(shape))

    return pl.pallas_call(
        _retention_kernel,
        grid=(B, G, NC),
        in_specs=[
            slab(0), slab(HEADS), slab(2 * HEADS), slab(3 * HEADS),
            pl.BlockSpec((C, 1), lambda b, g, n: (b * NC + n, 0)),
            pl.BlockSpec((1, HD), lambda b, g, n: (0, 0)),
            pl.BlockSpec((1, HD), lambda b, g, n: (0, 0)),
            table((C, C)), table((C, LANES)), table((C, LANES)), table((1, LANES)), table((1, HD)),
        ],
        out_specs=pl.BlockSpec((C, hg * HD), lambda b, g, n: (b * NC + n, g)),
        out_shape=jax.ShapeDtypeStruct((T, HEADS * HD), bf16),
        scratch_shapes=[pltpu.VMEM((hg, HD, HD), f32)],
        compiler_params=_params(("parallel", "parallel", "arbitrary")),
        name="retention",
    )(proj, proj, proj, proj, pos_col, inv2, sgn, dmask, qdec, kdec, cdec,
      gn_w.reshape(HEADS, 1, HD))


def _gdn_kernel(q_ref, k_ref, v_ref, z_ref, ba_ref, cw_ref, arow_ref, dtrow_ref, nw_ref, o_ref,
                state_ref, tail_ref, win_ref, *, hg):
    C = q_ref.shape[1]
    first = pl.program_id(2) == 0

    @pl.when(first)
    def _():
        state_ref[...] = jnp.zeros_like(state_ref)
        tail_ref[...] = jnp.zeros_like(tail_ref)

    row = lax.broadcasted_iota(jnp.int32, (C, C), 0)
    col = lax.broadcasted_iota(jnp.int32, (C, C), 1)
    causal = row >= col
    strict = row > col
    same8 = (row // 8) == (col // 8)
    level_masks = []
    s = 8
    while s < C:
        level_masks.append(((row // (2 * s)) == (col // (2 * s))) & ((row // s) != (col // s)))
        s *= 2
    eye = jnp.where(row == col, 1.0, 0.0).astype(f32)
    tri = jnp.where(causal, 1.0, 0.0).astype(bf16)

    ba = ba_ref[...]
    beta_all = _sigmoid(ba)
    xa = ba + dtrow_ref[...]
    softplus = jnp.maximum(xa, 0.0) + jnp.log(1.0 + jnp.exp(-jnp.abs(xa)))
    glog = -jnp.exp(arow_ref[...]) * softplus
    g_hi, g_mid, g_lo = _split3(glog)
    gcum = (jnp.dot(tri, g_hi, preferred_element_type=f32)
            + jnp.dot(tri, g_mid, preferred_element_type=f32)
            + jnp.dot(tri, g_lo, preferred_element_type=f32))
    gcum_t = gcum.T
    lane = lax.broadcasted_iota(jnp.int32, (C, LANES), 1)
    sub = lax.broadcasted_iota(jnp.int32, (LANES, C), 0)

    def conv_silu(x_ref, kind, i):
        slot = kind * hg + i
        win_ref[slot, 0:8, :] = tail_ref[slot]
        win_ref[slot, 8:8 + C, :] = x_ref[i].astype(f32)
        w = cw_ref[kind, i]
        y = win_ref[slot, 8:8 + C, :] * w[CONV_W - 1:CONV_W, :]
        for t in range(1, CONV_W):
            y = y + win_ref[slot, 8 - t:8 - t + C, :] * w[CONV_W - 1 - t:CONV_W - t, :]
        tail_ref[slot] = win_ref[slot, C:C + 8, :]
        return _silu(y)

    heads = range(hg)
    hd = [pl.program_id(1) * hg + i for i in heads]
    q = [conv_silu(q_ref, 0, i) for i in heads]
    k = [conv_silu(k_ref, 1, i) for i in heads]
    v = [conv_silu(v_ref, 2, i) for i in heads]
    q = [x * lax.rsqrt(jnp.sum(x * x, axis=-1, keepdims=True) + EPS) * (HD ** -0.5) for x in q]
    k = [x * lax.rsqrt(jnp.sum(x * x, axis=-1, keepdims=True) + EPS) for x in k]

    bcol = [jnp.sum(jnp.where(lane == h, beta_all, 0.0), axis=1, keepdims=True) for h in hd]
    gcol = [jnp.sum(jnp.where(lane == h + HEADS, gcum, 0.0), axis=1, keepdims=True) for h in hd]
    grow = [jnp.sum(jnp.where(sub == h + HEADS, gcum_t, 0.0), axis=0, keepdims=True) for h in hd]
    glast = [g[C - 1:C, :] for g in gcol]
    gam = [jnp.where(causal, jnp.exp(jnp.where(causal, gc - gr, 0.0)), 0.0) for gc, gr in zip(gcol, grow)]
    egc = [jnp.exp(g) for g in gcol]

    kk = [_dot_nt(x, x) for x in k]
    qk = [_dot_nt(x, y) for x, y in zip(q, k)]
    a = [jnp.where(strict, m * g * b, 0.0) for m, g, b in zip(kk, gam, bcol)]
    qk = [m * g for m, g in zip(qk, gam)]

    d = [jnp.where(same8, m, 0.0) for m in a]
    d2 = [_dot(m, m) for m in d]
    d3 = [_dot(m, m2) for m, m2 in zip(d, d2)]
    d4 = [_dot(m2, m2) for m2 in d2]
    x = [eye - m + m2 - m3 for m, m2, m3 in zip(d, d2, d3)]
    x = [xi + _dot(xi, m4) for xi, m4 in zip(x, d4)]
    for mask in level_masks:
        t = [_dot(xi, jnp.where(mask, m, 0.0)) for xi, m in zip(x, a)]
        x = [xi - _dot(ti, xi) for xi, ti in zip(x, t)]

    rhs = [jnp.concatenate([vi * b, ki * (b * e)], axis=1) for vi, ki, b, e in zip(v, k, bcol, egc)]
    sol = [_dot(xi, r) for xi, r in zip(x, rhs)]
    q_dec = [qi * e for qi, e in zip(q, egc)]
    k_dec = [ki * jnp.exp(gl - gc) for ki, gl, gc in zip(k, glast, gcol)]

    state = [state_ref[i] for i in heads]
    ws = [_dot(jnp.concatenate([s_[:, HD:], qd], axis=0), st) for s_, qd, st in zip(sol, q_dec, state)]
    v_new = [s_[:, :HD] - w_[:C] for s_, w_ in zip(sol, ws)]
    o = [w_[C:] + _dot(m, vn) for w_, m, vn in zip(ws, qk, v_new)]
    upd = [_dot_tn(kd, vn) for kd, vn in zip(k_dec, v_new)]
    for i in heads:
        state_ref[i] = state[i] * jnp.exp(glast[i]) + upd[i]
        y = o[i] * _rms_scale(o[i]) * nw_ref[...]
        y = y * _silu(z_ref[i].astype(f32))
        o_ref[:, i * HD:(i + 1) * HD] = y.astype(bf16)


def _gdn(proj, ba, conv_w, a_log, dt_bias, norm_w, B, S, hg):
    C = DN_CHUNK
    NC = S // C
    T = B * S
    G = HEADS // hg
    cw = conv_w.reshape(CONV_W, 3, HEADS, HD).transpose(1, 2, 0, 3)
    pad = jnp.zeros((LANES - 2 * HEADS,), f32)
    arow = jnp.concatenate([jnp.zeros((HEADS,), f32), a_log.astype(f32), pad]).reshape(1, LANES)
    dtrow = jnp.concatenate([jnp.zeros((HEADS,), f32), dt_bias.astype(f32), pad]).reshape(1, LANES)

    def slab(off):
        return pl.BlockSpec((hg, C, HD), lambda b, g, n: (off // hg + g, b * NC + n, 0))

    row_spec = pl.BlockSpec((1, LANES), lambda b, g, n: (0, 0))
    return pl.pallas_call(
        functools.partial(_gdn_kernel, hg=hg),
        grid=(B, G, NC),
        in_specs=[
            slab(4 * HEADS), slab(5 * HEADS), slab(6 * HEADS), slab(7 * HEADS),
            pl.BlockSpec((C, LANES), lambda b, g, n: (b * NC + n, 0)),
            pl.BlockSpec((3, hg, CONV_W, HD), lambda b, g, n: (0, g, 0, 0)),
            row_spec, row_spec, row_spec,
        ],
        out_specs=pl.BlockSpec((C, hg * HD), lambda b, g, n: (b * NC + n, g)),
        out_shape=jax.ShapeDtypeStruct((T, HEADS * HD), bf16),
        scratch_shapes=[
            pltpu.VMEM((hg, HD, HD), f32),
            pltpu.VMEM((3 * hg, 8, HD), f32),
            pltpu.VMEM((3 * hg, C + 8, HD), f32),
        ],
        compiler_params=_params(("parallel", "parallel", "arbitrary")),
        name="gdn",
    )(proj, proj, proj, proj, ba, cw, arow, dtrow, norm_w.reshape(1, HD))


def _mm_res_kernel(a1_ref, a2_ref, w_ref, r_ref, o_ref):
    k1 = a1_ref.shape[1]
    acc = jnp.dot(a1_ref[...], w_ref[0:k1, :], preferred_element_type=f32)
    acc = acc + jnp.dot(a2_ref[...], w_ref[k1:, :], preferred_element_type=f32)
    o_ref[...] = r_ref[...] + acc


def _mm_res(a1, a2, w, res, tm, tn):
    T, K1 = a1.shape
    K2 = a2.shape[1]
    N = w.shape[1]
    return pl.pallas_call(
        _mm_res_kernel,
        grid=(T // tm, N // tn),
        in_specs=[
            pl.BlockSpec((tm, K1), lambda i, j: (i, 0)),
            pl.BlockSpec((tm, K2), lambda i, j: (i, 0)),
            pl.BlockSpec((K1 + K2, tn), lambda i, j: (0, j)),
            pl.BlockSpec((tm, tn), lambda i, j: (i, j)),
        ],
        out_specs=pl.BlockSpec((tm, tn), lambda i, j: (i, j)),
        out_shape=jax.ShapeDtypeStruct((T, N), f32),
        compiler_params=_params(("parallel", "parallel")),
        name="out_proj",
    )(a1, a2, w, res)


def _norm_mm_kernel(x_ref, nw_ref, w_ref, o_ref, xn_ref):
    @pl.when(pl.program_id(1) == 0)
    def _():
        x = x_ref[...]
        xn_ref[...] = (x * _rms_scale(x) * nw_ref[...]).astype(bf16)

    o_ref[...] = jnp.dot(xn_ref[...], w_ref[...], preferred_element_type=f32).astype(o_ref.dtype)


def _norm_mm(x2, norm_w, w, tm, tn, name):
    T, D = x2.shape
    N = w.shape[1]
    return pl.pallas_call(
        _norm_mm_kernel,
        grid=(T // tm, N // tn),
        in_specs=[
            pl.BlockSpec((tm, D), lambda i, j: (i, 0)),
            pl.BlockSpec((1, D), lambda i, j: (0, 0)),
            pl.BlockSpec((D, tn), lambda i, j: (0, j)),
        ],
        out_specs=pl.BlockSpec((tm, tn), lambda i, j: (i, j)),
        out_shape=jax.ShapeDtypeStruct((T, N), bf16),
        scratch_shapes=[pltpu.VMEM((tm, D), bf16)],
        compiler_params=_params(("parallel", "arbitrary")),
        name=name,
    )(x2, norm_w.reshape(1, D), w)


def _attn_kernel(q_ref, k_ref, v_ref, wo_ref, r_ref, o_ref):
    D = q_ref.shape[1]
    dh = D // XA_HEADS
    acc = r_ref[...]
    for h in range(XA_HEADS):
        sl = slice(h * dh, (h + 1) * dh)
        s = _dot_nt(q_ref[:, sl], k_ref[:, sl]) * (dh ** -0.5)
        p = jnp.exp(s - jnp.max(s, axis=-1, keepdims=True))
        p = p / jnp.sum(p, axis=-1, keepdims=True)
        oh = _dot(p, v_ref[:, sl])
        acc = acc + _dot(oh, wo_ref[sl, :])
    o_ref[...] = acc


def _attn(q, kv, w_o, res, B, S, M, tm):
    T, D = q.shape
    nt = S // tm
    return pl.pallas_call(
        _attn_kernel,
        grid=(B, nt),
        in_specs=[
            pl.BlockSpec((tm, D), lambda b, i: (b * nt + i, 0)),
            pl.BlockSpec((M, D), lambda b, i: (b, 0)),
            pl.BlockSpec((M, D), lambda b, i: (b, 1)),
            pl.BlockSpec((D, D), lambda b, i: (0, 0)),
            pl.BlockSpec((tm, D), lambda b, i: (b * nt + i, 0)),
        ],
        out_specs=pl.BlockSpec((tm, D), lambda b, i: (b * nt + i, 0)),
        out_shape=jax.ShapeDtypeStruct((T, D), f32),
        compiler_params=_params(("parallel", "parallel")),
        name="attn",
    )(q, kv, kv, w_o, res)


def _pack_pair(a, b):
    au = lax.bitcast_convert_type(a.astype(f32), jnp.uint32)
    bu = lax.bitcast_convert_type(b.astype(f32), jnp.uint32)
    return (au >> 16) | bu


def _unpack_pair(u):
    a = lax.bitcast_convert_type(u << 16, f32)
    b = lax.bitcast_convert_type(u & jnp.uint32(0xFFFF0000), f32)
    return a, b


def _router_kernel(x_ref, nw_ref, whi_ref, wlo_ref, b_ref, hn_ref, o_ref, cnt_ref, run_ref):
    @pl.when(pl.program_id(0) == 0)
    def _():
        run_ref[...] = jnp.zeros_like(run_ref)

    x = x_ref[...]
    xn = x * _rms_scale(x) * nw_ref[...]
    xhi = xn.astype(bf16)
    xlo = (xn - xhi.astype(f32)).astype(bf16)
    whi = whi_ref[...]
    logits = (jnp.dot(xhi, whi, preferred_element_type=f32)
              + jnp.dot(xlo, whi, preferred_element_type=f32)
              + jnp.dot(xhi, wlo_ref[...], preferred_element_type=f32)) + b_ref[...]
    lane = lax.broadcasted_iota(jnp.int32, logits.shape, 1).astype(f32)
    neg = -jnp.inf
    big = float(LANES)

    def first_max(vals):
        m = jnp.max(vals, axis=-1, keepdims=True)
        return m, jnp.min(jnp.where(vals == m, lane, big), axis=-1, keepdims=True)

    is_group = lane < N_GROUPS
    gmax, gsel = first_max(jnp.where(is_group, logits, neg))
    gsum = jnp.sum(jnp.where(is_group, jnp.exp(logits - gmax), 0.0), axis=-1, keepdims=True)
    g_w = 1.0 / gsum
    lo = N_GROUPS + EXP_PER_GROUP * gsel
    in_group = (lane >= lo) & (lane < lo + EXP_PER_GROUP)
    el = jnp.where(in_group, logits, neg)
    l1, i1 = first_max(el)
    l2, i2 = first_max(jnp.where(lane == i1, neg, el))
    esum = jnp.sum(jnp.where(in_group, jnp.exp(logits - l1), 0.0), axis=-1, keepdims=True)
    p1 = 1.0 / esum
    p2 = jnp.exp(l2 - l1) / esum
    gate1 = g_w * p1 / (p1 + p2)
    gate2 = g_w * p2 / (p1 + p2)

    tm = x.shape[0]
    chosen = jnp.where((lane == i1) | (lane == i2), 1.0, 0.0)
    earlier = (lax.broadcasted_iota(jnp.int32, (tm, tm), 0) > lax.broadcasted_iota(jnp.int32, (tm, tm), 1))
    before = _dot(jnp.where(earlier, 1.0, 0.0), chosen) + run_ref[...]
    rank1 = jnp.sum(jnp.where(lane == i1, before, 0.0), axis=-1, keepdims=True)
    rank2 = jnp.sum(jnp.where(lane == i2, before, 0.0), axis=-1, keepdims=True)
    run_ref[...] = run_ref[...] + jnp.sum(chosen, axis=0, keepdims=True)
    cnt_ref[...] = run_ref[...]

    cols = (i1 - N_GROUPS, i2 - N_GROUPS, gate1, gate2, rank1, rank2)
    out = jnp.zeros_like(logits)
    for c, val in enumerate(cols):
        out = jnp.where(lane == c, val, out)
    o_ref[...] = out
    half = x.shape[1] // 2
    hn_ref[...] = _pack_pair(xhi[:, :half], xhi[:, half:])


def _router(h2, norm_w, w_hi, w_lo, bias_row, tm):
    T, D = h2.shape
    return pl.pallas_call(
        _router_kernel,
        grid=(T // tm,),
        in_specs=[
            pl.BlockSpec((tm, D), lambda i: (i, 0)),
            pl.BlockSpec((1, D), lambda i: (0, 0)),
            pl.BlockSpec((D, LANES), lambda i: (0, 0)),
            pl.BlockSpec((D, LANES), lambda i: (0, 0)),
            pl.BlockSpec((1, LANES), lambda i: (0, 0)),
        ],
        out_specs=[
            pl.BlockSpec((tm, D // 2), lambda i: (i, 0)),
            pl.BlockSpec((tm, LANES), lambda i: (i, 0)),
            pl.BlockSpec((1, LANES), lambda i: (0, 0)),
        ],
        out_shape=[
            jax.ShapeDtypeStruct((T, D // 2), jnp.uint32),
            jax.ShapeDtypeStruct((T, LANES), f32),
            jax.ShapeDtypeStruct((1, LANES), f32),
        ],
        scratch_shapes=[pltpu.VMEM((1, LANES), f32)],
        compiler_params=_params(("arbitrary",)),
        name="router",
    )(h2, norm_w.reshape(1, D), w_hi, w_lo, bias_row)


def _dispatch_kernel(dest_ref, hn_hbm, xs_in, xs_hbm, sem, *, tb):
    del xs_in
    i = pl.program_id(0)
    n = pl.num_programs(0)
    slot = i % 2

    def body(r, carry):
        t = i * tb + r
        for kk in range(2):
            pltpu.make_async_copy(hn_hbm.at[pl.ds(t, 1)], xs_hbm.at[pl.ds(dest_ref[2 * t + kk], 1)],
                                  sem.at[slot]).start()
        return carry

    lax.fori_loop(0, tb, body, 0, unroll=8)

    def wait_step(slot_):
        rows = xs_hbm.at[pl.ds(0, 2 * tb)]
        pltpu.make_async_copy(rows, rows, sem.at[slot_]).wait()

    @pl.when(i > 0)
    def _():
        wait_step(1 - slot)

    @pl.when(i == n - 1)
    def _():
        wait_step(slot)


def _dispatch(hn, dest, P, tb):
    T, half = hn.shape
    grid_spec = pltpu.PrefetchScalarGridSpec(
        num_scalar_prefetch=1,
        grid=(T // tb,),
        in_specs=[pl.BlockSpec(memory_space=pl.ANY), pl.BlockSpec(memory_space=pl.ANY)],
        out_specs=pl.BlockSpec(memory_space=pl.ANY),
        scratch_shapes=[pltpu.SemaphoreType.DMA((2,))],
    )
    return pl.pallas_call(
        functools.partial(_dispatch_kernel, tb=tb),
        grid_spec=grid_spec,
        out_shape=jax.ShapeDtypeStruct((P, half), jnp.uint32),
        input_output_aliases={2: 0},
        compiler_params=_params(("arbitrary",)),
        name="dispatch",
    )(dest, hn, jnp.zeros((P, half), jnp.uint32))


def _ffn_kernel(blk_e_ref, nused_ref, x_ref, wg_ref, wu_ref, wd_ref, y_ref):
    i = pl.program_id(0)
    half = x_ref.shape[1]

    @pl.when(i < nused_ref[0])
    def _():
        a, b = _unpack_pair(x_ref[...])
        a = a.astype(bf16)
        b = b.astype(bf16)
        g = (jnp.dot(a, wg_ref[0:half, :], preferred_element_type=f32)
             + jnp.dot(b, wg_ref[half:, :], preferred_element_type=f32))
        u = (jnp.dot(a, wu_ref[0:half, :], preferred_element_type=f32)
             + jnp.dot(b, wu_ref[half:, :], preferred_element_type=f32))
        hmid = (_silu(g) * u).astype(bf16)
        y = jnp.dot(hmid, wd_ref[...], preferred_element_type=f32)
        y_ref[...] = _pack_pair(y[:, :half].astype(bf16), y[:, half:].astype(bf16))

    @pl.when(i >= nused_ref[0])
    def _():
        y_ref[...] = jnp.zeros_like(y_ref)


def _ffn(xs, wg, wu, wd, blk_e, nused, bm):
    P, half = xs.shape
    D = 2 * half
    F = wg.shape[2]
    NB = blk_e.shape[0]
    grid_spec = pltpu.PrefetchScalarGridSpec(
        num_scalar_prefetch=2,
        grid=(NB,),
        in_specs=[
            pl.BlockSpec((bm, half), lambda i, be, n: (i, 0)),
            pl.BlockSpec((None, D, F), lambda i, be, n: (be[i], 0, 0)),
            pl.BlockSpec((None, D, F), lambda i, be, n: (be[i], 0, 0)),
            pl.BlockSpec((None, F, D), lambda i, be, n: (be[i], 0, 0)),
        ],
        out_specs=pl.BlockSpec((bm, half), lambda i, be, n: (i, 0)),
    )
    return pl.pallas_call(
        _ffn_kernel,
        grid_spec=grid_spec,
        out_shape=jax.ShapeDtypeStruct((P, half), jnp.uint32),
        compiler_params=_params(("arbitrary",)),
        name="ffn",
    )(blk_e, nused, xs, wg, wu, wd)


def _combine_kernel(dest_ref, y_hbm, h_ref, rt_ref, nw_ref, o_ref, ybuf, sem):
    i = pl.program_id(0)
    n = pl.num_programs(0)
    tm = h_ref.shape[0]
    slot = i % 2

    def issue(blk, slot_):
        def body(r, carry):
            a = 2 * (blk * tm + r)
            for kk in range(2):
                pltpu.make_async_copy(y_hbm.at[pl.ds(dest_ref[a + kk], 1)],
                                      ybuf.at[slot_, kk, pl.ds(r, 1)], sem.at[slot_]).start()
            return carry
        lax.fori_loop(0, tm, body, 0)

    @pl.when(i == 0)
    def _():
        issue(0, 0)

    @pl.when(i + 1 < n)
    def _():
        issue(i + 1, 1 - slot)

    pltpu.make_async_copy(ybuf.at[slot], ybuf.at[slot], sem.at[slot]).wait()
    rt = rt_ref[...]
    half = ybuf.shape[3]
    a1, b1 = _unpack_pair(ybuf[slot, 0])
    a2, b2 = _unpack_pair(ybuf[slot, 1])
    g1 = rt[:, 2:3]
    g2 = rt[:, 3:4]
    h_lo = h_ref[:, :half] + g1 * a1 + g2 * a2
    h_hi = h_ref[:, half:] + g1 * b1 + g2 * b2
    ms = (jnp.sum(h_lo * h_lo, axis=-1, keepdims=True)
          + jnp.sum(h_hi * h_hi, axis=-1, keepdims=True)) / (2 * half)
    scale = lax.rsqrt(ms + EPS)
    o_ref[:, :half] = h_lo * scale * nw_ref[:, :half]
    o_ref[:, half:] = h_hi * scale * nw_ref[:, half:]


def _combine(y, h2, rout, dest, norm_w, tm):
    T, D = h2.shape
    grid_spec = pltpu.PrefetchScalarGridSpec(
        num_scalar_prefetch=1,
        grid=(T // tm,),
        in_specs=[
            pl.BlockSpec(memory_space=pl.ANY),
            pl.BlockSpec((tm, D), lambda i, d: (i, 0)),
            pl.BlockSpec((tm, LANES), lambda i, d: (i, 0)),
            pl.BlockSpec((1, D), lambda i, d: (0, 0)),
        ],
        out_specs=pl.BlockSpec((tm, D), lambda i, d: (i, 0)),
        scratch_shapes=[pltpu.VMEM((2, 2, tm, D // 2), jnp.uint32), pltpu.SemaphoreType.DMA((2,))],
    )
    return pl.pallas_call(
        _combine_kernel,
        grid_spec=grid_spec,
        out_shape=jax.ShapeDtypeStruct((T, D), f32),
        compiler_params=_params(("arbitrary",)),
        name="combine",
    )(dest, y, h2, rout, norm_w.reshape(1, D))


def _dispatch_plan(rout, cnt, bm):
    T = rout.shape[0]
    NB = 2 * T // bm + N_EXPERTS
    e_tk = rout[:, 0:2].astype(jnp.int32)
    rank_tk = rout[:, 4:6].astype(jnp.int32)
    counts = cnt[0, N_GROUPS:N_GROUPS + N_EXPERTS].astype(jnp.int32)
    padded = (counts + bm - 1) // bm * bm
    pend = jnp.cumsum(padded)
    pstart = pend - padded
    first_row = jnp.arange(NB, dtype=jnp.int32) * bm
    blk_e = jnp.minimum(jnp.sum((pend[None, :] <= first_row[:, None]).astype(jnp.int32), axis=1),
                        N_EXPERTS - 1)
    nused = (pend[-1] // bm).astype(jnp.int32).reshape(1)
    onehot = e_tk[:, :, None] == jnp.arange(N_EXPERTS, dtype=jnp.int32)
    dest = jnp.sum(jnp.where(onehot, pstart, 0), axis=-1) + rank_tk
    return blk_e, nused, dest.reshape(2 * T).astype(jnp.int32)


def _tile(n, want):
    t = min(n, want)
    while n % t:
        t //= 2
    return t


def kernel(x, mem, positions, norm_mix_w, w_in, dn_conv_w, dn_a_log, dn_dt_bias, ret_gn_w, dn_norm_w,
           w_out, norm_xq_w, norm_mem_w, w_xq, w_xkv, w_xo, norm_moe_w, w_group_router, b_group_router,
           w_expert_router, b_expert_router, w_gate, w_up, w_down, norm_final_w):
    B, S, D = x.shape
    M = mem.shape[1]
    T = B * S
    depth = w_in.shape[0]
    n_main = 8 * HEADS * HD
    h = x.reshape(T, D)
    pos_col = positions.astype(f32).reshape(T, 1)
    mem2 = mem.reshape(B * M, D)
    hg = HEADS
    bm = 256
    for l in range(depth):
        w_main = w_in[l][:, :n_main].astype(bf16)
        w_ba = jnp.pad(w_in[l][:, n_main:], ((0, 0), (0, LANES - 2 * HEADS))).astype(bf16)
        proj, ba = _in_proj(h, norm_mix_w[l], w_main, w_ba, _tile(T, 1024), 512)
        mix_r = _retention(proj, pos_col, ret_gn_w[l], B, S, hg)
        mix_d = _gdn(proj, ba, dn_conv_w[l], dn_a_log[l], dn_dt_bias[l], dn_norm_w[l], B, S, hg)
        h = _mm_res(mix_r, mix_d, w_out[l].astype(bf16), h, _tile(T, 1024), _tile(D, 1024))
        q = _norm_mm(h, norm_xq_w[l], w_xq[l].astype(bf16), _tile(T, 1024), _tile(D, 1024), "xq")
        kv = _norm_mm(mem2, norm_mem_w[l], w_xkv[l].astype(bf16), _tile(B * M, 512), _tile(2 * D, 1024), "xkv")
        h = _attn(q, kv, w_xo[l].astype(bf16), h, B, S, M, _tile(S, 256))
        w_r = jnp.pad(jnp.concatenate([w_group_router[l], w_expert_router[l]], axis=1),
                      ((0, 0), (0, LANES - N_GROUPS - N_EXPERTS)))
        w_r_hi = w_r.astype(bf16)
        w_r_lo = (w_r - w_r_hi.astype(f32)).astype(bf16)
        b_r = jnp.pad(jnp.concatenate([b_group_router[l], b_expert_router[l]]),
                      (0, LANES - N_GROUPS - N_EXPERTS)).reshape(1, LANES).astype(f32)
        hn, rout, cnt = _router(h, norm_moe_w[l], w_r_hi, w_r_lo, b_r, _tile(T, 512))
        blk_e, nused, dest = _dispatch_plan(rout, cnt, bm)
        xs = _dispatch(hn, dest, blk_e.shape[0] * bm, _tile(T, 512))
        y = _ffn(xs, w_gate[l].astype(bf16), w_up[l].astype(bf16), w_down[l].astype(bf16),
                 blk_e, nused, bm)
        if l + 1 < depth:
            raise NotImplementedError("the fused MoE combine applies the final norm: depth 1 only")
        h = _combine(y, h, rout, dest, norm_final_w, _tile(T, 256))
    return h.reshape(B, S, D)
```

```python
import functools

import numpy as np
import jax
import jax.numpy as jnp
from jax import lax
from jax.experimental import pallas as pl
from jax.experimental.pallas import tpu as pltpu

f32 = jnp.float32
bf16 = jnp.bfloat16

EPS = 1e-6
HEADS = 8
HD = 128
LANES = 128
XA_HEADS = 4
CONV_W = 4
ROPE_BASE = 10000.0
N_GROUPS = 4
EXP_PER_GROUP = 8
N_EXPERTS = N_GROUPS * EXP_PER_GROUP
RET_CHUNK = 256
DN_CHUNK = 128
VMEM_LIMIT = 56 * 1024 * 1024


def _params(sem, vmem=VMEM_LIMIT):
    return pltpu.CompilerParams(dimension_semantics=sem, vmem_limit_bytes=vmem)


def _dot(a, b):
    return jnp.dot(a.astype(bf16), b.astype(bf16), preferred_element_type=f32)


def _dot_nt(a, b):
    return lax.dot_general(a.astype(bf16), b.astype(bf16), (((1,), (1,)), ((), ())),
                           preferred_element_type=f32)


def _dot_tn(a, b):
    return lax.dot_general(a.astype(bf16), b.astype(bf16), (((0,), (0,)), ((), ())),
                           preferred_element_type=f32)


def _split3(a):
    hi = a.astype(bf16)
    r = a - hi.astype(f32)
    mid = r.astype(bf16)
    lo = (r - mid.astype(f32)).astype(bf16)
    return hi, mid, lo


def _sigmoid(x):
    return 1.0 / (1.0 + jnp.exp(-x))


def _silu(x):
    return x * _sigmoid(x)


def _rms_scale(x):
    return lax.rsqrt(jnp.mean(x * x, axis=-1, keepdims=True) + EPS)


def _in_proj_kernel(x_ref, nw_ref, w_ref, wba_ref, o_ref, ba_ref, xn_ref):
    @pl.when(pl.program_id(1) == 0)
    def _():
        x = x_ref[...]
        xn = (x * _rms_scale(x) * nw_ref[...]).astype(bf16)
        xn_ref[...] = xn
        ba_ref[...] = jnp.dot(xn, wba_ref[...], preferred_element_type=f32)

    acc = jnp.dot(xn_ref[...], w_ref[...], preferred_element_type=f32)
    for c in range(o_ref.shape[0]):
        o_ref[c] = acc[:, c * LANES:(c + 1) * LANES].astype(bf16)


def _in_proj(x2, norm_w, w_main, w_ba, tm, tn):
    T, D = x2.shape
    N = w_main.shape[1]
    return pl.pallas_call(
        _in_proj_kernel,
        grid=(T // tm, N // tn),
        in_specs=[
            pl.BlockSpec((tm, D), lambda i, j: (i, 0)),
            pl.BlockSpec((1, D), lambda i, j: (0, 0)),
            pl.BlockSpec((D, tn), lambda i, j: (0, j)),
            pl.BlockSpec((D, LANES), lambda i, j: (0, 0)),
        ],
        out_specs=[
            pl.BlockSpec((tn // LANES, tm, LANES), lambda i, j: (j, i, 0)),
            pl.BlockSpec((tm, LANES), lambda i, j: (i, 0)),
        ],
        out_shape=[
            jax.ShapeDtypeStruct((N // LANES, T, LANES), bf16),
            jax.ShapeDtypeStruct((T, LANES), f32),
        ],
        scratch_shapes=[pltpu.VMEM((tm, D), bf16)],
        compiler_params=_params(("parallel", "arbitrary")),
        name="in_proj",
    )(x2, norm_w.reshape(1, D), w_main, w_ba)


def _retention_kernel(q_ref, k_ref, v_ref, g_ref, pos_ref, inv_ref, sgn_ref, dmask_ref, qdec_ref,
                      kdec_ref, cdec_ref, gnw_ref, o_ref, state_ref):
    @pl.when(pl.program_id(2) == 0)
    def _():
        state_ref[...] = jnp.zeros_like(state_ref)

    ang = pos_ref[...] * inv_ref[...]
    cos2 = jnp.cos(ang)
    sin2 = jnp.sin(ang) * sgn_ref[...]
    heads = range(q_ref.shape[0])

    def rope(x):
        return x * cos2 + pltpu.roll(x, HD // 2, 1) * sin2

    qr = [rope(q_ref[i].astype(f32)) for i in heads]
    kr = [rope(k_ref[i].astype(f32)) * (HD ** -0.5) for i in heads]
    state = [state_ref[i] for i in heads]
    s = [_dot_nt(qr[i], kr[i]) * dmask_ref[i] for i in heads]
    cross = [_dot(qr[i], state[i]) * qdec_ref[i] for i in heads]
    upd = [_dot_tn(kr[i] * kdec_ref[i], v_ref[i]) for i in heads]
    o = [_dot(s[i], v_ref[i]) + cross[i] for i in heads]
    for i in heads:
        state_ref[i] = state[i] * cdec_ref[i] + upd[i]
        d = o[i] - jnp.mean(o[i], axis=-1, keepdims=True)
        y = d * lax.rsqrt(jnp.mean(d * d, axis=-1, keepdims=True) + EPS) * gnw_ref[i]
        y = y * _silu(g_ref[i].astype(f32))
        o_ref[:, i * HD:(i + 1) * HD] = y.astype(bf16)


def _retention_tables(C):
    h = np.arange(HEADS, dtype=np.float64)
    log_gamma = np.log1p(-np.exp2(-5.0 - h))
    idx = np.arange(C, dtype=np.float64)
    rel = idx[:, None] - idx[None, :]
    dmask = np.where(rel >= 0, np.exp(log_gamma[:, None, None] * np.where(rel >= 0, rel, 0.0)), 0.0)
    qdec = np.exp(log_gamma[:, None] * (idx + 1.0))
    kdec = np.exp(log_gamma[:, None] * (C - 1.0 - idx))
    cdec = np.exp(log_gamma * C)
    rep = lambda a: np.broadcast_to(a[..., None], a.shape + (LANES,))
    return (jnp.asarray(dmask, f32), jnp.asarray(rep(qdec), f32), jnp.asarray(rep(kdec), f32),
            jnp.asarray(np.broadcast_to(cdec[:, None, None], (HEADS, 1, LANES)), f32))


def _retention(proj, pos_col, gn_w, B, S, hg):
    C = RET_CHUNK
    NC = S // C
    T = B * S
    G = HEADS // hg
    half = HD // 2
    inv = ROPE_BASE ** (-np.arange(half, dtype=np.float32) / half)
    inv2 = jnp.asarray(np.concatenate([inv, inv]).reshape(1, HD), f32)
    sgn = jnp.asarray(np.concatenate([-np.ones(half), np.ones(half)]).reshape(1, HD), f32)
    dmask, qdec, kdec, cdec = _retention_tables(C)

    def slab(off):
        return pl.BlockSpec((hg, C, HD), lambda b, g, n: (off // hg + g, b * NC + n, 0))

    def table(shape):
        return pl.BlockSpec((hg,) + shape, lambda b, g, n: (g,) + (0,) * len(shape))

    return pl.pallas_call(
        _retention_kernel,
        grid=(B, G, NC),
        in_specs=[
            slab(0), slab(HEADS), slab(2 * HEADS), slab(3 * HEADS),
            pl.BlockSpec((C, 1), lambda b, g, n: (b * NC + n, 0)),
            pl.BlockSpec((1, HD), lambda b, g, n: (0, 0)),
            pl.BlockSpec((1, HD), lambda b, g, n: (0, 0)),
            table((C, C)), table((C, LANES)), table((C, LANES)), table((1, LANES)), table((1, HD)),
        ],
        out_specs=pl.BlockSpec((C, hg * HD), lambda b, g, n: (b * NC + n, g)),
        out_shape=jax.ShapeDtypeStruct((T, HEADS * HD), bf16),
        scratch_shapes=[pltpu.VMEM((hg, HD, HD), f32)],
        compiler_params=_params(("parallel", "parallel", "arbitrary")),
        name="retention",
    )(proj, proj, proj, proj, pos_col, inv2, sgn, dmask, qdec, kdec, cdec,
      gn_w.reshape(HEADS, 1, HD))


def _gdn_kernel(q_ref, k_ref, v_ref, z_ref, ba_ref, cw_ref, arow_ref, dtrow_ref, nw_ref, o_ref,
                state_ref, tail_ref, win_ref, *, hg):
    C = q_ref.shape[1]
    first = pl.program_id(2) == 0

    @pl.when(first)
    def _():
        state_ref[...] = jnp.zeros_like(state_ref)
        tail_ref[...] = jnp.zeros_like(tail_ref)

    row = lax.broadcasted_iota(jnp.int32, (C, C), 0)
    col = lax.broadcasted_iota(jnp.int32, (C, C), 1)
    causal = row >= col
    strict = row > col
    same8 = (row // 8) == (col // 8)
    level_masks = []
    s = 8
    while s < C:
        level_masks.append(((row // (2 * s)) == (col // (2 * s))) & ((row // s) != (col // s)))
        s *= 2
    eye = jnp.where(row == col, 1.0, 0.0).astype(f32)
    tri = jnp.where(causal, 1.0, 0.0).astype(bf16)

    ba = ba_ref[...]
    beta_all = _sigmoid(ba)
    xa = ba + dtrow_ref[...]
    softplus = jnp.maximum(xa, 0.0) + jnp.log(1.0 + jnp.exp(-jnp.abs(xa)))
    glog = -jnp.exp(arow_ref[...]) * softplus
    g_hi, g_mid, g_lo = _split3(glog)
    gcum = (jnp.dot(tri, g_hi, preferred_element_type=f32)
            + jnp.dot(tri, g_mid, preferred_element_type=f32)
            + jnp.dot(tri, g_lo, preferred_element_type=f32))
    gcum_t = gcum.T
    lane = lax.broadcasted_iota(jnp.int32, (C, LANES), 1)
    sub = lax.broadcasted_iota(jnp.int32, (LANES, C), 0)

    def conv_silu(x_ref, kind, i):
        slot = kind * hg + i
        win_ref[slot, 0:8, :] = tail_ref[slot]
        win_ref[slot, 8:8 + C, :] = x_ref[i].astype(f32)
        w = cw_ref[kind, i]
        y = win_ref[slot, 8:8 + C, :] * w[CONV_W - 1:CONV_W, :]
        for t in range(1, CONV_W):
            y = y + win_ref[slot, 8 - t:8 - t + C, :] * w[CONV_W - 1 - t:CONV_W - t, :]
        tail_ref[slot] = win_ref[slot, C:C + 8, :]
        return _silu(y)

    heads = range(hg)
    hd = [pl.program_id(1) * hg + i for i in heads]
    q = [conv_silu(q_ref, 0, i) for i in heads]
    k = [conv_silu(k_ref, 1, i) for i in heads]
    v = [conv_silu(v_ref, 2, i) for i in heads]
    q = [x * lax.rsqrt(jnp.sum(x * x, axis=-1, keepdims=True) + EPS) * (HD ** -0.5) for x in q]
    k = [x * lax.rsqrt(jnp.sum(x * x, axis=-1, keepdims=True) + EPS) for x in k]

    bcol = [jnp.sum(jnp.where(lane == h, beta_all, 0.0), axis=1, keepdims=True) for h in hd]
    gcol = [jnp.sum(jnp.where(lane == h + HEADS, gcum, 0.0), axis=1, keepdims=True) for h in hd]
    grow = [jnp.sum(jnp.where(sub == h + HEADS, gcum_t, 0.0), axis=0, keepdims=True) for h in hd]
    glast = [g[C - 1:C, :] for g in gcol]
    gam = [jnp.where(causal, jnp.exp(jnp.where(causal, gc - gr, 0.0)), 0.0) for gc, gr in zip(gcol, grow)]
    egc = [jnp.exp(g) for g in gcol]

    kk = [_dot_nt(x, x) for x in k]
    qk = [_dot_nt(x, y) for x, y in zip(q, k)]
    a = [jnp.where(strict, m * g * b, 0.0) for m, g, b in zip(kk, gam, bcol)]
    qk = [m * g for m, g in zip(qk, gam)]

    d = [jnp.where(same8, m, 0.0) for m in a]
    d2 = [_dot(m, m) for m in d]
    d3 = [_dot(m, m2) for m, m2 in zip(d, d2)]
    d4 = [_dot(m2, m2) for m2 in d2]
    x = [eye - m + m2 - m3 for m, m2, m3 in zip(d, d2, d3)]
    x = [xi + _dot(xi, m4) for xi, m4 in zip(x, d4)]
    for mask in level_masks:
        t = [_dot(xi, jnp.where(mask, m, 0.0)) for xi, m in zip(x, a)]
        x = [xi - _dot(ti, xi) for xi, ti in zip(x, t)]

    rhs = [jnp.concatenate([vi * b, ki * (b * e)], axis=1) for vi, ki, b, e in zip(v, k, bcol, egc)]
    sol = [_dot(xi, r) for xi, r in zip(x, rhs)]
    q_dec = [qi * e for qi, e in zip(q, egc)]
    k_dec = [ki * jnp.exp(gl - gc) for ki, gl, gc in zip(k, glast, gcol)]

    state = [state_ref[i] for i in heads]
    ws = [_dot(jnp.concatenate([s_[:, HD:], qd], axis=0), st) for s_, qd, st in zip(sol, q_dec, state)]
    v_new = [s_[:, :HD] - w_[:C] for s_, w_ in zip(sol, ws)]
    o = [w_[C:] + _dot(m, vn) for w_, m, vn in zip(ws, qk, v_new)]
    upd = [_dot_tn(kd, vn) for kd, vn in zip(k_dec, v_new)]
    for i in heads:
        state_ref[i] = state[i] * jnp.exp(glast[i]) + upd[i]
        y = o[i] * _rms_scale(o[i]) * nw_ref[...]
        y = y * _silu(z_ref[i].astype(f32))
        o_ref[:, i * HD:(i + 1) * HD] = y.astype(bf16)


def _gdn(proj, ba, conv_w, a_log, dt_bias, norm_w, B, S, hg):
    C = DN_CHUNK
    NC = S // C
    T = B * S
    G = HEADS // hg
    cw = conv_w.reshape(CONV_W, 3, HEADS, HD).transpose(1, 2, 0, 3)
    pad = jnp.zeros((LANES - 2 * HEADS,), f32)
    arow = jnp.concatenate([jnp.zeros((HEADS,), f32), a_log.astype(f32), pad]).reshape(1, LANES)
    dtrow = jnp.concatenate([jnp.zeros((HEADS,), f32), dt_bias.astype(f32), pad]).reshape(1, LANES)

    def slab(off):
        return pl.BlockSpec((hg, C, HD), lambda b, g, n: (off // hg + g, b * NC + n, 0))

    row_spec = pl.BlockSpec((1, LANES), lambda b, g, n: (0, 0))
    return pl.pallas_call(
        functools.partial(_gdn_kernel, hg=hg),
        grid=(B, G, NC),
        in_specs=[
            slab(4 * HEADS), slab(5 * HEADS), slab(6 * HEADS), slab(7 * HEADS),
            pl.BlockSpec((C, LANES), lambda b, g, n: (b * NC + n, 0)),
            pl.BlockSpec((3, hg, CONV_W, HD), lambda b, g, n: (0, g, 0, 0)),
            row_spec, row_spec, row_spec,
        ],
        out_specs=pl.BlockSpec((C, hg * HD), lambda b, g, n: (b * NC + n, g)),
        out_shape=jax.ShapeDtypeStruct((T, HEADS * HD), bf16),
        scratch_shapes=[
            pltpu.VMEM((hg, HD, HD), f32),
            pltpu.VMEM((3 * hg, 8, HD), f32),
            pltpu.VMEM((3 * hg, C + 8, HD), f32),
        ],
        compiler_params=_params(("parallel", "parallel", "arbitrary")),
        name="gdn",
    )(proj, proj, proj, proj, ba, cw, arow, dtrow, norm_w.reshape(1, HD))


def _mm_res_kernel(a1_ref, a2_ref, w_ref, r_ref, o_ref):
    k1 = a1_ref.shape[1]
    acc = jnp.dot(a1_ref[...], w_ref[0:k1, :], preferred_element_type=f32)
    acc = acc + jnp.dot(a2_ref[...], w_ref[k1:, :], preferred_element_type=f32)
    o_ref[...] = r_ref[...] + acc


def _mm_res(a1, a2, w, res, tm, tn):
    T, K1 = a1.shape
    K2 = a2.shape[1]
    N = w.shape[1]
    return pl.pallas_call(
        _mm_res_kernel,
        grid=(T // tm, N // tn),
        in_specs=[
            pl.BlockSpec((tm, K1), lambda i, j: (i, 0)),
            pl.BlockSpec((tm, K2), lambda i, j: (i, 0)),
            pl.BlockSpec((K1 + K2, tn), lambda i, j: (0, j)),
            pl.BlockSpec((tm, tn), lambda i, j: (i, j)),
        ],
        out_specs=pl.BlockSpec((tm, tn), lambda i, j: (i, j)),
        out_shape=jax.ShapeDtypeStruct((T, N), f32),
        compiler_params=_params(("parallel", "parallel")),
        name="out_proj",
    )(a1, a2, w, res)


def _norm_mm_kernel(x_ref, nw_ref, w_ref, o_ref, xn_ref):
    @pl.when(pl.program_id(1) == 0)
    def _():
        x = x_ref[...]
        xn_ref[...] = (x * _rms_scale(x) * nw_ref[...]).astype(bf16)

    o_ref[...] = jnp.dot(xn_ref[...], w_ref[...], preferred_element_type=f32).astype(o_ref.dtype)


def _norm_mm(x2, norm_w, w, tm, tn, name):
    T, D = x2.shape
    N = w.shape[1]
    return pl.pallas_call(
        _norm_mm_kernel,
        grid=(T // tm, N // tn),
        in_specs=[
            pl.BlockSpec((tm, D), lambda i, j: (i, 0)),
            pl.BlockSpec((1, D), lambda i, j: (0, 0)),
            pl.BlockSpec((D, tn), lambda i, j: (0, j)),
        ],
        out_specs=pl.BlockSpec((tm, tn), lambda i, j: (i, j)),
        out_shape=jax.ShapeDtypeStruct((T, N), bf16),
        scratch_shapes=[pltpu.VMEM((tm, D), bf16)],
        compiler_params=_params(("parallel", "arbitrary")),
        name=name,
    )(x2, norm_w.reshape(1, D), w)


def _attn_kernel(q_ref, k_ref, v_ref, wo_ref, r_ref, o_ref):
    D = q_ref.shape[1]
    dh = D // XA_HEADS
    acc = r_ref[...]
    for h in range(XA_HEADS):
        sl = slice(h * dh, (h + 1) * dh)
        s = _dot_nt(q_ref[:, sl], k_ref[:, sl]) * (dh ** -0.5)
        p = jnp.exp(s - jnp.max(s, axis=-1, keepdims=True))
        p = p / jnp.sum(p, axis=-1, keepdims=True)
        oh = _dot(p, v_ref[:, sl])
        acc = acc + _dot(oh, wo_ref[sl, :])
    o_ref[...] = acc


def _attn(q, kv, w_o, res, B, S, M, tm):
    T, D = q.shape
    nt = S // tm
    return pl.pallas_call(
        _attn_kernel,
        grid=(B, nt),
        in_specs=[
            pl.BlockSpec((tm, D), lambda b, i: (b * nt + i, 0)),
            pl.BlockSpec((M, D), lambda b, i: (b, 0)),
            pl.BlockSpec((M, D), lambda b, i: (b, 1)),
            pl.BlockSpec((D, D), lambda b, i: (0, 0)),
            pl.BlockSpec((tm, D), lambda b, i: (b * nt + i, 0)),
        ],
        out_specs=pl.BlockSpec((tm, D), lambda b, i: (b * nt + i, 0)),
        out_shape=jax.ShapeDtypeStruct((T, D), f32),
        compiler_params=_params(("parallel", "parallel")),
        name="attn",
    )(q, kv, kv, w_o, res)


def _pack_pair(a, b):
    au = lax.bitcast_convert_type(a.astype(f32), jnp.uint32)
    bu = lax.bitcast_convert_type(b.astype(f32), jnp.uint32)
    return (au >> 16) | bu


def _unpack_pair(u):
    a = lax.bitcast_convert_type(u << 16, f32)
    b = lax.bitcast_convert_type(u & jnp.uint32(0xFFFF0000), f32)
    return a, b


def _store_slabs(ref, words):
    n = words.shape[0]
    g = words.shape[1] // LANES
    for c in range(g):
        ref[pl.ds(c, n, stride=g), :] = words[:, c * LANES:(c + 1) * LANES]


def _load_slabs(ref, n, g):
    return [ref[pl.ds(c, n, stride=g), :] for c in range(g)]


def _router_kernel(x_ref, nw_ref, whi_ref, wlo_ref, b_ref, hn_ref, o_ref, cnt_ref, run_ref):
    @pl.when(pl.program_id(0) == 0)
    def _():
        run_ref[...] = jnp.zeros_like(run_ref)

    x = x_ref[...]
    xn = x * _rms_scale(x) * nw_ref[...]
    xhi = xn.astype(bf16)
    xlo = (xn - xhi.astype(f32)).astype(bf16)
    whi = whi_ref[...]
    logits = (jnp.dot(xhi, whi, preferred_element_type=f32)
              + jnp.dot(xlo, whi, preferred_element_type=f32)
              + jnp.dot(xhi, wlo_ref[...], preferred_element_type=f32)) + b_ref[...]
    lane = lax.broadcasted_iota(jnp.int32, logits.shape, 1).astype(f32)
    neg = -jnp.inf
    big = float(LANES)

    def first_max(vals):
        m = jnp.max(vals, axis=-1, keepdims=True)
        return m, jnp.min(jnp.where(vals == m, lane, big), axis=-1, keepdims=True)

    is_group = lane < N_GROUPS
    gmax, gsel = first_max(jnp.where(is_group, logits, neg))
    gsum = jnp.sum(jnp.where(is_group, jnp.exp(logits - gmax), 0.0), axis=-1, keepdims=True)
    g_w = 1.0 / gsum
    lo = N_GROUPS + EXP_PER_GROUP * gsel
    in_group = (lane >= lo) & (lane < lo + EXP_PER_GROUP)
    el = jnp.where(in_group, logits, neg)
    l1, i1 = first_max(el)
    l2, i2 = first_max(jnp.where(lane == i1, neg, el))
    esum = jnp.sum(jnp.where(in_group, jnp.exp(logits - l1), 0.0), axis=-1, keepdims=True)
    p1 = 1.0 / esum
    p2 = jnp.exp(l2 - l1) / esum
    gate1 = g_w * p1 / (p1 + p2)
    gate2 = g_w * p2 / (p1 + p2)

    tm = x.shape[0]
    chosen = jnp.where((lane == i1) | (lane == i2), 1.0, 0.0)
    earlier = (lax.broadcasted_iota(jnp.int32, (tm, tm), 0) > lax.broadcasted_iota(jnp.int32, (tm, tm), 1))
    before = _dot(jnp.where(earlier, 1.0, 0.0), chosen) + run_ref[...]
    rank1 = jnp.sum(jnp.where(lane == i1, before, 0.0), axis=-1, keepdims=True)
    rank2 = jnp.sum(jnp.where(lane == i2, before, 0.0), axis=-1, keepdims=True)
    run_ref[...] = run_ref[...] + jnp.sum(chosen, axis=0, keepdims=True)
    cnt_ref[...] = run_ref[...]

    cols = (i1 - N_GROUPS, i2 - N_GROUPS, gate1, gate2, rank1, rank2)
    out = jnp.zeros_like(logits)
    for c, val in enumerate(cols):
        out = jnp.where(lane == c, val, out)
    o_ref[...] = out
    half = x.shape[1] // 2
    _store_slabs(hn_ref, _pack_pair(xhi[:, :half], xhi[:, half:]))


def _router(h2, norm_w, w_hi, w_lo, bias_row, tm):
    T, D = h2.shape
    g = D // (2 * LANES)
    return pl.pallas_call(
        _router_kernel,
        grid=(T // tm,),
        in_specs=[
            pl.BlockSpec((tm, D), lambda i: (i, 0)),
            pl.BlockSpec((1, D), lambda i: (0, 0)),
            pl.BlockSpec((D, LANES), lambda i: (0, 0)),
            pl.BlockSpec((D, LANES), lambda i: (0, 0)),
            pl.BlockSpec((1, LANES), lambda i: (0, 0)),
        ],
        out_specs=[
            pl.BlockSpec((tm * g, LANES), lambda i: (i, 0)),
            pl.BlockSpec((tm, LANES), lambda i: (i, 0)),
            pl.BlockSpec((1, LANES), lambda i: (0, 0)),
        ],
        out_shape=[
            jax.ShapeDtypeStruct((T * g, LANES), jnp.uint32),
            jax.ShapeDtypeStruct((T, LANES), f32),
            jax.ShapeDtypeStruct((1, LANES), f32),
        ],
        scratch_shapes=[pltpu.VMEM((1, LANES), f32)],
        compiler_params=_params(("arbitrary",)),
        name="router",
    )(h2, norm_w.reshape(1, D), w_hi, w_lo, bias_row)


def _dispatch_kernel(dest_ref, hn_ref, xs_in, xs_hbm, sem, *, tb, g):
    del xs_in
    base = pl.program_id(0) * tb

    def body(r, carry):
        src = hn_ref.at[pl.ds(pl.multiple_of(r * g, g), g)]
        for kk in range(2):
            d = dest_ref[2 * (base + r) + kk]
            pltpu.make_async_copy(src, xs_hbm.at[pl.ds(pl.multiple_of(d * g, g), g)], sem.at[kk]).start()
        return carry

    lax.fori_loop(0, tb, body, 0, unroll=8)
    for kk in range(2):
        pltpu.make_async_copy(hn_ref, hn_ref, sem.at[kk]).wait()


def _dispatch(hn, dest, P, tb, g):
    T = hn.shape[0] // g
    grid_spec = pltpu.PrefetchScalarGridSpec(
        num_scalar_prefetch=1,
        grid=(T // tb,),
        in_specs=[pl.BlockSpec((tb * g, LANES), lambda i, d: (i, 0)), pl.BlockSpec(memory_space=pl.ANY)],
        out_specs=pl.BlockSpec(memory_space=pl.ANY),
        scratch_shapes=[pltpu.SemaphoreType.DMA((2,))],
    )
    return pl.pallas_call(
        functools.partial(_dispatch_kernel, tb=tb, g=g),
        grid_spec=grid_spec,
        out_shape=jax.ShapeDtypeStruct((P * g, LANES), jnp.uint32),
        input_output_aliases={2: 0},
        compiler_params=_params(("arbitrary",)),
        name="dispatch",
    )(dest, hn, jnp.zeros((P * g, LANES), jnp.uint32))


def _ffn_kernel(blk_e_ref, nused_ref, x_ref, wg_ref, wu_ref, wd_ref, y_ref, *, bm):
    i = pl.program_id(0)
    half = wg_ref.shape[0] // 2

    @pl.when(i < nused_ref[0])
    def _():
        pairs = [_unpack_pair(w) for w in _load_slabs(x_ref, bm, half // LANES)]
        a = jnp.concatenate([p[0].astype(bf16) for p in pairs], axis=1)
        b = jnp.concatenate([p[1].astype(bf16) for p in pairs], axis=1)
        g = (jnp.dot(a, wg_ref[0:half, :], preferred_element_type=f32)
             + jnp.dot(b, wg_ref[half:, :], preferred_element_type=f32))
        u = (jnp.dot(a, wu_ref[0:half, :], preferred_element_type=f32)
             + jnp.dot(b, wu_ref[half:, :], preferred_element_type=f32))
        hmid = (_silu(g) * u).astype(bf16)
        y = jnp.dot(hmid, wd_ref[...], preferred_element_type=f32)
        _store_slabs(y_ref, _pack_pair(y[:, :half].astype(bf16), y[:, half:].astype(bf16)))

    @pl.when(i >= nused_ref[0])
    def _():
        y_ref[...] = jnp.zeros_like(y_ref)


def _ffn(xs, wg, wu, wd, blk_e, nused, bm):
    D, F = wg.shape[1:]
    g = D // (2 * LANES)
    NB = blk_e.shape[0]
    grid_spec = pltpu.PrefetchScalarGridSpec(
        num_scalar_prefetch=2,
        grid=(NB,),
        in_specs=[
            pl.BlockSpec((bm * g, LANES), lambda i, be, n: (i, 0)),
            pl.BlockSpec((None, D, F), lambda i, be, n: (be[i], 0, 0)),
            pl.BlockSpec((None, D, F), lambda i, be, n: (be[i], 0, 0)),
            pl.BlockSpec((None, F, D), lambda i, be, n: (be[i], 0, 0)),
        ],
        out_specs=pl.BlockSpec((bm * g, LANES), lambda i, be, n: (i, 0)),
    )
    return pl.pallas_call(
        functools.partial(_ffn_kernel, bm=bm),
        grid_spec=grid_spec,
        out_shape=jax.ShapeDtypeStruct(xs.shape, jnp.uint32),
        compiler_params=_params(("arbitrary",)),
        name="ffn",
    )(blk_e, nused, xs, wg, wu, wd)


def _combine_kernel(dest_ref, y_hbm, h_ref, rt_ref, nw_ref, o_ref, ybuf, sem):
    i = pl.program_id(0)
    n = pl.num_programs(0)
    tm = h_ref.shape[0]
    slot = i % 2

    D = h_ref.shape[1]
    half = D // 2
    g = half // LANES

    def issue(blk, slot_):
        def body(r, carry):
            a = 2 * (blk * tm + r)
            for kk in range(2):
                d = dest_ref[a + kk]
                pltpu.make_async_copy(y_hbm.at[pl.ds(pl.multiple_of(d * g, g), g)],
                                      ybuf.at[slot_, kk, pl.ds(pl.multiple_of(r * g, g), g)],
                                      sem.at[slot_]).start()
            return carry
        lax.fori_loop(0, tm, body, 0, unroll=8)

    @pl.when(i == 0)
    def _():
        issue(0, 0)

    @pl.when(i + 1 < n)
    def _():
        issue(i + 1, 1 - slot)

    pltpu.make_async_copy(ybuf.at[slot], ybuf.at[slot], sem.at[slot]).wait()
    rt = rt_ref[...]
    g1 = rt[:, 2:3]
    g2 = rt[:, 3:4]
    y1 = [_unpack_pair(w) for w in _load_slabs(ybuf.at[slot, 0], tm, g)]
    y2 = [_unpack_pair(w) for w in _load_slabs(ybuf.at[slot, 1], tm, g)]
    cols = [slice(c * LANES, (c + 1) * LANES) for c in range(2 * g)]
    hs = [h_ref[:, cols[c]] + g1 * y1[c % g][c // g] + g2 * y2[c % g][c // g] for c in range(2 * g)]
    ms = sum(jnp.sum(x * x, axis=-1, keepdims=True) for x in hs) / D
    scale = lax.rsqrt(ms + EPS)
    for c in range(2 * g):
        o_ref[:, cols[c]] = hs[c] * scale * nw_ref[:, cols[c]]


def _combine(y, h2, rout, dest, norm_w, tm):
    T, D = h2.shape
    g = D // (2 * LANES)
    grid_spec = pltpu.PrefetchScalarGridSpec(
        num_scalar_prefetch=1,
        grid=(T // tm,),
        in_specs=[
            pl.BlockSpec(memory_space=pl.ANY),
            pl.BlockSpec((tm, D), lambda i, d: (i, 0)),
            pl.BlockSpec((tm, LANES), lambda i, d: (i, 0)),
            pl.BlockSpec((1, D), lambda i, d: (0, 0)),
        ],
        out_specs=pl.BlockSpec((tm, D), lambda i, d: (i, 0)),
        scratch_shapes=[pltpu.VMEM((2, 2, tm * g, LANES), jnp.uint32), pltpu.SemaphoreType.DMA((2,))],
    )
    return pl.pallas_call(
        _combine_kernel,
        grid_spec=grid_spec,
        out_shape=jax.ShapeDtypeStruct((T, D), f32),
        compiler_params=_params(("arbitrary",)),
        name="combine",
    )(dest, y, h2, rout, norm_w.reshape(1, D))


def _dispatch_plan(rout, cnt, bm):
    T = rout.shape[0]
    NB = 2 * T // bm + N_EXPERTS
    e_tk = rout[:, 0:2].astype(jnp.int32)
    rank_tk = rout[:, 4:6].astype(jnp.int32)
    counts = cnt[0, N_GROUPS:N_GROUPS + N_EXPERTS].astype(jnp.int32)
    padded = (counts + bm - 1) // bm * bm
    pend = jnp.cumsum(padded)
    pstart = pend - padded
    first_row = jnp.arange(NB, dtype=jnp.int32) * bm
    blk_e = jnp.minimum(jnp.sum((pend[None, :] <= first_row[:, None]).astype(jnp.int32), axis=1),
                        N_EXPERTS - 1)
    nused = (pend[-1] // bm).astype(jnp.int32).reshape(1)
    onehot = e_tk[:, :, None] == jnp.arange(N_EXPERTS, dtype=jnp.int32)
    dest = jnp.sum(jnp.where(onehot, pstart, 0), axis=-1) + rank_tk
    return blk_e, nused, dest.reshape(2 * T).astype(jnp.int32)


def _tile(n, want):
    t = min(n, want)
    while n % t:
        t //= 2
    return t


def kernel(x, mem, positions, norm_mix_w, w_in, dn_conv_w, dn_a_log, dn_dt_bias, ret_gn_w, dn_norm_w,
           w_out, norm_xq_w, norm_mem_w, w_xq, w_xkv, w_xo, norm_moe_w, w_group_router, b_group_router,
           w_expert_router, b_expert_router, w_gate, w_up, w_down, norm_final_w):
    B, S, D = x.shape
    M = mem.shape[1]
    T = B * S
    depth = w_in.shape[0]
    n_main = 8 * HEADS * HD
    h = x.reshape(T, D)
    pos_col = positions.astype(f32).reshape(T, 1)
    mem2 = mem.reshape(B * M, D)
    hg = HEADS
    bm = 256
    for l in range(depth):
        w_main = w_in[l][:, :n_main].astype(bf16)
        w_ba = jnp.pad(w_in[l][:, n_main:], ((0, 0), (0, LANES - 2 * HEADS))).astype(bf16)
        proj, ba = _in_proj(h, norm_mix_w[l], w_main, w_ba, _tile(T, 1024), 512)
        mix_r = _retention(proj, pos_col, ret_gn_w[l], B, S, hg)
        mix_d = _gdn(proj, ba, dn_conv_w[l], dn_a_log[l], dn_dt_bias[l], dn_norm_w[l], B, S, hg)
        h = _mm_res(mix_r, mix_d, w_out[l].astype(bf16), h, _tile(T, 1024), _tile(D, 1024))
        q = _norm_mm(h, norm_xq_w[l], w_xq[l].astype(bf16), _tile(T, 1024), _tile(D, 1024), "xq")
        kv = _norm_mm(mem2, norm_mem_w[l], w_xkv[l].astype(bf16), _tile(B * M, 512), _tile(2 * D, 1024), "xkv")
        h = _attn(q, kv, w_xo[l].astype(bf16), h, B, S, M, _tile(S, 256))
        w_r = jnp.pad(jnp.concatenate([w_group_router[l], w_expert_router[l]], axis=1),
                      ((0, 0), (0, LANES - N_GROUPS - N_EXPERTS)))
        w_r_hi = w_r.astype(bf16)
        w_r_lo = (w_r - w_r_hi.astype(f32)).astype(bf16)
        b_r = jnp.pad(jnp.concatenate([b_group_router[l], b_expert_router[l]]),
                      (0, LANES - N_GROUPS - N_EXPERTS)).reshape(1, LANES).astype(f32)
        hn, rout, cnt = _router(h, norm_moe_w[l], w_r_hi, w_r_lo, b_r, _tile(T, 512))
        blk_e, nused, dest = _dispatch_plan(rout, cnt, bm)
        xs = _dispatch(hn, dest, blk_e.shape[0] * bm, _tile(T, 1024), D // (2 * LANES))
        y = _ffn(xs, w_gate[l].astype(bf16), w_up[l].astype(bf16), w_down[l].astype(bf16),
                 blk_e, nused, bm)
        if l + 1 < depth:
            raise NotImplementedError("the fused MoE combine applies the final norm: depth 1 only")
        h = _combine(y, h, rout, dest, norm_final_w, _tile(T, 256))
    return h.reshape(B, S, D)
```

```python
import functools

import numpy as np
import jax
import jax.numpy as jnp
from jax import lax
from jax.experimental import pallas as pl
from jax.experimental.pallas import tpu as pltpu

f32 = jnp.float32
bf16 = jnp.bfloat16

EPS = 1e-6
HEADS = 8
HD = 128
LANES = 128
XA_HEADS = 4
CONV_W = 4
ROPE_BASE = 10000.0
N_GROUPS = 4
EXP_PER_GROUP = 8
N_EXPERTS = N_GROUPS * EXP_PER_GROUP
RET_CHUNK = 256
DN_CHUNK = 128
VMEM_LIMIT = 56 * 1024 * 1024


def _params(sem, vmem=VMEM_LIMIT):
    return pltpu.CompilerParams(dimension_semantics=sem, vmem_limit_bytes=vmem)


def _dot(a, b):
    return jnp.dot(a.astype(bf16), b.astype(bf16), preferred_element_type=f32)


def _dot_nt(a, b):
    return lax.dot_general(a.astype(bf16), b.astype(bf16), (((1,), (1,)), ((), ())),
                           preferred_element_type=f32)


def _dot_tn(a, b):
    return lax.dot_general(a.astype(bf16), b.astype(bf16), (((0,), (0,)), ((), ())),
                           preferred_element_type=f32)


def _split3(a):
    hi = a.astype(bf16)
    r = a - hi.astype(f32)
    mid = r.astype(bf16)
    lo = (r - mid.astype(f32)).astype(bf16)
    return hi, mid, lo


def _sigmoid(x):
    return 1.0 / (1.0 + jnp.exp(-x))


def _silu(x):
    return x * _sigmoid(x)


def _rms_scale(x):
    return lax.rsqrt(jnp.mean(x * x, axis=-1, keepdims=True) + EPS)


def _in_proj_kernel(x_ref, nw_ref, w_ref, wba_ref, o_ref, ba_ref, xn_ref):
    @pl.when(pl.program_id(1) == 0)
    def _():
        x = x_ref[...]
        xn = (x * _rms_scale(x) * nw_ref[...]).astype(bf16)
        xn_ref[...] = xn
        ba_ref[...] = jnp.dot(xn, wba_ref[...], preferred_element_type=f32)

    acc = jnp.dot(xn_ref[...], w_ref[...], preferred_element_type=f32)
    for c in range(o_ref.shape[0]):
        o_ref[c] = acc[:, c * LANES:(c + 1) * LANES].astype(bf16)


def _in_proj(x2, norm_w, w_main, w_ba, tm, tn):
    T, D = x2.shape
    N = w_main.shape[1]
    return pl.pallas_call(
        _in_proj_kernel,
        grid=(T // tm, N // tn),
        in_specs=[
            pl.BlockSpec((tm, D), lambda i, j: (i, 0)),
            pl.BlockSpec((1, D), lambda i, j: (0, 0)),
            pl.BlockSpec((D, tn), lambda i, j: (0, j)),
            pl.BlockSpec((D, LANES), lambda i, j: (0, 0)),
        ],
        out_specs=[
            pl.BlockSpec((tn // LANES, tm, LANES), lambda i, j: (j, i, 0)),
            pl.BlockSpec((tm, LANES), lambda i, j: (i, 0)),
        ],
        out_shape=[
            jax.ShapeDtypeStruct((N // LANES, T, LANES), bf16),
            jax.ShapeDtypeStruct((T, LANES), f32),
        ],
        scratch_shapes=[pltpu.VMEM((tm, D), bf16)],
        compiler_params=_params(("parallel", "arbitrary")),
        name="in_proj",
    )(x2, norm_w.reshape(1, D), w_main, w_ba)


def _retention_kernel(q_ref, k_ref, v_ref, g_ref, pos_ref, inv_ref, sgn_ref, dmask_ref, qdec_ref,
                      kdec_ref, cdec_ref, gnw_ref, o_ref, state_ref):
    @pl.when(pl.program_id(2) == 0)
    def _():
        state_ref[...] = jnp.zeros_like(state_ref)

    C = pos_ref.shape[0]
    lane = lax.broadcasted_iota(jnp.int32, (C // 2, HD), 1)
    low = lane < HD // 2
    ang = jnp.where(low, pos_ref[0:C // 2, :], pos_ref[C // 2:, :]) * inv_ref[...]
    cos_p = jnp.cos(ang)
    sin_p = jnp.sin(ang)

    def spread(t):
        swapped = pltpu.roll(t, HD // 2, 1)
        return jnp.concatenate([jnp.where(low, t, swapped), jnp.where(low, swapped, t)], axis=0)

    cos2 = spread(cos_p)
    sin2 = spread(sin_p) * sgn_ref[...]
    heads = range(q_ref.shape[0])

    def rope(x):
        return x * cos2 + pltpu.roll(x, HD // 2, 1) * sin2

    qr = [rope(q_ref[i].astype(f32)) for i in heads]
    kr = [rope(k_ref[i].astype(f32)) * (HD ** -0.5) for i in heads]
    state = [state_ref[i] for i in heads]
    s = [_dot_nt(qr[i], kr[i]) * dmask_ref[i] for i in heads]
    cross = [_dot(qr[i], state[i]) * qdec_ref[i] for i in heads]
    upd = [_dot_tn(kr[i] * kdec_ref[i], v_ref[i]) for i in heads]
    o = [_dot(s[i], v_ref[i]) + cross[i] for i in heads]
    for i in heads:
        state_ref[i] = state[i] * cdec_ref[i] + upd[i]
        d = o[i] - jnp.mean(o[i], axis=-1, keepdims=True)
        y = d * lax.rsqrt(jnp.mean(d * d, axis=-1, keepdims=True) + EPS) * gnw_ref[i]
        y = y * _silu(g_ref[i].astype(f32))
        o_ref[:, i * HD:(i + 1) * HD] = y.astype(bf16)


def _retention_tables(C):
    h = np.arange(HEADS, dtype=np.float64)
    log_gamma = np.log1p(-np.exp2(-5.0 - h))
    idx = np.arange(C, dtype=np.float64)
    rel = idx[:, None] - idx[None, :]
    dmask = np.where(rel >= 0, np.exp(log_gamma[:, None, None] * np.where(rel >= 0, rel, 0.0)), 0.0)
    qdec = np.exp(log_gamma[:, None] * (idx + 1.0))
    kdec = np.exp(log_gamma[:, None] * (C - 1.0 - idx))
    cdec = np.exp(log_gamma * C)
    rep = lambda a: np.broadcast_to(a[..., None], a.shape + (LANES,))
    return (jnp.asarray(dmask, f32), jnp.asarray(rep(qdec), f32), jnp.asarray(rep(kdec), f32),
            jnp.asarray(np.broadcast_to(cdec[:, None, None], (HEADS, 1, LANES)), f32))


def _retention(proj, pos_col, gn_w, B, S, hg):
    C = RET_CHUNK
    NC = S // C
    T = B * S
    G = HEADS // hg
    half = HD // 2
    inv = ROPE_BASE ** (-np.arange(half, dtype=np.float32) / half)
    inv2 = jnp.asarray(np.concatenate([inv, inv]).reshape(1, HD), f32)
    sgn = jnp.asarray(np.concatenate([-np.ones(half), np.ones(half)]).reshape(1, HD), f32)
    dmask, qdec, kdec, cdec = _retention_tables(C)

    def slab(off):
        return pl.BlockSpec((hg, C, HD), lambda b, g, n: (off // hg + g, b * NC + n, 0))

    def table(shape):
        return pl.BlockSpec((hg,) + shape, lambda b, g, n: (g,) + (0,) * len(shape))

    return pl.pallas_call(
        _retention_kernel,
        grid=(B, G, NC),
        in_specs=[
            slab(0), slab(HEADS), slab(2 * HEADS), slab(3 * HEADS),
            pl.BlockSpec((C, 1), lambda b, g, n: (b * NC + n, 0)),
            pl.BlockSpec((1, HD), lambda b, g, n: (0, 0)),
            pl.BlockSpec((1, HD), lambda b, g, n: (0, 0)),
            table((C, C)), table((C, LANES)), table((C, LANES)), table((1, LANES)), table((1, HD)),
        ],
        out_specs=pl.BlockSpec((C, hg * HD), lambda b, g, n: (b * NC + n, g)),
        out_shape=jax.ShapeDtypeStruct((T, HEADS * HD), bf16),
        scratch_shapes=[pltpu.VMEM((hg, HD, HD), f32)],
        compiler_params=_params(("parallel", "parallel", "arbitrary")),
        name="retention",
    )(proj, proj, proj, proj, pos_col, inv2, sgn, dmask, qdec, kdec, cdec,
      gn_w.reshape(HEADS, 1, HD))


def _gdn_kernel(q_ref, k_ref, v_ref, z_ref, ba_ref, cw_ref, arow_ref, dtrow_ref, nw_ref, o_ref,
                state_ref, tail_ref, win_ref, *, hg):
    C = q_ref.shape[1]
    first = pl.program_id(2) == 0

    @pl.when(first)
    def _():
        state_ref[...] = jnp.zeros_like(state_ref)
        tail_ref[...] = jnp.zeros_like(tail_ref)

    row = lax.broadcasted_iota(jnp.int32, (C, C), 0)
    col = lax.broadcasted_iota(jnp.int32, (C, C), 1)
    causal = row >= col
    strict = row > col
    same8 = (row // 8) == (col // 8)
    level_masks = []
    s = 8
    while s < C:
        level_masks.append(((row // (2 * s)) == (col // (2 * s))) & ((row // s) != (col // s)))
        s *= 2
    eye = jnp.where(row == col, 1.0, 0.0).astype(f32)
    tri = jnp.where(causal, 1.0, 0.0).astype(bf16)

    ba = ba_ref[...]
    beta_all = _sigmoid(ba)
    xa = ba + dtrow_ref[...]
    softplus = jnp.maximum(xa, 0.0) + jnp.log(1.0 + jnp.exp(-jnp.abs(xa)))
    glog = -jnp.exp(arow_ref[...]) * softplus
    g_hi, g_mid, g_lo = _split3(glog)
    gcum = (jnp.dot(tri, g_hi, preferred_element_type=f32)
            + jnp.dot(tri, g_mid, preferred_element_type=f32)
            + jnp.dot(tri, g_lo, preferred_element_type=f32))
    gcum_t = gcum.T
    lane = lax.broadcasted_iota(jnp.int32, (C, LANES), 1)
    sub = lax.broadcasted_iota(jnp.int32, (LANES, C), 0)

    def conv_silu(x_ref, kind, i):
        slot = kind * hg + i
        win_ref[slot, 0:8, :] = tail_ref[slot]
        win_ref[slot, 8:8 + C, :] = x_ref[i].astype(f32)
        w = cw_ref[kind, i]
        y = win_ref[slot, 8:8 + C, :] * w[CONV_W - 1:CONV_W, :]
        for t in range(1, CONV_W):
            y = y + win_ref[slot, 8 - t:8 - t + C, :] * w[CONV_W - 1 - t:CONV_W - t, :]
        tail_ref[slot] = win_ref[slot, C:C + 8, :]
        return _silu(y)

    heads = range(hg)
    hd = [pl.program_id(1) * hg + i for i in heads]
    q = [conv_silu(q_ref, 0, i) for i in heads]
    k = [conv_silu(k_ref, 1, i) for i in heads]
    v = [conv_silu(v_ref, 2, i) for i in heads]
    q = [x * lax.rsqrt(jnp.sum(x * x, axis=-1, keepdims=True) + EPS) * (HD ** -0.5) for x in q]
    k = [x * lax.rsqrt(jnp.sum(x * x, axis=-1, keepdims=True) + EPS) for x in k]

    bcol = [jnp.sum(jnp.where(lane == h, beta_all, 0.0), axis=1, keepdims=True) for h in hd]
    gcol = [jnp.sum(jnp.where(lane == h + HEADS, gcum, 0.0), axis=1, keepdims=True) for h in hd]
    grow = [jnp.sum(jnp.where(sub == h + HEADS, gcum_t, 0.0), axis=0, keepdims=True) for h in hd]
    glast = [g[C - 1:C, :] for g in gcol]
    gam = [jnp.where(causal, jnp.exp(jnp.where(causal, gc - gr, 0.0)), 0.0) for gc, gr in zip(gcol, grow)]
    egc = [jnp.exp(g) for g in gcol]

    kk = [_dot_nt(x, x) for x in k]
    qk = [_dot_nt(x, y) for x, y in zip(q, k)]
    a = [jnp.where(strict, m * g * b, 0.0) for m, g, b in zip(kk, gam, bcol)]
    qk = [m * g for m, g in zip(qk, gam)]

    d = [jnp.where(same8, m, 0.0) for m in a]
    d2 = [_dot(m, m) for m in d]
    d3 = [_dot(m, m2) for m, m2 in zip(d, d2)]
    d4 = [_dot(m2, m2) for m2 in d2]
    x = [eye - m + m2 - m3 for m, m2, m3 in zip(d, d2, d3)]
    x = [xi + _dot(xi, m4) for xi, m4 in zip(x, d4)]
    for mask in level_masks:
        t = [_dot(xi, jnp.where(mask, m, 0.0)) for xi, m in zip(x, a)]
        x = [xi - _dot(ti, xi) for xi, ti in zip(x, t)]

    rhs = [jnp.concatenate([vi * b, ki * (b * e)], axis=1) for vi, ki, b, e in zip(v, k, bcol, egc)]
    sol = [_dot(xi, r) for xi, r in zip(x, rhs)]
    q_dec = [qi * e for qi, e in zip(q, egc)]
    k_dec = [ki * jnp.exp(gl - gc) for ki, gl, gc in zip(k, glast, gcol)]

    state = [state_ref[i] for i in heads]
    ws = [_dot(jnp.concatenate([s_[:, HD:], qd], axis=0), st) for s_, qd, st in zip(sol, q_dec, state)]
    v_new = [s_[:, :HD] - w_[:C] for s_, w_ in zip(sol, ws)]
    o = [w_[C:] + _dot(m, vn) for w_, m, vn in zip(ws, qk, v_new)]
    upd = [_dot_tn(kd, vn) for kd, vn in zip(k_dec, v_new)]
    for i in heads:
        state_ref[i] = state[i] * jnp.exp(glast[i]) + upd[i]
        y = o[i] * _rms_scale(o[i]) * nw_ref[...]
        y = y * _silu(z_ref[i].astype(f32))
        o_ref[:, i * HD:(i + 1) * HD] = y.astype(bf16)


def _gdn(proj, ba, conv_w, a_log, dt_bias, norm_w, B, S, hg):
    C = DN_CHUNK
    NC = S // C
    T = B * S
    G = HEADS // hg
    cw = conv_w.reshape(CONV_W, 3, HEADS, HD).transpose(1, 2, 0, 3)
    pad = jnp.zeros((LANES - 2 * HEADS,), f32)
    arow = jnp.concatenate([jnp.zeros((HEADS,), f32), a_log.astype(f32), pad]).reshape(1, LANES)
    dtrow = jnp.concatenate([jnp.zeros((HEADS,), f32), dt_bias.astype(f32), pad]).reshape(1, LANES)

    def slab(off):
        return pl.BlockSpec((hg, C, HD), lambda b, g, n: (off // hg + g, b * NC + n, 0))

    row_spec = pl.BlockSpec((1, LANES), lambda b, g, n: (0, 0))
    return pl.pallas_call(
        functools.partial(_gdn_kernel, hg=hg),
        grid=(B, G, NC),
        in_specs=[
            slab(4 * HEADS), slab(5 * HEADS), slab(6 * HEADS), slab(7 * HEADS),
            pl.BlockSpec((C, LANES), lambda b, g, n: (b * NC + n, 0)),
            pl.BlockSpec((3, hg, CONV_W, HD), lambda b, g, n: (0, g, 0, 0)),
            row_spec, row_spec, row_spec,
        ],
        out_specs=pl.BlockSpec((C, hg * HD), lambda b, g, n: (b * NC + n, g)),
        out_shape=jax.ShapeDtypeStruct((T, HEADS * HD), bf16),
        scratch_shapes=[
            pltpu.VMEM((hg, HD, HD), f32),
            pltpu.VMEM((3 * hg, 8, HD), f32),
            pltpu.VMEM((3 * hg, C + 8, HD), f32),
        ],
        compiler_params=_params(("parallel", "parallel", "arbitrary")),
        name="gdn",
    )(proj, proj, proj, proj, ba, cw, arow, dtrow, norm_w.reshape(1, HD))


def _mm_res_kernel(a1_ref, a2_ref, w_ref, r_ref, o_ref):
    k1 = a1_ref.shape[1]
    acc = jnp.dot(a1_ref[...], w_ref[0:k1, :], preferred_element_type=f32)
    acc = acc + jnp.dot(a2_ref[...], w_ref[k1:, :], preferred_element_type=f32)
    o_ref[...] = r_ref[...] + acc


def _mm_res(a1, a2, w, res, tm, tn):
    T, K1 = a1.shape
    K2 = a2.shape[1]
    N = w.shape[1]
    return pl.pallas_call(
        _mm_res_kernel,
        grid=(T // tm, N // tn),
        in_specs=[
            pl.BlockSpec((tm, K1), lambda i, j: (i, 0)),
            pl.BlockSpec((tm, K2), lambda i, j: (i, 0)),
            pl.BlockSpec((K1 + K2, tn), lambda i, j: (0, j)),
            pl.BlockSpec((tm, tn), lambda i, j: (i, j)),
        ],
        out_specs=pl.BlockSpec((tm, tn), lambda i, j: (i, j)),
        out_shape=jax.ShapeDtypeStruct((T, N), f32),
        compiler_params=_params(("parallel", "parallel")),
        name="out_proj",
    )(a1, a2, w, res)


def _norm_mm_kernel(x_ref, nw_ref, w_ref, o_ref, xn_ref):
    @pl.when(pl.program_id(1) == 0)
    def _():
        x = x_ref[...]
        xn_ref[...] = (x * _rms_scale(x) * nw_ref[...]).astype(bf16)

    o_ref[...] = jnp.dot(xn_ref[...], w_ref[...], preferred_element_type=f32).astype(o_ref.dtype)


def _norm_mm(x2, norm_w, w, tm, tn, name):
    T, D = x2.shape
    N = w.shape[1]
    return pl.pallas_call(
        _norm_mm_kernel,
        grid=(T // tm, N // tn),
        in_specs=[
            pl.BlockSpec((tm, D), lambda i, j: (i, 0)),
            pl.BlockSpec((1, D), lambda i, j: (0, 0)),
            pl.BlockSpec((D, tn), lambda i, j: (0, j)),
        ],
        out_specs=pl.BlockSpec((tm, tn), lambda i, j: (i, j)),
        out_shape=jax.ShapeDtypeStruct((T, N), bf16),
        scratch_shapes=[pltpu.VMEM((tm, D), bf16)],
        compiler_params=_params(("parallel", "arbitrary")),
        name=name,
    )(x2, norm_w.reshape(1, D), w)


def _xattn_router_kernel(h_ref, nq_ref, wq_ref, k_ref, v_ref, wo_ref, nm_ref, wr_ref, b_ref,
                         h2_ref, hn_ref, rt_ref, cnt_ref, run_ref):
    @pl.when((pl.program_id(0) == 0) & (pl.program_id(1) == 0))
    def _():
        run_ref[...] = jnp.zeros_like(run_ref)

    h1 = h_ref[...]
    D = h1.shape[1]
    dh = D // XA_HEADS
    xn = (h1 * _rms_scale(h1) * nq_ref[...]).astype(bf16)
    q = jnp.dot(xn, wq_ref[...], preferred_element_type=f32).astype(bf16)
    acc = h1
    for h in range(XA_HEADS):
        sl = slice(h * dh, (h + 1) * dh)
        s = _dot_nt(q[:, sl], k_ref[:, sl]) * (dh ** -0.5)
        p = jnp.exp(s - jnp.max(s, axis=-1, keepdims=True))
        p = p / jnp.sum(p, axis=-1, keepdims=True)
        oh = _dot(p, v_ref[:, sl])
        acc = acc + _dot(oh, wo_ref[sl, :])
    h2_ref[...] = acc
    _route(acc, nm_ref, wr_ref, b_ref, hn_ref, rt_ref, cnt_ref, run_ref)


def _xattn_router(h1, norm_q, w_q, kv, w_o, norm_moe, w_r, bias_row, B, S, M, tm):
    T, D = h1.shape
    nt = S // tm
    g = D // (2 * LANES)
    tok = lambda b, i: (b * nt + i, 0)
    const = lambda b, i: (0, 0)
    resident = dict(pipeline_mode=pl.Buffered(1))
    return pl.pallas_call(
        _xattn_router_kernel,
        grid=(B, nt),
        in_specs=[
            pl.BlockSpec((tm, D), tok),
            pl.BlockSpec((1, D), const),
            pl.BlockSpec((D, D), const, **resident),
            pl.BlockSpec((M, D), lambda b, i: (b, 0)),
            pl.BlockSpec((M, D), lambda b, i: (b, 1)),
            pl.BlockSpec((D, D), const, **resident),
            pl.BlockSpec((1, D), const),
            pl.BlockSpec((D, LANES), const),
            pl.BlockSpec((1, LANES), const),
        ],
        out_specs=[
            pl.BlockSpec((tm, D), tok),
            pl.BlockSpec((tm * g, LANES), tok),
            pl.BlockSpec((tm, LANES), tok),
            pl.BlockSpec((1, LANES), const),
        ],
        out_shape=[
            jax.ShapeDtypeStruct((T, D), f32),
            jax.ShapeDtypeStruct((T * g, LANES), jnp.uint32),
            jax.ShapeDtypeStruct((T, LANES), f32),
            jax.ShapeDtypeStruct((1, LANES), f32),
        ],
        scratch_shapes=[pltpu.VMEM((1, LANES), f32)],
        compiler_params=_params(("arbitrary", "arbitrary")),
        name="xattn_router",
    )(h1, norm_q.reshape(1, D), w_q, kv, kv, w_o, norm_moe.reshape(1, D), w_r, bias_row)


def _pack_pair(a, b):
    au = lax.bitcast_convert_type(a.astype(f32), jnp.uint32)
    bu = lax.bitcast_convert_type(b.astype(f32), jnp.uint32)
    return (au >> 16) | bu


def _unpack_pair(u):
    a = lax.bitcast_convert_type(u << 16, f32)
    b = lax.bitcast_convert_type(u & jnp.uint32(0xFFFF0000), f32)
    return a, b


def _store_slabs(ref, words):
    n = words.shape[0]
    g = words.shape[1] // LANES
    for c in range(g):
        ref[pl.ds(c, n, stride=g), :] = words[:, c * LANES:(c + 1) * LANES]


def _load_slabs(ref, n, g):
    return [ref[pl.ds(c, n, stride=g), :] for c in range(g)]


def _route(x, nw_ref, wr_ref, b_ref, hn_ref, o_ref, cnt_ref, run_ref):
    xn = x * _rms_scale(x) * nw_ref[...]
    xhi = xn.astype(bf16)
    logits = jnp.dot(xhi, wr_ref[...], preferred_element_type=f32) + b_ref[...]
    lane = lax.broadcasted_iota(jnp.int32, logits.shape, 1).astype(f32)
    neg = -jnp.inf
    big = float(LANES)

    def first_max(vals):
        m = jnp.max(vals, axis=-1, keepdims=True)
        return m, jnp.min(jnp.where(vals == m, lane, big), axis=-1, keepdims=True)

    is_group = lane < N_GROUPS
    gmax, gsel = first_max(jnp.where(is_group, logits, neg))
    gsum = jnp.sum(jnp.where(is_group, jnp.exp(logits - gmax), 0.0), axis=-1, keepdims=True)
    g_w = 1.0 / gsum
    lo = N_GROUPS + EXP_PER_GROUP * gsel
    in_group = (lane >= lo) & (lane < lo + EXP_PER_GROUP)
    el = jnp.where(in_group, logits, neg)
    l1, i1 = first_max(el)
    l2, i2 = first_max(jnp.where(lane == i1, neg, el))
    esum = jnp.sum(jnp.where(in_group, jnp.exp(logits - l1), 0.0), axis=-1, keepdims=True)
    p1 = 1.0 / esum
    p2 = jnp.exp(l2 - l1) / esum
    gate1 = g_w * p1 / (p1 + p2)
    gate2 = g_w * p2 / (p1 + p2)

    tm = x.shape[0]
    chosen = jnp.where((lane == i1) | (lane == i2), 1.0, 0.0)
    earlier = (lax.broadcasted_iota(jnp.int32, (tm, tm), 0) > lax.broadcasted_iota(jnp.int32, (tm, tm), 1))
    before = _dot(jnp.where(earlier, 1.0, 0.0), chosen) + run_ref[...]
    rank1 = jnp.sum(jnp.where(lane == i1, before, 0.0), axis=-1, keepdims=True)
    rank2 = jnp.sum(jnp.where(lane == i2, before, 0.0), axis=-1, keepdims=True)
    run_ref[...] = run_ref[...] + jnp.sum(chosen, axis=0, keepdims=True)
    cnt_ref[...] = run_ref[...]

    cols = (i1 - N_GROUPS, i2 - N_GROUPS, gate1, gate2, rank1, rank2)
    out = jnp.zeros_like(logits)
    for c, val in enumerate(cols):
        out = jnp.where(lane == c, val, out)
    o_ref[...] = out
    half = x.shape[1] // 2
    _store_slabs(hn_ref, _pack_pair(xhi[:, :half], xhi[:, half:]))


def _dispatch_kernel(dest_ref, pad_lo_ref, pad_n_ref, zcnt_ref, tail_ref, hn_ref, xs_hbm, zeros_ref, sem,
                     zsem, *, tb, g, bm):
    i = pl.program_id(0)
    base = i * tb
    sizes = [1 << b for b in range(bm.bit_length() - 1)]

    def zero_copy(first, p):
        return pltpu.make_async_copy(zeros_ref.at[pl.ds(0, p * g)], xs_hbm.at[pl.ds(first, p * g)], zsem)

    @pl.when(i == 0)
    def _():
        zeros_ref[...] = jnp.zeros_like(zeros_ref)

        def per_expert(e, carry):
            row = pad_lo_ref[e]
            n = pad_n_ref[e]
            for p in sizes:
                hit = (n & p) != 0

                @pl.when(hit)
                def _(row=row, p=p):
                    zero_copy(pl.multiple_of(row * g, g), p).start()

                row = row + jnp.where(hit, p, 0)
            return carry

        lax.fori_loop(0, N_EXPERTS, per_expert, 0)

        def per_tail_chunk(c, carry):
            zero_copy(pl.multiple_of((tail_ref[0] + c * sizes[-1]) * g, g), sizes[-1]).start()
            return carry

        lax.fori_loop(0, tail_ref[1], per_tail_chunk, 0)

    def body(r, carry):
        src = hn_ref.at[pl.ds(pl.multiple_of(r * g, g), g)]
        for kk in range(2):
            d = dest_ref[2 * (base + r) + kk]
            pltpu.make_async_copy(src, xs_hbm.at[pl.ds(pl.multiple_of(d * g, g), g)], sem.at[kk]).start()
        return carry

    lax.fori_loop(0, tb, body, 0, unroll=8)
    for kk in range(2):
        pltpu.make_async_copy(hn_ref, hn_ref, sem.at[kk]).wait()

    @pl.when(i == pl.num_programs(0) - 1)
    def _():
        for b, p in enumerate(sizes):
            def wait_one(_, carry, p=p):
                zero_copy(0, p).wait()
                return carry
            lax.fori_loop(0, zcnt_ref[b], wait_one, 0)


def _dispatch(hn, dest, pads, P, tb, g, bm):
    T = hn.shape[0] // g
    grid_spec = pltpu.PrefetchScalarGridSpec(
        num_scalar_prefetch=5,
        grid=(T // tb,),
        in_specs=[pl.BlockSpec((tb * g, LANES), lambda i, *_: (i, 0))],
        out_specs=pl.BlockSpec(memory_space=pl.ANY),
        scratch_shapes=[pltpu.VMEM((bm // 2 * g, LANES), jnp.uint32),
                        pltpu.SemaphoreType.DMA((2,)), pltpu.SemaphoreType.DMA(())],
    )
    return pl.pallas_call(
        functools.partial(_dispatch_kernel, tb=tb, g=g, bm=bm),
        grid_spec=grid_spec,
        out_shape=jax.ShapeDtypeStruct((P * g, LANES), jnp.uint32),
        compiler_params=_params(("arbitrary",)),
        name="dispatch",
    )(dest, *pads, hn)


def _ffn_kernel(blk_e_ref, nused_ref, x_ref, wg_ref, wu_ref, wd_ref, y_ref, *, bm):
    i = pl.program_id(0)
    half = wg_ref.shape[0] // 2

    @pl.when(i < nused_ref[0])
    def _():
        pairs = [_unpack_pair(w) for w in _load_slabs(x_ref, bm, half // LANES)]
        a = jnp.concatenate([p[0].astype(bf16) for p in pairs], axis=1)
        b = jnp.concatenate([p[1].astype(bf16) for p in pairs], axis=1)
        g = (jnp.dot(a, wg_ref[0:half, :], preferred_element_type=f32)
             + jnp.dot(b, wg_ref[half:, :], preferred_element_type=f32))
        u = (jnp.dot(a, wu_ref[0:half, :], preferred_element_type=f32)
             + jnp.dot(b, wu_ref[half:, :], preferred_element_type=f32))
        hmid = (_silu(g) * u).astype(bf16)
        y = jnp.dot(hmid, wd_ref[...], preferred_element_type=f32)
        _store_slabs(y_ref, _pack_pair(y[:, :half].astype(bf16), y[:, half:].astype(bf16)))

    @pl.when(i >= nused_ref[0])
    def _():
        y_ref[...] = jnp.zeros_like(y_ref)


def _ffn(xs, wg, wu, wd, blk_e, nused, bm):
    D, F = wg.shape[1:]
    g = D // (2 * LANES)
    NB = blk_e.shape[0]
    grid_spec = pltpu.PrefetchScalarGridSpec(
        num_scalar_prefetch=2,
        grid=(NB,),
        in_specs=[
            pl.BlockSpec((bm * g, LANES), lambda i, be, n: (jnp.minimum(i, n[0] - 1), 0)),
            pl.BlockSpec((None, D, F), lambda i, be, n: (be[i], 0, 0)),
            pl.BlockSpec((None, D, F), lambda i, be, n: (be[i], 0, 0)),
            pl.BlockSpec((None, F, D), lambda i, be, n: (be[i], 0, 0)),
        ],
        out_specs=pl.BlockSpec((bm * g, LANES), lambda i, be, n: (i, 0)),
    )
    return pl.pallas_call(
        functools.partial(_ffn_kernel, bm=bm),
        grid_spec=grid_spec,
        out_shape=jax.ShapeDtypeStruct(xs.shape, jnp.uint32),
        compiler_params=_params(("arbitrary",)),
        name="ffn",
    )(blk_e, nused, xs, wg, wu, wd)


def _combine_kernel(dest_ref, y_hbm, h_ref, rt_ref, nw_ref, o_ref, ybuf, sem):
    i = pl.program_id(0)
    n = pl.num_programs(0)
    tm = h_ref.shape[0]
    slot = i % 2

    D = h_ref.shape[1]
    half = D // 2
    g = half // LANES

    def issue(blk, slot_):
        def body(r, carry):
            a = 2 * (blk * tm + r)
            for kk in range(2):
                d = dest_ref[a + kk]
                pltpu.make_async_copy(y_hbm.at[pl.ds(pl.multiple_of(d * g, g), g)],
                                      ybuf.at[slot_, kk, pl.ds(pl.multiple_of(r * g, g), g)],
                                      sem.at[slot_]).start()
            return carry
        lax.fori_loop(0, tm, body, 0, unroll=8)

    @pl.when(i == 0)
    def _():
        issue(0, 0)

    @pl.when(i + 1 < n)
    def _():
        issue(i + 1, 1 - slot)

    pltpu.make_async_copy(ybuf.at[slot], ybuf.at[slot], sem.at[slot]).wait()
    rt = rt_ref[...]
    g1 = rt[:, 2:3]
    g2 = rt[:, 3:4]
    y1 = [_unpack_pair(w) for w in _load_slabs(ybuf.at[slot, 0], tm, g)]
    y2 = [_unpack_pair(w) for w in _load_slabs(ybuf.at[slot, 1], tm, g)]
    cols = [slice(c * LANES, (c + 1) * LANES) for c in range(2 * g)]
    hs = [h_ref[:, cols[c]] + g1 * y1[c % g][c // g] + g2 * y2[c % g][c // g] for c in range(2 * g)]
    ms = sum(jnp.sum(x * x, axis=-1, keepdims=True) for x in hs) / D
    scale = lax.rsqrt(ms + EPS)
    for c in range(2 * g):
        o_ref[:, cols[c]] = hs[c] * scale * nw_ref[:, cols[c]]


def _combine(y, h2, rout, dest, norm_w, tm):
    T, D = h2.shape
    g = D // (2 * LANES)
    grid_spec = pltpu.PrefetchScalarGridSpec(
        num_scalar_prefetch=1,
        grid=(T // tm,),
        in_specs=[
            pl.BlockSpec(memory_space=pl.ANY),
            pl.BlockSpec((tm, D), lambda i, d: (i, 0)),
            pl.BlockSpec((tm, LANES), lambda i, d: (i, 0)),
            pl.BlockSpec((1, D), lambda i, d: (0, 0)),
        ],
        out_specs=pl.BlockSpec((tm, D), lambda i, d: (i, 0)),
        scratch_shapes=[pltpu.VMEM((2, 2, tm * g, LANES), jnp.uint32), pltpu.SemaphoreType.DMA((2,))],
    )
    return pl.pallas_call(
        _combine_kernel,
        grid_spec=grid_spec,
        out_shape=jax.ShapeDtypeStruct((T, D), f32),
        compiler_params=_params(("arbitrary",)),
        name="combine",
    )(dest, y, h2, rout, norm_w.reshape(1, D))


def _dispatch_plan(rout, cnt, bm):
    T = rout.shape[0]
    NB = 2 * T // bm + N_EXPERTS
    e_tk = rout[:, 0:2].astype(jnp.int32)
    rank_tk = rout[:, 4:6].astype(jnp.int32)
    counts = cnt[0, N_GROUPS:N_GROUPS + N_EXPERTS].astype(jnp.int32)
    padded = (counts + bm - 1) // bm * bm
    pend = jnp.cumsum(padded)
    pstart = pend - padded
    first_row = jnp.arange(NB, dtype=jnp.int32) * bm
    blk_e = jnp.minimum(jnp.sum((pend[None, :] <= first_row[:, None]).astype(jnp.int32), axis=1),
                        N_EXPERTS - 1)
    nused = (pend[-1] // bm).astype(jnp.int32).reshape(1)
    onehot = e_tk[:, :, None] == jnp.arange(N_EXPERTS, dtype=jnp.int32)
    dest = jnp.sum(jnp.where(onehot, pstart, 0), axis=-1) + rank_tk
    pad_n = padded - counts
    nbits = bm.bit_length() - 1
    bits = (pad_n[:, None] >> jnp.arange(nbits, dtype=jnp.int32)) & 1
    tail_chunks = 2 * (NB - nused[0])
    zcnt = jnp.sum(bits, axis=0) + jnp.where(jnp.arange(nbits) == nbits - 1, tail_chunks, 0)
    pads = ((pstart + counts).astype(jnp.int32), pad_n.astype(jnp.int32), zcnt.astype(jnp.int32),
            jnp.stack([pend[-1], tail_chunks]).astype(jnp.int32))
    return blk_e, nused, dest.reshape(2 * T).astype(jnp.int32), pads


def _tile(n, want):
    t = min(n, want)
    while n % t:
        t //= 2
    return t


def kernel(x, mem, positions, norm_mix_w, w_in, dn_conv_w, dn_a_log, dn_dt_bias, ret_gn_w, dn_norm_w,
           w_out, norm_xq_w, norm_mem_w, w_xq, w_xkv, w_xo, norm_moe_w, w_group_router, b_group_router,
           w_expert_router, b_expert_router, w_gate, w_up, w_down, norm_final_w):
    B, S, D = x.shape
    M = mem.shape[1]
    T = B * S
    depth = w_in.shape[0]
    n_main = 8 * HEADS * HD
    h = x.reshape(T, D)
    pos_col = positions.astype(f32).reshape(T, 1)
    mem2 = mem.reshape(B * M, D)
    hg = HEADS
    bm = 256
    for l in range(depth):
        w_main = w_in[l][:, :n_main].astype(bf16)
        w_ba = jnp.pad(w_in[l][:, n_main:], ((0, 0), (0, LANES - 2 * HEADS))).astype(bf16)
        proj, ba = _in_proj(h, norm_mix_w[l], w_main, w_ba, _tile(T, 1024), 1024)
        mix_r = _retention(proj, pos_col, ret_gn_w[l], B, S, hg)
        mix_d = _gdn(proj, ba, dn_conv_w[l], dn_a_log[l], dn_dt_bias[l], dn_norm_w[l], B, S, hg)
        h = _mm_res(mix_r, mix_d, w_out[l].astype(bf16), h, _tile(T, 1024), _tile(D, 1024))
        kv = _norm_mm(mem2, norm_mem_w[l], w_xkv[l].astype(bf16), _tile(B * M, 512), _tile(2 * D, 1024), "xkv")
        w_r = jnp.pad(jnp.concatenate([w_group_router[l], w_expert_router[l]], axis=1),
                      ((0, 0), (0, LANES - N_GROUPS - N_EXPERTS))).astype(bf16)
        b_r = jnp.pad(jnp.concatenate([b_group_router[l], b_expert_router[l]]),
                      (0, LANES - N_GROUPS - N_EXPERTS)).reshape(1, LANES).astype(f32)
        h, hn, rout, cnt = _xattn_router(h, norm_xq_w[l], w_xq[l].astype(bf16), kv, w_xo[l].astype(bf16),
                                         norm_moe_w[l], w_r, b_r, B, S, M, _tile(S, 512))
        blk_e, nused, dest, pads = _dispatch_plan(rout, cnt, bm)
        xs = _dispatch(hn, dest, pads, blk_e.shape[0] * bm, _tile(T, 1024), D // (2 * LANES), bm)
        y = _ffn(xs, w_gate[l].astype(bf16), w_up[l].astype(bf16), w_down[l].astype(bf16),
                 blk_e, nused, bm)
        if l + 1 < depth:
            raise NotImplementedError("the fused MoE combine applies the final norm: depth 1 only")
        h = _combine(y, h, rout, dest, norm_final_w, _tile(T, 256))
    return h.reshape(B, S, D)
```

```python
import functools

import numpy as np
import jax
import jax.numpy as jnp
from jax import lax
from jax.experimental import pallas as pl
from jax.experimental.pallas import tpu as pltpu

f32 = jnp.float32
bf16 = jnp.bfloat16

EPS = 1e-6
HEADS = 8
HD = 128
LANES = 128
XA_HEADS = 4
CONV_W = 4
ROPE_BASE = 10000.0
N_GROUPS = 4
EXP_PER_GROUP = 8
N_EXPERTS = N_GROUPS * EXP_PER_GROUP
RET_CHUNK = 256
DN_CHUNK = 128
VMEM_LIMIT = 56 * 1024 * 1024


def _params(sem, vmem=VMEM_LIMIT):
    return pltpu.CompilerParams(dimension_semantics=sem, vmem_limit_bytes=vmem)


def _dot(a, b):
    return jnp.dot(a.astype(bf16), b.astype(bf16), preferred_element_type=f32)


def _dot_nt(a, b):
    return lax.dot_general(a.astype(bf16), b.astype(bf16), (((1,), (1,)), ((), ())),
                           preferred_element_type=f32)


def _dot_tn(a, b):
    return lax.dot_general(a.astype(bf16), b.astype(bf16), (((0,), (0,)), ((), ())),
                           preferred_element_type=f32)


def _split3(a):
    hi = a.astype(bf16)
    r = a - hi.astype(f32)
    mid = r.astype(bf16)
    lo = (r - mid.astype(f32)).astype(bf16)
    return hi, mid, lo


def _sigmoid(x):
    return 1.0 / (1.0 + jnp.exp(-x))


def _silu(x):
    return x * _sigmoid(x)


def _rms_scale(x):
    return lax.rsqrt(jnp.mean(x * x, axis=-1, keepdims=True) + EPS)


def _side_cast_step(step, nsteps, srcs, dsts, stage_in, stage_out, sem_in, sem_out):
    slot = step % 2
    mats = range(len(srcs))

    def in_copy(m, st, sl):
        r = stage_in[m].shape[1]
        rows = pl.ds(pl.multiple_of(st * r, r), r)
        return pltpu.make_async_copy(srcs[m].at[rows], stage_in[m].at[sl], sem_in.at[m, sl])

    def out_copy(m, st, sl):
        r = stage_out[m].shape[1]
        rows = pl.ds(pl.multiple_of(st * r, r), r)
        return pltpu.make_async_copy(stage_out[m].at[sl], dsts[m].at[rows], sem_out.at[m, sl])

    @pl.when(step == 0)
    def _():
        for m in mats:
            in_copy(m, 0, 0).start()

    @pl.when(step + 1 < nsteps)
    def _():
        for m in mats:
            in_copy(m, step + 1, 1 - slot).start()

    for m in mats:
        in_copy(m, step, slot).wait()

    @pl.when(step >= 2)
    def _():
        for m in mats:
            out_copy(m, step - 2, slot).wait()

    for m in mats:
        stage_out[m][slot] = stage_in[m][slot].astype(bf16)
    for m in mats:
        out_copy(m, step, slot).start()

    @pl.when(step == nsteps - 1)
    def _():
        for m in mats:
            if nsteps >= 2:
                out_copy(m, step - 1, 1 - slot).wait()
            out_copy(m, step, slot).wait()


def _in_proj_kernel(x_ref, nw_ref, w_ref, wba_ref, *rest, n_side):
    srcs, rest = rest[:n_side], rest[n_side:]
    o_ref, ba_ref = rest[:2]
    dsts, rest = rest[2:2 + n_side], rest[2 + n_side:]
    xn_ref = rest[0]
    stage_in, stage_out = rest[1:1 + n_side], rest[1 + n_side:1 + 2 * n_side]
    sem_in, sem_out = rest[1 + 2 * n_side:]

    step = pl.program_id(0) * pl.num_programs(1) + pl.program_id(1)
    nsteps = pl.num_programs(0) * pl.num_programs(1)
    _side_cast_step(step, nsteps, srcs, dsts, stage_in, stage_out, sem_in, sem_out)

    @pl.when(pl.program_id(1) == 0)
    def _():
        x = x_ref[...]
        xn = (x * _rms_scale(x) * nw_ref[...]).astype(bf16)
        xn_ref[...] = xn
        ba_ref[...] = jnp.dot(xn, wba_ref[...], preferred_element_type=f32)

    acc = jnp.dot(xn_ref[...], w_ref[...], preferred_element_type=f32)
    for c in range(o_ref.shape[0]):
        o_ref[c] = acc[:, c * LANES:(c + 1) * LANES].astype(bf16)


def _in_proj(x2, norm_w, w_main, w_ba, side, tm, tn):
    T, D = x2.shape
    N = w_main.shape[1]
    nsteps = (T // tm) * (N // tn)
    rows = [a.shape[0] // nsteps for a in side]
    assert all(a.shape[0] == r * nsteps and r % 16 == 0 for a, r in zip(side, rows))
    any_spec = pl.BlockSpec(memory_space=pl.ANY)
    return pl.pallas_call(
        functools.partial(_in_proj_kernel, n_side=len(side)),
        grid=(T // tm, N // tn),
        in_specs=[
            pl.BlockSpec((tm, D), lambda i, j: (i, 0)),
            pl.BlockSpec((1, D), lambda i, j: (0, 0)),
            pl.BlockSpec((D, tn), lambda i, j: (0, j)),
            pl.BlockSpec((D, LANES), lambda i, j: (0, 0)),
        ] + [any_spec] * len(side),
        out_specs=[
            pl.BlockSpec((tn // LANES, tm, LANES), lambda i, j: (j, i, 0)),
            pl.BlockSpec((tm, LANES), lambda i, j: (i, 0)),
        ] + [any_spec] * len(side),
        out_shape=[
            jax.ShapeDtypeStruct((N // LANES, T, LANES), bf16),
            jax.ShapeDtypeStruct((T, LANES), f32),
        ] + [jax.ShapeDtypeStruct(a.shape, bf16) for a in side],
        scratch_shapes=[pltpu.VMEM((tm, D), bf16)]
        + [pltpu.VMEM((2, r, a.shape[1]), f32) for a, r in zip(side, rows)]
        + [pltpu.VMEM((2, r, a.shape[1]), bf16) for a, r in zip(side, rows)]
        + [pltpu.SemaphoreType.DMA((len(side), 2)), pltpu.SemaphoreType.DMA((len(side), 2))],
        compiler_params=_params(("arbitrary", "arbitrary")),
        name="in_proj",
    )(x2, norm_w.reshape(1, D), w_main, w_ba, *side)


def _retention_kernel(q_ref, k_ref, v_ref, g_ref, pos_ref, inv_ref, sgn_ref, dmask_ref, qdec_ref,
                      kdec_ref, cdec_ref, gnw_ref, o_ref, state_ref):
    @pl.when(pl.program_id(2) == 0)
    def _():
        state_ref[...] = jnp.zeros_like(state_ref)

    C = pos_ref.shape[0]
    lane = lax.broadcasted_iota(jnp.int32, (C // 2, HD), 1)
    low = lane < HD // 2
    ang = jnp.where(low, pos_ref[0:C // 2, :], pos_ref[C // 2:, :]) * inv_ref[...]
    cos_p = jnp.cos(ang)
    sin_p = jnp.sin(ang)

    def spread(t):
        swapped = pltpu.roll(t, HD // 2, 1)
        return jnp.concatenate([jnp.where(low, t, swapped), jnp.where(low, swapped, t)], axis=0)

    cos2 = spread(cos_p)
    sin2 = spread(sin_p) * sgn_ref[...]
    heads = range(q_ref.shape[0])

    def rope(x):
        return x * cos2 + pltpu.roll(x, HD // 2, 1) * sin2

    qr = [rope(q_ref[i].astype(f32)) for i in heads]
    kr = [rope(k_ref[i].astype(f32)) * (HD ** -0.5) for i in heads]
    state = [state_ref[i] for i in heads]
    s = [_dot_nt(qr[i], kr[i]) * dmask_ref[i] for i in heads]
    cross = [_dot(qr[i], state[i]) * qdec_ref[i] for i in heads]
    upd = [_dot_tn(kr[i] * kdec_ref[i], v_ref[i]) for i in heads]
    o = [_dot(s[i], v_ref[i]) + cross[i] for i in heads]
    for i in heads:
        state_ref[i] = state[i] * cdec_ref[i] + upd[i]
        d = o[i] - jnp.mean(o[i], axis=-1, keepdims=True)
        y = d * lax.rsqrt(jnp.mean(d * d, axis=-1, keepdims=True) + EPS) * gnw_ref[i]
        y = y * _silu(g_ref[i].astype(f32))
        o_ref[:, i * HD:(i + 1) * HD] = y.astype(bf16)


def _retention_tables(C):
    h = np.arange(HEADS, dtype=np.float64)
    log_gamma = np.log1p(-np.exp2(-5.0 - h))
    idx = np.arange(C, dtype=np.float64)
    rel = idx[:, None] - idx[None, :]
    dmask = np.where(rel >= 0, np.exp(log_gamma[:, None, None] * np.where(rel >= 0, rel, 0.0)), 0.0)
    qdec = np.exp(log_gamma[:, None] * (idx + 1.0))
    kdec = np.exp(log_gamma[:, None] * (C - 1.0 - idx))
    cdec = np.exp(log_gamma * C)
    rep = lambda a: np.broadcast_to(a[..., None], a.shape + (LANES,))
    return (jnp.asarray(dmask, f32), jnp.asarray(rep(qdec), f32), jnp.asarray(rep(kdec), f32),
            jnp.asarray(np.broadcast_to(cdec[:, None, None], (HEADS, 1, LANES)), f32))


def _retention(proj, pos_col, gn_w, B, S, hg):
    C = RET_CHUNK
    NC = S // C
    T = B * S
    G = HEADS // hg
    half = HD // 2
    inv = ROPE_BASE ** (-np.arange(half, dtype=np.float32) / half)
    inv2 = jnp.asarray(np.concatenate([inv, inv]).reshape(1, HD), f32)
    sgn = jnp.asarray(np.concatenate([-np.ones(half), np.ones(half)]).reshape(1, HD), f32)
    dmask, qdec, kdec, cdec = _retention_tables(C)

    def slab(off):
        return pl.BlockSpec((hg, C, HD), lambda b, g, n: (off // hg + g, b * NC + n, 0))

    def table(shape):
        return pl.BlockSpec((hg,) + shape, lambda b, g, n: (g,) + (0,) * len(shape))

    return pl.pallas_call(
        _retention_kernel,
        grid=(B, G, NC),
        in_specs=[
            slab(0), slab(HEADS), slab(2 * HEADS), slab(3 * HEADS),
            pl.BlockSpec((C, 1), lambda b, g, n: (b * NC + n, 0)),
            pl.BlockSpec((1, HD), lambda b, g, n: (0, 0)),
            pl.BlockSpec((1, HD), lambda b, g, n: (0, 0)),
            table((C, C)), table((C, LANES)), table((C, LANES)), table((1, LANES)), table((1, HD)),
        ],
        out_specs=pl.BlockSpec((C, hg * HD), lambda b, g, n: (b * NC + n, g)),
        out_shape=jax.ShapeDtypeStruct((T, HEADS * HD), bf16),
        scratch_shapes=[pltpu.VMEM((hg, HD, HD), f32)],
        compiler_params=_params(("parallel", "parallel", "arbitrary")),
        name="retention",
    )(proj, proj, proj, proj, pos_col, inv2, sgn, dmask, qdec, kdec, cdec,
      gn_w.reshape(HEADS, 1, HD))


def _gdn_kernel(q_ref, k_ref, v_ref, z_ref, ba_ref, cw_ref, arow_ref, dtrow_ref, nw_ref, o_ref,
                state_ref, tail_ref, win_ref, *, hg):
    C = q_ref.shape[1]
    first = pl.program_id(2) == 0

    @pl.when(first)
    def _():
        state_ref[...] = jnp.zeros_like(state_ref)
        tail_ref[...] = jnp.zeros_like(tail_ref)

    row = lax.broadcasted_iota(jnp.int32, (C, C), 0)
    col = lax.broadcasted_iota(jnp.int32, (C, C), 1)
    causal = row >= col
    strict = row > col
    same8 = (row // 8) == (col // 8)
    level_masks = []
    s = 8
    while s < C:
        level_masks.append(((row // (2 * s)) == (col // (2 * s))) & ((row // s) != (col // s)))
        s *= 2
    eye = jnp.where(row == col, 1.0, 0.0).astype(f32)
    tri = jnp.where(causal, 1.0, 0.0).astype(bf16)

    ba = ba_ref[...]
    beta_all = _sigmoid(ba)
    xa = ba + dtrow_ref[...]
    softplus = jnp.maximum(xa, 0.0) + jnp.log(1.0 + jnp.exp(-jnp.abs(xa)))
    glog = -jnp.exp(arow_ref[...]) * softplus
    g_hi, g_mid, g_lo = _split3(glog)
    gcum = (jnp.dot(tri, g_hi, preferred_element_type=f32)
            + jnp.dot(tri, g_mid, preferred_element_type=f32)
            + jnp.dot(tri, g_lo, preferred_element_type=f32))
    gcum_t = gcum.T
    lane = lax.broadcasted_iota(jnp.int32, (C, LANES), 1)
    sub = lax.broadcasted_iota(jnp.int32, (LANES, C), 0)

    def conv_silu(x_ref, kind, i):
        slot = kind * hg + i
        win_ref[slot, 0:8, :] = tail_ref[slot]
        win_ref[slot, 8:8 + C, :] = x_ref[i].astype(f32)
        w = cw_ref[kind, i]
        y = win_ref[slot, 8:8 + C, :] * w[CONV_W - 1:CONV_W, :]
        for t in range(1, CONV_W):
            y = y + win_ref[slot, 8 - t:8 - t + C, :] * w[CONV_W - 1 - t:CONV_W - t, :]
        tail_ref[slot] = win_ref[slot, C:C + 8, :]
        return _silu(y)

    heads = range(hg)
    hd = [pl.program_id(1) * hg + i for i in heads]
    q = [conv_silu(q_ref, 0, i) for i in heads]
    k = [conv_silu(k_ref, 1, i) for i in heads]
    v = [conv_silu(v_ref, 2, i) for i in heads]
    q = [x * lax.rsqrt(jnp.sum(x * x, axis=-1, keepdims=True) + EPS) * (HD ** -0.5) for x in q]
    k = [x * lax.rsqrt(jnp.sum(x * x, axis=-1, keepdims=True) + EPS) for x in k]

    bcol = [jnp.sum(jnp.where(lane == h, beta_all, 0.0), axis=1, keepdims=True) for h in hd]
    gcol = [jnp.sum(jnp.where(lane == h + HEADS, gcum, 0.0), axis=1, keepdims=True) for h in hd]
    grow = [jnp.sum(jnp.where(sub == h + HEADS, gcum_t, 0.0), axis=0, keepdims=True) for h in hd]
    glast = [g[C - 1:C, :] for g in gcol]
    gam = [jnp.where(causal, jnp.exp(jnp.where(causal, gc - gr, 0.0)), 0.0) for gc, gr in zip(gcol, grow)]
    egc = [jnp.exp(g) for g in gcol]

    kk = [_dot_nt(x, x) for x in k]
    qk = [_dot_nt(x, y) for x, y in zip(q, k)]
    a = [jnp.where(strict, m * g * b, 0.0) for m, g, b in zip(kk, gam, bcol)]
    qk = [m * g for m, g in zip(qk, gam)]

    d = [jnp.where(same8, m, 0.0) for m in a]
    d2 = [_dot(m, m) for m in d]
    d3 = [_dot(m, m2) for m, m2 in zip(d, d2)]
    d4 = [_dot(m2, m2) for m2 in d2]
    x = [eye - m + m2 - m3 for m, m2, m3 in zip(d, d2, d3)]
    x = [xi + _dot(xi, m4) for xi, m4 in zip(x, d4)]
    for mask in level_masks:
        t = [_dot(xi, jnp.where(mask, m, 0.0)) for xi, m in zip(x, a)]
        x = [xi - _dot(ti, xi) for xi, ti in zip(x, t)]

    rhs = [jnp.concatenate([vi * b, ki * (b * e)], axis=1) for vi, ki, b, e in zip(v, k, bcol, egc)]
    sol = [_dot(xi, r) for xi, r in zip(x, rhs)]
    q_dec = [qi * e for qi, e in zip(q, egc)]
    k_dec = [ki * jnp.exp(gl - gc) for ki, gl, gc in zip(k, glast, gcol)]

    state = [state_ref[i] for i in heads]
    ws = [_dot(jnp.concatenate([s_[:, HD:], qd], axis=0), st) for s_, qd, st in zip(sol, q_dec, state)]
    v_new = [s_[:, :HD] - w_[:C] for s_, w_ in zip(sol, ws)]
    o = [w_[C:] + _dot(m, vn) for w_, m, vn in zip(ws, qk, v_new)]
    upd = [_dot_tn(kd, vn) for kd, vn in zip(k_dec, v_new)]
    for i in heads:
        state_ref[i] = state[i] * jnp.exp(glast[i]) + upd[i]
        y = o[i] * _rms_scale(o[i]) * nw_ref[...]
        y = y * _silu(z_ref[i].astype(f32))
        o_ref[:, i * HD:(i + 1) * HD] = y.astype(bf16)


def _gdn(proj, ba, conv_w, a_log, dt_bias, norm_w, B, S, hg):
    C = DN_CHUNK
    NC = S // C
    T = B * S
    G = HEADS // hg
    cw = conv_w.reshape(CONV_W, 3, HEADS, HD).transpose(1, 2, 0, 3)
    pad = jnp.zeros((LANES - 2 * HEADS,), f32)
    arow = jnp.concatenate([jnp.zeros((HEADS,), f32), a_log.astype(f32), pad]).reshape(1, LANES)
    dtrow = jnp.concatenate([jnp.zeros((HEADS,), f32), dt_bias.astype(f32), pad]).reshape(1, LANES)

    def slab(off):
        return pl.BlockSpec((hg, C, HD), lambda b, g, n: (off // hg + g, b * NC + n, 0))

    row_spec = pl.BlockSpec((1, LANES), lambda b, g, n: (0, 0))
    return pl.pallas_call(
        functools.partial(_gdn_kernel, hg=hg),
        grid=(B, G, NC),
        in_specs=[
            slab(4 * HEADS), slab(5 * HEADS), slab(6 * HEADS), slab(7 * HEADS),
            pl.BlockSpec((C, LANES), lambda b, g, n: (b * NC + n, 0)),
            pl.BlockSpec((3, hg, CONV_W, HD), lambda b, g, n: (0, g, 0, 0)),
            row_spec, row_spec, row_spec,
        ],
        out_specs=pl.BlockSpec((C, hg * HD), lambda b, g, n: (b * NC + n, g)),
        out_shape=jax.ShapeDtypeStruct((T, HEADS * HD), bf16),
        scratch_shapes=[
            pltpu.VMEM((hg, HD, HD), f32),
            pltpu.VMEM((3 * hg, 8, HD), f32),
            pltpu.VMEM((3 * hg, C + 8, HD), f32),
        ],
        compiler_params=_params(("parallel", "parallel", "arbitrary")),
        name="gdn",
    )(proj, proj, proj, proj, ba, cw, arow, dtrow, norm_w.reshape(1, HD))


def _mm_res_kernel(a1_ref, a2_ref, w_ref, r_ref, o_ref):
    k1 = a1_ref.shape[1]
    acc = jnp.dot(a1_ref[...], w_ref[0:k1, :], preferred_element_type=f32)
    acc = acc + jnp.dot(a2_ref[...], w_ref[k1:, :], preferred_element_type=f32)
    o_ref[...] = r_ref[...] + acc


def _mm_res(a1, a2, w, res, tm, tn):
    T, K1 = a1.shape
    K2 = a2.shape[1]
    N = w.shape[1]
    return pl.pallas_call(
        _mm_res_kernel,
        grid=(T // tm, N // tn),
        in_specs=[
            pl.BlockSpec((tm, K1), lambda i, j: (i, 0)),
            pl.BlockSpec((tm, K2), lambda i, j: (i, 0)),
            pl.BlockSpec((K1 + K2, tn), lambda i, j: (0, j)),
            pl.BlockSpec((tm, tn), lambda i, j: (i, j)),
        ],
        out_specs=pl.BlockSpec((tm, tn), lambda i, j: (i, j)),
        out_shape=jax.ShapeDtypeStruct((T, N), f32),
        compiler_params=_params(("parallel", "parallel")),
        name="out_proj",
    )(a1, a2, w, res)


def _norm_mm_kernel(x_ref, nw_ref, w_ref, o_ref, xn_ref):
    @pl.when(pl.program_id(1) == 0)
    def _():
        x = x_ref[...]
        xn_ref[...] = (x * _rms_scale(x) * nw_ref[...]).astype(bf16)

    o_ref[...] = jnp.dot(xn_ref[...], w_ref[...], preferred_element_type=f32).astype(o_ref.dtype)


def _norm_mm(x2, norm_w, w, tm, tn, name):
    T, D = x2.shape
    N = w.shape[1]
    return pl.pallas_call(
        _norm_mm_kernel,
        grid=(T // tm, N // tn),
        in_specs=[
            pl.BlockSpec((tm, D), lambda i, j: (i, 0)),
            pl.BlockSpec((1, D), lambda i, j: (0, 0)),
            pl.BlockSpec((D, tn), lambda i, j: (0, j)),
        ],
        out_specs=pl.BlockSpec((tm, tn), lambda i, j: (i, j)),
        out_shape=jax.ShapeDtypeStruct((T, N), bf16),
        scratch_shapes=[pltpu.VMEM((tm, D), bf16)],
        compiler_params=_params(("parallel", "arbitrary")),
        name=name,
    )(x2, norm_w.reshape(1, D), w)


def _xattn_router_kernel(h_ref, nq_ref, wq_ref, k_ref, v_ref, wo_ref, nm_ref, wr_ref, b_ref,
                         h2_ref, hn_ref, rt_ref, cnt_ref, run_ref):
    @pl.when((pl.program_id(0) == 0) & (pl.program_id(1) == 0))
    def _():
        run_ref[...] = jnp.zeros_like(run_ref)

    h1 = h_ref[...]
    D = h1.shape[1]
    dh = D // XA_HEADS
    xn = (h1 * _rms_scale(h1) * nq_ref[...]).astype(bf16)
    q = jnp.dot(xn, wq_ref[...], preferred_element_type=f32).astype(bf16)
    acc = h1
    for h in range(XA_HEADS):
        sl = slice(h * dh, (h + 1) * dh)
        s = _dot_nt(q[:, sl], k_ref[:, sl]) * (dh ** -0.5)
        p = jnp.exp(s - jnp.max(s, axis=-1, keepdims=True))
        p = p / jnp.sum(p, axis=-1, keepdims=True)
        oh = _dot(p, v_ref[:, sl])
        acc = acc + _dot(oh, wo_ref[sl, :])
    h2_ref[...] = acc
    _route(acc, nm_ref, wr_ref, b_ref, hn_ref, rt_ref, cnt_ref, run_ref)


def _xattn_router(h1, norm_q, w_q, kv, w_o, norm_moe, w_r, bias_row, B, S, M, tm):
    T, D = h1.shape
    nt = S // tm
    g = D // (2 * LANES)
    tok = lambda b, i: (b * nt + i, 0)
    const = lambda b, i: (0, 0)
    resident = dict(pipeline_mode=pl.Buffered(1))
    return pl.pallas_call(
        _xattn_router_kernel,
        grid=(B, nt),
        in_specs=[
            pl.BlockSpec((tm, D), tok),
            pl.BlockSpec((1, D), const),
            pl.BlockSpec((D, D), const, **resident),
            pl.BlockSpec((M, D), lambda b, i: (b, 0)),
            pl.BlockSpec((M, D), lambda b, i: (b, 1)),
            pl.BlockSpec((D, D), const, **resident),
            pl.BlockSpec((1, D), const),
            pl.BlockSpec((D, LANES), const),
            pl.BlockSpec((1, LANES), const),
        ],
        out_specs=[
            pl.BlockSpec((tm, D), tok),
            pl.BlockSpec((tm * g, LANES), tok),
            pl.BlockSpec((tm, LANES), tok),
            pl.BlockSpec((1, LANES), const),
        ],
        out_shape=[
            jax.ShapeDtypeStruct((T, D), f32),
            jax.ShapeDtypeStruct((T * g, LANES), jnp.uint32),
            jax.ShapeDtypeStruct((T, LANES), f32),
            jax.ShapeDtypeStruct((1, LANES), f32),
        ],
        scratch_shapes=[pltpu.VMEM((1, LANES), f32)],
        compiler_params=_params(("arbitrary", "arbitrary")),
        name="xattn_router",
    )(h1, norm_q.reshape(1, D), w_q, kv, kv, w_o, norm_moe.reshape(1, D), w_r, bias_row)


def _pack_pair(a, b):
    au = lax.bitcast_convert_type(a.astype(f32), jnp.uint32)
    bu = lax.bitcast_convert_type(b.astype(f32), jnp.uint32)
    return (au >> 16) | bu


def _unpack_pair(u):
    a = lax.bitcast_convert_type(u << 16, f32)
    b = lax.bitcast_convert_type(u & jnp.uint32(0xFFFF0000), f32)
    return a, b


def _store_slabs(ref, words):
    n = words.shape[0]
    g = words.shape[1] // LANES
    for c in range(g):
        ref[pl.ds(c, n, stride=g), :] = words[:, c * LANES:(c + 1) * LANES]


def _load_slabs(ref, n, g):
    return [ref[pl.ds(c, n, stride=g), :] for c in range(g)]


def _route(x, nw_ref, wr_ref, b_ref, hn_ref, o_ref, cnt_ref, run_ref):
    xn = x * _rms_scale(x) * nw_ref[...]
    xhi = xn.astype(bf16)
    logits = jnp.dot(xhi, wr_ref[...], preferred_element_type=f32) + b_ref[...]
    lane = lax.broadcasted_iota(jnp.int32, logits.shape, 1).astype(f32)
    neg = -jnp.inf
    big = float(LANES)

    def first_max(vals):
        m = jnp.max(vals, axis=-1, keepdims=True)
        return m, jnp.min(jnp.where(vals == m, lane, big), axis=-1, keepdims=True)

    is_group = lane < N_GROUPS
    gmax, gsel = first_max(jnp.where(is_group, logits, neg))
    gsum = jnp.sum(jnp.where(is_group, jnp.exp(logits - gmax), 0.0), axis=-1, keepdims=True)
    g_w = 1.0 / gsum
    lo = N_GROUPS + EXP_PER_GROUP * gsel
    in_group = (lane >= lo) & (lane < lo + EXP_PER_GROUP)
    el = jnp.where(in_group, logits, neg)
    l1, i1 = first_max(el)
    l2, i2 = first_max(jnp.where(lane == i1, neg, el))
    esum = jnp.sum(jnp.where(in_group, jnp.exp(logits - l1), 0.0), axis=-1, keepdims=True)
    p1 = 1.0 / esum
    p2 = jnp.exp(l2 - l1) / esum
    gate1 = g_w * p1 / (p1 + p2)
    gate2 = g_w * p2 / (p1 + p2)

    tm = x.shape[0]
    chosen = jnp.where((lane == i1) | (lane == i2), 1.0, 0.0)
    earlier = (lax.broadcasted_iota(jnp.int32, (tm, tm), 0) > lax.broadcasted_iota(jnp.int32, (tm, tm), 1))
    before = _dot(jnp.where(earlier, 1.0, 0.0), chosen) + run_ref[...]
    rank1 = jnp.sum(jnp.where(lane == i1, before, 0.0), axis=-1, keepdims=True)
    rank2 = jnp.sum(jnp.where(lane == i2, before, 0.0), axis=-1, keepdims=True)
    run_ref[...] = run_ref[...] + jnp.sum(chosen, axis=0, keepdims=True)
    cnt_ref[...] = run_ref[...]

    cols = (i1 - N_GROUPS, i2 - N_GROUPS, gate1, gate2, rank1, rank2)
    out = jnp.zeros_like(logits)
    for c, val in enumerate(cols):
        out = jnp.where(lane == c, val, out)
    o_ref[...] = out
    half = x.shape[1] // 2
    _store_slabs(hn_ref, _pack_pair(xhi[:, :half], xhi[:, half:]))


def _dispatch_kernel(dest_ref, pad_lo_ref, pad_n_ref, zcnt_ref, tail_ref, hn_ref, xs_hbm, zeros_ref, sem,
                     zsem, *, tb, g, bm):
    i = pl.program_id(0)
    base = i * tb
    sizes = [1 << b for b in range(bm.bit_length() - 1)]

    def zero_copy(first, p):
        return pltpu.make_async_copy(zeros_ref.at[pl.ds(0, p * g)], xs_hbm.at[pl.ds(first, p * g)], zsem)

    @pl.when(i == 0)
    def _():
        zeros_ref[...] = jnp.zeros_like(zeros_ref)

        def per_expert(e, carry):
            row = pad_lo_ref[e]
            n = pad_n_ref[e]
            for p in sizes:
                hit = (n & p) != 0

                @pl.when(hit)
                def _(row=row, p=p):
                    zero_copy(pl.multiple_of(row * g, g), p).start()

                row = row + jnp.where(hit, p, 0)
            return carry

        lax.fori_loop(0, N_EXPERTS, per_expert, 0)

        def per_tail_chunk(c, carry):
            zero_copy(pl.multiple_of((tail_ref[0] + c * sizes[-1]) * g, g), sizes[-1]).start()
            return carry

        lax.fori_loop(0, tail_ref[1], per_tail_chunk, 0)

    def body(r, carry):
        src = hn_ref.at[pl.ds(pl.multiple_of(r * g, g), g)]
        for kk in range(2):
            d = dest_ref[2 * (base + r) + kk]
            pltpu.make_async_copy(src, xs_hbm.at[pl.ds(pl.multiple_of(d * g, g), g)], sem.at[kk]).start()
        return carry

    lax.fori_loop(0, tb, body, 0, unroll=8)
    for kk in range(2):
        pltpu.make_async_copy(hn_ref, hn_ref, sem.at[kk]).wait()

    @pl.when(i == pl.num_programs(0) - 1)
    def _():
        for b, p in enumerate(sizes):
            def wait_one(_, carry, p=p):
                zero_copy(0, p).wait()
                return carry
            lax.fori_loop(0, zcnt_ref[b], wait_one, 0)


def _dispatch(hn, dest, pads, P, tb, g, bm):
    T = hn.shape[0] // g
    grid_spec = pltpu.PrefetchScalarGridSpec(
        num_scalar_prefetch=5,
        grid=(T // tb,),
        in_specs=[pl.BlockSpec((tb * g, LANES), lambda i, *_: (i, 0))],
        out_specs=pl.BlockSpec(memory_space=pl.ANY),
        scratch_shapes=[pltpu.VMEM((bm // 2 * g, LANES), jnp.uint32),
                        pltpu.SemaphoreType.DMA((2,)), pltpu.SemaphoreType.DMA(())],
    )
    return pl.pallas_call(
        functools.partial(_dispatch_kernel, tb=tb, g=g, bm=bm),
        grid_spec=grid_spec,
        out_shape=jax.ShapeDtypeStruct((P * g, LANES), jnp.uint32),
        compiler_params=_params(("arbitrary",)),
        name="dispatch",
    )(dest, *pads, hn)


def _ffn_kernel(blk_e_ref, nused_ref, x_ref, wg_ref, wu_ref, wd_ref, y_ref, *, bm):
    i = pl.program_id(0)
    half = wg_ref.shape[0] // 2

    @pl.when(i < nused_ref[0])
    def _():
        pairs = [_unpack_pair(w) for w in _load_slabs(x_ref, bm, half // LANES)]
        a = jnp.concatenate([p[0].astype(bf16) for p in pairs], axis=1)
        b = jnp.concatenate([p[1].astype(bf16) for p in pairs], axis=1)
        g = (jnp.dot(a, wg_ref[0:half, :], preferred_element_type=f32)
             + jnp.dot(b, wg_ref[half:, :], preferred_element_type=f32))
        u = (jnp.dot(a, wu_ref[0:half, :], preferred_element_type=f32)
             + jnp.dot(b, wu_ref[half:, :], preferred_element_type=f32))
        hmid = (_silu(g) * u).astype(bf16)
        y = jnp.dot(hmid, wd_ref[...], preferred_element_type=f32)
        _store_slabs(y_ref, _pack_pair(y[:, :half].astype(bf16), y[:, half:].astype(bf16)))

    @pl.when(i >= nused_ref[0])
    def _():
        y_ref[...] = jnp.zeros_like(y_ref)


def _ffn(xs, wg, wu, wd, blk_e, nused, bm):
    D, F = wg.shape[1:]
    g = D // (2 * LANES)
    NB = blk_e.shape[0]
    grid_spec = pltpu.PrefetchScalarGridSpec(
        num_scalar_prefetch=2,
        grid=(NB,),
        in_specs=[
            pl.BlockSpec((bm * g, LANES), lambda i, be, n: (jnp.minimum(i, n[0] - 1), 0)),
            pl.BlockSpec((None, D, F), lambda i, be, n: (be[i], 0, 0)),
            pl.BlockSpec((None, D, F), lambda i, be, n: (be[i], 0, 0)),
            pl.BlockSpec((None, F, D), lambda i, be, n: (be[i], 0, 0)),
        ],
        out_specs=pl.BlockSpec((bm * g, LANES), lambda i, be, n: (i, 0)),
    )
    return pl.pallas_call(
        functools.partial(_ffn_kernel, bm=bm),
        grid_spec=grid_spec,
        out_shape=jax.ShapeDtypeStruct(xs.shape, jnp.uint32),
        compiler_params=_params(("arbitrary",)),
        name="ffn",
    )(blk_e, nused, xs, wg, wu, wd)


def _combine_kernel(dest_ref, y_hbm, h_ref, rt_ref, nw_ref, o_ref, ybuf, sem):
    i = pl.program_id(0)
    n = pl.num_programs(0)
    tm = h_ref.shape[0]
    slot = i % 2

    D = h_ref.shape[1]
    half = D // 2
    g = half // LANES

    def issue(blk, slot_):
        def body(r, carry):
            a = 2 * (blk * tm + r)
            for kk in range(2):
                d = dest_ref[a + kk]
                pltpu.make_async_copy(y_hbm.at[pl.ds(pl.multiple_of(d * g, g), g)],
                                      ybuf.at[slot_, kk, pl.ds(pl.multiple_of(r * g, g), g)],
                                      sem.at[slot_]).start()
            return carry
        lax.fori_loop(0, tm, body, 0, unroll=8)

    @pl.when(i == 0)
    def _():
        issue(0, 0)

    @pl.when(i + 1 < n)
    def _():
        issue(i + 1, 1 - slot)

    pltpu.make_async_copy(ybuf.at[slot], ybuf.at[slot], sem.at[slot]).wait()
    rt = rt_ref[...]
    g1 = rt[:, 2:3]
    g2 = rt[:, 3:4]
    y1 = [_unpack_pair(w) for w in _load_slabs(ybuf.at[slot, 0], tm, g)]
    y2 = [_unpack_pair(w) for w in _load_slabs(ybuf.at[slot, 1], tm, g)]
    cols = [slice(c * LANES, (c + 1) * LANES) for c in range(2 * g)]
    hs = [h_ref[:, cols[c]] + g1 * y1[c % g][c // g] + g2 * y2[c % g][c // g] for c in range(2 * g)]
    ms = sum(jnp.sum(x * x, axis=-1, keepdims=True) for x in hs) / D
    scale = lax.rsqrt(ms + EPS)
    for c in range(2 * g):
        o_ref[:, cols[c]] = hs[c] * scale * nw_ref[:, cols[c]]


def _combine(y, h2, rout, dest, norm_w, tm):
    T, D = h2.shape
    g = D // (2 * LANES)
    grid_spec = pltpu.PrefetchScalarGridSpec(
        num_scalar_prefetch=1,
        grid=(T // tm,),
        in_specs=[
            pl.BlockSpec(memory_space=pl.ANY),
            pl.BlockSpec((tm, D), lambda i, d: (i, 0)),
            pl.BlockSpec((tm, LANES), lambda i, d: (i, 0)),
            pl.BlockSpec((1, D), lambda i, d: (0, 0)),
        ],
        out_specs=pl.BlockSpec((tm, D), lambda i, d: (i, 0)),
        scratch_shapes=[pltpu.VMEM((2, 2, tm * g, LANES), jnp.uint32), pltpu.SemaphoreType.DMA((2,))],
    )
    return pl.pallas_call(
        _combine_kernel,
        grid_spec=grid_spec,
        out_shape=jax.ShapeDtypeStruct((T, D), f32),
        compiler_params=_params(("arbitrary",)),
        name="combine",
    )(dest, y, h2, rout, norm_w.reshape(1, D))


def _dispatch_plan(rout, cnt, bm):
    T = rout.shape[0]
    NB = 2 * T // bm + N_EXPERTS
    e_tk = rout[:, 0:2].astype(jnp.int32)
    rank_tk = rout[:, 4:6].astype(jnp.int32)
    counts = cnt[0, N_GROUPS:N_GROUPS + N_EXPERTS].astype(jnp.int32)
    padded = (counts + bm - 1) // bm * bm
    pend = jnp.cumsum(padded)
    pstart = pend - padded
    first_row = jnp.arange(NB, dtype=jnp.int32) * bm
    blk_e = jnp.minimum(jnp.sum((pend[None, :] <= first_row[:, None]).astype(jnp.int32), axis=1),
                        N_EXPERTS - 1)
    nused = (pend[-1] // bm).astype(jnp.int32).reshape(1)
    onehot = e_tk[:, :, None] == jnp.arange(N_EXPERTS, dtype=jnp.int32)
    dest = jnp.sum(jnp.where(onehot, pstart, 0), axis=-1) + rank_tk
    pad_n = padded - counts
    nbits = bm.bit_length() - 1
    bits = (pad_n[:, None] >> jnp.arange(nbits, dtype=jnp.int32)) & 1
    tail_chunks = 2 * (NB - nused[0])
    zcnt = jnp.sum(bits, axis=0) + jnp.where(jnp.arange(nbits) == nbits - 1, tail_chunks, 0)
    pads = ((pstart + counts).astype(jnp.int32), pad_n.astype(jnp.int32), zcnt.astype(jnp.int32),
            jnp.stack([pend[-1], tail_chunks]).astype(jnp.int32))
    return blk_e, nused, dest.reshape(2 * T).astype(jnp.int32), pads


def _tile(n, want):
    t = min(n, want)
    while n % t:
        t //= 2
    return t


def kernel(x, mem, positions, norm_mix_w, w_in, dn_conv_w, dn_a_log, dn_dt_bias, ret_gn_w, dn_norm_w,
           w_out, norm_xq_w, norm_mem_w, w_xq, w_xkv, w_xo, norm_moe_w, w_group_router, b_group_router,
           w_expert_router, b_expert_router, w_gate, w_up, w_down, norm_final_w):
    B, S, D = x.shape
    M = mem.shape[1]
    T = B * S
    depth = w_in.shape[0]
    n_main = 8 * HEADS * HD
    h = x.reshape(T, D)
    pos_col = positions.astype(f32).reshape(T, 1)
    mem2 = mem.reshape(B * M, D)
    hg = HEADS
    bm = 256
    for l in range(depth):
        w_main = w_in[l][:, :n_main].astype(bf16)
        w_ba = jnp.pad(w_in[l][:, n_main:], ((0, 0), (0, LANES - 2 * HEADS))).astype(bf16)
        E, _, F = w_gate[l].shape
        expert_w = [w_gate[l].reshape(E * D, F), w_up[l].reshape(E * D, F), w_down[l].reshape(E * F, D)]
        proj, ba, wg, wu, wd = _in_proj(h, norm_mix_w[l], w_main, w_ba, expert_w, _tile(T, 1024), 1024)
        wg, wu, wd = wg.reshape(E, D, F), wu.reshape(E, D, F), wd.reshape(E, F, D)
        mix_r = _retention(proj, pos_col, ret_gn_w[l], B, S, hg)
        mix_d = _gdn(proj, ba, dn_conv_w[l], dn_a_log[l], dn_dt_bias[l], dn_norm_w[l], B, S, hg)
        h = _mm_res(mix_r, mix_d, w_out[l].astype(bf16), h, _tile(T, 1024), _tile(D, 1024))
        kv = _norm_mm(mem2, norm_mem_w[l], w_xkv[l].astype(bf16), _tile(B * M, 512), _tile(2 * D, 1024), "xkv")
        w_r = jnp.pad(jnp.concatenate([w_group_router[l], w_expert_router[l]], axis=1),
                      ((0, 0), (0, LANES - N_GROUPS - N_EXPERTS))).astype(bf16)
        b_r = jnp.pad(jnp.concatenate([b_group_router[l], b_expert_router[l]]),
                      (0, LANES - N_GROUPS - N_EXPERTS)).reshape(1, LANES).astype(f32)
        h, hn, rout, cnt = _xattn_router(h, norm_xq_w[l], w_xq[l].astype(bf16), kv, w_xo[l].astype(bf16),
                                         norm_moe_w[l], w_r, b_r, B, S, M, _tile(S, 512))
        blk_e, nused, dest, pads = _dispatch_plan(rout, cnt, bm)
        xs = _dispatch(hn, dest, pads, blk_e.shape[0] * bm, _tile(T, 1024), D // (2 * LANES), bm)
        y = _ffn(xs, wg, wu, wd, blk_e, nused, bm)
        if l + 1 < depth:
            raise NotImplementedError("the fused MoE combine applies the final norm: depth 1 only")
        h = _combine(y, h, rout, dest, norm_final_w, _tile(T, 256))
    return h.reshape(B, S, D)
```

```python
import functools

import numpy as np
import jax
import jax.numpy as jnp
from jax import lax
from jax.experimental import pallas as pl
from jax.experimental.pallas import tpu as pltpu

f32 = jnp.float32
bf16 = jnp.bfloat16

EPS = 1e-6
HEADS = 8
HD = 128
LANES = 128
XA_HEADS = 4
CONV_W = 4
ROPE_BASE = 10000.0
N_GROUPS = 4
EXP_PER_GROUP = 8
N_EXPERTS = N_GROUPS * EXP_PER_GROUP
RET_CHUNK = 256
DN_CHUNK = 128
VMEM_LIMIT = 56 * 1024 * 1024


def _params(sem, vmem=VMEM_LIMIT):
    return pltpu.CompilerParams(dimension_semantics=sem, vmem_limit_bytes=vmem)


def _dot(a, b):
    return jnp.dot(a.astype(bf16), b.astype(bf16), preferred_element_type=f32)


def _dot_nt(a, b):
    return lax.dot_general(a.astype(bf16), b.astype(bf16), (((1,), (1,)), ((), ())),
                           preferred_element_type=f32)


def _dot_tn(a, b):
    return lax.dot_general(a.astype(bf16), b.astype(bf16), (((0,), (0,)), ((), ())),
                           preferred_element_type=f32)


def _split3(a):
    hi = a.astype(bf16)
    r = a - hi.astype(f32)
    mid = r.astype(bf16)
    lo = (r - mid.astype(f32)).astype(bf16)
    return hi, mid, lo


def _sigmoid(x):
    return 1.0 / (1.0 + jnp.exp(-x))


def _silu(x):
    hx = 0.5 * x
    return hx + hx * jnp.tanh(hx)


def _rms_scale(x):
    return lax.rsqrt(jnp.mean(x * x, axis=-1, keepdims=True) + EPS)


def _side_cast_step(step, nsteps, srcs, dsts, stage_in, stage_out, sem_in, sem_out):
    slot = step % 2
    mats = range(len(srcs))

    def in_copy(m, st, sl):
        r = stage_in[m].shape[1]
        rows = pl.ds(pl.multiple_of(st * r, r), r)
        return pltpu.make_async_copy(srcs[m].at[rows], stage_in[m].at[sl], sem_in.at[m, sl])

    def out_copy(m, st, sl):
        r = stage_out[m].shape[1]
        rows = pl.ds(pl.multiple_of(st * r, r), r)
        return pltpu.make_async_copy(stage_out[m].at[sl], dsts[m].at[rows], sem_out.at[m, sl])

    @pl.when(step == 0)
    def _():
        for m in mats:
            in_copy(m, 0, 0).start()

    @pl.when(step + 1 < nsteps)
    def _():
        for m in mats:
            in_copy(m, step + 1, 1 - slot).start()

    for m in mats:
        in_copy(m, step, slot).wait()

    @pl.when(step >= 2)
    def _():
        for m in mats:
            out_copy(m, step - 2, slot).wait()

    for m in mats:
        stage_out[m][slot] = stage_in[m][slot].astype(bf16)
    for m in mats:
        out_copy(m, step, slot).start()

    @pl.when(step == nsteps - 1)
    def _():
        for m in mats:
            if nsteps >= 2:
                out_copy(m, step - 1, 1 - slot).wait()
            out_copy(m, step, slot).wait()


def _in_proj_kernel(x_ref, nw_ref, w_ref, wba_ref, *rest, n_side):
    srcs, rest = rest[:n_side], rest[n_side:]
    o_ref, ba_ref = rest[:2]
    dsts, rest = rest[2:2 + n_side], rest[2 + n_side:]
    xn_ref = rest[0]
    stage_in, stage_out = rest[1:1 + n_side], rest[1 + n_side:1 + 2 * n_side]
    sem_in, sem_out = rest[1 + 2 * n_side:]

    step = pl.program_id(0) * pl.num_programs(1) + pl.program_id(1)
    nsteps = pl.num_programs(0) * pl.num_programs(1)
    _side_cast_step(step, nsteps, srcs, dsts, stage_in, stage_out, sem_in, sem_out)

    @pl.when(pl.program_id(1) == 0)
    def _():
        x = x_ref[...]
        xn = (x * _rms_scale(x) * nw_ref[...]).astype(bf16)
        xn_ref[...] = xn
        ba_ref[...] = jnp.dot(xn, wba_ref[...], preferred_element_type=f32)

    acc = jnp.dot(xn_ref[...], w_ref[...], preferred_element_type=f32)
    for c in range(o_ref.shape[0]):
        o_ref[c] = acc[:, c * LANES:(c + 1) * LANES].astype(bf16)


def _in_proj(x2, norm_w, w_main, w_ba, side, tm, tn):
    T, D = x2.shape
    N = w_main.shape[1]
    nsteps = (T // tm) * (N // tn)
    rows = [a.shape[0] // nsteps for a in side]
    assert all(a.shape[0] == r * nsteps and r % 16 == 0 for a, r in zip(side, rows))
    any_spec = pl.BlockSpec(memory_space=pl.ANY)
    return pl.pallas_call(
        functools.partial(_in_proj_kernel, n_side=len(side)),
        grid=(T // tm, N // tn),
        in_specs=[
            pl.BlockSpec((tm, D), lambda i, j: (i, 0)),
            pl.BlockSpec((1, D), lambda i, j: (0, 0)),
            pl.BlockSpec((D, tn), lambda i, j: (0, j)),
            pl.BlockSpec((D, LANES), lambda i, j: (0, 0)),
        ] + [any_spec] * len(side),
        out_specs=[
            pl.BlockSpec((tn // LANES, tm, LANES), lambda i, j: (j, i, 0)),
            pl.BlockSpec((tm, LANES), lambda i, j: (i, 0)),
        ] + [any_spec] * len(side),
        out_shape=[
            jax.ShapeDtypeStruct((N // LANES, T, LANES), bf16),
            jax.ShapeDtypeStruct((T, LANES), f32),
        ] + [jax.ShapeDtypeStruct(a.shape, bf16) for a in side],
        scratch_shapes=[pltpu.VMEM((tm, D), bf16)]
        + [pltpu.VMEM((2, r, a.shape[1]), f32) for a, r in zip(side, rows)]
        + [pltpu.VMEM((2, r, a.shape[1]), bf16) for a, r in zip(side, rows)]
        + [pltpu.SemaphoreType.DMA((len(side), 2)), pltpu.SemaphoreType.DMA((len(side), 2))],
        compiler_params=_params(("arbitrary", "arbitrary")),
        name="in_proj",
    )(x2, norm_w.reshape(1, D), w_main, w_ba, *side)


def _retention_kernel(q_ref, k_ref, v_ref, g_ref, pos_ref, inv_ref, sgn_ref, dmask_ref, qdec_ref,
                      kdec_ref, cdec_ref, gnw_ref, o_ref, state_ref):
    @pl.when(pl.program_id(2) == 0)
    def _():
        state_ref[...] = jnp.zeros_like(state_ref)

    C = pos_ref.shape[0]
    lane = lax.broadcasted_iota(jnp.int32, (C // 2, HD), 1)
    low = lane < HD // 2
    ang = jnp.where(low, pos_ref[0:C // 2, :], pos_ref[C // 2:, :]) * inv_ref[...]
    cos_p = jnp.cos(ang)
    sin_p = jnp.sin(ang)

    def spread(t):
        swapped = pltpu.roll(t, HD // 2, 1)
        return jnp.concatenate([jnp.where(low, t, swapped), jnp.where(low, swapped, t)], axis=0)

    cos2 = spread(cos_p)
    sin2 = spread(sin_p) * sgn_ref[...]
    heads = range(q_ref.shape[0])

    def rope(x):
        return x * cos2 + pltpu.roll(x, HD // 2, 1) * sin2

    qr = [rope(q_ref[i].astype(f32)) for i in heads]
    kr = [rope(k_ref[i].astype(f32)) * (HD ** -0.5) for i in heads]
    state = [state_ref[i] for i in heads]
    s = [_dot_nt(qr[i], kr[i]) * dmask_ref[i] for i in heads]
    cross = [_dot(qr[i], state[i]) * qdec_ref[i] for i in heads]
    upd = [_dot_tn(kr[i] * kdec_ref[i], v_ref[i]) for i in heads]
    o = [_dot(s[i], v_ref[i]) + cross[i] for i in heads]
    for i in heads:
        state_ref[i] = state[i] * cdec_ref[i] + upd[i]
        d = o[i] - jnp.mean(o[i], axis=-1, keepdims=True)
        y = d * lax.rsqrt(jnp.mean(d * d, axis=-1, keepdims=True) + EPS) * gnw_ref[i]
        y = y * _silu(g_ref[i].astype(f32))
        o_ref[:, i * HD:(i + 1) * HD] = y.astype(bf16)


def _retention_tables(C):
    h = np.arange(HEADS, dtype=np.float64)
    log_gamma = np.log1p(-np.exp2(-5.0 - h))
    idx = np.arange(C, dtype=np.float64)
    rel = idx[:, None] - idx[None, :]
    dmask = np.where(rel >= 0, np.exp(log_gamma[:, None, None] * np.where(rel >= 0, rel, 0.0)), 0.0)
    qdec = np.exp(log_gamma[:, None] * (idx + 1.0))
    kdec = np.exp(log_gamma[:, None] * (C - 1.0 - idx))
    cdec = np.exp(log_gamma * C)
    rep = lambda a: np.broadcast_to(a[..., None], a.shape + (LANES,))
    return (jnp.asarray(dmask, f32), jnp.asarray(rep(qdec), f32), jnp.asarray(rep(kdec), f32),
            jnp.asarray(np.broadcast_to(cdec[:, None, None], (HEADS, 1, LANES)), f32))


def _retention(proj, pos_col, gn_w, B, S, hg):
    C = RET_CHUNK
    NC = S // C
    T = B * S
    G = HEADS // hg
    half = HD // 2
    inv = ROPE_BASE ** (-np.arange(half, dtype=np.float32) / half)
    inv2 = jnp.asarray(np.concatenate([inv, inv]).reshape(1, HD), f32)
    sgn = jnp.asarray(np.concatenate([-np.ones(half), np.ones(half)]).reshape(1, HD), f32)
    dmask, qdec, kdec, cdec = _retention_tables(C)

    def slab(off):
        return pl.BlockSpec((hg, C, HD), lambda b, g, n: (off // hg + g, b * NC + n, 0))

    def table(shape):
        return pl.BlockSpec((hg,) + shape, lambda b, g, n: (g,) + (0,) * len(shape))

    return pl.pallas_call(
        _retention_kernel,
        grid=(B, G, NC),
        in_specs=[
            slab(0), slab(HEADS), slab(2 * HEADS), slab(3 * HEADS),
            pl.BlockSpec((C, 1), lambda b, g, n: (b * NC + n, 0)),
            pl.BlockSpec((1, HD), lambda b, g, n: (0, 0)),
            pl.BlockSpec((1, HD), lambda b, g, n: (0, 0)),
            table((C, C)), table((C, LANES)), table((C, LANES)), table((1, LANES)), table((1, HD)),
        ],
        out_specs=pl.BlockSpec((C, hg * HD), lambda b, g, n: (b * NC + n, g)),
        out_shape=jax.ShapeDtypeStruct((T, HEADS * HD), bf16),
        scratch_shapes=[pltpu.VMEM((hg, HD, HD), f32)],
        compiler_params=_params(("parallel", "parallel", "arbitrary")),
        name="retention",
    )(proj, proj, proj, proj, pos_col, inv2, sgn, dmask, qdec, kdec, cdec,
      gn_w.reshape(HEADS, 1, HD))


def _gdn_kernel(q_ref, k_ref, v_ref, z_ref, ba_ref, cw_ref, arow_ref, dtrow_ref, nw_ref, o_ref,
                state_ref, tail_ref, win_ref, *, hg):
    C = q_ref.shape[1]
    first = pl.program_id(2) == 0

    @pl.when(first)
    def _():
        state_ref[...] = jnp.zeros_like(state_ref)
        tail_ref[...] = jnp.zeros_like(tail_ref)

    row = lax.broadcasted_iota(jnp.int32, (C, C), 0)
    col = lax.broadcasted_iota(jnp.int32, (C, C), 1)
    causal = row >= col
    strict = row > col
    same8 = (row // 8) == (col // 8)
    level_masks = []
    s = 8
    while s < C:
        level_masks.append(((row // (2 * s)) == (col // (2 * s))) & ((row // s) != (col // s)))
        s *= 2
    eye = jnp.where(row == col, 1.0, 0.0).astype(f32)
    tri = jnp.where(causal, 1.0, 0.0).astype(bf16)

    ba = ba_ref[...]
    beta_all = _sigmoid(ba)
    xa = ba + dtrow_ref[...]
    softplus = jnp.maximum(xa, 0.0) + jnp.log(1.0 + jnp.exp(-jnp.abs(xa)))
    glog = -jnp.exp(arow_ref[...]) * softplus
    g_hi, g_mid, g_lo = _split3(glog)
    gcum = (jnp.dot(tri, g_hi, preferred_element_type=f32)
            + jnp.dot(tri, g_mid, preferred_element_type=f32)
            + jnp.dot(tri, g_lo, preferred_element_type=f32))
    gcum_t = gcum.T
    lane = lax.broadcasted_iota(jnp.int32, (C, LANES), 1)
    sub = lax.broadcasted_iota(jnp.int32, (LANES, C), 0)

    def conv_silu(x_ref, kind, i):
        slot = kind * hg + i
        win_ref[slot, 0:8, :] = tail_ref[slot]
        win_ref[slot, 8:8 + C, :] = x_ref[i].astype(f32)
        w = cw_ref[kind, i]
        y = win_ref[slot, 8:8 + C, :] * w[CONV_W - 1:CONV_W, :]
        for t in range(1, CONV_W):
            y = y + win_ref[slot, 8 - t:8 - t + C, :] * w[CONV_W - 1 - t:CONV_W - t, :]
        tail_ref[slot] = win_ref[slot, C:C + 8, :]
        return _silu(y)

    heads = range(hg)
    hd = [pl.program_id(1) * hg + i for i in heads]
    q = [conv_silu(q_ref, 0, i) for i in heads]
    k = [conv_silu(k_ref, 1, i) for i in heads]
    v = [conv_silu(v_ref, 2, i) for i in heads]
    q = [x * lax.rsqrt(jnp.sum(x * x, axis=-1, keepdims=True) + EPS) * (HD ** -0.5) for x in q]
    k = [x * lax.rsqrt(jnp.sum(x * x, axis=-1, keepdims=True) + EPS) for x in k]

    bcol = [jnp.sum(jnp.where(lane == h, beta_all, 0.0), axis=1, keepdims=True) for h in hd]
    gcol = [jnp.sum(jnp.where(lane == h + HEADS, gcum, 0.0), axis=1, keepdims=True) for h in hd]
    grow = [jnp.sum(jnp.where(sub == h + HEADS, gcum_t, 0.0), axis=0, keepdims=True) for h in hd]
    glast = [g[C - 1:C, :] for g in gcol]
    gam = [jnp.where(causal, jnp.exp(jnp.where(causal, gc - gr, 0.0)), 0.0) for gc, gr in zip(gcol, grow)]
    egc = [jnp.exp(g) for g in gcol]

    kk = [_dot_nt(x, x) for x in k]
    qk = [_dot_nt(x, y) for x, y in zip(q, k)]
    a = [jnp.where(strict, m * g * b, 0.0) for m, g, b in zip(kk, gam, bcol)]
    qk = [m * g for m, g in zip(qk, gam)]

    d = [jnp.where(same8, m, 0.0) for m in a]
    d2 = [_dot(m, m) for m in d]
    d3 = [_dot(m, m2) for m, m2 in zip(d, d2)]
    d4 = [_dot(m2, m2) for m2 in d2]
    x = [eye - m + m2 - m3 for m, m2, m3 in zip(d, d2, d3)]
    x = [xi + _dot(xi, m4) for xi, m4 in zip(x, d4)]
    for mask in level_masks:
        t = [_dot(xi, jnp.where(mask, m, 0.0)) for xi, m in zip(x, a)]
        x = [xi - _dot(ti, xi) for xi, ti in zip(x, t)]

    rhs = [jnp.concatenate([vi * b, ki * (b * e)], axis=1) for vi, ki, b, e in zip(v, k, bcol, egc)]
    sol = [_dot(xi, r) for xi, r in zip(x, rhs)]
    q_dec = [qi * e for qi, e in zip(q, egc)]
    k_dec = [ki * jnp.exp(gl - gc) for ki, gl, gc in zip(k, glast, gcol)]

    state = [state_ref[i] for i in heads]
    ws = [_dot(jnp.concatenate([s_[:, HD:], qd], axis=0), st) for s_, qd, st in zip(sol, q_dec, state)]
    v_new = [s_[:, :HD] - w_[:C] for s_, w_ in zip(sol, ws)]
    o = [w_[C:] + _dot(m, vn) for w_, m, vn in zip(ws, qk, v_new)]
    upd = [_dot_tn(kd, vn) for kd, vn in zip(k_dec, v_new)]
    for i in heads:
        state_ref[i] = state[i] * jnp.exp(glast[i]) + upd[i]
        y = o[i] * _rms_scale(o[i]) * nw_ref[...]
        y = y * _silu(z_ref[i].astype(f32))
        o_ref[:, i * HD:(i + 1) * HD] = y.astype(bf16)


def _gdn(proj, ba, conv_w, a_log, dt_bias, norm_w, B, S, hg):
    C = DN_CHUNK
    NC = S // C
    T = B * S
    G = HEADS // hg
    cw = conv_w.reshape(CONV_W, 3, HEADS, HD).transpose(1, 2, 0, 3)
    pad = jnp.zeros((LANES - 2 * HEADS,), f32)
    arow = jnp.concatenate([jnp.zeros((HEADS,), f32), a_log.astype(f32), pad]).reshape(1, LANES)
    dtrow = jnp.concatenate([jnp.zeros((HEADS,), f32), dt_bias.astype(f32), pad]).reshape(1, LANES)

    def slab(off):
        return pl.BlockSpec((hg, C, HD), lambda b, g, n: (off // hg + g, b * NC + n, 0))

    row_spec = pl.BlockSpec((1, LANES), lambda b, g, n: (0, 0))
    return pl.pallas_call(
        functools.partial(_gdn_kernel, hg=hg),
        grid=(B, G, NC),
        in_specs=[
            slab(4 * HEADS), slab(5 * HEADS), slab(6 * HEADS), slab(7 * HEADS),
            pl.BlockSpec((C, LANES), lambda b, g, n: (b * NC + n, 0)),
            pl.BlockSpec((3, hg, CONV_W, HD), lambda b, g, n: (0, g, 0, 0)),
            row_spec, row_spec, row_spec,
        ],
        out_specs=pl.BlockSpec((C, hg * HD), lambda b, g, n: (b * NC + n, g)),
        out_shape=jax.ShapeDtypeStruct((T, HEADS * HD), bf16),
        scratch_shapes=[
            pltpu.VMEM((hg, HD, HD), f32),
            pltpu.VMEM((3 * hg, 8, HD), f32),
            pltpu.VMEM((3 * hg, C + 8, HD), f32),
        ],
        compiler_params=_params(("parallel", "parallel", "arbitrary")),
        name="gdn",
    )(proj, proj, proj, proj, ba, cw, arow, dtrow, norm_w.reshape(1, HD))


def _mm_res_kernel(a1_ref, a2_ref, w_ref, r_ref, o_ref):
    a = jnp.concatenate([a1_ref[...], a2_ref[...]], axis=1)
    o_ref[...] = r_ref[...] + jnp.dot(a, w_ref[...], preferred_element_type=f32)


def _mm_res(a1, a2, w, res, tm, tn):
    T, K1 = a1.shape
    K2 = a2.shape[1]
    N = w.shape[1]
    return pl.pallas_call(
        _mm_res_kernel,
        grid=(T // tm, N // tn),
        in_specs=[
            pl.BlockSpec((tm, K1), lambda i, j: (i, 0)),
            pl.BlockSpec((tm, K2), lambda i, j: (i, 0)),
            pl.BlockSpec((K1 + K2, tn), lambda i, j: (0, j)),
            pl.BlockSpec((tm, tn), lambda i, j: (i, j)),
        ],
        out_specs=pl.BlockSpec((tm, tn), lambda i, j: (i, j)),
        out_shape=jax.ShapeDtypeStruct((T, N), f32),
        compiler_params=_params(("parallel", "parallel")),
        name="out_proj",
    )(a1, a2, w, res)


def _norm_mm_kernel(x_ref, nw_ref, w_ref, o_ref, xn_ref):
    @pl.when(pl.program_id(1) == 0)
    def _():
        x = x_ref[...]
        xn_ref[...] = (x * _rms_scale(x) * nw_ref[...]).astype(bf16)

    o_ref[...] = jnp.dot(xn_ref[...], w_ref[...], preferred_element_type=f32).astype(o_ref.dtype)


def _norm_mm(x2, norm_w, w, tm, tn, name):
    T, D = x2.shape
    N = w.shape[1]
    return pl.pallas_call(
        _norm_mm_kernel,
        grid=(T // tm, N // tn),
        in_specs=[
            pl.BlockSpec((tm, D), lambda i, j: (i, 0)),
            pl.BlockSpec((1, D), lambda i, j: (0, 0)),
            pl.BlockSpec((D, tn), lambda i, j: (0, j)),
        ],
        out_specs=pl.BlockSpec((tm, tn), lambda i, j: (i, j)),
        out_shape=jax.ShapeDtypeStruct((T, N), bf16),
        scratch_shapes=[pltpu.VMEM((tm, D), bf16)],
        compiler_params=_params(("parallel", "arbitrary")),
        name=name,
    )(x2, norm_w.reshape(1, D), w)


def _xattn_router_kernel(h_ref, nq_ref, wq_ref, k_ref, v_ref, wo_ref, nm_ref, wr_ref, b_ref,
                         h2_ref, hn_ref, rt_ref, cnt_ref, run_ref):
    @pl.when((pl.program_id(0) == 0) & (pl.program_id(1) == 0))
    def _():
        run_ref[...] = jnp.zeros_like(run_ref)

    h1 = h_ref[...]
    D = h1.shape[1]
    dh = D // XA_HEADS
    xn = (h1 * _rms_scale(h1) * nq_ref[...]).astype(bf16)
    q = jnp.dot(xn, wq_ref[...], preferred_element_type=f32).astype(bf16)
    heads = []
    for h in range(XA_HEADS):
        sl = slice(h * dh, (h + 1) * dh)
        s = _dot_nt(q[:, sl], k_ref[:, sl]) * (dh ** -0.5)
        p = jnp.exp(s - jnp.max(s, axis=-1, keepdims=True))
        p = p / jnp.sum(p, axis=-1, keepdims=True)
        heads.append(_dot(p, v_ref[:, sl]).astype(bf16))
    h2 = h1 + jnp.dot(jnp.concatenate(heads, axis=1), wo_ref[...], preferred_element_type=f32)
    h2_ref[...] = h2
    _route(h2, nm_ref, wr_ref, b_ref, hn_ref, rt_ref, cnt_ref, run_ref)


def _xattn_router(h1, norm_q, w_q, kv, w_o, norm_moe, w_r, bias_row, B, S, M, tm):
    T, D = h1.shape
    nt = S // tm
    g = D // (2 * LANES)
    tok = lambda b, i: (b * nt + i, 0)
    const = lambda b, i: (0, 0)
    resident = dict(pipeline_mode=pl.Buffered(1))
    return pl.pallas_call(
        _xattn_router_kernel,
        grid=(B, nt),
        in_specs=[
            pl.BlockSpec((tm, D), tok),
            pl.BlockSpec((1, D), const),
            pl.BlockSpec((D, D), const, **resident),
            pl.BlockSpec((M, D), lambda b, i: (b, 0)),
            pl.BlockSpec((M, D), lambda b, i: (b, 1)),
            pl.BlockSpec((D, D), const, **resident),
            pl.BlockSpec((1, D), const),
            pl.BlockSpec((D, LANES), const),
            pl.BlockSpec((1, LANES), const),
        ],
        out_specs=[
            pl.BlockSpec((tm, D), tok),
            pl.BlockSpec((tm * g, LANES), tok),
            pl.BlockSpec((tm, LANES), tok),
            pl.BlockSpec((1, LANES), const),
        ],
        out_shape=[
            jax.ShapeDtypeStruct((T, D), f32),
            jax.ShapeDtypeStruct((T * g, LANES), jnp.uint32),
            jax.ShapeDtypeStruct((T, LANES), f32),
            jax.ShapeDtypeStruct((1, LANES), f32),
        ],
        scratch_shapes=[pltpu.VMEM((1, LANES), f32)],
        compiler_params=_params(("arbitrary", "arbitrary")),
        name="xattn_router",
    )(h1, norm_q.reshape(1, D), w_q, kv, kv, w_o, norm_moe.reshape(1, D), w_r, bias_row)


def _pack_pair(a, b):
    au = lax.bitcast_convert_type(a.astype(f32), jnp.uint32)
    bu = lax.bitcast_convert_type(b.astype(f32), jnp.uint32)
    return (au >> 16) | bu


def _unpack_pair(u):
    a = lax.bitcast_convert_type(u << 16, f32)
    b = lax.bitcast_convert_type(u & jnp.uint32(0xFFFF0000), f32)
    return a, b


def _store_slabs(ref, words):
    n = words.shape[0]
    g = words.shape[1] // LANES
    for c in range(g):
        ref[pl.ds(c, n, stride=g), :] = words[:, c * LANES:(c + 1) * LANES]


def _load_slabs(ref, n, g):
    return [ref[pl.ds(c, n, stride=g), :] for c in range(g)]


def _route(x, nw_ref, wr_ref, b_ref, hn_ref, o_ref, cnt_ref, run_ref):
    xn = x * _rms_scale(x) * nw_ref[...]
    xhi = xn.astype(bf16)
    logits = jnp.dot(xhi, wr_ref[...], preferred_element_type=f32) + b_ref[...]
    lane = lax.broadcasted_iota(jnp.int32, logits.shape, 1).astype(f32)
    neg = -jnp.inf
    big = float(LANES)

    def first_max(vals):
        m = jnp.max(vals, axis=-1, keepdims=True)
        return m, jnp.min(jnp.where(vals == m, lane, big), axis=-1, keepdims=True)

    is_group = lane < N_GROUPS
    gmax, gsel = first_max(jnp.where(is_group, logits, neg))
    gsum = jnp.sum(jnp.where(is_group, jnp.exp(logits - gmax), 0.0), axis=-1, keepdims=True)
    g_w = 1.0 / gsum
    lo = N_GROUPS + EXP_PER_GROUP * gsel
    in_group = (lane >= lo) & (lane < lo + EXP_PER_GROUP)
    el = jnp.where(in_group, logits, neg)
    l1, i1 = first_max(el)
    l2, i2 = first_max(jnp.where(lane == i1, neg, el))
    esum = jnp.sum(jnp.where(in_group, jnp.exp(logits - l1), 0.0), axis=-1, keepdims=True)
    p1 = 1.0 / esum
    p2 = jnp.exp(l2 - l1) / esum
    gate1 = g_w * p1 / (p1 + p2)
    gate2 = g_w * p2 / (p1 + p2)

    tm = x.shape[0]
    chosen = jnp.where((lane == i1) | (lane == i2), 1.0, 0.0)
    earlier = (lax.broadcasted_iota(jnp.int32, (tm, tm), 0) > lax.broadcasted_iota(jnp.int32, (tm, tm), 1))
    before = _dot(jnp.where(earlier, 1.0, 0.0), chosen) + run_ref[...]
    rank1 = jnp.sum(jnp.where(lane == i1, before, 0.0), axis=-1, keepdims=True)
    rank2 = jnp.sum(jnp.where(lane == i2, before, 0.0), axis=-1, keepdims=True)
    run_ref[...] = run_ref[...] + jnp.sum(chosen, axis=0, keepdims=True)
    cnt_ref[...] = run_ref[...]

    cols = (i1 - N_GROUPS, i2 - N_GROUPS, gate1, gate2, rank1, rank2)
    out = jnp.zeros_like(logits)
    for c, val in enumerate(cols):
        out = jnp.where(lane == c, val, out)
    o_ref[...] = out
    half = x.shape[1] // 2
    _store_slabs(hn_ref, _pack_pair(xhi[:, :half], xhi[:, half:]))


def _dispatch_kernel(dest_ref, pad_lo_ref, pad_n_ref, zcnt_ref, tail_ref, hn_ref, xs_hbm, zeros_ref, sem,
                     zsem, *, tb, g, bm):
    i = pl.program_id(0)
    base = i * tb
    sizes = [1 << b for b in range(bm.bit_length() - 1)]

    def zero_copy(first, p):
        return pltpu.make_async_copy(zeros_ref.at[pl.ds(0, p * g)], xs_hbm.at[pl.ds(first, p * g)], zsem)

    @pl.when(i == 0)
    def _():
        zeros_ref[...] = jnp.zeros_like(zeros_ref)

        def per_expert(e, carry):
            row = pad_lo_ref[e]
            n = pad_n_ref[e]
            for p in sizes:
                hit = (n & p) != 0

                @pl.when(hit)
                def _(row=row, p=p):
                    zero_copy(pl.multiple_of(row * g, g), p).start()

                row = row + jnp.where(hit, p, 0)
            return carry

        lax.fori_loop(0, N_EXPERTS, per_expert, 0)

        def per_tail_chunk(c, carry):
            zero_copy(pl.multiple_of((tail_ref[0] + c * sizes[-1]) * g, g), sizes[-1]).start()
            return carry

        lax.fori_loop(0, tail_ref[1], per_tail_chunk, 0)

    def body(r, carry):
        src = hn_ref.at[pl.ds(pl.multiple_of(r * g, g), g)]
        for kk in range(2):
            d = dest_ref[2 * (base + r) + kk]
            pltpu.make_async_copy(src, xs_hbm.at[pl.ds(pl.multiple_of(d * g, g), g)], sem.at[kk]).start()
        return carry

    lax.fori_loop(0, tb, body, 0, unroll=8)
    for kk in range(2):
        pltpu.make_async_copy(hn_ref, hn_ref, sem.at[kk]).wait()

    @pl.when(i == pl.num_programs(0) - 1)
    def _():
        for b, p in enumerate(sizes):
            def wait_one(_, carry, p=p):
                zero_copy(0, p).wait()
                return carry
            lax.fori_loop(0, zcnt_ref[b], wait_one, 0)


def _dispatch(hn, dest, pads, P, tb, g, bm):
    T = hn.shape[0] // g
    grid_spec = pltpu.PrefetchScalarGridSpec(
        num_scalar_prefetch=5,
        grid=(T // tb,),
        in_specs=[pl.BlockSpec((tb * g, LANES), lambda i, *_: (i, 0))],
        out_specs=pl.BlockSpec(memory_space=pl.ANY),
        scratch_shapes=[pltpu.VMEM((bm // 2 * g, LANES), jnp.uint32),
                        pltpu.SemaphoreType.DMA((2,)), pltpu.SemaphoreType.DMA(())],
    )
    return pl.pallas_call(
        functools.partial(_dispatch_kernel, tb=tb, g=g, bm=bm),
        grid_spec=grid_spec,
        out_shape=jax.ShapeDtypeStruct((P * g, LANES), jnp.uint32),
        compiler_params=_params(("arbitrary",)),
        name="dispatch",
    )(dest, *pads, hn)


def _ffn_kernel(blk_e_ref, nused_ref, x_ref, wg_ref, wu_ref, wd_ref, y_ref, *, bm):
    i = pl.program_id(0)
    half = wg_ref.shape[0] // 2

    @pl.when(i < nused_ref[0])
    def _():
        pairs = [_unpack_pair(w) for w in _load_slabs(x_ref, bm, half // LANES)]
        x = jnp.concatenate([p[0].astype(bf16) for p in pairs] + [p[1].astype(bf16) for p in pairs], axis=1)
        g = jnp.dot(x, wg_ref[...], preferred_element_type=f32)
        u = jnp.dot(x, wu_ref[...], preferred_element_type=f32)
        hmid = (_silu(g) * u).astype(bf16)
        g_out = half // LANES
        for s in range(g_out):
            ys = jnp.dot(hmid, wd_ref[:, 2 * s * LANES:2 * (s + 1) * LANES], preferred_element_type=f32)
            y_ref[pl.ds(s, bm, stride=g_out), :] = _pack_pair(ys[:, :LANES].astype(bf16),
                                                              ys[:, LANES:].astype(bf16))

    @pl.when(i >= nused_ref[0])
    def _():
        y_ref[...] = jnp.zeros_like(y_ref)


def _ffn(xs, wg, wu, wd, blk_e, nused, bm):
    D, F = wg.shape[1:]
    g = D // (2 * LANES)
    NB = blk_e.shape[0]
    grid_spec = pltpu.PrefetchScalarGridSpec(
        num_scalar_prefetch=2,
        grid=(NB,),
        in_specs=[
            pl.BlockSpec((bm * g, LANES), lambda i, be, n: (jnp.minimum(i, n[0] - 1), 0)),
            pl.BlockSpec((None, D, F), lambda i, be, n: (be[i], 0, 0)),
            pl.BlockSpec((None, D, F), lambda i, be, n: (be[i], 0, 0)),
            pl.BlockSpec((None, F, D), lambda i, be, n: (be[i], 0, 0)),
        ],
        out_specs=pl.BlockSpec((bm * g, LANES), lambda i, be, n: (i, 0)),
    )
    return pl.pallas_call(
        functools.partial(_ffn_kernel, bm=bm),
        grid_spec=grid_spec,
        out_shape=jax.ShapeDtypeStruct(xs.shape, jnp.uint32),
        compiler_params=_params(("arbitrary",)),
        name="ffn",
    )(blk_e, nused, xs, wg, wu, wd)


def _combine_kernel(dest_ref, y_hbm, h_ref, rt_ref, nw_ref, o_ref, ybuf, sem):
    i = pl.program_id(0)
    n = pl.num_programs(0)
    tm = h_ref.shape[0]
    slot = i % 2

    D = h_ref.shape[1]
    half = D // 2
    g = half // LANES

    def issue(blk, slot_):
        def body(r, carry):
            a = 2 * (blk * tm + r)
            for kk in range(2):
                d = dest_ref[a + kk]
                pltpu.make_async_copy(y_hbm.at[pl.ds(pl.multiple_of(d * g, g), g)],
                                      ybuf.at[slot_, kk, pl.ds(pl.multiple_of(r * g, g), g)],
                                      sem.at[slot_]).start()
            return carry
        lax.fori_loop(0, tm, body, 0, unroll=8)

    @pl.when(i == 0)
    def _():
        issue(0, 0)

    @pl.when(i + 1 < n)
    def _():
        issue(i + 1, 1 - slot)

    pltpu.make_async_copy(ybuf.at[slot], ybuf.at[slot], sem.at[slot]).wait()
    rt = rt_ref[...]
    g1 = rt[:, 2:3]
    g2 = rt[:, 3:4]
    y1 = [_unpack_pair(w) for w in _load_slabs(ybuf.at[slot, 0], tm, g)]
    y2 = [_unpack_pair(w) for w in _load_slabs(ybuf.at[slot, 1], tm, g)]
    cols = [slice(c * LANES, (c + 1) * LANES) for c in range(2 * g)]
    hs = [h_ref[:, cols[c]] + g1 * y1[c // 2][c % 2] + g2 * y2[c // 2][c % 2] for c in range(2 * g)]
    ms = sum(jnp.sum(x * x, axis=-1, keepdims=True) for x in hs) / D
    scale = lax.rsqrt(ms + EPS)
    for c in range(2 * g):
        o_ref[:, cols[c]] = hs[c] * scale * nw_ref[:, cols[c]]


def _combine(y, h2, rout, dest, norm_w, tm):
    T, D = h2.shape
    g = D // (2 * LANES)
    grid_spec = pltpu.PrefetchScalarGridSpec(
        num_scalar_prefetch=1,
        grid=(T // tm,),
        in_specs=[
            pl.BlockSpec(memory_space=pl.ANY),
            pl.BlockSpec((tm, D), lambda i, d: (i, 0)),
            pl.BlockSpec((tm, LANES), lambda i, d: (i, 0)),
            pl.BlockSpec((1, D), lambda i, d: (0, 0)),
        ],
        out_specs=pl.BlockSpec((tm, D), lambda i, d: (i, 0)),
        scratch_shapes=[pltpu.VMEM((2, 2, tm * g, LANES), jnp.uint32), pltpu.SemaphoreType.DMA((2,))],
    )
    return pl.pallas_call(
        _combine_kernel,
        grid_spec=grid_spec,
        out_shape=jax.ShapeDtypeStruct((T, D), f32),
        compiler_params=_params(("arbitrary",)),
        name="combine",
    )(dest, y, h2, rout, norm_w.reshape(1, D))


def _dispatch_plan(rout, cnt, bm):
    T = rout.shape[0]
    NB = 2 * T // bm + N_EXPERTS
    e_tk = rout[:, 0:2].astype(jnp.int32)
    rank_tk = rout[:, 4:6].astype(jnp.int32)
    counts = cnt[0, N_GROUPS:N_GROUPS + N_EXPERTS].astype(jnp.int32)
    padded = (counts + bm - 1) // bm * bm
    pend = jnp.cumsum(padded)
    pstart = pend - padded
    first_row = jnp.arange(NB, dtype=jnp.int32) * bm
    blk_e = jnp.minimum(jnp.sum((pend[None, :] <= first_row[:, None]).astype(jnp.int32), axis=1),
                        N_EXPERTS - 1)
    nused = (pend[-1] // bm).astype(jnp.int32).reshape(1)
    onehot = e_tk[:, :, None] == jnp.arange(N_EXPERTS, dtype=jnp.int32)
    dest = jnp.sum(jnp.where(onehot, pstart, 0), axis=-1) + rank_tk
    pad_n = padded - counts
    nbits = bm.bit_length() - 1
    bits = (pad_n[:, None] >> jnp.arange(nbits, dtype=jnp.int32)) & 1
    tail_chunks = 2 * (NB - nused[0])
    zcnt = jnp.sum(bits, axis=0) + jnp.where(jnp.arange(nbits) == nbits - 1, tail_chunks, 0)
    pads = ((pstart + counts).astype(jnp.int32), pad_n.astype(jnp.int32), zcnt.astype(jnp.int32),
            jnp.stack([pend[-1], tail_chunks]).astype(jnp.int32))
    return blk_e, nused, dest.reshape(2 * T).astype(jnp.int32), pads


def _tile(n, want):
    t = min(n, want)
    while n % t:
        t //= 2
    return t


def kernel(x, mem, positions, norm_mix_w, w_in, dn_conv_w, dn_a_log, dn_dt_bias, ret_gn_w, dn_norm_w,
           w_out, norm_xq_w, norm_mem_w, w_xq, w_xkv, w_xo, norm_moe_w, w_group_router, b_group_router,
           w_expert_router, b_expert_router, w_gate, w_up, w_down, norm_final_w):
    B, S, D = x.shape
    M = mem.shape[1]
    T = B * S
    depth = w_in.shape[0]
    n_main = 8 * HEADS * HD
    h = x.reshape(T, D)
    pos_col = positions.astype(f32).reshape(T, 1)
    mem2 = mem.reshape(B * M, D)
    hg = HEADS
    bm = 256
    for l in range(depth):
        w_main = w_in[l][:, :n_main].astype(bf16)
        w_ba = jnp.pad(w_in[l][:, n_main:], ((0, 0), (0, LANES - 2 * HEADS))).astype(bf16)
        E, _, F = w_gate[l].shape
        expert_w = [w_gate[l].reshape(E * D, F), w_up[l].reshape(E * D, F), w_down[l].reshape(E * F, D)]
        proj, ba, wg, wu, wd = _in_proj(h, norm_mix_w[l], w_main, w_ba, expert_w, _tile(T, 1024), 1024)
        wg, wu, wd = wg.reshape(E, D, F), wu.reshape(E, D, F), wd.reshape(E, F, D)
        mix_r = _retention(proj, pos_col, ret_gn_w[l], B, S, hg)
        mix_d = _gdn(proj, ba, dn_conv_w[l], dn_a_log[l], dn_dt_bias[l], dn_norm_w[l], B, S, hg)
        h = _mm_res(mix_r, mix_d, w_out[l].astype(bf16), h, _tile(T, 512), D)
        kv = _norm_mm(mem2, norm_mem_w[l], w_xkv[l].astype(bf16), _tile(B * M, 512), _tile(2 * D, 1024), "xkv")
        w_r = jnp.pad(jnp.concatenate([w_group_router[l], w_expert_router[l]], axis=1),
                      ((0, 0), (0, LANES - N_GROUPS - N_EXPERTS))).astype(bf16)
        b_r = jnp.pad(jnp.concatenate([b_group_router[l], b_expert_router[l]]),
                      (0, LANES - N_GROUPS - N_EXPERTS)).reshape(1, LANES).astype(f32)
        h, hn, rout, cnt = _xattn_router(h, norm_xq_w[l], w_xq[l].astype(bf16), kv, w_xo[l].astype(bf16),
                                         norm_moe_w[l], w_r, b_r, B, S, M, _tile(S, 512))
        blk_e, nused, dest, pads = _dispatch_plan(rout, cnt, bm)
        xs = _dispatch(hn, dest, pads, blk_e.shape[0] * bm, _tile(T, 1024), D // (2 * LANES), bm)
        y = _ffn(xs, wg, wu, wd, blk_e, nused, bm)
        if l + 1 < depth:
            raise NotImplementedError("the fused MoE combine applies the final norm: depth 1 only")
        h = _combine(y, h, rout, dest, norm_final_w, _tile(T, 512))
    return h.reshape(B, S, D)
```

```python
import functools

import numpy as np
import jax
import jax.numpy as jnp
from jax import lax
from jax.experimental import pallas as pl
from jax.experimental.pallas import tpu as pltpu

f32 = jnp.float32
bf16 = jnp.bfloat16

EPS = 1e-6
HEADS = 8
HD = 128
LANES = 128
XA_HEADS = 4
CONV_W = 4
ROPE_BASE = 10000.0
N_GROUPS = 4
EXP_PER_GROUP = 8
N_EXPERTS = N_GROUPS * EXP_PER_GROUP
RET_CHUNK = 256
DN_CHUNK = 128
VMEM_LIMIT = 56 * 1024 * 1024


def _params(sem, vmem=VMEM_LIMIT):
    return pltpu.CompilerParams(dimension_semantics=sem, vmem_limit_bytes=vmem)


def _dot(a, b):
    return jnp.dot(a.astype(bf16), b.astype(bf16), preferred_element_type=f32)


def _dot_nt(a, b):
    return lax.dot_general(a.astype(bf16), b.astype(bf16), (((1,), (1,)), ((), ())),
                           preferred_element_type=f32)


def _dot_tn(a, b):
    return lax.dot_general(a.astype(bf16), b.astype(bf16), (((0,), (0,)), ((), ())),
                           preferred_element_type=f32)


def _split3(a):
    hi = a.astype(bf16)
    r = a - hi.astype(f32)
    mid = r.astype(bf16)
    lo = (r - mid.astype(f32)).astype(bf16)
    return hi, mid, lo


def _sigmoid(x):
    return 1.0 / (1.0 + jnp.exp(-x))


def _silu(x):
    hx = 0.5 * x
    return hx + hx * jnp.tanh(hx)


def _rms_scale(x):
    return lax.rsqrt(jnp.mean(x * x, axis=-1, keepdims=True) + EPS)


def _side_cast_step(step, nsteps, srcs, dsts, stage_in, stage_out, sem_in, sem_out):
    slot = step % 2
    mats = range(len(srcs))

    def in_copy(m, st, sl):
        r = stage_in[m].shape[1]
        rows = pl.ds(pl.multiple_of(st * r, r), r)
        return pltpu.make_async_copy(srcs[m].at[rows], stage_in[m].at[sl], sem_in.at[m, sl])

    def out_copy(m, st, sl):
        r = stage_out[m].shape[1]
        rows = pl.ds(pl.multiple_of(st * r, r), r)
        return pltpu.make_async_copy(stage_out[m].at[sl], dsts[m].at[rows], sem_out.at[m, sl])

    @pl.when(step == 0)
    def _():
        for m in mats:
            in_copy(m, 0, 0).start()

    @pl.when(step + 1 < nsteps)
    def _():
        for m in mats:
            in_copy(m, step + 1, 1 - slot).start()

    for m in mats:
        in_copy(m, step, slot).wait()

    @pl.when(step >= 2)
    def _():
        for m in mats:
            out_copy(m, step - 2, slot).wait()

    def convert():
        for m in mats:
            stage_out[m][slot] = stage_in[m][slot].astype(bf16)

    def finish():
        for m in mats:
            out_copy(m, step, slot).start()

        @pl.when(step == nsteps - 1)
        def _():
            for m in mats:
                if nsteps >= 2:
                    out_copy(m, step - 1, 1 - slot).wait()
                out_copy(m, step, slot).wait()

    return convert, finish


def _in_proj_kernel(x_ref, nw_ref, w_ref, wba_ref, *rest, n_side):
    srcs, rest = rest[:n_side], rest[n_side:]
    o_ref, ba_ref = rest[:2]
    dsts, rest = rest[2:2 + n_side], rest[2 + n_side:]
    xn_ref = rest[0]
    stage_in, stage_out = rest[1:1 + n_side], rest[1 + n_side:1 + 2 * n_side]
    sem_in, sem_out = rest[1 + 2 * n_side:]

    step = pl.program_id(0) * pl.num_programs(1) + pl.program_id(1)
    nsteps = pl.num_programs(0) * pl.num_programs(1)
    convert, finish = _side_cast_step(step, nsteps, srcs, dsts, stage_in, stage_out, sem_in, sem_out)

    @pl.when(pl.program_id(1) == 0)
    def _():
        x = x_ref[...]
        xn = (x * _rms_scale(x) * nw_ref[...]).astype(bf16)
        xn_ref[...] = xn
        ba_ref[...] = jnp.dot(xn, wba_ref[...], preferred_element_type=f32)

    convert()
    acc = jnp.dot(xn_ref[...], w_ref[...], preferred_element_type=f32)
    for c in range(o_ref.shape[0]):
        o_ref[c] = acc[:, c * LANES:(c + 1) * LANES].astype(bf16)
    finish()


def _in_proj(x2, norm_w, w_main, w_ba, side, tm, tn):
    T, D = x2.shape
    N = w_main.shape[1]
    nsteps = (T // tm) * (N // tn)
    rows = [a.shape[0] // nsteps for a in side]
    assert all(a.shape[0] == r * nsteps and r % 16 == 0 for a, r in zip(side, rows))
    any_spec = pl.BlockSpec(memory_space=pl.ANY)
    return pl.pallas_call(
        functools.partial(_in_proj_kernel, n_side=len(side)),
        grid=(T // tm, N // tn),
        in_specs=[
            pl.BlockSpec((tm, D), lambda i, j: (i, 0)),
            pl.BlockSpec((1, D), lambda i, j: (0, 0)),
            pl.BlockSpec((D, tn), lambda i, j: (0, j)),
            pl.BlockSpec((D, LANES), lambda i, j: (0, 0)),
        ] + [any_spec] * len(side),
        out_specs=[
            pl.BlockSpec((tn // LANES, tm, LANES), lambda i, j: (j, i, 0)),
            pl.BlockSpec((tm, LANES), lambda i, j: (i, 0)),
        ] + [any_spec] * len(side),
        out_shape=[
            jax.ShapeDtypeStruct((N // LANES, T, LANES), bf16),
            jax.ShapeDtypeStruct((T, LANES), f32),
        ] + [jax.ShapeDtypeStruct(a.shape, bf16) for a in side],
        scratch_shapes=[pltpu.VMEM((tm, D), bf16)]
        + [pltpu.VMEM((2, r, a.shape[1]), f32) for a, r in zip(side, rows)]
        + [pltpu.VMEM((2, r, a.shape[1]), bf16) for a, r in zip(side, rows)]
        + [pltpu.SemaphoreType.DMA((len(side), 2)), pltpu.SemaphoreType.DMA((len(side), 2))],
        compiler_params=_params(("arbitrary", "arbitrary")),
        name="in_proj",
    )(x2, norm_w.reshape(1, D), w_main, w_ba, *side)


def _retention_kernel(q_ref, k_ref, v_ref, g_ref, pos_ref, inv_ref, sgn_ref, dmask_ref, qdec_ref,
                      kdec_ref, cdec_ref, gnw_ref, o_ref, state_ref):
    @pl.when(pl.program_id(2) == 0)
    def _():
        state_ref[...] = jnp.zeros_like(state_ref)

    C = pos_ref.shape[0]
    lane = lax.broadcasted_iota(jnp.int32, (C // 2, HD), 1)
    low = lane < HD // 2
    ang = jnp.where(low, pos_ref[0:C // 2, :], pos_ref[C // 2:, :]) * inv_ref[...]
    cos_p = jnp.cos(ang)
    sin_p = jnp.sin(ang)

    def spread(t):
        swapped = pltpu.roll(t, HD // 2, 1)
        return jnp.concatenate([jnp.where(low, t, swapped), jnp.where(low, swapped, t)], axis=0)

    cos2 = spread(cos_p)
    sin2 = spread(sin_p) * sgn_ref[...]
    heads = range(q_ref.shape[0])

    def rope(x):
        return x * cos2 + pltpu.roll(x, HD // 2, 1) * sin2

    qr = [rope(q_ref[i].astype(f32)) for i in heads]
    kr = [rope(k_ref[i].astype(f32)) * (HD ** -0.5) for i in heads]
    state = [state_ref[i] for i in heads]
    s = [_dot_nt(qr[i], kr[i]) * dmask_ref[i] for i in heads]
    cross = [_dot(qr[i], state[i]) * qdec_ref[i] for i in heads]
    upd = [_dot_tn(kr[i] * kdec_ref[i], v_ref[i]) for i in heads]
    o = [_dot(s[i], v_ref[i]) + cross[i] for i in heads]
    for i in heads:
        state_ref[i] = state[i] * cdec_ref[i] + upd[i]
        d = o[i] - jnp.mean(o[i], axis=-1, keepdims=True)
        y = d * lax.rsqrt(jnp.mean(d * d, axis=-1, keepdims=True) + EPS) * gnw_ref[i]
        y = y * _silu(g_ref[i].astype(f32))
        o_ref[:, i * HD:(i + 1) * HD] = y.astype(bf16)


def _retention_tables(C):
    h = np.arange(HEADS, dtype=np.float64)
    log_gamma = np.log1p(-np.exp2(-5.0 - h))
    idx = np.arange(C, dtype=np.float64)
    rel = idx[:, None] - idx[None, :]
    dmask = np.where(rel >= 0, np.exp(log_gamma[:, None, None] * np.where(rel >= 0, rel, 0.0)), 0.0)
    qdec = np.exp(log_gamma[:, None] * (idx + 1.0))
    kdec = np.exp(log_gamma[:, None] * (C - 1.0 - idx))
    cdec = np.exp(log_gamma * C)
    rep = lambda a: np.broadcast_to(a[..., None], a.shape + (LANES,))
    return (jnp.asarray(dmask, f32), jnp.asarray(rep(qdec), f32), jnp.asarray(rep(kdec), f32),
            jnp.asarray(np.broadcast_to(cdec[:, None, None], (HEADS, 1, LANES)), f32))


def _retention(proj, pos_col, gn_w, B, S, hg):
    C = RET_CHUNK
    NC = S // C
    T = B * S
    G = HEADS // hg
    half = HD // 2
    inv = ROPE_BASE ** (-np.arange(half, dtype=np.float32) / half)
    inv2 = jnp.asarray(np.concatenate([inv, inv]).reshape(1, HD), f32)
    sgn = jnp.asarray(np.concatenate([-np.ones(half), np.ones(half)]).reshape(1, HD), f32)
    dmask, qdec, kdec, cdec = _retention_tables(C)

    def slab(off):
        return pl.BlockSpec((hg, C, HD), lambda b, g, n: (off // hg + g, b * NC + n, 0))

    def table(shape):
        return pl.BlockSpec((hg,) + shape, lambda b, g, n: (g,) + (0,) * len(shape))

    return pl.pallas_call(
        _retention_kernel,
        grid=(B, G, NC),
        in_specs=[
            slab(0), slab(HEADS), slab(2 * HEADS), slab(3 * HEADS),
            pl.BlockSpec((C, 1), lambda b, g, n: (b * NC + n, 0)),
            pl.BlockSpec((1, HD), lambda b, g, n: (0, 0)),
            pl.BlockSpec((1, HD), lambda b, g, n: (0, 0)),
            table((C, C)), table((C, LANES)), table((C, LANES)), table((1, LANES)), table((1, HD)),
        ],
        out_specs=pl.BlockSpec((C, hg * HD), lambda b, g, n: (b * NC + n, g)),
        out_shape=jax.ShapeDtypeStruct((T, HEADS * HD), bf16),
        scratch_shapes=[pltpu.VMEM((hg, HD, HD), f32)],
        compiler_params=_params(("parallel", "parallel", "arbitrary")),
        name="retention",
    )(proj, proj, proj, proj, pos_col, inv2, sgn, dmask, qdec, kdec, cdec,
      gn_w.reshape(HEADS, 1, HD))


def _gdn_kernel(q_ref, k_ref, v_ref, z_ref, ba_ref, cw_ref, arow_ref, dtrow_ref, nw_ref, o_ref,
                state_ref, tail_ref, win_ref, *, hg):
    C = q_ref.shape[1]
    first = pl.program_id(2) == 0

    @pl.when(first)
    def _():
        state_ref[...] = jnp.zeros_like(state_ref)
        tail_ref[...] = jnp.zeros_like(tail_ref)

    row = lax.broadcasted_iota(jnp.int32, (C, C), 0)
    col = lax.broadcasted_iota(jnp.int32, (C, C), 1)
    causal = row >= col
    strict = row > col
    same8 = (row // 8) == (col // 8)
    level_masks = []
    s = 8
    while s < C:
        level_masks.append(((row // (2 * s)) == (col // (2 * s))) & ((row // s) != (col // s)))
        s *= 2
    eye = jnp.where(row == col, 1.0, 0.0).astype(f32)
    tri = jnp.where(causal, 1.0, 0.0).astype(bf16)

    ba = ba_ref[...]
    beta_all = _sigmoid(ba)
    xa = ba + dtrow_ref[...]
    softplus = jnp.maximum(xa, 0.0) + jnp.log(1.0 + jnp.exp(-jnp.abs(xa)))
    glog = -jnp.exp(arow_ref[...]) * softplus
    g_hi, g_mid, g_lo = _split3(glog)
    gcum = (jnp.dot(tri, g_hi, preferred_element_type=f32)
            + jnp.dot(tri, g_mid, preferred_element_type=f32)
            + jnp.dot(tri, g_lo, preferred_element_type=f32))
    gcum_t = gcum.T
    lane = lax.broadcasted_iota(jnp.int32, (C, LANES), 1)
    sub = lax.broadcasted_iota(jnp.int32, (LANES, C), 0)

    def conv_silu(x_ref, kind, i):
        slot = kind * hg + i
        win_ref[slot, 0:8, :] = tail_ref[slot]
        win_ref[slot, 8:8 + C, :] = x_ref[i].astype(f32)
        w = cw_ref[kind, i]
        y = win_ref[slot, 8:8 + C, :] * w[CONV_W - 1:CONV_W, :]
        for t in range(1, CONV_W):
            y = y + win_ref[slot, 8 - t:8 - t + C, :] * w[CONV_W - 1 - t:CONV_W - t, :]
        tail_ref[slot] = win_ref[slot, C:C + 8, :]
        return _silu(y)

    heads = range(hg)
    hd = [pl.program_id(1) * hg + i for i in heads]
    q = [conv_silu(q_ref, 0, i) for i in heads]
    k = [conv_silu(k_ref, 1, i) for i in heads]
    v = [conv_silu(v_ref, 2, i) for i in heads]
    q = [x * lax.rsqrt(jnp.sum(x * x, axis=-1, keepdims=True) + EPS) * (HD ** -0.5) for x in q]
    k = [x * lax.rsqrt(jnp.sum(x * x, axis=-1, keepdims=True) + EPS) for x in k]

    bcol = [jnp.sum(jnp.where(lane == h, beta_all, 0.0), axis=1, keepdims=True) for h in hd]
    gcol = [jnp.sum(jnp.where(lane == h + HEADS, gcum, 0.0), axis=1, keepdims=True) for h in hd]
    grow = [jnp.sum(jnp.where(sub == h + HEADS, gcum_t, 0.0), axis=0, keepdims=True) for h in hd]
    glast = [g[C - 1:C, :] for g in gcol]
    gam = [jnp.where(causal, jnp.exp(jnp.where(causal, gc - gr, 0.0)), 0.0) for gc, gr in zip(gcol, grow)]
    egc = [jnp.exp(g) for g in gcol]

    kk = [_dot_nt(x, x) for x in k]
    qk = [_dot_nt(x, y) for x, y in zip(q, k)]
    a = [jnp.where(strict, m * g * b, 0.0) for m, g, b in zip(kk, gam, bcol)]
    qk = [m * g for m, g in zip(qk, gam)]

    d = [jnp.where(same8, m, 0.0) for m in a]
    d2 = [_dot(m, m) for m in d]
    d3 = [_dot(m, m2) for m, m2 in zip(d, d2)]
    d4 = [_dot(m2, m2) for m2 in d2]
    x = [eye - m + m2 - m3 for m, m2, m3 in zip(d, d2, d3)]
    x = [xi + _dot(xi, m4) for xi, m4 in zip(x, d4)]
    for mask in level_masks:
        t = [_dot(xi, jnp.where(mask, m, 0.0)) for xi, m in zip(x, a)]
        x = [xi - _dot(ti, xi) for xi, ti in zip(x, t)]

    rhs = [jnp.concatenate([vi * b, ki * (b * e)], axis=1) for vi, ki, b, e in zip(v, k, bcol, egc)]
    sol = [_dot(xi, r) for xi, r in zip(x, rhs)]
    q_dec = [qi * e for qi, e in zip(q, egc)]
    k_dec = [ki * jnp.exp(gl - gc) for ki, gl, gc in zip(k, glast, gcol)]

    state = [state_ref[i] for i in heads]
    ws = [_dot(jnp.concatenate([s_[:, HD:], qd], axis=0), st) for s_, qd, st in zip(sol, q_dec, state)]
    v_new = [s_[:, :HD] - w_[:C] for s_, w_ in zip(sol, ws)]
    o = [w_[C:] + _dot(m, vn) for w_, m, vn in zip(ws, qk, v_new)]
    upd = [_dot_tn(kd, vn) for kd, vn in zip(k_dec, v_new)]
    for i in heads:
        state_ref[i] = state[i] * jnp.exp(glast[i]) + upd[i]
        y = o[i] * _rms_scale(o[i]) * nw_ref[...]
        y = y * _silu(z_ref[i].astype(f32))
        o_ref[:, i * HD:(i + 1) * HD] = y.astype(bf16)


def _gdn(proj, ba, conv_w, a_log, dt_bias, norm_w, B, S, hg):
    C = DN_CHUNK
    NC = S // C
    T = B * S
    G = HEADS // hg
    cw = conv_w.reshape(CONV_W, 3, HEADS, HD).transpose(1, 2, 0, 3)
    pad = jnp.zeros((LANES - 2 * HEADS,), f32)
    arow = jnp.concatenate([jnp.zeros((HEADS,), f32), a_log.astype(f32), pad]).reshape(1, LANES)
    dtrow = jnp.concatenate([jnp.zeros((HEADS,), f32), dt_bias.astype(f32), pad]).reshape(1, LANES)

    def slab(off):
        return pl.BlockSpec((hg, C, HD), lambda b, g, n: (off // hg + g, b * NC + n, 0))

    row_spec = pl.BlockSpec((1, LANES), lambda b, g, n: (0, 0))
    return pl.pallas_call(
        functools.partial(_gdn_kernel, hg=hg),
        grid=(B, G, NC),
        in_specs=[
            slab(4 * HEADS), slab(5 * HEADS), slab(6 * HEADS), slab(7 * HEADS),
            pl.BlockSpec((C, LANES), lambda b, g, n: (b * NC + n, 0)),
            pl.BlockSpec((3, hg, CONV_W, HD), lambda b, g, n: (0, g, 0, 0)),
            row_spec, row_spec, row_spec,
        ],
        out_specs=pl.BlockSpec((C, hg * HD), lambda b, g, n: (b * NC + n, g)),
        out_shape=jax.ShapeDtypeStruct((T, HEADS * HD), bf16),
        scratch_shapes=[
            pltpu.VMEM((hg, HD, HD), f32),
            pltpu.VMEM((3 * hg, 8, HD), f32),
            pltpu.VMEM((3 * hg, C + 8, HD), f32),
        ],
        compiler_params=_params(("parallel", "parallel", "arbitrary")),
        name="gdn",
    )(proj, proj, proj, proj, ba, cw, arow, dtrow, norm_w.reshape(1, HD))


def _mm_res_kernel(a1_ref, a2_ref, w_ref, r_ref, o_ref):
    a = jnp.concatenate([a1_ref[...], a2_ref[...]], axis=1)
    o_ref[...] = r_ref[...] + jnp.dot(a, w_ref[...], preferred_element_type=f32)


def _mm_res(a1, a2, w, res, tm, tn):
    T, K1 = a1.shape
    K2 = a2.shape[1]
    N = w.shape[1]
    return pl.pallas_call(
        _mm_res_kernel,
        grid=(T // tm, N // tn),
        in_specs=[
            pl.BlockSpec((tm, K1), lambda i, j: (i, 0)),
            pl.BlockSpec((tm, K2), lambda i, j: (i, 0)),
            pl.BlockSpec((K1 + K2, tn), lambda i, j: (0, j)),
            pl.BlockSpec((tm, tn), lambda i, j: (i, j)),
        ],
        out_specs=pl.BlockSpec((tm, tn), lambda i, j: (i, j)),
        out_shape=jax.ShapeDtypeStruct((T, N), f32),
        compiler_params=_params(("parallel", "parallel")),
        name="out_proj",
    )(a1, a2, w, res)


def _norm_mm_kernel(x_ref, nw_ref, w_ref, o_ref, xn_ref):
    @pl.when(pl.program_id(1) == 0)
    def _():
        x = x_ref[...]
        xn_ref[...] = (x * _rms_scale(x) * nw_ref[...]).astype(bf16)

    o_ref[...] = jnp.dot(xn_ref[...], w_ref[...], preferred_element_type=f32).astype(o_ref.dtype)


def _norm_mm(x2, norm_w, w, tm, tn, name):
    T, D = x2.shape
    N = w.shape[1]
    return pl.pallas_call(
        _norm_mm_kernel,
        grid=(T // tm, N // tn),
        in_specs=[
            pl.BlockSpec((tm, D), lambda i, j: (i, 0)),
            pl.BlockSpec((1, D), lambda i, j: (0, 0)),
            pl.BlockSpec((D, tn), lambda i, j: (0, j)),
        ],
        out_specs=pl.BlockSpec((tm, tn), lambda i, j: (i, j)),
        out_shape=jax.ShapeDtypeStruct((T, N), bf16),
        scratch_shapes=[pltpu.VMEM((tm, D), bf16)],
        compiler_params=_params(("parallel", "arbitrary")),
        name=name,
    )(x2, norm_w.reshape(1, D), w)


def _xattn_router_kernel(h_ref, nq_ref, wq_ref, k_ref, v_ref, wo_ref, nm_ref, wr_ref, b_ref,
                         h2_ref, hn_ref, rt_ref, cnt_ref, run_ref):
    @pl.when((pl.program_id(0) == 0) & (pl.program_id(1) == 0))
    def _():
        run_ref[...] = jnp.zeros_like(run_ref)

    h1 = h_ref[...]
    D = h1.shape[1]
    dh = D // XA_HEADS
    xn = (h1 * _rms_scale(h1) * nq_ref[...]).astype(bf16)
    q = jnp.dot(xn, wq_ref[...], preferred_element_type=f32).astype(bf16)
    heads = []
    for h in range(XA_HEADS):
        sl = slice(h * dh, (h + 1) * dh)
        s = _dot_nt(q[:, sl], k_ref[:, sl]) * (dh ** -0.5)
        p = jnp.exp(s - jnp.max(s, axis=-1, keepdims=True))
        p = p / jnp.sum(p, axis=-1, keepdims=True)
        heads.append(_dot(p, v_ref[:, sl]).astype(bf16))
    h2 = h1 + jnp.dot(jnp.concatenate(heads, axis=1), wo_ref[...], preferred_element_type=f32)
    h2_ref[...] = h2
    _route(h2, nm_ref, wr_ref, b_ref, hn_ref, rt_ref, cnt_ref, run_ref)


def _xattn_router(h1, norm_q, w_q, kv, w_o, norm_moe, w_r, bias_row, B, S, M, tm):
    T, D = h1.shape
    nt = S // tm
    g = D // (2 * LANES)
    tok = lambda b, i: (b * nt + i, 0)
    const = lambda b, i: (0, 0)
    resident = dict(pipeline_mode=pl.Buffered(1))
    return pl.pallas_call(
        _xattn_router_kernel,
        grid=(B, nt),
        in_specs=[
            pl.BlockSpec((tm, D), tok),
            pl.BlockSpec((1, D), const),
            pl.BlockSpec((D, D), const, **resident),
            pl.BlockSpec((M, D), lambda b, i: (b, 0)),
            pl.BlockSpec((M, D), lambda b, i: (b, 1)),
            pl.BlockSpec((D, D), const, **resident),
            pl.BlockSpec((1, D), const),
            pl.BlockSpec((D, LANES), const),
            pl.BlockSpec((1, LANES), const),
        ],
        out_specs=[
            pl.BlockSpec((tm, D), tok),
            pl.BlockSpec((tm * g, LANES), tok),
            pl.BlockSpec((tm, LANES), tok),
            pl.BlockSpec((1, LANES), const),
        ],
        out_shape=[
            jax.ShapeDtypeStruct((T, D), f32),
            jax.ShapeDtypeStruct((T * g, LANES), jnp.uint32),
            jax.ShapeDtypeStruct((T, LANES), f32),
            jax.ShapeDtypeStruct((1, LANES), f32),
        ],
        scratch_shapes=[pltpu.VMEM((1, LANES), f32)],
        compiler_params=_params(("arbitrary", "arbitrary")),
        name="xattn_router",
    )(h1, norm_q.reshape(1, D), w_q, kv, kv, w_o, norm_moe.reshape(1, D), w_r, bias_row)


def _pack_pair(a, b):
    au = lax.bitcast_convert_type(a.astype(f32), jnp.uint32)
    bu = lax.bitcast_convert_type(b.astype(f32), jnp.uint32)
    return (au >> 16) | bu


def _unpack_pair(u):
    a = lax.bitcast_convert_type(u << 16, f32)
    b = lax.bitcast_convert_type(u & jnp.uint32(0xFFFF0000), f32)
    return a, b


def _store_slabs(ref, words):
    n = words.shape[0]
    g = words.shape[1] // LANES
    for c in range(g):
        ref[pl.ds(c, n, stride=g), :] = words[:, c * LANES:(c + 1) * LANES]


def _load_slabs(ref, n, g):
    return [ref[pl.ds(c, n, stride=g), :] for c in range(g)]


def _route(x, nw_ref, wr_ref, b_ref, hn_ref, o_ref, cnt_ref, run_ref):
    xn = x * _rms_scale(x) * nw_ref[...]
    xhi = xn.astype(bf16)
    logits = jnp.dot(xhi, wr_ref[...], preferred_element_type=f32) + b_ref[...]
    lane = lax.broadcasted_iota(jnp.int32, logits.shape, 1).astype(f32)
    neg = -jnp.inf
    big = float(LANES)

    def first_max(vals):
        m = jnp.max(vals, axis=-1, keepdims=True)
        return m, jnp.min(jnp.where(vals == m, lane, big), axis=-1, keepdims=True)

    is_group = lane < N_GROUPS
    gmax, gsel = first_max(jnp.where(is_group, logits, neg))
    gsum = jnp.sum(jnp.where(is_group, jnp.exp(logits - gmax), 0.0), axis=-1, keepdims=True)
    g_w = 1.0 / gsum
    lo = N_GROUPS + EXP_PER_GROUP * gsel
    in_group = (lane >= lo) & (lane < lo + EXP_PER_GROUP)
    el = jnp.where(in_group, logits, neg)
    l1, i1 = first_max(el)
    l2, i2 = first_max(jnp.where(lane == i1, neg, el))
    esum = jnp.sum(jnp.where(in_group, jnp.exp(logits - l1), 0.0), axis=-1, keepdims=True)
    p1 = 1.0 / esum
    p2 = jnp.exp(l2 - l1) / esum
    gate1 = g_w * p1 / (p1 + p2)
    gate2 = g_w * p2 / (p1 + p2)

    tm = x.shape[0]
    chosen = jnp.where((lane == i1) | (lane == i2), 1.0, 0.0)
    earlier = (lax.broadcasted_iota(jnp.int32, (tm, tm), 0) > lax.broadcasted_iota(jnp.int32, (tm, tm), 1))
    before = _dot(jnp.where(earlier, 1.0, 0.0), chosen) + run_ref[...]
    rank1 = jnp.sum(jnp.where(lane == i1, before, 0.0), axis=-1, keepdims=True)
    rank2 = jnp.sum(jnp.where(lane == i2, before, 0.0), axis=-1, keepdims=True)
    run_ref[...] = run_ref[...] + jnp.sum(chosen, axis=0, keepdims=True)
    cnt_ref[...] = run_ref[...]

    cols = (i1 - N_GROUPS, i2 - N_GROUPS, gate1, gate2, rank1, rank2)
    out = jnp.zeros_like(logits)
    for c, val in enumerate(cols):
        out = jnp.where(lane == c, val, out)
    o_ref[...] = out
    half = x.shape[1] // 2
    _store_slabs(hn_ref, _pack_pair(xhi[:, :half], xhi[:, half:]))


def _dispatch_kernel(dest_ref, pad_lo_ref, pad_n_ref, zcnt_ref, tail_ref, hn_ref, xs_hbm, zeros_ref, sem,
                     zsem, *, tb, g, bm):
    i = pl.program_id(0)
    base = i * tb
    sizes = [1 << b for b in range(bm.bit_length() - 1)]

    def zero_copy(first, p):
        return pltpu.make_async_copy(zeros_ref.at[pl.ds(0, p * g)], xs_hbm.at[pl.ds(first, p * g)], zsem)

    @pl.when(i == 0)
    def _():
        zeros_ref[...] = jnp.zeros_like(zeros_ref)

        def per_expert(e, carry):
            row = pad_lo_ref[e]
            n = pad_n_ref[e]
            for p in sizes:
                hit = (n & p) != 0

                @pl.when(hit)
                def _(row=row, p=p):
                    zero_copy(pl.multiple_of(row * g, g), p).start()

                row = row + jnp.where(hit, p, 0)
            return carry

        lax.fori_loop(0, N_EXPERTS, per_expert, 0)

        def per_tail_chunk(c, carry):
            zero_copy(pl.multiple_of((tail_ref[0] + c * sizes[-1]) * g, g), sizes[-1]).start()
            return carry

        lax.fori_loop(0, tail_ref[1], per_tail_chunk, 0)

    def body(r, carry):
        src = hn_ref.at[pl.ds(pl.multiple_of(r * g, g), g)]
        for kk in range(2):
            d = dest_ref[2 * (base + r) + kk]
            pltpu.make_async_copy(src, xs_hbm.at[pl.ds(pl.multiple_of(d * g, g), g)], sem.at[kk]).start()
        return carry

    lax.fori_loop(0, tb, body, 0, unroll=8)
    for kk in range(2):
        pltpu.make_async_copy(hn_ref, hn_ref, sem.at[kk]).wait()

    @pl.when(i == pl.num_programs(0) - 1)
    def _():
        for b, p in enumerate(sizes):
            def wait_one(_, carry, p=p):
                zero_copy(0, p).wait()
                return carry
            lax.fori_loop(0, zcnt_ref[b], wait_one, 0)


def _dispatch(hn, dest, pads, P, tb, g, bm):
    T = hn.shape[0] // g
    grid_spec = pltpu.PrefetchScalarGridSpec(
        num_scalar_prefetch=5,
        grid=(T // tb,),
        in_specs=[pl.BlockSpec((tb * g, LANES), lambda i, *_: (i, 0))],
        out_specs=pl.BlockSpec(memory_space=pl.ANY),
        scratch_shapes=[pltpu.VMEM((bm // 2 * g, LANES), jnp.uint32),
                        pltpu.SemaphoreType.DMA((2,)), pltpu.SemaphoreType.DMA(())],
    )
    return pl.pallas_call(
        functools.partial(_dispatch_kernel, tb=tb, g=g, bm=bm),
        grid_spec=grid_spec,
        out_shape=jax.ShapeDtypeStruct((P * g, LANES), jnp.uint32),
        compiler_params=_params(("arbitrary",)),
        name="dispatch",
    )(dest, *pads, hn)


def _ffn_kernel(blk_e_ref, nused_ref, nvalid_ref, x_ref, wg_ref, wu_ref, wd_ref, y_ref, *, bm):
    nv = nvalid_ref[pl.program_id(0)]
    half = wg_ref.shape[0] // 2
    g = half // LANES

    def run(rows):
        pairs = [_unpack_pair(w) for w in _load_slabs(x_ref.at[pl.ds(0, rows * g)], rows, g)]
        x = jnp.concatenate([p[0].astype(bf16) for p in pairs] + [p[1].astype(bf16) for p in pairs], axis=1)
        gate = jnp.dot(x, wg_ref[...], preferred_element_type=f32)
        up = jnp.dot(x, wu_ref[...], preferred_element_type=f32)
        hmid = (_silu(gate) * up).astype(bf16)
        for s in range(g):
            ys = jnp.dot(hmid, wd_ref[:, 2 * s * LANES:2 * (s + 1) * LANES], preferred_element_type=f32)
            y_ref[pl.ds(s, rows, stride=g), :] = _pack_pair(ys[:, :LANES].astype(bf16),
                                                            ys[:, LANES:].astype(bf16))

    @pl.when(nv > bm // 2)
    def _():
        run(bm)

    @pl.when((nv > 0) & (nv <= bm // 2))
    def _():
        run(bm // 2)
        y_ref[pl.ds(bm // 2 * g, bm // 2 * g), :] = jnp.zeros((bm // 2 * g, LANES), jnp.uint32)

    @pl.when(nv == 0)
    def _():
        y_ref[...] = jnp.zeros_like(y_ref)


def _ffn(xs, wg, wu, wd, blk_e, nused, nvalid, bm):
    D, F = wg.shape[1:]
    g = D // (2 * LANES)
    NB = blk_e.shape[0]
    grid_spec = pltpu.PrefetchScalarGridSpec(
        num_scalar_prefetch=3,
        grid=(NB,),
        in_specs=[
            pl.BlockSpec((bm * g, LANES), lambda i, be, n, nv: (jnp.minimum(i, n[0] - 1), 0)),
            pl.BlockSpec((None, D, F), lambda i, be, n, nv: (be[i], 0, 0)),
            pl.BlockSpec((None, D, F), lambda i, be, n, nv: (be[i], 0, 0)),
            pl.BlockSpec((None, F, D), lambda i, be, n, nv: (be[i], 0, 0)),
        ],
        out_specs=pl.BlockSpec((bm * g, LANES), lambda i, be, n, nv: (i, 0)),
    )
    return pl.pallas_call(
        functools.partial(_ffn_kernel, bm=bm),
        grid_spec=grid_spec,
        out_shape=jax.ShapeDtypeStruct(xs.shape, jnp.uint32),
        compiler_params=_params(("arbitrary",)),
        name="ffn",
    )(blk_e, nused, nvalid, xs, wg, wu, wd)


def _combine_kernel(dest_ref, y_hbm, h_ref, rt_ref, nw_ref, o_ref, ybuf, sem):
    i = pl.program_id(0)
    n = pl.num_programs(0)
    tm = h_ref.shape[0]
    slot = i % 2

    D = h_ref.shape[1]
    half = D // 2
    g = half // LANES

    def issue(blk, slot_):
        def body(r, carry):
            a = 2 * (blk * tm + r)
            for kk in range(2):
                d = dest_ref[a + kk]
                pltpu.make_async_copy(y_hbm.at[pl.ds(pl.multiple_of(d * g, g), g)],
                                      ybuf.at[slot_, kk, pl.ds(pl.multiple_of(r * g, g), g)],
                                      sem.at[slot_]).start()
            return carry
        lax.fori_loop(0, tm, body, 0, unroll=8)

    @pl.when(i == 0)
    def _():
        issue(0, 0)

    @pl.when(i + 1 < n)
    def _():
        issue(i + 1, 1 - slot)

    pltpu.make_async_copy(ybuf.at[slot], ybuf.at[slot], sem.at[slot]).wait()
    rt = rt_ref[...]
    g1 = rt[:, 2:3]
    g2 = rt[:, 3:4]
    y1 = [_unpack_pair(w) for w in _load_slabs(ybuf.at[slot, 0], tm, g)]
    y2 = [_unpack_pair(w) for w in _load_slabs(ybuf.at[slot, 1], tm, g)]
    cols = [slice(c * LANES, (c + 1) * LANES) for c in range(2 * g)]
    hs = [h_ref[:, cols[c]] + g1 * y1[c // 2][c % 2] + g2 * y2[c // 2][c % 2] for c in range(2 * g)]
    ms = sum(jnp.sum(x * x, axis=-1, keepdims=True) for x in hs) / D
    scale = lax.rsqrt(ms + EPS)
    for c in range(2 * g):
        o_ref[:, cols[c]] = hs[c] * scale * nw_ref[:, cols[c]]


def _combine(y, h2, rout, dest, norm_w, tm):
    T, D = h2.shape
    g = D // (2 * LANES)
    grid_spec = pltpu.PrefetchScalarGridSpec(
        num_scalar_prefetch=1,
        grid=(T // tm,),
        in_specs=[
            pl.BlockSpec(memory_space=pl.ANY),
            pl.BlockSpec((tm, D), lambda i, d: (i, 0)),
            pl.BlockSpec((tm, LANES), lambda i, d: (i, 0)),
            pl.BlockSpec((1, D), lambda i, d: (0, 0)),
        ],
        out_specs=pl.BlockSpec((tm, D), lambda i, d: (i, 0)),
        scratch_shapes=[pltpu.VMEM((2, 2, tm * g, LANES), jnp.uint32), pltpu.SemaphoreType.DMA((2,))],
    )
    return pl.pallas_call(
        _combine_kernel,
        grid_spec=grid_spec,
        out_shape=jax.ShapeDtypeStruct((T, D), f32),
        compiler_params=_params(("arbitrary",)),
        name="combine",
    )(dest, y, h2, rout, norm_w.reshape(1, D))


def _dispatch_plan(rout, cnt, bm):
    T = rout.shape[0]
    NB = 2 * T // bm + N_EXPERTS
    e_tk = rout[:, 0:2].astype(jnp.int32)
    rank_tk = rout[:, 4:6].astype(jnp.int32)
    counts = cnt[0, N_GROUPS:N_GROUPS + N_EXPERTS].astype(jnp.int32)
    padded = (counts + bm - 1) // bm * bm
    pend = jnp.cumsum(padded)
    pstart = pend - padded
    first_row = jnp.arange(NB, dtype=jnp.int32) * bm
    blk_e = jnp.minimum(jnp.sum((pend[None, :] <= first_row[:, None]).astype(jnp.int32), axis=1),
                        N_EXPERTS - 1)
    nused = (pend[-1] // bm).astype(jnp.int32).reshape(1)
    of_blk = blk_e[:, None] == jnp.arange(N_EXPERTS, dtype=jnp.int32)
    row_end = jnp.sum(jnp.where(of_blk, pstart + counts, 0), axis=1)
    nvalid = jnp.where(first_row < pend[-1], jnp.clip(row_end - first_row, 0, bm), 0).astype(jnp.int32)
    onehot = e_tk[:, :, None] == jnp.arange(N_EXPERTS, dtype=jnp.int32)
    dest = jnp.sum(jnp.where(onehot, pstart, 0), axis=-1) + rank_tk
    pad_n = padded - counts
    nbits = bm.bit_length() - 1
    bits = (pad_n[:, None] >> jnp.arange(nbits, dtype=jnp.int32)) & 1
    tail_chunks = 2 * (NB - nused[0])
    zcnt = jnp.sum(bits, axis=0) + jnp.where(jnp.arange(nbits) == nbits - 1, tail_chunks, 0)
    pads = ((pstart + counts).astype(jnp.int32), pad_n.astype(jnp.int32), zcnt.astype(jnp.int32),
            jnp.stack([pend[-1], tail_chunks]).astype(jnp.int32))
    return blk_e, nused, nvalid, dest.reshape(2 * T).astype(jnp.int32), pads


def _tile(n, want):
    t = min(n, want)
    while n % t:
        t //= 2
    return t


def kernel(x, mem, positions, norm_mix_w, w_in, dn_conv_w, dn_a_log, dn_dt_bias, ret_gn_w, dn_norm_w,
           w_out, norm_xq_w, norm_mem_w, w_xq, w_xkv, w_xo, norm_moe_w, w_group_router, b_group_router,
           w_expert_router, b_expert_router, w_gate, w_up, w_down, norm_final_w):
    B, S, D = x.shape
    M = mem.shape[1]
    T = B * S
    depth = w_in.shape[0]
    n_main = 8 * HEADS * HD
    h = x.reshape(T, D)
    pos_col = positions.astype(f32).reshape(T, 1)
    mem2 = mem.reshape(B * M, D)
    hg = HEADS
    bm = 512
    for l in range(depth):
        w_main = w_in[l][:, :n_main].astype(bf16)
        w_ba = jnp.pad(w_in[l][:, n_main:], ((0, 0), (0, LANES - 2 * HEADS))).astype(bf16)
        E, _, F = w_gate[l].shape
        expert_w = [w_gate[l].reshape(E * D, F), w_up[l].reshape(E * D, F), w_down[l].reshape(E * F, D)]
        proj, ba, wg, wu, wd = _in_proj(h, norm_mix_w[l], w_main, w_ba, expert_w, _tile(T, 1024), 1024)
        wg, wu, wd = wg.reshape(E, D, F), wu.reshape(E, D, F), wd.reshape(E, F, D)
        mix_r = _retention(proj, pos_col, ret_gn_w[l], B, S, hg)
        mix_d = _gdn(proj, ba, dn_conv_w[l], dn_a_log[l], dn_dt_bias[l], dn_norm_w[l], B, S, hg)
        h = _mm_res(mix_r, mix_d, w_out[l].astype(bf16), h, _tile(T, 512), D)
        kv = _norm_mm(mem2, norm_mem_w[l], w_xkv[l].astype(bf16), _tile(B * M, 512), _tile(2 * D, 1024), "xkv")
        w_r = jnp.pad(jnp.concatenate([w_group_router[l], w_expert_router[l]], axis=1),
                      ((0, 0), (0, LANES - N_GROUPS - N_EXPERTS))).astype(bf16)
        b_r = jnp.pad(jnp.concatenate([b_group_router[l], b_expert_router[l]]),
                      (0, LANES - N_GROUPS - N_EXPERTS)).reshape(1, LANES).astype(f32)
        h, hn, rout, cnt = _xattn_router(h, norm_xq_w[l], w_xq[l].astype(bf16), kv, w_xo[l].astype(bf16),
                                         norm_moe_w[l], w_r, b_r, B, S, M, _tile(S, 512))
        blk_e, nused, nvalid, dest, pads = _dispatch_plan(rout, cnt, bm)
        xs = _dispatch(hn, dest, pads, blk_e.shape[0] * bm, _tile(T, 1024), D // (2 * LANES), bm)
        y = _ffn(xs, wg, wu, wd, blk_e, nused, nvalid, bm)
        if l + 1 < depth:
            raise NotImplementedError("the fused MoE combine applies the final norm: depth 1 only")
        h = _combine(y, h, rout, dest, norm_final_w, _tile(T, 256))
    return h.reshape(B, S, D)
```

```python
import functools

import numpy as np
import jax
import jax.numpy as jnp
from jax import lax
from jax.experimental import pallas as pl
from jax.experimental.pallas import tpu as pltpu

f32 = jnp.float32
bf16 = jnp.bfloat16

EPS = 1e-6
HEADS = 8
HD = 128
LANES = 128
XA_HEADS = 4
CONV_W = 4
ROPE_BASE = 10000.0
N_GROUPS = 4
EXP_PER_GROUP = 8
N_EXPERTS = N_GROUPS * EXP_PER_GROUP
RET_CHUNK = 256
DN_CHUNK = 128
VMEM_LIMIT = 56 * 1024 * 1024


def _params(sem, vmem=VMEM_LIMIT):
    return pltpu.CompilerParams(dimension_semantics=sem, vmem_limit_bytes=vmem)


def _dot(a, b):
    return jnp.dot(a.astype(bf16), b.astype(bf16), preferred_element_type=f32)


def _dot_nt(a, b):
    return lax.dot_general(a.astype(bf16), b.astype(bf16), (((1,), (1,)), ((), ())),
                           preferred_element_type=f32)


def _dot_tn(a, b):
    return lax.dot_general(a.astype(bf16), b.astype(bf16), (((0,), (0,)), ((), ())),
                           preferred_element_type=f32)


def _split3(a):
    hi = a.astype(bf16)
    r = a - hi.astype(f32)
    mid = r.astype(bf16)
    lo = (r - mid.astype(f32)).astype(bf16)
    return hi, mid, lo


def _sigmoid(x):
    return 1.0 / (1.0 + jnp.exp(-x))


def _silu(x):
    hx = 0.5 * x
    return hx + hx * jnp.tanh(hx)


def _rms_scale(x):
    return lax.rsqrt(jnp.mean(x * x, axis=-1, keepdims=True) + EPS)


def _cast_kernel(w_ref, o_ref):
    o_ref[...] = w_ref[...].astype(bf16)


def _cast_cols(w, n, tn):
    K = w.shape[0]
    return pl.pallas_call(
        _cast_kernel,
        grid=(n // tn,),
        in_specs=[pl.BlockSpec((K, tn), lambda j: (0, j))],
        out_specs=pl.BlockSpec((K, tn), lambda j: (0, j)),
        out_shape=jax.ShapeDtypeStruct((K, n), bf16),
        compiler_params=_params(("parallel",)),
        name="cast_w_in",
    )(w)


def _side_cast_step(step, nsteps, srcs, dsts, stage_in, stage_out, sem_in, sem_out):
    slot = step % 2
    mats = range(len(srcs))

    def in_copy(m, st, sl):
        r = stage_in[m].shape[1]
        rows = pl.ds(pl.multiple_of(st * r, r), r)
        return pltpu.make_async_copy(srcs[m].at[rows], stage_in[m].at[sl], sem_in.at[m, sl])

    def out_copy(m, st, sl):
        r = stage_out[m].shape[1]
        rows = pl.ds(pl.multiple_of(st * r, r), r)
        return pltpu.make_async_copy(stage_out[m].at[sl], dsts[m].at[rows], sem_out.at[m, sl])

    @pl.when(step == 0)
    def _():
        for m in mats:
            in_copy(m, 0, 0).start()

    @pl.when(step + 1 < nsteps)
    def _():
        for m in mats:
            in_copy(m, step + 1, 1 - slot).start()

    for m in mats:
        in_copy(m, step, slot).wait()

    @pl.when(step >= 2)
    def _():
        for m in mats:
            out_copy(m, step - 2, slot).wait()

    def convert():
        for m in mats:
            stage_out[m][slot] = stage_in[m][slot].astype(bf16)

    def finish():
        for m in mats:
            out_copy(m, step, slot).start()

        @pl.when(step == nsteps - 1)
        def _():
            for m in mats:
                if nsteps >= 2:
                    out_copy(m, step - 1, 1 - slot).wait()
                out_copy(m, step, slot).wait()

    return convert, finish


def _in_proj_kernel(x_ref, nw_ref, w_ref, wba_ref, *rest, n_side):
    srcs, rest = rest[:n_side], rest[n_side:]
    o_ref, ba_ref = rest[:2]
    dsts, rest = rest[2:2 + n_side], rest[2 + n_side:]
    xn_ref = rest[0]
    stage_in, stage_out = rest[1:1 + n_side], rest[1 + n_side:1 + 2 * n_side]
    sem_in, sem_out = rest[1 + 2 * n_side:]

    step = pl.program_id(0) * pl.num_programs(1) + pl.program_id(1)
    nsteps = pl.num_programs(0) * pl.num_programs(1)
    convert, finish = _side_cast_step(step, nsteps, srcs, dsts, stage_in, stage_out, sem_in, sem_out)

    @pl.when(pl.program_id(1) == 0)
    def _():
        x = x_ref[...]
        xn = (x * _rms_scale(x) * nw_ref[...]).astype(bf16)
        xn_ref[...] = xn
        ba_ref[...] = jnp.dot(xn, wba_ref[...], preferred_element_type=f32)

    convert()
    acc = jnp.dot(xn_ref[...], w_ref[...], preferred_element_type=f32)
    for c in range(o_ref.shape[0]):
        o_ref[c] = acc[:, c * LANES:(c + 1) * LANES].astype(bf16)
    finish()


def _in_proj(x2, norm_w, w_main, w_ba, side, tm, tn):
    T, D = x2.shape
    N = w_main.shape[1]
    nsteps = (T // tm) * (N // tn)
    rows = [a.shape[0] // nsteps for a in side]
    assert all(a.shape[0] == r * nsteps and r % 16 == 0 for a, r in zip(side, rows))
    any_spec = pl.BlockSpec(memory_space=pl.ANY)
    return pl.pallas_call(
        functools.partial(_in_proj_kernel, n_side=len(side)),
        grid=(T // tm, N // tn),
        in_specs=[
            pl.BlockSpec((tm, D), lambda i, j: (i, 0)),
            pl.BlockSpec((1, D), lambda i, j: (0, 0)),
            pl.BlockSpec((D, tn), lambda i, j: (0, j)),
            pl.BlockSpec((D, LANES), lambda i, j: (0, 0)),
        ] + [any_spec] * len(side),
        out_specs=[
            pl.BlockSpec((tn // LANES, tm, LANES), lambda i, j: (j, i, 0)),
            pl.BlockSpec((tm, LANES), lambda i, j: (i, 0)),
        ] + [any_spec] * len(side),
        out_shape=[
            jax.ShapeDtypeStruct((N // LANES, T, LANES), bf16),
            jax.ShapeDtypeStruct((T, LANES), f32),
        ] + [jax.ShapeDtypeStruct(a.shape, bf16) for a in side],
        scratch_shapes=[pltpu.VMEM((tm, D), bf16)]
        + [pltpu.VMEM((2, r, a.shape[1]), f32) for a, r in zip(side, rows)]
        + [pltpu.VMEM((2, r, a.shape[1]), bf16) for a, r in zip(side, rows)]
        + [pltpu.SemaphoreType.DMA((len(side), 2)), pltpu.SemaphoreType.DMA((len(side), 2))],
        compiler_params=_params(("arbitrary", "arbitrary")),
        name="in_proj",
    )(x2, norm_w.reshape(1, D), w_main, w_ba, *side)


def _retention_kernel(q_ref, k_ref, v_ref, g_ref, pos_ref, inv_ref, sgn_ref, dmask_ref, qdec_ref,
                      kdec_ref, cdec_ref, gnw_ref, o_ref, state_ref):
    @pl.when(pl.program_id(2) == 0)
    def _():
        state_ref[...] = jnp.zeros_like(state_ref)

    C = pos_ref.shape[0]
    lane = lax.broadcasted_iota(jnp.int32, (C // 2, HD), 1)
    low = lane < HD // 2
    ang = jnp.where(low, pos_ref[0:C // 2, :], pos_ref[C // 2:, :]) * inv_ref[...]
    cos_p = jnp.cos(ang)
    sin_p = jnp.sin(ang)

    def spread(t):
        swapped = pltpu.roll(t, HD // 2, 1)
        return jnp.concatenate([jnp.where(low, t, swapped), jnp.where(low, swapped, t)], axis=0)

    cos2 = spread(cos_p)
    sin2 = spread(sin_p) * sgn_ref[...]
    heads = range(q_ref.shape[0])

    def rope(x):
        return x * cos2 + pltpu.roll(x, HD // 2, 1) * sin2

    qr = [rope(q_ref[i].astype(f32)) for i in heads]
    kr = [rope(k_ref[i].astype(f32)) * (HD ** -0.5) for i in heads]
    state = [state_ref[i] for i in heads]
    s = [_dot_nt(qr[i], kr[i]) * dmask_ref[i] for i in heads]
    cross = [_dot(qr[i], state[i]) * qdec_ref[i] for i in heads]
    upd = [_dot_tn(kr[i] * kdec_ref[i], v_ref[i]) for i in heads]
    o = [_dot(s[i], v_ref[i]) + cross[i] for i in heads]
    for i in heads:
        state_ref[i] = state[i] * cdec_ref[i] + upd[i]
        d = o[i] - jnp.mean(o[i], axis=-1, keepdims=True)
        y = d * lax.rsqrt(jnp.mean(d * d, axis=-1, keepdims=True) + EPS) * gnw_ref[i]
        y = y * _silu(g_ref[i].astype(f32))
        o_ref[:, i * HD:(i + 1) * HD] = y.astype(bf16)


def _retention_tables(C):
    h = np.arange(HEADS, dtype=np.float64)
    log_gamma = np.log1p(-np.exp2(-5.0 - h))
    idx = np.arange(C, dtype=np.float64)
    rel = idx[:, None] - idx[None, :]
    dmask = np.where(rel >= 0, np.exp(log_gamma[:, None, None] * np.where(rel >= 0, rel, 0.0)), 0.0)
    qdec = np.exp(log_gamma[:, None] * (idx + 1.0))
    kdec = np.exp(log_gamma[:, None] * (C - 1.0 - idx))
    cdec = np.exp(log_gamma * C)
    rep = lambda a: np.broadcast_to(a[..., None], a.shape + (LANES,))
    return (jnp.asarray(dmask, f32), jnp.asarray(rep(qdec), f32), jnp.asarray(rep(kdec), f32),
            jnp.asarray(np.broadcast_to(cdec[:, None, None], (HEADS, 1, LANES)), f32))


def _retention(proj, pos_col, gn_w, B, S, hg):
    C = RET_CHUNK
    NC = S // C
    T = B * S
    G = HEADS // hg
    half = HD // 2
    inv = ROPE_BASE ** (-np.arange(half, dtype=np.float32) / half)
    inv2 = jnp.asarray(np.concatenate([inv, inv]).reshape(1, HD), f32)
    sgn = jnp.asarray(np.concatenate([-np.ones(half), np.ones(half)]).reshape(1, HD), f32)
    dmask, qdec, kdec, cdec = _retention_tables(C)

    def slab(off):
        return pl.BlockSpec((hg, C, HD), lambda b, g, n: (off // hg + g, b * NC + n, 0))

    def table(shape):
        return pl.BlockSpec((hg,) + shape, lambda b, g, n: (g,) + (0,) * len(shape))

    return pl.pallas_call(
        _retention_kernel,
        grid=(B, G, NC),
        in_specs=[
            slab(0), slab(HEADS), slab(2 * HEADS), slab(3 * HEADS),
            pl.BlockSpec((C, 1), lambda b, g, n: (b * NC + n, 0)),
            pl.BlockSpec((1, HD), lambda b, g, n: (0, 0)),
            pl.BlockSpec((1, HD), lambda b, g, n: (0, 0)),
            table((C, C)), table((C, LANES)), table((C, LANES)), table((1, LANES)), table((1, HD)),
        ],
        out_specs=pl.BlockSpec((C, hg * HD), lambda b, g, n: (b * NC + n, g)),
        out_shape=jax.ShapeDtypeStruct((T, HEADS * HD), bf16),
        scratch_shapes=[pltpu.VMEM((hg, HD, HD), f32)],
        compiler_params=_params(("parallel", "parallel", "arbitrary")),
        name="retention",
    )(proj, proj, proj, proj, pos_col, inv2, sgn, dmask, qdec, kdec, cdec,
      gn_w.reshape(HEADS, 1, HD))


def _gdn_kernel(q_ref, k_ref, v_ref, z_ref, ba_ref, cw_ref, arow_ref, dtrow_ref, nw_ref, o_ref,
                state_ref, tail_ref, win_ref, *, hg):
    C = q_ref.shape[1]
    first = pl.program_id(2) == 0

    @pl.when(first)
    def _():
        state_ref[...] = jnp.zeros_like(state_ref)
        tail_ref[...] = jnp.zeros_like(tail_ref)

    row = lax.broadcasted_iota(jnp.int32, (C, C), 0)
    col = lax.broadcasted_iota(jnp.int32, (C, C), 1)
    causal = row >= col
    strict = row > col
    same8 = (row // 8) == (col // 8)
    level_masks = []
    s = 8
    while s < C:
        level_masks.append(((row // (2 * s)) == (col // (2 * s))) & ((row // s) != (col // s)))
        s *= 2
    eye = jnp.where(row == col, 1.0, 0.0).astype(f32)
    tri = jnp.where(causal, 1.0, 0.0).astype(bf16)

    ba = ba_ref[...]
    beta_all = _sigmoid(ba)
    xa = ba + dtrow_ref[...]
    softplus = jnp.maximum(xa, 0.0) + jnp.log(1.0 + jnp.exp(-jnp.abs(xa)))
    glog = -jnp.exp(arow_ref[...]) * softplus
    g_hi, g_mid, g_lo = _split3(glog)
    gcum = (jnp.dot(tri, g_hi, preferred_element_type=f32)
            + jnp.dot(tri, g_mid, preferred_element_type=f32)
            + jnp.dot(tri, g_lo, preferred_element_type=f32))
    gcum_t = gcum.T
    lane = lax.broadcasted_iota(jnp.int32, (C, LANES), 1)
    sub = lax.broadcasted_iota(jnp.int32, (LANES, C), 0)

    def conv_silu(x_ref, kind, i):
        slot = kind * hg + i
        win_ref[slot, 0:8, :] = tail_ref[slot]
        win_ref[slot, 8:8 + C, :] = x_ref[i].astype(f32)
        w = cw_ref[kind, i]
        y = win_ref[slot, 8:8 + C, :] * w[CONV_W - 1:CONV_W, :]
        for t in range(1, CONV_W):
            y = y + win_ref[slot, 8 - t:8 - t + C, :] * w[CONV_W - 1 - t:CONV_W - t, :]
        tail_ref[slot] = win_ref[slot, C:C + 8, :]
        return _silu(y)

    heads = range(hg)
    hd = [pl.program_id(1) * hg + i for i in heads]
    q = [conv_silu(q_ref, 0, i) for i in heads]
    k = [conv_silu(k_ref, 1, i) for i in heads]
    v = [conv_silu(v_ref, 2, i) for i in heads]
    q = [x * lax.rsqrt(jnp.sum(x * x, axis=-1, keepdims=True) + EPS) * (HD ** -0.5) for x in q]
    k = [x * lax.rsqrt(jnp.sum(x * x, axis=-1, keepdims=True) + EPS) for x in k]

    bcol = [jnp.sum(jnp.where(lane == h, beta_all, 0.0), axis=1, keepdims=True) for h in hd]
    gcol = [jnp.sum(jnp.where(lane == h + HEADS, gcum, 0.0), axis=1, keepdims=True) for h in hd]
    grow = [jnp.sum(jnp.where(sub == h + HEADS, gcum_t, 0.0), axis=0, keepdims=True) for h in hd]
    glast = [g[C - 1:C, :] for g in gcol]
    gam = [jnp.where(causal, jnp.exp(jnp.where(causal, gc - gr, 0.0)), 0.0) for gc, gr in zip(gcol, grow)]
    egc = [jnp.exp(g) for g in gcol]

    kk = [_dot_nt(x, x) for x in k]
    qk = [_dot_nt(x, y) for x, y in zip(q, k)]
    a = [jnp.where(strict, m * g * b, 0.0) for m, g, b in zip(kk, gam, bcol)]
    qk = [m * g for m, g in zip(qk, gam)]

    d = [jnp.where(same8, m, 0.0) for m in a]
    d2 = [_dot(m, m) for m in d]
    d3 = [_dot(m, m2) for m, m2 in zip(d, d2)]
    d4 = [_dot(m2, m2) for m2 in d2]
    x = [eye - m + m2 - m3 for m, m2, m3 in zip(d, d2, d3)]
    x = [xi + _dot(xi, m4) for xi, m4 in zip(x, d4)]
    for mask in level_masks:
        t = [_dot(xi, jnp.where(mask, m, 0.0)) for xi, m in zip(x, a)]
        x = [xi - _dot(ti, xi) for xi, ti in zip(x, t)]

    rhs = [jnp.concatenate([vi * b, ki * (b * e)], axis=1) for vi, ki, b, e in zip(v, k, bcol, egc)]
    sol = [_dot(xi, r) for xi, r in zip(x, rhs)]
    q_dec = [qi * e for qi, e in zip(q, egc)]
    k_dec = [ki * jnp.exp(gl - gc) for ki, gl, gc in zip(k, glast, gcol)]

    state = [state_ref[i] for i in heads]
    ws = [_dot(jnp.concatenate([s_[:, HD:], qd], axis=0), st) for s_, qd, st in zip(sol, q_dec, state)]
    v_new = [s_[:, :HD] - w_[:C] for s_, w_ in zip(sol, ws)]
    o = [w_[C:] + _dot(m, vn) for w_, m, vn in zip(ws, qk, v_new)]
    upd = [_dot_tn(kd, vn) for kd, vn in zip(k_dec, v_new)]
    for i in heads:
        state_ref[i] = state[i] * jnp.exp(glast[i]) + upd[i]
        y = o[i] * _rms_scale(o[i]) * nw_ref[...]
        y = y * _silu(z_ref[i].astype(f32))
        o_ref[:, i * HD:(i + 1) * HD] = y.astype(bf16)


def _gdn(proj, ba, conv_w, a_log, dt_bias, norm_w, B, S, hg):
    C = DN_CHUNK
    NC = S // C
    T = B * S
    G = HEADS // hg
    cw = conv_w.reshape(CONV_W, 3, HEADS, HD).transpose(1, 2, 0, 3)
    pad = jnp.zeros((LANES - 2 * HEADS,), f32)
    arow = jnp.concatenate([jnp.zeros((HEADS,), f32), a_log.astype(f32), pad]).reshape(1, LANES)
    dtrow = jnp.concatenate([jnp.zeros((HEADS,), f32), dt_bias.astype(f32), pad]).reshape(1, LANES)

    def slab(off):
        return pl.BlockSpec((hg, C, HD), lambda b, g, n: (off // hg + g, b * NC + n, 0))

    row_spec = pl.BlockSpec((1, LANES), lambda b, g, n: (0, 0))
    return pl.pallas_call(
        functools.partial(_gdn_kernel, hg=hg),
        grid=(B, G, NC),
        in_specs=[
            slab(4 * HEADS), slab(5 * HEADS), slab(6 * HEADS), slab(7 * HEADS),
            pl.BlockSpec((C, LANES), lambda b, g, n: (b * NC + n, 0)),
            pl.BlockSpec((3, hg, CONV_W, HD), lambda b, g, n: (0, g, 0, 0)),
            row_spec, row_spec, row_spec,
        ],
        out_specs=pl.BlockSpec((C, hg * HD), lambda b, g, n: (b * NC + n, g)),
        out_shape=jax.ShapeDtypeStruct((T, HEADS * HD), bf16),
        scratch_shapes=[
            pltpu.VMEM((hg, HD, HD), f32),
            pltpu.VMEM((3 * hg, 8, HD), f32),
            pltpu.VMEM((3 * hg, C + 8, HD), f32),
        ],
        compiler_params=_params(("parallel", "parallel", "arbitrary")),
        name="gdn",
    )(proj, proj, proj, proj, ba, cw, arow, dtrow, norm_w.reshape(1, HD))


def _mm_res_kernel(a1_ref, a2_ref, w_ref, r_ref, o_ref):
    a = jnp.concatenate([a1_ref[...], a2_ref[...]], axis=1)
    o_ref[...] = r_ref[...] + jnp.dot(a, w_ref[...], preferred_element_type=f32)


def _mm_res(a1, a2, w, res, tm, tn):
    T, K1 = a1.shape
    K2 = a2.shape[1]
    N = w.shape[1]
    return pl.pallas_call(
        _mm_res_kernel,
        grid=(T // tm, N // tn),
        in_specs=[
            pl.BlockSpec((tm, K1), lambda i, j: (i, 0)),
            pl.BlockSpec((tm, K2), lambda i, j: (i, 0)),
            pl.BlockSpec((K1 + K2, tn), lambda i, j: (0, j)),
            pl.BlockSpec((tm, tn), lambda i, j: (i, j)),
        ],
        out_specs=pl.BlockSpec((tm, tn), lambda i, j: (i, j)),
        out_shape=jax.ShapeDtypeStruct((T, N), f32),
        compiler_params=_params(("parallel", "parallel")),
        name="out_proj",
    )(a1, a2, w, res)


def _norm_mm_kernel(x_ref, nw_ref, w_ref, o_ref, xn_ref):
    @pl.when(pl.program_id(1) == 0)
    def _():
        x = x_ref[...]
        xn_ref[...] = (x * _rms_scale(x) * nw_ref[...]).astype(bf16)

    o_ref[...] = jnp.dot(xn_ref[...], w_ref[...], preferred_element_type=f32).astype(o_ref.dtype)


def _norm_mm(x2, norm_w, w, tm, tn, name):
    T, D = x2.shape
    N = w.shape[1]
    return pl.pallas_call(
        _norm_mm_kernel,
        grid=(T // tm, N // tn),
        in_specs=[
            pl.BlockSpec((tm, D), lambda i, j: (i, 0)),
            pl.BlockSpec((1, D), lambda i, j: (0, 0)),
            pl.BlockSpec((D, tn), lambda i, j: (0, j)),
        ],
        out_specs=pl.BlockSpec((tm, tn), lambda i, j: (i, j)),
        out_shape=jax.ShapeDtypeStruct((T, N), bf16),
        scratch_shapes=[pltpu.VMEM((tm, D), bf16)],
        compiler_params=_params(("parallel", "arbitrary")),
        name=name,
    )(x2, norm_w.reshape(1, D), w)


def _xattn_router_kernel(h_ref, nq_ref, wq_ref, k_ref, v_ref, wo_ref, nm_ref, wr_ref, b_ref,
                         h2_ref, hn_ref, rt_ref, cnt_ref, run_ref):
    @pl.when((pl.program_id(0) == 0) & (pl.program_id(1) == 0))
    def _():
        run_ref[...] = jnp.zeros_like(run_ref)

    h1 = h_ref[...]
    D = h1.shape[1]
    dh = D // XA_HEADS
    xn = (h1 * _rms_scale(h1) * nq_ref[...]).astype(bf16)
    q = jnp.dot(xn, wq_ref[...], preferred_element_type=f32).astype(bf16)
    heads = []
    for h in range(XA_HEADS):
        sl = slice(h * dh, (h + 1) * dh)
        s = _dot_nt(q[:, sl], k_ref[:, sl]) * (dh ** -0.5)
        p = jnp.exp(s - jnp.max(s, axis=-1, keepdims=True))
        p = p / jnp.sum(p, axis=-1, keepdims=True)
        heads.append(_dot(p, v_ref[:, sl]).astype(bf16))
    h2 = h1 + jnp.dot(jnp.concatenate(heads, axis=1), wo_ref[...], preferred_element_type=f32)
    h2_ref[...] = h2
    _route(h2, nm_ref, wr_ref, b_ref, hn_ref, rt_ref, cnt_ref, run_ref)


def _xattn_router(h1, norm_q, w_q, kv, w_o, norm_moe, w_r, bias_row, B, S, M, tm):
    T, D = h1.shape
    nt = S // tm
    g = D // (2 * LANES)
    tok = lambda b, i: (b * nt + i, 0)
    const = lambda b, i: (0, 0)
    resident = dict(pipeline_mode=pl.Buffered(1))
    return pl.pallas_call(
        _xattn_router_kernel,
        grid=(B, nt),
        in_specs=[
            pl.BlockSpec((tm, D), tok),
            pl.BlockSpec((1, D), const),
            pl.BlockSpec((D, D), const, **resident),
            pl.BlockSpec((M, D), lambda b, i: (b, 0)),
            pl.BlockSpec((M, D), lambda b, i: (b, 1)),
            pl.BlockSpec((D, D), const, **resident),
            pl.BlockSpec((1, D), const),
            pl.BlockSpec((D, LANES), const),
            pl.BlockSpec((1, LANES), const),
        ],
        out_specs=[
            pl.BlockSpec((tm, D), tok),
            pl.BlockSpec((tm * g, LANES), tok),
            pl.BlockSpec((tm, LANES), tok),
            pl.BlockSpec((1, LANES), const),
        ],
        out_shape=[
            jax.ShapeDtypeStruct((T, D), f32),
            jax.ShapeDtypeStruct((T * g, LANES), jnp.uint32),
            jax.ShapeDtypeStruct((T, LANES), f32),
            jax.ShapeDtypeStruct((1, LANES), f32),
        ],
        scratch_shapes=[pltpu.VMEM((1, LANES), f32)],
        compiler_params=_params(("arbitrary", "arbitrary")),
        name="xattn_router",
    )(h1, norm_q.reshape(1, D), w_q, kv, kv, w_o, norm_moe.reshape(1, D), w_r, bias_row)


def _pack_pair(a, b):
    au = lax.bitcast_convert_type(a.astype(f32), jnp.uint32)
    bu = lax.bitcast_convert_type(b.astype(f32), jnp.uint32)
    return (au >> 16) | bu


def _unpack_pair(u):
    a = lax.bitcast_convert_type(u << 16, f32)
    b = lax.bitcast_convert_type(u & jnp.uint32(0xFFFF0000), f32)
    return a, b


def _store_slabs(ref, words):
    n = words.shape[0]
    g = words.shape[1] // LANES
    for c in range(g):
        ref[pl.ds(c, n, stride=g), :] = words[:, c * LANES:(c + 1) * LANES]


def _load_slabs(ref, n, g):
    return [ref[pl.ds(c, n, stride=g), :] for c in range(g)]


def _route(x, nw_ref, wr_ref, b_ref, hn_ref, o_ref, cnt_ref, run_ref):
    xn = x * _rms_scale(x) * nw_ref[...]
    xhi = xn.astype(bf16)
    logits = jnp.dot(xhi, wr_ref[...], preferred_element_type=f32) + b_ref[...]
    lane = lax.broadcasted_iota(jnp.int32, logits.shape, 1).astype(f32)
    neg = -jnp.inf
    big = float(LANES)

    def first_max(vals):
        m = jnp.max(vals, axis=-1, keepdims=True)
        return m, jnp.min(jnp.where(vals == m, lane, big), axis=-1, keepdims=True)

    is_group = lane < N_GROUPS
    gmax, gsel = first_max(jnp.where(is_group, logits, neg))
    gsum = jnp.sum(jnp.where(is_group, jnp.exp(logits - gmax), 0.0), axis=-1, keepdims=True)
    g_w = 1.0 / gsum
    lo = N_GROUPS + EXP_PER_GROUP * gsel
    in_group = (lane >= lo) & (lane < lo + EXP_PER_GROUP)
    el = jnp.where(in_group, logits, neg)
    l1, i1 = first_max(el)
    l2, i2 = first_max(jnp.where(lane == i1, neg, el))
    esum = jnp.sum(jnp.where(in_group, jnp.exp(logits - l1), 0.0), axis=-1, keepdims=True)
    p1 = 1.0 / esum
    p2 = jnp.exp(l2 - l1) / esum
    gate1 = g_w * p1 / (p1 + p2)
    gate2 = g_w * p2 / (p1 + p2)

    tm = x.shape[0]
    chosen = jnp.where((lane == i1) | (lane == i2), 1.0, 0.0)
    earlier = (lax.broadcasted_iota(jnp.int32, (tm, tm), 0) > lax.broadcasted_iota(jnp.int32, (tm, tm), 1))
    before = _dot(jnp.where(earlier, 1.0, 0.0), chosen) + run_ref[...]
    rank1 = jnp.sum(jnp.where(lane == i1, before, 0.0), axis=-1, keepdims=True)
    rank2 = jnp.sum(jnp.where(lane == i2, before, 0.0), axis=-1, keepdims=True)
    run_ref[...] = run_ref[...] + jnp.sum(chosen, axis=0, keepdims=True)
    cnt_ref[...] = run_ref[...]

    cols = (i1 - N_GROUPS, i2 - N_GROUPS, gate1, gate2, rank1, rank2)
    out = jnp.zeros_like(logits)
    for c, val in enumerate(cols):
        out = jnp.where(lane == c, val, out)
    o_ref[...] = out
    half = x.shape[1] // 2
    _store_slabs(hn_ref, _pack_pair(xhi[:, :half], xhi[:, half:]))


def _dispatch_kernel(dest_ref, pad_lo_ref, pad_n_ref, zcnt_ref, tail_ref, hn_ref, xs_hbm, zeros_ref, sem,
                     zsem, *, tb, g, bm):
    i = pl.program_id(0)
    base = i * tb
    sizes = [1 << b for b in range(bm.bit_length() - 1)]

    def zero_copy(first, p):
        return pltpu.make_async_copy(zeros_ref.at[pl.ds(0, p * g)], xs_hbm.at[pl.ds(first, p * g)], zsem)

    @pl.when(i == 0)
    def _():
        zeros_ref[...] = jnp.zeros_like(zeros_ref)

        def per_expert(e, carry):
            row = pad_lo_ref[e]
            n = pad_n_ref[e]
            for p in sizes:
                hit = (n & p) != 0

                @pl.when(hit)
                def _(row=row, p=p):
                    zero_copy(pl.multiple_of(row * g, g), p).start()

                row = row + jnp.where(hit, p, 0)
            return carry

        lax.fori_loop(0, N_EXPERTS, per_expert, 0)

        def per_tail_chunk(c, carry):
            zero_copy(pl.multiple_of((tail_ref[0] + c * sizes[-1]) * g, g), sizes[-1]).start()
            return carry

        lax.fori_loop(0, tail_ref[1], per_tail_chunk, 0)

    def body(r, carry):
        src = hn_ref.at[pl.ds(pl.multiple_of(r * g, g), g)]
        for kk in range(2):
            d = dest_ref[2 * (base + r) + kk]
            pltpu.make_async_copy(src, xs_hbm.at[pl.ds(pl.multiple_of(d * g, g), g)],
                                  sem.at[kk]).start(priority=kk)
        return carry

    lax.fori_loop(0, tb, body, 0, unroll=8)
    for kk in range(2):
        pltpu.make_async_copy(hn_ref, hn_ref, sem.at[kk]).wait()

    @pl.when(i == pl.num_programs(0) - 1)
    def _():
        for b, p in enumerate(sizes):
            def wait_one(_, carry, p=p):
                zero_copy(0, p).wait()
                return carry
            lax.fori_loop(0, zcnt_ref[b], wait_one, 0)


def _dispatch(hn, dest, pads, P, tb, g, bm):
    T = hn.shape[0] // g
    grid_spec = pltpu.PrefetchScalarGridSpec(
        num_scalar_prefetch=5,
        grid=(T // tb,),
        in_specs=[pl.BlockSpec((tb * g, LANES), lambda i, *_: (i, 0))],
        out_specs=pl.BlockSpec(memory_space=pl.ANY),
        scratch_shapes=[pltpu.VMEM((bm // 2 * g, LANES), jnp.uint32),
                        pltpu.SemaphoreType.DMA((2,)), pltpu.SemaphoreType.DMA(())],
    )
    return pl.pallas_call(
        functools.partial(_dispatch_kernel, tb=tb, g=g, bm=bm),
        grid_spec=grid_spec,
        out_shape=jax.ShapeDtypeStruct((P * g, LANES), jnp.uint32),
        compiler_params=_params(("arbitrary",)),
        name="dispatch",
    )(dest, *pads, hn)


def _ffn_kernel(blk_e_ref, nused_ref, nvalid_ref, x_ref, wg_ref, wu_ref, wd_ref, y_ref, *, bm):
    nv = nvalid_ref[pl.program_id(0)]
    half = wg_ref.shape[0] // 2
    g = half // LANES

    def run(rows):
        pairs = [_unpack_pair(w) for w in _load_slabs(x_ref.at[pl.ds(0, rows * g)], rows, g)]
        x = jnp.concatenate([p[0].astype(bf16) for p in pairs] + [p[1].astype(bf16) for p in pairs], axis=1)
        gate = jnp.dot(x, wg_ref[...], preferred_element_type=f32)
        up = jnp.dot(x, wu_ref[...], preferred_element_type=f32)
        hmid = (_silu(gate) * up).astype(bf16)
        for s in range(g):
            ys = jnp.dot(hmid, wd_ref[:, 2 * s * LANES:2 * (s + 1) * LANES], preferred_element_type=f32)
            y_ref[pl.ds(s, rows, stride=g), :] = _pack_pair(ys[:, :LANES].astype(bf16),
                                                            ys[:, LANES:].astype(bf16))

    @pl.when(nv > bm // 2)
    def _():
        run(bm)

    @pl.when((nv > 0) & (nv <= bm // 2))
    def _():
        run(bm // 2)
        y_ref[pl.ds(bm // 2 * g, bm // 2 * g), :] = jnp.zeros((bm // 2 * g, LANES), jnp.uint32)

    @pl.when(nv == 0)
    def _():
        y_ref[...] = jnp.zeros_like(y_ref)


def _ffn(xs, wg, wu, wd, blk_e, nused, nvalid, bm):
    D, F = wg.shape[1:]
    g = D // (2 * LANES)
    NB = blk_e.shape[0]
    grid_spec = pltpu.PrefetchScalarGridSpec(
        num_scalar_prefetch=3,
        grid=(NB,),
        in_specs=[
            pl.BlockSpec((bm * g, LANES), lambda i, be, n, nv: (jnp.minimum(i, n[0] - 1), 0)),
            pl.BlockSpec((None, D, F), lambda i, be, n, nv: (be[i], 0, 0)),
            pl.BlockSpec((None, D, F), lambda i, be, n, nv: (be[i], 0, 0)),
            pl.BlockSpec((None, F, D), lambda i, be, n, nv: (be[i], 0, 0)),
        ],
        out_specs=pl.BlockSpec((bm * g, LANES), lambda i, be, n, nv: (i, 0)),
    )
    return pl.pallas_call(
        functools.partial(_ffn_kernel, bm=bm),
        grid_spec=grid_spec,
        out_shape=jax.ShapeDtypeStruct(xs.shape, jnp.uint32),
        compiler_params=_params(("arbitrary",)),
        name="ffn",
    )(blk_e, nused, nvalid, xs, wg, wu, wd)


def _combine_kernel(dest_ref, y_hbm, h_ref, rt_ref, nw_ref, o_ref, ybuf, sem):
    i = pl.program_id(0)
    n = pl.num_programs(0)
    tm = h_ref.shape[0]
    slot = i % 2

    D = h_ref.shape[1]
    half = D // 2
    g = half // LANES

    def issue(blk, slot_):
        def body(r, carry):
            a = 2 * (blk * tm + r)
            for kk in range(2):
                d = dest_ref[a + kk]
                pltpu.make_async_copy(y_hbm.at[pl.ds(pl.multiple_of(d * g, g), g)],
                                      ybuf.at[slot_, kk, pl.ds(pl.multiple_of(r * g, g), g)],
                                      sem.at[slot_]).start(priority=kk)
            return carry
        lax.fori_loop(0, tm, body, 0, unroll=8)

    @pl.when(i == 0)
    def _():
        issue(0, 0)

    @pl.when(i + 1 < n)
    def _():
        issue(i + 1, 1 - slot)

    pltpu.make_async_copy(ybuf.at[slot], ybuf.at[slot], sem.at[slot]).wait()
    rt = rt_ref[...]
    g1 = rt[:, 2:3]
    g2 = rt[:, 3:4]
    y1 = [_unpack_pair(w) for w in _load_slabs(ybuf.at[slot, 0], tm, g)]
    y2 = [_unpack_pair(w) for w in _load_slabs(ybuf.at[slot, 1], tm, g)]
    cols = [slice(c * LANES, (c + 1) * LANES) for c in range(2 * g)]
    hs = [h_ref[:, cols[c]] + g1 * y1[c // 2][c % 2] + g2 * y2[c // 2][c % 2] for c in range(2 * g)]
    ms = sum(jnp.sum(x * x, axis=-1, keepdims=True) for x in hs) / D
    scale = lax.rsqrt(ms + EPS)
    for c in range(2 * g):
        o_ref[:, cols[c]] = hs[c] * scale * nw_ref[:, cols[c]]


def _combine(y, h2, rout, dest, norm_w, tm):
    T, D = h2.shape
    g = D // (2 * LANES)
    grid_spec = pltpu.PrefetchScalarGridSpec(
        num_scalar_prefetch=1,
        grid=(T // tm,),
        in_specs=[
            pl.BlockSpec(memory_space=pl.ANY),
            pl.BlockSpec((tm, D), lambda i, d: (i, 0)),
            pl.BlockSpec((tm, LANES), lambda i, d: (i, 0)),
            pl.BlockSpec((1, D), lambda i, d: (0, 0)),
        ],
        out_specs=pl.BlockSpec((tm, D), lambda i, d: (i, 0)),
        scratch_shapes=[pltpu.VMEM((2, 2, tm * g, LANES), jnp.uint32), pltpu.SemaphoreType.DMA((2,))],
    )
    return pl.pallas_call(
        _combine_kernel,
        grid_spec=grid_spec,
        out_shape=jax.ShapeDtypeStruct((T, D), f32),
        compiler_params=_params(("arbitrary",)),
        name="combine",
    )(dest, y, h2, rout, norm_w.reshape(1, D))


def _dispatch_plan(rout, cnt, bm):
    T = rout.shape[0]
    NB = 2 * T // bm + N_EXPERTS
    e_tk = rout[:, 0:2].astype(jnp.int32)
    rank_tk = rout[:, 4:6].astype(jnp.int32)
    counts = cnt[0, N_GROUPS:N_GROUPS + N_EXPERTS].astype(jnp.int32)
    padded = (counts + bm - 1) // bm * bm
    pend = jnp.cumsum(padded)
    pstart = pend - padded
    first_row = jnp.arange(NB, dtype=jnp.int32) * bm
    blk_e = jnp.minimum(jnp.sum((pend[None, :] <= first_row[:, None]).astype(jnp.int32), axis=1),
                        N_EXPERTS - 1)
    nused = (pend[-1] // bm).astype(jnp.int32).reshape(1)
    of_blk = blk_e[:, None] == jnp.arange(N_EXPERTS, dtype=jnp.int32)
    row_end = jnp.sum(jnp.where(of_blk, pstart + counts, 0), axis=1)
    nvalid = jnp.where(first_row < pend[-1], jnp.clip(row_end - first_row, 0, bm), 0).astype(jnp.int32)
    onehot = e_tk[:, :, None] == jnp.arange(N_EXPERTS, dtype=jnp.int32)
    dest = jnp.sum(jnp.where(onehot, pstart, 0), axis=-1) + rank_tk
    pad_n = padded - counts
    nbits = bm.bit_length() - 1
    bits = (pad_n[:, None] >> jnp.arange(nbits, dtype=jnp.int32)) & 1
    tail_chunks = 2 * (NB - nused[0])
    zcnt = jnp.sum(bits, axis=0) + jnp.where(jnp.arange(nbits) == nbits - 1, tail_chunks, 0)
    pads = ((pstart + counts).astype(jnp.int32), pad_n.astype(jnp.int32), zcnt.astype(jnp.int32),
            jnp.stack([pend[-1], tail_chunks]).astype(jnp.int32))
    return blk_e, nused, nvalid, dest.reshape(2 * T).astype(jnp.int32), pads


def _tile(n, want):
    t = min(n, want)
    while n % t:
        t //= 2
    return t


def kernel(x, mem, positions, norm_mix_w, w_in, dn_conv_w, dn_a_log, dn_dt_bias, ret_gn_w, dn_norm_w,
           w_out, norm_xq_w, norm_mem_w, w_xq, w_xkv, w_xo, norm_moe_w, w_group_router, b_group_router,
           w_expert_router, b_expert_router, w_gate, w_up, w_down, norm_final_w):
    B, S, D = x.shape
    M = mem.shape[1]
    T = B * S
    depth = w_in.shape[0]
    n_main = 8 * HEADS * HD
    h = x.reshape(T, D)
    pos_col = positions.astype(f32).reshape(T, 1)
    mem2 = mem.reshape(B * M, D)
    hg = HEADS
    bm = 512
    for l in range(depth):
        w_main = _cast_cols(w_in[l], n_main, 1024)
        w_ba = jnp.pad(w_in[l][:, n_main:], ((0, 0), (0, LANES - 2 * HEADS))).astype(bf16)
        E, _, F = w_gate[l].shape
        expert_w = [w_gate[l].reshape(E * D, F), w_up[l].reshape(E * D, F), w_down[l].reshape(E * F, D)]
        proj, ba, wg, wu, wd = _in_proj(h, norm_mix_w[l], w_main, w_ba, expert_w, _tile(T, 1024), 1024)
        wg, wu, wd = wg.reshape(E, D, F), wu.reshape(E, D, F), wd.reshape(E, F, D)
        mix_r = _retention(proj, pos_col, ret_gn_w[l], B, S, hg)
        mix_d = _gdn(proj, ba, dn_conv_w[l], dn_a_log[l], dn_dt_bias[l], dn_norm_w[l], B, S, hg)
        h = _mm_res(mix_r, mix_d, w_out[l].astype(bf16), h, _tile(T, 512), D)
        kv = _norm_mm(mem2, norm_mem_w[l], w_xkv[l].astype(bf16), _tile(B * M, 512), _tile(2 * D, 1024), "xkv")
        w_r = jnp.pad(jnp.concatenate([w_group_router[l], w_expert_router[l]], axis=1),
                      ((0, 0), (0, LANES - N_GROUPS - N_EXPERTS))).astype(bf16)
        b_r = jnp.pad(jnp.concatenate([b_group_router[l], b_expert_router[l]]),
                      (0, LANES - N_GROUPS - N_EXPERTS)).reshape(1, LANES).astype(f32)
        h, hn, rout, cnt = _xattn_router(h, norm_xq_w[l], w_xq[l].astype(bf16), kv, w_xo[l].astype(bf16),
                                         norm_moe_w[l], w_r, b_r, B, S, M, _tile(S, 512))
        blk_e, nused, nvalid, dest, pads = _dispatch_plan(rout, cnt, bm)
        xs = _dispatch(hn, dest, pads, blk_e.shape[0] * bm, _tile(T, 1024), D // (2 * LANES), bm)
        y = _ffn(xs, wg, wu, wd, blk_e, nused, nvalid, bm)
        if l + 1 < depth:
            raise NotImplementedError("the fused MoE combine applies the final norm: depth 1 only")
        h = _combine(y, h, rout, dest, norm_final_w, _tile(T, 256))
    return h.reshape(B, S, D)
```

```python
import functools

import numpy as np
import jax
import jax.numpy as jnp
from jax import lax
from jax.experimental import pallas as pl
from jax.experimental.pallas import tpu as pltpu

f32 = jnp.float32
bf16 = jnp.bfloat16

EPS = 1e-6
HEADS = 8
HD = 128
LANES = 128
XA_HEADS = 4
CONV_W = 4
ROPE_BASE = 10000.0
N_GROUPS = 4
EXP_PER_GROUP = 8
N_EXPERTS = N_GROUPS * EXP_PER_GROUP
RET_CHUNK = 256
DN_CHUNK = 128
VMEM_LIMIT = 56 * 1024 * 1024


def _params(sem, vmem=VMEM_LIMIT):
    return pltpu.CompilerParams(dimension_semantics=sem, vmem_limit_bytes=vmem)


def _dot(a, b):
    return jnp.dot(a.astype(bf16), b.astype(bf16), preferred_element_type=f32)


def _dot_nt(a, b):
    return lax.dot_general(a.astype(bf16), b.astype(bf16), (((1,), (1,)), ((), ())),
                           preferred_element_type=f32)


def _dot_tn(a, b):
    return lax.dot_general(a.astype(bf16), b.astype(bf16), (((0,), (0,)), ((), ())),
                           preferred_element_type=f32)


def _split3(a):
    hi = a.astype(bf16)
    r = a - hi.astype(f32)
    mid = r.astype(bf16)
    lo = (r - mid.astype(f32)).astype(bf16)
    return hi, mid, lo


def _sigmoid(x):
    return 1.0 / (1.0 + jnp.exp(-x))


def _silu(x):
    hx = 0.5 * x
    return hx + hx * jnp.tanh(hx)


def _rms_scale(x):
    return lax.rsqrt(jnp.mean(x * x, axis=-1, keepdims=True) + EPS)


def _cast_t_kernel(w_ref, o_ref):
    o_ref[...] = w_ref[...].T.astype(bf16)


def _cast_rows_t(wt, n, tn):
    K = wt.shape[1]
    return pl.pallas_call(
        _cast_t_kernel,
        grid=(n // tn,),
        in_specs=[pl.BlockSpec((tn, K), lambda j: (j, 0))],
        out_specs=pl.BlockSpec((K, tn), lambda j: (0, j)),
        out_shape=jax.ShapeDtypeStruct((K, n), bf16),
        compiler_params=_params(("parallel",)),
        name="cast_w_in",
    )(wt)


def _side_cast_step(step, nsteps, srcs, dsts, stage_in, stage_out, sem_in, sem_out):
    slot = step % 2
    mats = range(len(srcs))

    def in_copy(m, st, sl):
        r = stage_in[m].shape[1]
        rows = pl.ds(pl.multiple_of(st * r, r), r)
        return pltpu.make_async_copy(srcs[m].at[rows], stage_in[m].at[sl], sem_in.at[m, sl])

    def out_copy(m, st, sl):
        r = stage_out[m].shape[1]
        rows = pl.ds(pl.multiple_of(st * r, r), r)
        return pltpu.make_async_copy(stage_out[m].at[sl], dsts[m].at[rows], sem_out.at[m, sl])

    @pl.when(step == 0)
    def _():
        for m in mats:
            in_copy(m, 0, 0).start()

    @pl.when(step + 1 < nsteps)
    def _():
        for m in mats:
            in_copy(m, step + 1, 1 - slot).start()

    for m in mats:
        in_copy(m, step, slot).wait()

    @pl.when(step >= 2)
    def _():
        for m in mats:
            out_copy(m, step - 2, slot).wait()

    def convert():
        for m in mats:
            stage_out[m][slot] = stage_in[m][slot].astype(bf16)

    def finish():
        for m in mats:
            out_copy(m, step, slot).start()

        @pl.when(step == nsteps - 1)
        def _():
            for m in mats:
                if nsteps >= 2:
                    out_copy(m, step - 1, 1 - slot).wait()
                out_copy(m, step, slot).wait()

    return convert, finish


def _in_proj_kernel(x_ref, nw_ref, w_ref, wba_ref, *rest, n_side):
    srcs, rest = rest[:n_side], rest[n_side:]
    o_ref, ba_ref = rest[:2]
    dsts, rest = rest[2:2 + n_side], rest[2 + n_side:]
    xn_ref = rest[0]
    stage_in, stage_out = rest[1:1 + n_side], rest[1 + n_side:1 + 2 * n_side]
    sem_in, sem_out = rest[1 + 2 * n_side:]

    step = pl.program_id(0) * pl.num_programs(1) + pl.program_id(1)
    nsteps = pl.num_programs(0) * pl.num_programs(1)
    convert, finish = _side_cast_step(step, nsteps, srcs, dsts, stage_in, stage_out, sem_in, sem_out)

    @pl.when(pl.program_id(1) == 0)
    def _():
        x = x_ref[...]
        xn = (x * _rms_scale(x) * nw_ref[...]).astype(bf16)
        xn_ref[...] = xn
        ba_ref[...] = jnp.dot(xn, wba_ref[...], preferred_element_type=f32)

    convert()
    acc = jnp.dot(xn_ref[...], w_ref[...], preferred_element_type=f32)
    for c in range(o_ref.shape[0]):
        o_ref[c] = acc[:, c * LANES:(c + 1) * LANES].astype(bf16)
    finish()


def _in_proj(x2, norm_w, w_main, w_ba, side, tm, tn):
    T, D = x2.shape
    N = w_main.shape[1]
    nsteps = (T // tm) * (N // tn)
    rows = [a.shape[0] // nsteps for a in side]
    assert all(a.shape[0] == r * nsteps and r % 16 == 0 for a, r in zip(side, rows))
    any_spec = pl.BlockSpec(memory_space=pl.ANY)
    return pl.pallas_call(
        functools.partial(_in_proj_kernel, n_side=len(side)),
        grid=(T // tm, N // tn),
        in_specs=[
            pl.BlockSpec((tm, D), lambda i, j: (i, 0)),
            pl.BlockSpec((1, D), lambda i, j: (0, 0)),
            pl.BlockSpec((D, tn), lambda i, j: (0, j)),
            pl.BlockSpec((D, LANES), lambda i, j: (0, 0)),
        ] + [any_spec] * len(side),
        out_specs=[
            pl.BlockSpec((tn // LANES, tm, LANES), lambda i, j: (j, i, 0)),
            pl.BlockSpec((tm, LANES), lambda i, j: (i, 0)),
        ] + [any_spec] * len(side),
        out_shape=[
            jax.ShapeDtypeStruct((N // LANES, T, LANES), bf16),
            jax.ShapeDtypeStruct((T, LANES), f32),
        ] + [jax.ShapeDtypeStruct(a.shape, bf16) for a in side],
        scratch_shapes=[pltpu.VMEM((tm, D), bf16)]
        + [pltpu.VMEM((2, r, a.shape[1]), f32) for a, r in zip(side, rows)]
        + [pltpu.VMEM((2, r, a.shape[1]), bf16) for a, r in zip(side, rows)]
        + [pltpu.SemaphoreType.DMA((len(side), 2)), pltpu.SemaphoreType.DMA((len(side), 2))],
        compiler_params=_params(("arbitrary", "arbitrary")),
        name="in_proj",
    )(x2, norm_w.reshape(1, D), w_main, w_ba, *side)


def _retention_kernel(q_ref, k_ref, v_ref, g_ref, pos_ref, inv_ref, sgn_ref, dmask_ref, qdec_ref,
                      kdec_ref, cdec_ref, gnw_ref, o_ref, state_ref):
    @pl.when(pl.program_id(2) == 0)
    def _():
        state_ref[...] = jnp.zeros_like(state_ref)

    C = pos_ref.shape[0]
    lane = lax.broadcasted_iota(jnp.int32, (C // 2, HD), 1)
    low = lane < HD // 2
    ang = jnp.where(low, pos_ref[0:C // 2, :], pos_ref[C // 2:, :]) * inv_ref[...]
    cos_p = jnp.cos(ang)
    sin_p = jnp.sin(ang)

    def spread(t):
        swapped = pltpu.roll(t, HD // 2, 1)
        return jnp.concatenate([jnp.where(low, t, swapped), jnp.where(low, swapped, t)], axis=0)

    cos2 = spread(cos_p)
    sin2 = spread(sin_p) * sgn_ref[...]
    heads = range(q_ref.shape[0])

    def rope(x):
        return x * cos2 + pltpu.roll(x, HD // 2, 1) * sin2

    qr = [rope(q_ref[i].astype(f32)) for i in heads]
    kr = [rope(k_ref[i].astype(f32)) * (HD ** -0.5) for i in heads]
    state = [state_ref[i] for i in heads]
    s = [_dot_nt(qr[i], kr[i]) * dmask_ref[i] for i in heads]
    cross = [_dot(qr[i], state[i]) * qdec_ref[i] for i in heads]
    upd = [_dot_tn(kr[i] * kdec_ref[i], v_ref[i]) for i in heads]
    o = [_dot(s[i], v_ref[i]) + cross[i] for i in heads]
    for i in heads:
        state_ref[i] = state[i] * cdec_ref[i] + upd[i]
        d = o[i] - jnp.mean(o[i], axis=-1, keepdims=True)
        y = d * lax.rsqrt(jnp.mean(d * d, axis=-1, keepdims=True) + EPS) * gnw_ref[i]
        y = y * _silu(g_ref[i].astype(f32))
        o_ref[:, i * HD:(i + 1) * HD] = y.astype(bf16)


def _retention_tables(C):
    h = np.arange(HEADS, dtype=np.float64)
    log_gamma = np.log1p(-np.exp2(-5.0 - h))
    idx = np.arange(C, dtype=np.float64)
    rel = idx[:, None] - idx[None, :]
    dmask = np.where(rel >= 0, np.exp(log_gamma[:, None, None] * np.where(rel >= 0, rel, 0.0)), 0.0)
    qdec = np.exp(log_gamma[:, None] * (idx + 1.0))
    kdec = np.exp(log_gamma[:, None] * (C - 1.0 - idx))
    cdec = np.exp(log_gamma * C)
    rep = lambda a: np.broadcast_to(a[..., None], a.shape + (LANES,))
    return (jnp.asarray(dmask, f32), jnp.asarray(rep(qdec), f32), jnp.asarray(rep(kdec), f32),
            jnp.asarray(np.broadcast_to(cdec[:, None, None], (HEADS, 1, LANES)), f32))


def _retention(proj, pos_col, gn_w, B, S, hg):
    C = RET_CHUNK
    NC = S // C
    T = B * S
    G = HEADS // hg
    half = HD // 2
    inv = ROPE_BASE ** (-np.arange(half, dtype=np.float32) / half)
    inv2 = jnp.asarray(np.concatenate([inv, inv]).reshape(1, HD), f32)
    sgn = jnp.asarray(np.concatenate([-np.ones(half), np.ones(half)]).reshape(1, HD), f32)
    dmask, qdec, kdec, cdec = _retention_tables(C)

    def slab(off):
        return pl.BlockSpec((hg, C, HD), lambda b, g, n: (off // hg + g, b * NC + n, 0))

    def table(shape):
        return pl.BlockSpec((hg,) + shape, lambda b, g, n: (g,) + (0,) * len(shape))

    return pl.pallas_call(
        _retention_kernel,
        grid=(B, G, NC),
        in_specs=[
            slab(0), slab(HEADS), slab(2 * HEADS), slab(3 * HEADS),
            pl.BlockSpec((C, 1), lambda b, g, n: (b * NC + n, 0)),
            pl.BlockSpec((1, HD), lambda b, g, n: (0, 0)),
            pl.BlockSpec((1, HD), lambda b, g, n: (0, 0)),
            table((C, C)), table((C, LANES)), table((C, LANES)), table((1, LANES)), table((1, HD)),
        ],
        out_specs=pl.BlockSpec((C, hg * HD), lambda b, g, n: (b * NC + n, g)),
        out_shape=jax.ShapeDtypeStruct((T, HEADS * HD), bf16),
        scratch_shapes=[pltpu.VMEM((hg, HD, HD), f32)],
        compiler_params=_params(("parallel", "parallel", "arbitrary")),
        name="retention",
    )(proj, proj, proj, proj, pos_col, inv2, sgn, dmask, qdec, kdec, cdec,
      gn_w.reshape(HEADS, 1, HD))


def _gdn_kernel(q_ref, k_ref, v_ref, z_ref, ba_ref, cw_ref, arow_ref, dtrow_ref, nw_ref, o_ref,
                state_ref, tail_ref, win_ref, *, hg):
    C = q_ref.shape[1]
    first = pl.program_id(2) == 0

    @pl.when(first)
    def _():
        state_ref[...] = jnp.zeros_like(state_ref)
        tail_ref[...] = jnp.zeros_like(tail_ref)

    row = lax.broadcasted_iota(jnp.int32, (C, C), 0)
    col = lax.broadcasted_iota(jnp.int32, (C, C), 1)
    causal = row >= col
    strict = row > col
    same8 = (row // 8) == (col // 8)
    level_masks = []
    s = 8
    while s < C:
        level_masks.append(((row // (2 * s)) == (col // (2 * s))) & ((row // s) != (col // s)))
        s *= 2
    eye = jnp.where(row == col, 1.0, 0.0).astype(f32)
    tri = jnp.where(causal, 1.0, 0.0).astype(bf16)

    ba = ba_ref[...]
    beta_all = _sigmoid(ba)
    xa = ba + dtrow_ref[...]
    softplus = jnp.maximum(xa, 0.0) + jnp.log(1.0 + jnp.exp(-jnp.abs(xa)))
    glog = -jnp.exp(arow_ref[...]) * softplus
    g_hi, g_mid, g_lo = _split3(glog)
    gcum = (jnp.dot(tri, g_hi, preferred_element_type=f32)
            + jnp.dot(tri, g_mid, preferred_element_type=f32)
            + jnp.dot(tri, g_lo, preferred_element_type=f32))
    gcum_t = gcum.T
    lane = lax.broadcasted_iota(jnp.int32, (C, LANES), 1)
    sub = lax.broadcasted_iota(jnp.int32, (LANES, C), 0)

    def conv_silu(x_ref, kind, i):
        slot = kind * hg + i
        win_ref[slot, 0:8, :] = tail_ref[slot]
        win_ref[slot, 8:8 + C, :] = x_ref[i].astype(f32)
        w = cw_ref[kind, i]
        y = win_ref[slot, 8:8 + C, :] * w[CONV_W - 1:CONV_W, :]
        for t in range(1, CONV_W):
            y = y + win_ref[slot, 8 - t:8 - t + C, :] * w[CONV_W - 1 - t:CONV_W - t, :]
        tail_ref[slot] = win_ref[slot, C:C + 8, :]
        return _silu(y)

    heads = range(hg)
    hd = [pl.program_id(1) * hg + i for i in heads]
    q = [conv_silu(q_ref, 0, i) for i in heads]
    k = [conv_silu(k_ref, 1, i) for i in heads]
    v = [conv_silu(v_ref, 2, i) for i in heads]
    q = [x * lax.rsqrt(jnp.sum(x * x, axis=-1, keepdims=True) + EPS) * (HD ** -0.5) for x in q]
    k = [x * lax.rsqrt(jnp.sum(x * x, axis=-1, keepdims=True) + EPS) for x in k]

    bcol = [jnp.sum(jnp.where(lane == h, beta_all, 0.0), axis=1, keepdims=True) for h in hd]
    gcol = [jnp.sum(jnp.where(lane == h + HEADS, gcum, 0.0), axis=1, keepdims=True) for h in hd]
    grow = [jnp.sum(jnp.where(sub == h + HEADS, gcum_t, 0.0), axis=0, keepdims=True) for h in hd]
    glast = [g[C - 1:C, :] for g in gcol]
    gam = [jnp.where(causal, jnp.exp(jnp.where(causal, gc - gr, 0.0)), 0.0) for gc, gr in zip(gcol, grow)]
    egc = [jnp.exp(g) for g in gcol]

    kk = [_dot_nt(x, x) for x in k]
    qk = [_dot_nt(x, y) for x, y in zip(q, k)]
    a = [jnp.where(strict, m * g * b, 0.0) for m, g, b in zip(kk, gam, bcol)]
    qk = [m * g for m, g in zip(qk, gam)]

    d = [jnp.where(same8, m, 0.0) for m in a]
    d2 = [_dot(m, m) for m in d]
    d3 = [_dot(m, m2) for m, m2 in zip(d, d2)]
    d4 = [_dot(m2, m2) for m2 in d2]
    x = [eye - m + m2 - m3 for m, m2, m3 in zip(d, d2, d3)]
    x = [xi + _dot(xi, m4) for xi, m4 in zip(x, d4)]
    for mask in level_masks:
        t = [_dot(xi, jnp.where(mask, m, 0.0)) for xi, m in zip(x, a)]
        x = [xi - _dot(ti, xi) for xi, ti in zip(x, t)]

    rhs = [jnp.concatenate([vi * b, ki * (b * e)], axis=1) for vi, ki, b, e in zip(v, k, bcol, egc)]
    sol = [_dot(xi, r) for xi, r in zip(x, rhs)]
    q_dec = [qi * e for qi, e in zip(q, egc)]
    k_dec = [ki * jnp.exp(gl - gc) for ki, gl, gc in zip(k, glast, gcol)]

    state = [state_ref[i] for i in heads]
    ws = [_dot(jnp.concatenate([s_[:, HD:], qd], axis=0), st) for s_, qd, st in zip(sol, q_dec, state)]
    v_new = [s_[:, :HD] - w_[:C] for s_, w_ in zip(sol, ws)]
    o = [w_[C:] + _dot(m, vn) for w_, m, vn in zip(ws, qk, v_new)]
    upd = [_dot_tn(kd, vn) for kd, vn in zip(k_dec, v_new)]
    for i in heads:
        state_ref[i] = state[i] * jnp.exp(glast[i]) + upd[i]
        y = o[i] * _rms_scale(o[i]) * nw_ref[...]
        y = y * _silu(z_ref[i].astype(f32))
        o_ref[:, i * HD:(i + 1) * HD] = y.astype(bf16)


def _gdn(proj, ba, conv_w, a_log, dt_bias, norm_w, B, S, hg):
    C = DN_CHUNK
    NC = S // C
    T = B * S
    G = HEADS // hg
    cw = conv_w.reshape(CONV_W, 3, HEADS, HD).transpose(1, 2, 0, 3)
    pad = jnp.zeros((LANES - 2 * HEADS,), f32)
    arow = jnp.concatenate([jnp.zeros((HEADS,), f32), a_log.astype(f32), pad]).reshape(1, LANES)
    dtrow = jnp.concatenate([jnp.zeros((HEADS,), f32), dt_bias.astype(f32), pad]).reshape(1, LANES)

    def slab(off):
        return pl.BlockSpec((hg, C, HD), lambda b, g, n: (off // hg + g, b * NC + n, 0))

    row_spec = pl.BlockSpec((1, LANES), lambda b, g, n: (0, 0))
    return pl.pallas_call(
        functools.partial(_gdn_kernel, hg=hg),
        grid=(B, G, NC),
        in_specs=[
            slab(4 * HEADS), slab(5 * HEADS), slab(6 * HEADS), slab(7 * HEADS),
            pl.BlockSpec((C, LANES), lambda b, g, n: (b * NC + n, 0)),
            pl.BlockSpec((3, hg, CONV_W, HD), lambda b, g, n: (0, g, 0, 0)),
            row_spec, row_spec, row_spec,
        ],
        out_specs=pl.BlockSpec((C, hg * HD), lambda b, g, n: (b * NC + n, g)),
        out_shape=jax.ShapeDtypeStruct((T, HEADS * HD), bf16),
        scratch_shapes=[
            pltpu.VMEM((hg, HD, HD), f32),
            pltpu.VMEM((3 * hg, 8, HD), f32),
            pltpu.VMEM((3 * hg, C + 8, HD), f32),
        ],
        compiler_params=_params(("parallel", "parallel", "arbitrary")),
        name="gdn",
    )(proj, proj, proj, proj, ba, cw, arow, dtrow, norm_w.reshape(1, HD))


def _mm_res_kernel(a1_ref, a2_ref, w_ref, r_ref, o_ref):
    a = jnp.concatenate([a1_ref[...], a2_ref[...]], axis=1)
    o_ref[...] = r_ref[...] + jnp.dot(a, w_ref[...], preferred_element_type=f32)


def _mm_res(a1, a2, w, res, tm, tn):
    T, K1 = a1.shape
    K2 = a2.shape[1]
    N = w.shape[1]
    return pl.pallas_call(
        _mm_res_kernel,
        grid=(T // tm, N // tn),
        in_specs=[
            pl.BlockSpec((tm, K1), lambda i, j: (i, 0)),
            pl.BlockSpec((tm, K2), lambda i, j: (i, 0)),
            pl.BlockSpec((K1 + K2, tn), lambda i, j: (0, j)),
            pl.BlockSpec((tm, tn), lambda i, j: (i, j)),
        ],
        out_specs=pl.BlockSpec((tm, tn), lambda i, j: (i, j)),
        out_shape=jax.ShapeDtypeStruct((T, N), f32),
        compiler_params=_params(("parallel", "parallel")),
        name="out_proj",
    )(a1, a2, w, res)


def _norm_mm_kernel(x_ref, nw_ref, w_ref, o_ref, xn_ref):
    @pl.when(pl.program_id(1) == 0)
    def _():
        x = x_ref[...]
        xn_ref[...] = (x * _rms_scale(x) * nw_ref[...]).astype(bf16)

    o_ref[...] = jnp.dot(xn_ref[...], w_ref[...], preferred_element_type=f32).astype(o_ref.dtype)


def _norm_mm(x2, norm_w, w, tm, tn, name):
    T, D = x2.shape
    N = w.shape[1]
    return pl.pallas_call(
        _norm_mm_kernel,
        grid=(T // tm, N // tn),
        in_specs=[
            pl.BlockSpec((tm, D), lambda i, j: (i, 0)),
            pl.BlockSpec((1, D), lambda i, j: (0, 0)),
            pl.BlockSpec((D, tn), lambda i, j: (0, j)),
        ],
        out_specs=pl.BlockSpec((tm, tn), lambda i, j: (i, j)),
        out_shape=jax.ShapeDtypeStruct((T, N), bf16),
        scratch_shapes=[pltpu.VMEM((tm, D), bf16)],
        compiler_params=_params(("parallel", "arbitrary")),
        name=name,
    )(x2, norm_w.reshape(1, D), w)


def _xattn_router_kernel(h_ref, nq_ref, wq_ref, k_ref, v_ref, wo_ref, nm_ref, wr_ref, b_ref,
                         h2_ref, hn_ref, rt_ref, cnt_ref, run_ref):
    @pl.when((pl.program_id(0) == 0) & (pl.program_id(1) == 0))
    def _():
        run_ref[...] = jnp.zeros_like(run_ref)

    h1 = h_ref[...]
    D = h1.shape[1]
    dh = D // XA_HEADS
    xn = (h1 * _rms_scale(h1) * nq_ref[...]).astype(bf16)
    q = jnp.dot(xn, wq_ref[...], preferred_element_type=f32).astype(bf16)
    heads = []
    for h in range(XA_HEADS):
        sl = slice(h * dh, (h + 1) * dh)
        s = _dot_nt(q[:, sl], k_ref[:, sl]) * (dh ** -0.5)
        p = jnp.exp(s - jnp.max(s, axis=-1, keepdims=True))
        p = p / jnp.sum(p, axis=-1, keepdims=True)
        heads.append(_dot(p, v_ref[:, sl]).astype(bf16))
    h2 = h1 + jnp.dot(jnp.concatenate(heads, axis=1), wo_ref[...], preferred_element_type=f32)
    h2_ref[...] = h2
    _route(h2, nm_ref, wr_ref, b_ref, hn_ref, rt_ref, cnt_ref, run_ref)


def _xattn_router(h1, norm_q, w_q, kv, w_o, norm_moe, w_r, bias_row, B, S, M, tm):
    T, D = h1.shape
    nt = S // tm
    g = D // (2 * LANES)
    tok = lambda b, i: (b * nt + i, 0)
    const = lambda b, i: (0, 0)
    resident = dict(pipeline_mode=pl.Buffered(1))
    return pl.pallas_call(
        _xattn_router_kernel,
        grid=(B, nt),
        in_specs=[
            pl.BlockSpec((tm, D), tok),
            pl.BlockSpec((1, D), const),
            pl.BlockSpec((D, D), const, **resident),
            pl.BlockSpec((M, D), lambda b, i: (b, 0)),
            pl.BlockSpec((M, D), lambda b, i: (b, 1)),
            pl.BlockSpec((D, D), const, **resident),
            pl.BlockSpec((1, D), const),
            pl.BlockSpec((D, LANES), const),
            pl.BlockSpec((1, LANES), const),
        ],
        out_specs=[
            pl.BlockSpec((tm, D), tok),
            pl.BlockSpec((tm * g, LANES), tok),
            pl.BlockSpec((tm, LANES), tok),
            pl.BlockSpec((1, LANES), const),
        ],
        out_shape=[
            jax.ShapeDtypeStruct((T, D), f32),
            jax.ShapeDtypeStruct((T * g, LANES), jnp.uint32),
            jax.ShapeDtypeStruct((T, LANES), f32),
            jax.ShapeDtypeStruct((1, LANES), f32),
        ],
        scratch_shapes=[pltpu.VMEM((1, LANES), f32)],
        compiler_params=_params(("arbitrary", "arbitrary")),
        name="xattn_router",
    )(h1, norm_q.reshape(1, D), w_q, kv, kv, w_o, norm_moe.reshape(1, D), w_r, bias_row)


def _pack_pair(a, b):
    au = lax.bitcast_convert_type(a.astype(f32), jnp.uint32)
    bu = lax.bitcast_convert_type(b.astype(f32), jnp.uint32)
    return (au >> 16) | bu


def _unpack_pair(u):
    a = lax.bitcast_convert_type(u << 16, f32)
    b = lax.bitcast_convert_type(u & jnp.uint32(0xFFFF0000), f32)
    return a, b


def _store_slabs(ref, words):
    n = words.shape[0]
    g = words.shape[1] // LANES
    for c in range(g):
        ref[pl.ds(c, n, stride=g), :] = words[:, c * LANES:(c + 1) * LANES]


def _load_slabs(ref, n, g):
    return [ref[pl.ds(c, n, stride=g), :] for c in range(g)]


def _route(x, nw_ref, wr_ref, b_ref, hn_ref, o_ref, cnt_ref, run_ref):
    xn = x * _rms_scale(x) * nw_ref[...]
    xhi = xn.astype(bf16)
    logits = jnp.dot(xhi, wr_ref[...], preferred_element_type=f32) + b_ref[...]
    lane = lax.broadcasted_iota(jnp.int32, logits.shape, 1).astype(f32)
    neg = -jnp.inf
    big = float(LANES)

    def first_max(vals):
        m = jnp.max(vals, axis=-1, keepdims=True)
        return m, jnp.min(jnp.where(vals == m, lane, big), axis=-1, keepdims=True)

    is_group = lane < N_GROUPS
    gmax, gsel = first_max(jnp.where(is_group, logits, neg))
    gsum = jnp.sum(jnp.where(is_group, jnp.exp(logits - gmax), 0.0), axis=-1, keepdims=True)
    g_w = 1.0 / gsum
    lo = N_GROUPS + EXP_PER_GROUP * gsel
    in_group = (lane >= lo) & (lane < lo + EXP_PER_GROUP)
    el = jnp.where(in_group, logits, neg)
    l1, i1 = first_max(el)
    l2, i2 = first_max(jnp.where(lane == i1, neg, el))
    esum = jnp.sum(jnp.where(in_group, jnp.exp(logits - l1), 0.0), axis=-1, keepdims=True)
    p1 = 1.0 / esum
    p2 = jnp.exp(l2 - l1) / esum
    gate1 = g_w * p1 / (p1 + p2)
    gate2 = g_w * p2 / (p1 + p2)

    tm = x.shape[0]
    chosen = jnp.where((lane == i1) | (lane == i2), 1.0, 0.0)
    earlier = (lax.broadcasted_iota(jnp.int32, (tm, tm), 0) > lax.broadcasted_iota(jnp.int32, (tm, tm), 1))
    before = _dot(jnp.where(earlier, 1.0, 0.0), chosen) + run_ref[...]
    rank1 = jnp.sum(jnp.where(lane == i1, before, 0.0), axis=-1, keepdims=True)
    rank2 = jnp.sum(jnp.where(lane == i2, before, 0.0), axis=-1, keepdims=True)
    run_ref[...] = run_ref[...] + jnp.sum(chosen, axis=0, keepdims=True)
    cnt_ref[...] = run_ref[...]

    cols = (i1 - N_GROUPS, i2 - N_GROUPS, gate1, gate2, rank1, rank2)
    out = jnp.zeros_like(logits)
    for c, val in enumerate(cols):
        out = jnp.where(lane == c, val, out)
    o_ref[...] = out
    half = x.shape[1] // 2
    _store_slabs(hn_ref, _pack_pair(xhi[:, :half], xhi[:, half:]))


def _dispatch_kernel(dest_ref, pad_lo_ref, pad_n_ref, zcnt_ref, tail_ref, hn_ref, xs_hbm, zeros_ref, sem,
                     zsem, *, tb, g, bm):
    i = pl.program_id(0)
    base = i * tb
    sizes = [1 << b for b in range(bm.bit_length() - 1)]

    def zero_copy(first, p):
        return pltpu.make_async_copy(zeros_ref.at[pl.ds(0, p * g)], xs_hbm.at[pl.ds(first, p * g)], zsem)

    @pl.when(i == 0)
    def _():
        zeros_ref[...] = jnp.zeros_like(zeros_ref)

        def per_expert(e, carry):
            row = pad_lo_ref[e]
            n = pad_n_ref[e]
            for p in sizes:
                hit = (n & p) != 0

                @pl.when(hit)
                def _(row=row, p=p):
                    zero_copy(pl.multiple_of(row * g, g), p).start()

                row = row + jnp.where(hit, p, 0)
            return carry

        lax.fori_loop(0, N_EXPERTS, per_expert, 0)

        def per_tail_chunk(c, carry):
            zero_copy(pl.multiple_of((tail_ref[0] + c * sizes[-1]) * g, g), sizes[-1]).start()
            return carry

        lax.fori_loop(0, tail_ref[1], per_tail_chunk, 0)

    def body(r, carry):
        src = hn_ref.at[pl.ds(pl.multiple_of(r * g, g), g)]
        for kk in range(2):
            d = dest_ref[2 * (base + r) + kk]
            pltpu.make_async_copy(src, xs_hbm.at[pl.ds(pl.multiple_of(d * g, g), g)],
                                  sem.at[kk]).start(priority=kk)
        return carry

    lax.fori_loop(0, tb, body, 0, unroll=8)
    for kk in range(2):
        pltpu.make_async_copy(hn_ref, hn_ref, sem.at[kk]).wait()

    @pl.when(i == pl.num_programs(0) - 1)
    def _():
        for b, p in enumerate(sizes):
            def wait_one(_, carry, p=p):
                zero_copy(0, p).wait()
                return carry
            lax.fori_loop(0, zcnt_ref[b], wait_one, 0)


def _dispatch(hn, dest, pads, P, tb, g, bm):
    T = hn.shape[0] // g
    grid_spec = pltpu.PrefetchScalarGridSpec(
        num_scalar_prefetch=5,
        grid=(T // tb,),
        in_specs=[pl.BlockSpec((tb * g, LANES), lambda i, *_: (i, 0))],
        out_specs=pl.BlockSpec(memory_space=pl.ANY),
        scratch_shapes=[pltpu.VMEM((bm // 2 * g, LANES), jnp.uint32),
                        pltpu.SemaphoreType.DMA((2,)), pltpu.SemaphoreType.DMA(())],
    )
    return pl.pallas_call(
        functools.partial(_dispatch_kernel, tb=tb, g=g, bm=bm),
        grid_spec=grid_spec,
        out_shape=jax.ShapeDtypeStruct((P * g, LANES), jnp.uint32),
        compiler_params=_params(("arbitrary",)),
        name="dispatch",
    )(dest, *pads, hn)


def _ffn_kernel(blk_e_ref, nused_ref, nvalid_ref, x_ref, wg_ref, wu_ref, wd_ref, y_ref, *, bm):
    nv = nvalid_ref[pl.program_id(0)]
    half = wg_ref.shape[0] // 2
    g = half // LANES

    def run(rows):
        pairs = [_unpack_pair(w) for w in _load_slabs(x_ref.at[pl.ds(0, rows * g)], rows, g)]
        x = jnp.concatenate([p[0].astype(bf16) for p in pairs] + [p[1].astype(bf16) for p in pairs], axis=1)
        gate = jnp.dot(x, wg_ref[...], preferred_element_type=f32)
        up = jnp.dot(x, wu_ref[...], preferred_element_type=f32)
        hmid = (_silu(gate) * up).astype(bf16)
        for s in range(g):
            ys = jnp.dot(hmid, wd_ref[:, 2 * s * LANES:2 * (s + 1) * LANES], preferred_element_type=f32)
            y_ref[pl.ds(s, rows, stride=g), :] = _pack_pair(ys[:, :LANES].astype(bf16),
                                                            ys[:, LANES:].astype(bf16))

    @pl.when(nv > bm // 2)
    def _():
        run(bm)

    @pl.when((nv > 0) & (nv <= bm // 2))
    def _():
        run(bm // 2)
        y_ref[pl.ds(bm // 2 * g, bm // 2 * g), :] = jnp.zeros((bm // 2 * g, LANES), jnp.uint32)

    @pl.when(nv == 0)
    def _():
        y_ref[...] = jnp.zeros_like(y_ref)


def _ffn(xs, wg, wu, wd, blk_e, nused, nvalid, bm):
    D, F = wg.shape[1:]
    g = D // (2 * LANES)
    NB = blk_e.shape[0]
    grid_spec = pltpu.PrefetchScalarGridSpec(
        num_scalar_prefetch=3,
        grid=(NB,),
        in_specs=[
            pl.BlockSpec((bm * g, LANES), lambda i, be, n, nv: (jnp.minimum(i, n[0] - 1), 0)),
            pl.BlockSpec((None, D, F), lambda i, be, n, nv: (be[i], 0, 0)),
            pl.BlockSpec((None, D, F), lambda i, be, n, nv: (be[i], 0, 0)),
            pl.BlockSpec((None, F, D), lambda i, be, n, nv: (be[i], 0, 0)),
        ],
        out_specs=pl.BlockSpec((bm * g, LANES), lambda i, be, n, nv: (i, 0)),
    )
    return pl.pallas_call(
        functools.partial(_ffn_kernel, bm=bm),
        grid_spec=grid_spec,
        out_shape=jax.ShapeDtypeStruct(xs.shape, jnp.uint32),
        compiler_params=_params(("arbitrary",)),
        name="ffn",
    )(blk_e, nused, nvalid, xs, wg, wu, wd)


def _combine_kernel(dest_ref, y_hbm, h_ref, rt_ref, nw_ref, o_ref, ybuf, sem):
    i = pl.program_id(0)
    n = pl.num_programs(0)
    tm = h_ref.shape[0]
    slot = i % 2

    D = h_ref.shape[1]
    half = D // 2
    g = half // LANES

    def row_copy(blk, r, kk, slot_):
        d = dest_ref[2 * (blk * tm + r) + kk]
        first = r * g if isinstance(r, int) else pl.multiple_of(r * g, g)
        return pltpu.make_async_copy(y_hbm.at[pl.ds(pl.multiple_of(d * g, g), g)],
                                     ybuf.at[slot_, kk, pl.ds(first, g)], sem.at[slot_])

    def wait_slot(slot_):
        pltpu.make_async_copy(ybuf.at[slot_], ybuf.at[slot_], sem.at[slot_]).wait()

    @pl.when(i == 0)
    def _():
        def body(r, carry):
            for kk in range(2):
                row_copy(0, r, kk, 0).start(priority=kk)
            return carry
        lax.fori_loop(0, tm, body, 0, unroll=8)

    nxt = jnp.minimum(i + 1, n - 1)
    pieces = 4 * g
    per = tm // pieces

    def step(slot_):
        def issue_batch(p):
            for r in range(p * per, (p + 1) * per):
                for kk in range(2):
                    row_copy(nxt, r, kk, 1 - slot_).start(priority=kk)

        wait_slot(slot_)
        rt = rt_ref[...]
        g1 = rt[:, 2:3]
        g2 = rt[:, 3:4]
        cols = [slice(c * LANES, (c + 1) * LANES) for c in range(2 * g)]
        hs = []
        for s in range(g):
            a1 = _unpack_pair(ybuf[slot_, 0, pl.ds(s, tm, stride=g), :])
            a2 = _unpack_pair(ybuf[slot_, 1, pl.ds(s, tm, stride=g), :])
            for hw in range(2):
                hs.append(h_ref[:, cols[2 * s + hw]] + g1 * a1[hw] + g2 * a2[hw])
                issue_batch(2 * s + hw)
        ms = sum(jnp.sum(x * x, axis=-1, keepdims=True) for x in hs) / D
        scale = lax.rsqrt(ms + EPS)
        for c in range(2 * g):
            o_ref[:, cols[c]] = hs[c] * scale * nw_ref[:, cols[c]]
            issue_batch(2 * g + c)

    @pl.when(slot == 0)
    def _():
        step(0)

    @pl.when(slot == 1)
    def _():
        step(1)

    @pl.when(i == n - 1)
    def _():
        wait_slot(1 - slot)


def _combine(y, h2, rout, dest, norm_w, tm):
    T, D = h2.shape
    g = D // (2 * LANES)
    grid_spec = pltpu.PrefetchScalarGridSpec(
        num_scalar_prefetch=1,
        grid=(T // tm,),
        in_specs=[
            pl.BlockSpec(memory_space=pl.ANY),
            pl.BlockSpec((tm, D), lambda i, d: (i, 0)),
            pl.BlockSpec((tm, LANES), lambda i, d: (i, 0)),
            pl.BlockSpec((1, D), lambda i, d: (0, 0)),
        ],
        out_specs=pl.BlockSpec((tm, D), lambda i, d: (i, 0)),
        scratch_shapes=[pltpu.VMEM((2, 2, tm * g, LANES), jnp.uint32), pltpu.SemaphoreType.DMA((2,))],
    )
    return pl.pallas_call(
        _combine_kernel,
        grid_spec=grid_spec,
        out_shape=jax.ShapeDtypeStruct((T, D), f32),
        compiler_params=_params(("arbitrary",)),
        name="combine",
    )(dest, y, h2, rout, norm_w.reshape(1, D))


def _dispatch_plan(rout, cnt, bm):
    T = rout.shape[0]
    NB = 2 * T // bm + N_EXPERTS
    e_tk = rout[:, 0:2].astype(jnp.int32)
    rank_tk = rout[:, 4:6].astype(jnp.int32)
    counts = cnt[0, N_GROUPS:N_GROUPS + N_EXPERTS].astype(jnp.int32)
    padded = (counts + bm - 1) // bm * bm
    pend = jnp.cumsum(padded)
    pstart = pend - padded
    first_row = jnp.arange(NB, dtype=jnp.int32) * bm
    blk_e = jnp.minimum(jnp.sum((pend[None, :] <= first_row[:, None]).astype(jnp.int32), axis=1),
                        N_EXPERTS - 1)
    nused = (pend[-1] // bm).astype(jnp.int32).reshape(1)
    of_blk = blk_e[:, None] == jnp.arange(N_EXPERTS, dtype=jnp.int32)
    row_end = jnp.sum(jnp.where(of_blk, pstart + counts, 0), axis=1)
    nvalid = jnp.where(first_row < pend[-1], jnp.clip(row_end - first_row, 0, bm), 0).astype(jnp.int32)
    onehot = e_tk[:, :, None] == jnp.arange(N_EXPERTS, dtype=jnp.int32)
    dest = jnp.sum(jnp.where(onehot, pstart, 0), axis=-1) + rank_tk
    pad_n = padded - counts
    nbits = bm.bit_length() - 1
    bits = (pad_n[:, None] >> jnp.arange(nbits, dtype=jnp.int32)) & 1
    tail_chunks = 2 * (NB - nused[0])
    zcnt = jnp.sum(bits, axis=0) + jnp.where(jnp.arange(nbits) == nbits - 1, tail_chunks, 0)
    pads = ((pstart + counts).astype(jnp.int32), pad_n.astype(jnp.int32), zcnt.astype(jnp.int32),
            jnp.stack([pend[-1], tail_chunks]).astype(jnp.int32))
    return blk_e, nused, nvalid, dest.reshape(2 * T).astype(jnp.int32), pads


def _tile(n, want):
    t = min(n, want)
    while n % t:
        t //= 2
    return t


def kernel(x, mem, positions, norm_mix_w, w_in, dn_conv_w, dn_a_log, dn_dt_bias, ret_gn_w, dn_norm_w,
           w_out, norm_xq_w, norm_mem_w, w_xq, w_xkv, w_xo, norm_moe_w, w_group_router, b_group_router,
           w_expert_router, b_expert_router, w_gate, w_up, w_down, norm_final_w):
    B, S, D = x.shape
    M = mem.shape[1]
    T = B * S
    depth = w_in.shape[0]
    n_main = 8 * HEADS * HD
    h = x.reshape(T, D)
    pos_col = positions.astype(f32).reshape(T, 1)
    mem2 = mem.reshape(B * M, D)
    hg = HEADS
    bm = 512
    for l in range(depth):
        w_in_t = jnp.swapaxes(w_in[l], 0, 1)
        w_main = _cast_rows_t(w_in_t, n_main, 512)
        w_ba = jnp.pad(w_in_t[n_main:].T, ((0, 0), (0, LANES - 2 * HEADS))).astype(bf16)
        E, _, F = w_gate[l].shape
        expert_w = [w_gate[l].reshape(E * D, F), w_up[l].reshape(E * D, F), w_down[l].reshape(E * F, D)]
        proj, ba, wg, wu, wd = _in_proj(h, norm_mix_w[l], w_main, w_ba, expert_w, _tile(T, 1024), 1024)
        wg, wu, wd = wg.reshape(E, D, F), wu.reshape(E, D, F), wd.reshape(E, F, D)
        mix_r = _retention(proj, pos_col, ret_gn_w[l], B, S, hg)
        mix_d = _gdn(proj, ba, dn_conv_w[l], dn_a_log[l], dn_dt_bias[l], dn_norm_w[l], B, S, hg)
        h = _mm_res(mix_r, mix_d, w_out[l].astype(bf16), h, _tile(T, 512), D)
        kv = _norm_mm(mem2, norm_mem_w[l], w_xkv[l].astype(bf16), _tile(B * M, 512), _tile(2 * D, 1024), "xkv")
        w_r = jnp.pad(jnp.concatenate([w_group_router[l], w_expert_router[l]], axis=1),
                      ((0, 0), (0, LANES - N_GROUPS - N_EXPERTS))).astype(bf16)
        b_r = jnp.pad(jnp.concatenate([b_group_router[l], b_expert_router[l]]),
                      (0, LANES - N_GROUPS - N_EXPERTS)).reshape(1, LANES).astype(f32)
        h, hn, rout, cnt = _xattn_router(h, norm_xq_w[l], w_xq[l].astype(bf16), kv, w_xo[l].astype(bf16),
                                         norm_moe_w[l], w_r, b_r, B, S, M, _tile(S, 512))
        blk_e, nused, nvalid, dest, pads = _dispatch_plan(rout, cnt, bm)
        xs = _dispatch(hn, dest, pads, blk_e.shape[0] * bm, _tile(T, 1024), D // (2 * LANES), bm)
        y = _ffn(xs, wg, wu, wd, blk_e, nused, nvalid, bm)
        if l + 1 < depth:
            raise NotImplementedError("the fused MoE combine applies the final norm: depth 1 only")
        h = _combine(y, h, rout, dest, norm_final_w, _tile(T, 256))
    return h.reshape(B, S, D)
```

```python
import functools

import numpy as np
import jax
import jax.numpy as jnp
from jax import lax
from jax.experimental import pallas as pl
from jax.experimental.pallas import tpu as pltpu

f32 = jnp.float32
bf16 = jnp.bfloat16

EPS = 1e-6
HEADS = 8
HD = 128
LANES = 128
XA_HEADS = 4
CONV_W = 4
ROPE_BASE = 10000.0
N_GROUPS = 4
EXP_PER_GROUP = 8
N_EXPERTS = N_GROUPS * EXP_PER_GROUP
RET_CHUNK = 256
DN_CHUNK = 128
VMEM_LIMIT = 56 * 1024 * 1024


def _params(sem, vmem=VMEM_LIMIT):
    return pltpu.CompilerParams(dimension_semantics=sem, vmem_limit_bytes=vmem)


def _dot(a, b):
    return jnp.dot(a.astype(bf16), b.astype(bf16), preferred_element_type=f32)


def _dot_nt(a, b):
    return lax.dot_general(a.astype(bf16), b.astype(bf16), (((1,), (1,)), ((), ())),
                           preferred_element_type=f32)


def _dot_tn(a, b):
    return lax.dot_general(a.astype(bf16), b.astype(bf16), (((0,), (0,)), ((), ())),
                           preferred_element_type=f32)


def _split3(a):
    hi = a.astype(bf16)
    r = a - hi.astype(f32)
    mid = r.astype(bf16)
    lo = (r - mid.astype(f32)).astype(bf16)
    return hi, mid, lo


def _sigmoid(x):
    return 1.0 / (1.0 + jnp.exp(-x))


def _silu(x):
    hx = 0.5 * x
    return hx + hx * jnp.tanh(hx)


def _rms_scale(x):
    return lax.rsqrt(jnp.mean(x * x, axis=-1, keepdims=True) + EPS)


def _cast_t_kernel(w_ref, tail_ref, o_ref, otail_ref):
    o_ref[...] = w_ref[...].T.astype(bf16)

    @pl.when(pl.program_id(0) == 0)
    def _():
        r = tail_ref.shape[0]
        pick = (lax.broadcasted_iota(jnp.int32, (r, LANES), 0)
                == lax.broadcasted_iota(jnp.int32, (r, LANES), 1))
        otail_ref[...] = _dot_tn(tail_ref[...], jnp.where(pick, 1.0, 0.0)).astype(bf16)


def _cast_rows_t(wt, n, tn):
    N, K = wt.shape
    r = N - n
    assert n % r == 0 and r % 8 == 0 and r <= LANES
    return pl.pallas_call(
        _cast_t_kernel,
        grid=(n // tn,),
        in_specs=[pl.BlockSpec((tn, K), lambda j: (j, 0)),
                  pl.BlockSpec((r, K), lambda j: (n // r, 0))],
        out_specs=[pl.BlockSpec((K, tn), lambda j: (0, j)),
                   pl.BlockSpec((K, LANES), lambda j: (0, 0))],
        out_shape=[jax.ShapeDtypeStruct((K, n), bf16), jax.ShapeDtypeStruct((K, LANES), bf16)],
        compiler_params=_params(("arbitrary",)),
        name="cast_w_in",
    )(wt, wt)


def _side_cast_step(step, nsteps, srcs, dsts, stage_in, stage_out, sem_in, sem_out):
    slot = step % 2
    mats = range(len(srcs))

    def in_copy(m, st, sl):
        r = stage_in[m].shape[1]
        rows = pl.ds(pl.multiple_of(st * r, r), r)
        return pltpu.make_async_copy(srcs[m].at[rows], stage_in[m].at[sl], sem_in.at[m, sl])

    def out_copy(m, st, sl):
        r = stage_out[m].shape[1]
        rows = pl.ds(pl.multiple_of(st * r, r), r)
        return pltpu.make_async_copy(stage_out[m].at[sl], dsts[m].at[rows], sem_out.at[m, sl])

    @pl.when(step == 0)
    def _():
        for m in mats:
            in_copy(m, 0, 0).start()

    @pl.when(step + 1 < nsteps)
    def _():
        for m in mats:
            in_copy(m, step + 1, 1 - slot).start()

    for m in mats:
        in_copy(m, step, slot).wait()

    @pl.when(step >= 2)
    def _():
        for m in mats:
            out_copy(m, step - 2, slot).wait()

    def convert():
        for m in mats:
            stage_out[m][slot] = stage_in[m][slot].astype(bf16)

    def finish():
        for m in mats:
            out_copy(m, step, slot).start()

        @pl.when(step == nsteps - 1)
        def _():
            for m in mats:
                if nsteps >= 2:
                    out_copy(m, step - 1, 1 - slot).wait()
                out_copy(m, step, slot).wait()

    return convert, finish


def _in_proj_kernel(x_ref, nw_ref, w_ref, wba_ref, *rest, n_side):
    srcs, rest = rest[:n_side], rest[n_side:]
    o_ref, ba_ref = rest[:2]
    dsts, rest = rest[2:2 + n_side], rest[2 + n_side:]
    xn_ref = rest[0]
    stage_in, stage_out = rest[1:1 + n_side], rest[1 + n_side:1 + 2 * n_side]
    sem_in, sem_out = rest[1 + 2 * n_side:]

    step = pl.program_id(0) * pl.num_programs(1) + pl.program_id(1)
    nsteps = pl.num_programs(0) * pl.num_programs(1)
    convert, finish = _side_cast_step(step, nsteps, srcs, dsts, stage_in, stage_out, sem_in, sem_out)

    @pl.when(pl.program_id(1) == 0)
    def _():
        x = x_ref[...]
        xn = (x * _rms_scale(x) * nw_ref[...]).astype(bf16)
        xn_ref[...] = xn
        ba_ref[...] = jnp.dot(xn, wba_ref[...], preferred_element_type=f32)

    convert()
    acc = jnp.dot(xn_ref[...], w_ref[...], preferred_element_type=f32)
    for c in range(o_ref.shape[0]):
        o_ref[c] = acc[:, c * LANES:(c + 1) * LANES].astype(bf16)
    finish()


def _in_proj(x2, norm_w, w_main, w_ba, side, tm, tn):
    T, D = x2.shape
    N = w_main.shape[1]
    nsteps = (T // tm) * (N // tn)
    rows = [a.shape[0] // nsteps for a in side]
    assert all(a.shape[0] == r * nsteps and r % 16 == 0 for a, r in zip(side, rows))
    any_spec = pl.BlockSpec(memory_space=pl.ANY)
    return pl.pallas_call(
        functools.partial(_in_proj_kernel, n_side=len(side)),
        grid=(T // tm, N // tn),
        in_specs=[
            pl.BlockSpec((tm, D), lambda i, j: (i, 0)),
            pl.BlockSpec((1, D), lambda i, j: (0, 0)),
            pl.BlockSpec((D, tn), lambda i, j: (0, j)),
            pl.BlockSpec((D, LANES), lambda i, j: (0, 0)),
        ] + [any_spec] * len(side),
        out_specs=[
            pl.BlockSpec((tn // LANES, tm, LANES), lambda i, j: (j, i, 0)),
            pl.BlockSpec((tm, LANES), lambda i, j: (i, 0)),
        ] + [any_spec] * len(side),
        out_shape=[
            jax.ShapeDtypeStruct((N // LANES, T, LANES), bf16),
            jax.ShapeDtypeStruct((T, LANES), f32),
        ] + [jax.ShapeDtypeStruct(a.shape, bf16) for a in side],
        scratch_shapes=[pltpu.VMEM((tm, D), bf16)]
        + [pltpu.VMEM((2, r, a.shape[1]), f32) for a, r in zip(side, rows)]
        + [pltpu.VMEM((2, r, a.shape[1]), bf16) for a, r in zip(side, rows)]
        + [pltpu.SemaphoreType.DMA((len(side), 2)), pltpu.SemaphoreType.DMA((len(side), 2))],
        compiler_params=_params(("arbitrary", "arbitrary")),
        name="in_proj",
    )(x2, norm_w.reshape(1, D), w_main, w_ba, *side)


def _retention_kernel(q_ref, k_ref, v_ref, g_ref, pos_ref, inv_ref, sgn_ref, dmask_ref, qdec_ref,
                      kdec_ref, cdec_ref, gnw_ref, o_ref, state_ref):
    @pl.when(pl.program_id(2) == 0)
    def _():
        state_ref[...] = jnp.zeros_like(state_ref)

    C = pos_ref.shape[0]
    lane = lax.broadcasted_iota(jnp.int32, (C // 2, HD), 1)
    low = lane < HD // 2
    ang = jnp.where(low, pos_ref[0:C // 2, :], pos_ref[C // 2:, :]) * inv_ref[...]
    cos_p = jnp.cos(ang)
    sin_p = jnp.sin(ang)

    def spread(t):
        swapped = pltpu.roll(t, HD // 2, 1)
        return jnp.concatenate([jnp.where(low, t, swapped), jnp.where(low, swapped, t)], axis=0)

    cos2 = spread(cos_p)
    sin2 = spread(sin_p) * sgn_ref[...]
    heads = range(q_ref.shape[0])

    def rope(x):
        return x * cos2 + pltpu.roll(x, HD // 2, 1) * sin2

    qr = [rope(q_ref[i].astype(f32)) for i in heads]
    kr = [rope(k_ref[i].astype(f32)) * (HD ** -0.5) for i in heads]
    state = [state_ref[i] for i in heads]
    s = [_dot_nt(qr[i], kr[i]) * dmask_ref[i] for i in heads]
    cross = [_dot(qr[i], state[i]) * qdec_ref[i] for i in heads]
    upd = [_dot_tn(kr[i] * kdec_ref[i], v_ref[i]) for i in heads]
    o = [_dot(s[i], v_ref[i]) + cross[i] for i in heads]
    for i in heads:
        state_ref[i] = state[i] * cdec_ref[i] + upd[i]
        d = o[i] - jnp.mean(o[i], axis=-1, keepdims=True)
        y = d * lax.rsqrt(jnp.mean(d * d, axis=-1, keepdims=True) + EPS) * gnw_ref[i]
        y = y * _silu(g_ref[i].astype(f32))
        o_ref[:, i * HD:(i + 1) * HD] = y.astype(bf16)


def _retention_tables(C):
    h = np.arange(HEADS, dtype=np.float64)
    log_gamma = np.log1p(-np.exp2(-5.0 - h))
    idx = np.arange(C, dtype=np.float64)
    rel = idx[:, None] - idx[None, :]
    dmask = np.where(rel >= 0, np.exp(log_gamma[:, None, None] * np.where(rel >= 0, rel, 0.0)), 0.0)
    qdec = np.exp(log_gamma[:, None] * (idx + 1.0))
    kdec = np.exp(log_gamma[:, None] * (C - 1.0 - idx))
    cdec = np.exp(log_gamma * C)
    rep = lambda a: np.broadcast_to(a[..., None], a.shape + (LANES,))
    return (jnp.asarray(dmask, f32), jnp.asarray(rep(qdec), f32), jnp.asarray(rep(kdec), f32),
            jnp.asarray(np.broadcast_to(cdec[:, None, None], (HEADS, 1, LANES)), f32))


def _retention(proj, pos_col, gn_w, B, S, hg):
    C = RET_CHUNK
    NC = S // C
    T = B * S
    G = HEADS // hg
    half = HD // 2
    inv = ROPE_BASE ** (-np.arange(half, dtype=np.float32) / half)
    inv2 = jnp.asarray(np.concatenate([inv, inv]).reshape(1, HD), f32)
    sgn = jnp.asarray(np.concatenate([-np.ones(half), np.ones(half)]).reshape(1, HD), f32)
    dmask, qdec, kdec, cdec = _retention_tables(C)

    def slab(off):
        return pl.BlockSpec((hg, C, HD), lambda b, g, n: (off // hg + g, b * NC + n, 0))

    def table(shape):
        return pl.BlockSpec((hg,) + shape, lambda b, g, n: (g,) + (0,) * len(shape))

    return pl.pallas_call(
        _retention_kernel,
        grid=(B, G, NC),
        in_specs=[
            slab(0), slab(HEADS), slab(2 * HEADS), slab(3 * HEADS),
            pl.BlockSpec((C, 1), lambda b, g, n: (b * NC + n, 0)),
            pl.BlockSpec((1, HD), lambda b, g, n: (0, 0)),
            pl.BlockSpec((1, HD), lambda b, g, n: (0, 0)),
            table((C, C)), table((C, LANES)), table((C, LANES)), table((1, LANES)), table((1, HD)),
        ],
        out_specs=pl.BlockSpec((C, hg * HD), lambda b, g, n: (b * NC + n, g)),
        out_shape=jax.ShapeDtypeStruct((T, HEADS * HD), bf16),
        scratch_shapes=[pltpu.VMEM((hg, HD, HD), f32)],
        compiler_params=_params(("parallel", "parallel", "arbitrary")),
        name="retention",
    )(proj, proj, proj, proj, pos_col, inv2, sgn, dmask, qdec, kdec, cdec,
      gn_w.reshape(HEADS, 1, HD))


def _gdn_kernel(q_ref, k_ref, v_ref, z_ref, ba_ref, cw_ref, arow_ref, dtrow_ref, nw_ref, o_ref,
                state_ref, tail_ref, win_ref, *, hg):
    C = q_ref.shape[1]
    first = pl.program_id(2) == 0

    @pl.when(first)
    def _():
        state_ref[...] = jnp.zeros_like(state_ref)
        tail_ref[...] = jnp.zeros_like(tail_ref)

    row = lax.broadcasted_iota(jnp.int32, (C, C), 0)
    col = lax.broadcasted_iota(jnp.int32, (C, C), 1)
    causal = row >= col
    strict = row > col
    same8 = (row // 8) == (col // 8)
    level_masks = []
    s = 8
    while s < C:
        level_masks.append(((row // (2 * s)) == (col // (2 * s))) & ((row // s) != (col // s)))
        s *= 2
    eye = jnp.where(row == col, 1.0, 0.0).astype(f32)
    tri = jnp.where(causal, 1.0, 0.0).astype(bf16)

    ba = ba_ref[...]
    beta_all = _sigmoid(ba)
    xa = ba + dtrow_ref[...]
    softplus = jnp.maximum(xa, 0.0) + jnp.log(1.0 + jnp.exp(-jnp.abs(xa)))
    glog = -jnp.exp(arow_ref[...]) * softplus
    g_hi, g_mid, g_lo = _split3(glog)
    gcum = (jnp.dot(tri, g_hi, preferred_element_type=f32)
            + jnp.dot(tri, g_mid, preferred_element_type=f32)
            + jnp.dot(tri, g_lo, preferred_element_type=f32))
    gcum_t = gcum.T
    lane = lax.broadcasted_iota(jnp.int32, (C, LANES), 1)
    sub = lax.broadcasted_iota(jnp.int32, (LANES, C), 0)

    def conv_silu(x_ref, kind, i):
        slot = kind * hg + i
        win_ref[slot, 0:8, :] = tail_ref[slot]
        win_ref[slot, 8:8 + C, :] = x_ref[i].astype(f32)
        w = cw_ref[kind, i]
        y = win_ref[slot, 8:8 + C, :] * w[CONV_W - 1:CONV_W, :]
        for t in range(1, CONV_W):
            y = y + win_ref[slot, 8 - t:8 - t + C, :] * w[CONV_W - 1 - t:CONV_W - t, :]
        tail_ref[slot] = win_ref[slot, C:C + 8, :]
        return _silu(y)

    heads = range(hg)
    hd = [pl.program_id(1) * hg + i for i in heads]
    q = [conv_silu(q_ref, 0, i) for i in heads]
    k = [conv_silu(k_ref, 1, i) for i in heads]
    v = [conv_silu(v_ref, 2, i) for i in heads]
    q = [x * lax.rsqrt(jnp.sum(x * x, axis=-1, keepdims=True) + EPS) * (HD ** -0.5) for x in q]
    k = [x * lax.rsqrt(jnp.sum(x * x, axis=-1, keepdims=True) + EPS) for x in k]

    bcol = [jnp.sum(jnp.where(lane == h, beta_all, 0.0), axis=1, keepdims=True) for h in hd]
    gcol = [jnp.sum(jnp.where(lane == h + HEADS, gcum, 0.0), axis=1, keepdims=True) for h in hd]
    grow = [jnp.sum(jnp.where(sub == h + HEADS, gcum_t, 0.0), axis=0, keepdims=True) for h in hd]
    glast = [g[C - 1:C, :] for g in gcol]
    gam = [jnp.where(causal, jnp.exp(jnp.where(causal, gc - gr, 0.0)), 0.0) for gc, gr in zip(gcol, grow)]
    egc = [jnp.exp(g) for g in gcol]

    kk = [_dot_nt(x, x) for x in k]
    qk = [_dot_nt(x, y) for x, y in zip(q, k)]
    a = [jnp.where(strict, m * g * b, 0.0) for m, g, b in zip(kk, gam, bcol)]
    qk = [m * g for m, g in zip(qk, gam)]

    d = [jnp.where(same8, m, 0.0) for m in a]
    d2 = [_dot(m, m) for m in d]
    d3 = [_dot(m, m2) for m, m2 in zip(d, d2)]
    d4 = [_dot(m2, m2) for m2 in d2]
    x = [eye - m + m2 - m3 for m, m2, m3 in zip(d, d2, d3)]
    x = [xi + _dot(xi, m4) for xi, m4 in zip(x, d4)]
    for mask in level_masks:
        t = [_dot(xi, jnp.where(mask, m, 0.0)) for xi, m in zip(x, a)]
        x = [xi - _dot(ti, xi) for xi, ti in zip(x, t)]

    rhs = [jnp.concatenate([vi * b, ki * (b * e)], axis=1) for vi, ki, b, e in zip(v, k, bcol, egc)]
    sol = [_dot(xi, r) for xi, r in zip(x, rhs)]
    q_dec = [qi * e for qi, e in zip(q, egc)]
    k_dec = [ki * jnp.exp(gl - gc) for ki, gl, gc in zip(k, glast, gcol)]

    state = [state_ref[i] for i in heads]
    ws = [_dot(jnp.concatenate([s_[:, HD:], qd], axis=0), st) for s_, qd, st in zip(sol, q_dec, state)]
    v_new = [s_[:, :HD] - w_[:C] for s_, w_ in zip(sol, ws)]
    o = [w_[C:] + _dot(m, vn) for w_, m, vn in zip(ws, qk, v_new)]
    upd = [_dot_tn(kd, vn) for kd, vn in zip(k_dec, v_new)]
    for i in heads:
        state_ref[i] = state[i] * jnp.exp(glast[i]) + upd[i]
        y = o[i] * _rms_scale(o[i]) * nw_ref[...]
        y = y * _silu(z_ref[i].astype(f32))
        o_ref[:, i * HD:(i + 1) * HD] = y.astype(bf16)


def _gdn(proj, ba, conv_w, a_log, dt_bias, norm_w, B, S, hg):
    C = DN_CHUNK
    NC = S // C
    T = B * S
    G = HEADS // hg
    cw = conv_w.reshape(CONV_W, 3, HEADS, HD).transpose(1, 2, 0, 3)
    pad = jnp.zeros((LANES - 2 * HEADS,), f32)
    arow = jnp.concatenate([jnp.zeros((HEADS,), f32), a_log.astype(f32), pad]).reshape(1, LANES)
    dtrow = jnp.concatenate([jnp.zeros((HEADS,), f32), dt_bias.astype(f32), pad]).reshape(1, LANES)

    def slab(off):
        return pl.BlockSpec((hg, C, HD), lambda b, g, n: (off // hg + g, b * NC + n, 0))

    row_spec = pl.BlockSpec((1, LANES), lambda b, g, n: (0, 0))
    return pl.pallas_call(
        functools.partial(_gdn_kernel, hg=hg),
        grid=(B, G, NC),
        in_specs=[
            slab(4 * HEADS), slab(5 * HEADS), slab(6 * HEADS), slab(7 * HEADS),
            pl.BlockSpec((C, LANES), lambda b, g, n: (b * NC + n, 0)),
            pl.BlockSpec((3, hg, CONV_W, HD), lambda b, g, n: (0, g, 0, 0)),
            row_spec, row_spec, row_spec,
        ],
        out_specs=pl.BlockSpec((C, hg * HD), lambda b, g, n: (b * NC + n, g)),
        out_shape=jax.ShapeDtypeStruct((T, HEADS * HD), bf16),
        scratch_shapes=[
            pltpu.VMEM((hg, HD, HD), f32),
            pltpu.VMEM((3 * hg, 8, HD), f32),
            pltpu.VMEM((3 * hg, C + 8, HD), f32),
        ],
        compiler_params=_params(("parallel", "parallel", "arbitrary")),
        name="gdn",
    )(proj, proj, proj, proj, ba, cw, arow, dtrow, norm_w.reshape(1, HD))


def _mm_res_kernel(a1_ref, a2_ref, w_ref, r_ref, o_ref):
    a = jnp.concatenate([a1_ref[...], a2_ref[...]], axis=1)
    o_ref[...] = r_ref[...] + jnp.dot(a, w_ref[...], preferred_element_type=f32)


def _mm_res(a1, a2, w, res, tm, tn):
    T, K1 = a1.shape
    K2 = a2.shape[1]
    N = w.shape[1]
    return pl.pallas_call(
        _mm_res_kernel,
        grid=(T // tm, N // tn),
        in_specs=[
            pl.BlockSpec((tm, K1), lambda i, j: (i, 0)),
            pl.BlockSpec((tm, K2), lambda i, j: (i, 0)),
            pl.BlockSpec((K1 + K2, tn), lambda i, j: (0, j)),
            pl.BlockSpec((tm, tn), lambda i, j: (i, j)),
        ],
        out_specs=pl.BlockSpec((tm, tn), lambda i, j: (i, j)),
        out_shape=jax.ShapeDtypeStruct((T, N), f32),
        compiler_params=_params(("parallel", "parallel")),
        name="out_proj",
    )(a1, a2, w, res)


def _norm_mm_kernel(x_ref, nw_ref, w_ref, o_ref, xn_ref):
    @pl.when(pl.program_id(1) == 0)
    def _():
        x = x_ref[...]
        xn_ref[...] = (x * _rms_scale(x) * nw_ref[...]).astype(bf16)

    o_ref[...] = jnp.dot(xn_ref[...], w_ref[...], preferred_element_type=f32).astype(o_ref.dtype)


def _norm_mm(x2, norm_w, w, tm, tn, name):
    T, D = x2.shape
    N = w.shape[1]
    return pl.pallas_call(
        _norm_mm_kernel,
        grid=(T // tm, N // tn),
        in_specs=[
            pl.BlockSpec((tm, D), lambda i, j: (i, 0)),
            pl.BlockSpec((1, D), lambda i, j: (0, 0)),
            pl.BlockSpec((D, tn), lambda i, j: (0, j)),
        ],
        out_specs=pl.BlockSpec((tm, tn), lambda i, j: (i, j)),
        out_shape=jax.ShapeDtypeStruct((T, N), bf16),
        scratch_shapes=[pltpu.VMEM((tm, D), bf16)],
        compiler_params=_params(("parallel", "arbitrary")),
        name=name,
    )(x2, norm_w.reshape(1, D), w)


def _xattn_router_kernel(h_ref, nq_ref, wq_ref, k_ref, v_ref, wo_ref, nm_ref, wr_ref, b_ref,
                         h2_ref, hn_ref, rt_ref, rtt_ref, cnt_ref, run_ref):
    @pl.when((pl.program_id(0) == 0) & (pl.program_id(1) == 0))
    def _():
        run_ref[...] = jnp.zeros_like(run_ref)

    h1 = h_ref[...]
    D = h1.shape[1]
    dh = D // XA_HEADS
    xn = (h1 * _rms_scale(h1) * nq_ref[...]).astype(bf16)
    q = jnp.dot(xn, wq_ref[...], preferred_element_type=f32).astype(bf16)
    heads = []
    for h in range(XA_HEADS):
        sl = slice(h * dh, (h + 1) * dh)
        s = _dot_nt(q[:, sl], k_ref[:, sl]) * (dh ** -0.5)
        p = jnp.exp(s - jnp.max(s, axis=-1, keepdims=True))
        p = p / jnp.sum(p, axis=-1, keepdims=True)
        heads.append(_dot(p, v_ref[:, sl]).astype(bf16))
    h2 = h1 + jnp.dot(jnp.concatenate(heads, axis=1), wo_ref[...], preferred_element_type=f32)
    h2_ref[...] = h2
    _route(h2, nm_ref, wr_ref, b_ref, hn_ref, rt_ref, rtt_ref, cnt_ref, run_ref)


def _xattn_router(h1, norm_q, w_q, kv, w_o, norm_moe, w_r, bias_row, B, S, M, tm):
    T, D = h1.shape
    nt = S // tm
    g = D // (2 * LANES)
    tok = lambda b, i: (b * nt + i, 0)
    const = lambda b, i: (0, 0)
    resident = dict(pipeline_mode=pl.Buffered(1))
    return pl.pallas_call(
        _xattn_router_kernel,
        grid=(B, nt),
        in_specs=[
            pl.BlockSpec((tm, D), tok),
            pl.BlockSpec((1, D), const),
            pl.BlockSpec((D, D), const, **resident),
            pl.BlockSpec((M, D), lambda b, i: (b, 0)),
            pl.BlockSpec((M, D), lambda b, i: (b, 1)),
            pl.BlockSpec((D, D), const, **resident),
            pl.BlockSpec((1, D), const),
            pl.BlockSpec((D, LANES), const),
            pl.BlockSpec((1, LANES), const),
        ],
        out_specs=[
            pl.BlockSpec((tm, D), tok),
            pl.BlockSpec((tm * g, LANES), tok),
            pl.BlockSpec((tm, LANES), tok),
            pl.BlockSpec((8, tm), lambda b, i: (0, b * nt + i)),
            pl.BlockSpec((1, LANES), const),
        ],
        out_shape=[
            jax.ShapeDtypeStruct((T, D), f32),
            jax.ShapeDtypeStruct((T * g, LANES), jnp.uint32),
            jax.ShapeDtypeStruct((T, LANES), f32),
            jax.ShapeDtypeStruct((8, T), f32),
            jax.ShapeDtypeStruct((1, LANES), f32),
        ],
        scratch_shapes=[pltpu.VMEM((1, LANES), f32)],
        compiler_params=_params(("arbitrary", "arbitrary")),
        name="xattn_router",
    )(h1, norm_q.reshape(1, D), w_q, kv, kv, w_o, norm_moe.reshape(1, D), w_r, bias_row)


def _pack_pair(a, b):
    au = lax.bitcast_convert_type(a.astype(f32), jnp.uint32)
    bu = lax.bitcast_convert_type(b.astype(f32), jnp.uint32)
    return (au >> 16) | bu


def _unpack_pair(u):
    a = lax.bitcast_convert_type(u << 16, f32)
    b = lax.bitcast_convert_type(u & jnp.uint32(0xFFFF0000), f32)
    return a, b


def _store_slabs(ref, words):
    n = words.shape[0]
    g = words.shape[1] // LANES
    for c in range(g):
        ref[pl.ds(c, n, stride=g), :] = words[:, c * LANES:(c + 1) * LANES]


def _load_slabs(ref, n, g):
    return [ref[pl.ds(c, n, stride=g), :] for c in range(g)]


def _route(x, nw_ref, wr_ref, b_ref, hn_ref, o_ref, ot_ref, cnt_ref, run_ref):
    xn = x * _rms_scale(x) * nw_ref[...]
    xhi = xn.astype(bf16)
    logits = jnp.dot(xhi, wr_ref[...], preferred_element_type=f32) + b_ref[...]
    lane = lax.broadcasted_iota(jnp.int32, logits.shape, 1).astype(f32)
    neg = -jnp.inf
    big = float(LANES)

    def first_max(vals):
        m = jnp.max(vals, axis=-1, keepdims=True)
        return m, jnp.min(jnp.where(vals == m, lane, big), axis=-1, keepdims=True)

    is_group = lane < N_GROUPS
    gmax, gsel = first_max(jnp.where(is_group, logits, neg))
    gsum = jnp.sum(jnp.where(is_group, jnp.exp(logits - gmax), 0.0), axis=-1, keepdims=True)
    g_w = 1.0 / gsum
    lo = N_GROUPS + EXP_PER_GROUP * gsel
    in_group = (lane >= lo) & (lane < lo + EXP_PER_GROUP)
    el = jnp.where(in_group, logits, neg)
    l1, i1 = first_max(el)
    l2, i2 = first_max(jnp.where(lane == i1, neg, el))
    esum = jnp.sum(jnp.where(in_group, jnp.exp(logits - l1), 0.0), axis=-1, keepdims=True)
    p1 = 1.0 / esum
    p2 = jnp.exp(l2 - l1) / esum
    gate1 = g_w * p1 / (p1 + p2)
    gate2 = g_w * p2 / (p1 + p2)

    tm = x.shape[0]
    chosen = jnp.where((lane == i1) | (lane == i2), 1.0, 0.0)
    earlier = (lax.broadcasted_iota(jnp.int32, (tm, tm), 0) > lax.broadcasted_iota(jnp.int32, (tm, tm), 1))
    before = _dot(jnp.where(earlier, 1.0, 0.0), chosen) + run_ref[...]
    rank1 = jnp.sum(jnp.where(lane == i1, before, 0.0), axis=-1, keepdims=True)
    rank2 = jnp.sum(jnp.where(lane == i2, before, 0.0), axis=-1, keepdims=True)
    run_ref[...] = run_ref[...] + jnp.sum(chosen, axis=0, keepdims=True)
    cnt_ref[...] = run_ref[...]

    cols = (i1 - N_GROUPS, i2 - N_GROUPS, gate1, gate2, rank1, rank2)
    out = jnp.zeros_like(logits)
    for c, val in enumerate(cols):
        out = jnp.where(lane == c, val, out)
    o_ref[...] = out
    ot_ref[...] = out.T[0:ot_ref.shape[0], :]
    half = x.shape[1] // 2
    _store_slabs(hn_ref, _pack_pair(xhi[:, :half], xhi[:, half:]))


def _dispatch_kernel(dest_ref, pad_lo_ref, pad_n_ref, zcnt_ref, tail_ref, hn_ref, xs_hbm, zeros_ref, sem,
                     zsem, *, tb, g, bm):
    i = pl.program_id(0)
    base = i * tb
    sizes = [1 << b for b in range(bm.bit_length() - 1)]

    def zero_copy(first, p):
        return pltpu.make_async_copy(zeros_ref.at[pl.ds(0, p * g)], xs_hbm.at[pl.ds(first, p * g)], zsem)

    @pl.when(i == 0)
    def _():
        zeros_ref[...] = jnp.zeros_like(zeros_ref)

        def per_expert(e, carry):
            row = pad_lo_ref[e]
            n = pad_n_ref[e]
            for p in sizes:
                hit = (n & p) != 0

                @pl.when(hit)
                def _(row=row, p=p):
                    zero_copy(pl.multiple_of(row * g, g), p).start()

                row = row + jnp.where(hit, p, 0)
            return carry

        lax.fori_loop(0, N_EXPERTS, per_expert, 0)

        def per_tail_chunk(c, carry):
            zero_copy(pl.multiple_of((tail_ref[0] + c * sizes[-1]) * g, g), sizes[-1]).start()
            return carry

        lax.fori_loop(0, tail_ref[1], per_tail_chunk, 0)

    def body(r, carry):
        src = hn_ref.at[pl.ds(pl.multiple_of(r * g, g), g)]
        for kk in range(2):
            d = dest_ref[kk * (tb * pl.num_programs(0)) + base + r]
            pltpu.make_async_copy(src, xs_hbm.at[pl.ds(pl.multiple_of(d * g, g), g)],
                                  sem.at[kk]).start(priority=kk)
        return carry

    lax.fori_loop(0, tb, body, 0, unroll=8)
    for kk in range(2):
        pltpu.make_async_copy(hn_ref, hn_ref, sem.at[kk]).wait()

    @pl.when(i == pl.num_programs(0) - 1)
    def _():
        for b, p in enumerate(sizes):
            def wait_one(_, carry, p=p):
                zero_copy(0, p).wait()
                return carry
            lax.fori_loop(0, zcnt_ref[b], wait_one, 0)


def _dispatch(hn, dest, pads, P, tb, g, bm):
    T = hn.shape[0] // g
    grid_spec = pltpu.PrefetchScalarGridSpec(
        num_scalar_prefetch=5,
        grid=(T // tb,),
        in_specs=[pl.BlockSpec((tb * g, LANES), lambda i, *_: (i, 0))],
        out_specs=pl.BlockSpec(memory_space=pl.ANY),
        scratch_shapes=[pltpu.VMEM((bm // 2 * g, LANES), jnp.uint32),
                        pltpu.SemaphoreType.DMA((2,)), pltpu.SemaphoreType.DMA(())],
    )
    return pl.pallas_call(
        functools.partial(_dispatch_kernel, tb=tb, g=g, bm=bm),
        grid_spec=grid_spec,
        out_shape=jax.ShapeDtypeStruct((P * g, LANES), jnp.uint32),
        compiler_params=_params(("arbitrary",)),
        name="dispatch",
    )(dest, *pads, hn)


def _ffn_kernel(blk_e_ref, nused_ref, nvalid_ref, x_ref, wg_ref, wu_ref, wd_ref, y_ref, *, bm):
    nv = nvalid_ref[pl.program_id(0)]
    half = wg_ref.shape[0] // 2
    g = half // LANES

    def run(rows):
        pairs = [_unpack_pair(w) for w in _load_slabs(x_ref.at[pl.ds(0, rows * g)], rows, g)]
        x = jnp.concatenate([p[0].astype(bf16) for p in pairs] + [p[1].astype(bf16) for p in pairs], axis=1)
        gate = jnp.dot(x, wg_ref[...], preferred_element_type=f32)
        up = jnp.dot(x, wu_ref[...], preferred_element_type=f32)
        hmid = (_silu(gate) * up).astype(bf16)
        for s in range(g):
            ys = jnp.dot(hmid, wd_ref[:, 2 * s * LANES:2 * (s + 1) * LANES], preferred_element_type=f32)
            y_ref[pl.ds(s, rows, stride=g), :] = _pack_pair(ys[:, :LANES].astype(bf16),
                                                            ys[:, LANES:].astype(bf16))

    @pl.when(nv > bm // 2)
    def _():
        run(bm)

    @pl.when((nv > 0) & (nv <= bm // 2))
    def _():
        run(bm // 2)
        y_ref[pl.ds(bm // 2 * g, bm // 2 * g), :] = jnp.zeros((bm // 2 * g, LANES), jnp.uint32)

    @pl.when(nv == 0)
    def _():
        y_ref[...] = jnp.zeros_like(y_ref)


def _ffn(xs, wg, wu, wd, blk_e, nused, nvalid, bm):
    D, F = wg.shape[1:]
    g = D // (2 * LANES)
    NB = blk_e.shape[0]
    grid_spec = pltpu.PrefetchScalarGridSpec(
        num_scalar_prefetch=3,
        grid=(NB,),
        in_specs=[
            pl.BlockSpec((bm * g, LANES), lambda i, be, n, nv: (jnp.minimum(i, n[0] - 1), 0)),
            pl.BlockSpec((None, D, F), lambda i, be, n, nv: (be[i], 0, 0)),
            pl.BlockSpec((None, D, F), lambda i, be, n, nv: (be[i], 0, 0)),
            pl.BlockSpec((None, F, D), lambda i, be, n, nv: (be[i], 0, 0)),
        ],
        out_specs=pl.BlockSpec((bm * g, LANES), lambda i, be, n, nv: (i, 0)),
    )
    return pl.pallas_call(
        functools.partial(_ffn_kernel, bm=bm),
        grid_spec=grid_spec,
        out_shape=jax.ShapeDtypeStruct(xs.shape, jnp.uint32),
        compiler_params=_params(("arbitrary",)),
        name="ffn",
    )(blk_e, nused, nvalid, xs, wg, wu, wd)


def _combine_kernel(dest_ref, y_hbm, h_ref, rt_ref, nw_ref, o_ref, ybuf, sem):
    i = pl.program_id(0)
    n = pl.num_programs(0)
    tm = h_ref.shape[0]
    nslots = ybuf.shape[0]

    D = h_ref.shape[1]
    half = D // 2
    g = half // LANES

    def row_copy(blk, r, kk, slot_):
        d = dest_ref[kk * (tm * n) + blk * tm + r]
        first = r * g if isinstance(r, int) else pl.multiple_of(r * g, g)
        return pltpu.make_async_copy(y_hbm.at[pl.ds(pl.multiple_of(d * g, g), g)],
                                     ybuf.at[slot_, kk, pl.ds(first, g)], sem.at[slot_])

    def wait_slot(slot_):
        pltpu.make_async_copy(ybuf.at[slot_], ybuf.at[slot_], sem.at[slot_]).wait()

    @pl.when(i == 0)
    def _():
        for first in range(nslots - 1):
            def body(r, carry, first=first):
                for kk in range(2):
                    row_copy(jnp.minimum(first, n - 1), r, kk, first).start(priority=kk)
                return carry
            lax.fori_loop(0, tm, body, 0, unroll=8)

    nxt = jnp.minimum(i + nslots - 1, n - 1)
    pieces = 4 * g
    per = tm // pieces

    def step(slot_):
        def issue_batch(p):
            for r in range(p * per, (p + 1) * per):
                for kk in range(2):
                    row_copy(nxt, r, kk, (slot_ + nslots - 1) % nslots).start(priority=kk)

        wait_slot(slot_)
        rt = rt_ref[...]
        g1 = rt[:, 2:3]
        g2 = rt[:, 3:4]
        cols = [slice(c * LANES, (c + 1) * LANES) for c in range(2 * g)]
        hs = []
        for s in range(g):
            a1 = _unpack_pair(ybuf[slot_, 0, pl.ds(s, tm, stride=g), :])
            a2 = _unpack_pair(ybuf[slot_, 1, pl.ds(s, tm, stride=g), :])
            for hw in range(2):
                hs.append(h_ref[:, cols[2 * s + hw]] + g1 * a1[hw] + g2 * a2[hw])
                issue_batch(2 * s + hw)
        ms = sum(jnp.sum(x * x, axis=-1, keepdims=True) for x in hs) / D
        scale = lax.rsqrt(ms + EPS)
        for c in range(2 * g):
            o_ref[:, cols[c]] = hs[c] * scale * nw_ref[:, cols[c]]
            issue_batch(2 * g + c)

    for s in range(nslots):
        @pl.when(i % nslots == s)
        def _(s=s):
            step(s)

    @pl.when(i == n - 1)
    def _():
        for ahead in range(1, nslots):
            wait_slot((i + ahead) % nslots)


def _combine(y, h2, rout, dest, norm_w, tm):
    T, D = h2.shape
    g = D // (2 * LANES)
    grid_spec = pltpu.PrefetchScalarGridSpec(
        num_scalar_prefetch=1,
        grid=(T // tm,),
        in_specs=[
            pl.BlockSpec(memory_space=pl.ANY),
            pl.BlockSpec((tm, D), lambda i, d: (i, 0)),
            pl.BlockSpec((tm, LANES), lambda i, d: (i, 0)),
            pl.BlockSpec((1, D), lambda i, d: (0, 0)),
        ],
        out_specs=pl.BlockSpec((tm, D), lambda i, d: (i, 0)),
        scratch_shapes=[pltpu.VMEM((3, 2, tm * g, LANES), jnp.uint32), pltpu.SemaphoreType.DMA((3,))],
    )
    return pl.pallas_call(
        _combine_kernel,
        grid_spec=grid_spec,
        out_shape=jax.ShapeDtypeStruct((T, D), f32),
        compiler_params=_params(("arbitrary",)),
        name="combine",
    )(dest, y, h2, rout, norm_w.reshape(1, D))


def _dispatch_plan(rout_t, cnt, bm):
    T = rout_t.shape[1]
    NB = 2 * T // bm + N_EXPERTS
    e_kt = rout_t[0:2].astype(jnp.int32)
    rank_kt = rout_t[4:6].astype(jnp.int32)
    counts = cnt[0, N_GROUPS:N_GROUPS + N_EXPERTS].astype(jnp.int32)
    padded = (counts + bm - 1) // bm * bm
    pend = jnp.cumsum(padded)
    pstart = pend - padded
    first_row = jnp.arange(NB, dtype=jnp.int32) * bm
    blk_e = jnp.minimum(jnp.sum((pend[None, :] <= first_row[:, None]).astype(jnp.int32), axis=1),
                        N_EXPERTS - 1)
    nused = (pend[-1] // bm).astype(jnp.int32).reshape(1)
    of_blk = blk_e[:, None] == jnp.arange(N_EXPERTS, dtype=jnp.int32)
    row_end = jnp.sum(jnp.where(of_blk, pstart + counts, 0), axis=1)
    nvalid = jnp.where(first_row < pend[-1], jnp.clip(row_end - first_row, 0, bm), 0).astype(jnp.int32)
    onehot = e_kt[None] == jnp.arange(N_EXPERTS, dtype=jnp.int32)[:, None, None]
    dest = jnp.sum(jnp.where(onehot, pstart[:, None, None], 0), axis=0) + rank_kt
    pad_n = padded - counts
    nbits = bm.bit_length() - 1
    bits = (pad_n[:, None] >> jnp.arange(nbits, dtype=jnp.int32)) & 1
    tail_chunks = 2 * (NB - nused[0])
    zcnt = jnp.sum(bits, axis=0) + jnp.where(jnp.arange(nbits) == nbits - 1, tail_chunks, 0)
    pads = ((pstart + counts).astype(jnp.int32), pad_n.astype(jnp.int32), zcnt.astype(jnp.int32),
            jnp.stack([pend[-1], tail_chunks]).astype(jnp.int32))
    return blk_e, nused, nvalid, dest.reshape(2 * T).astype(jnp.int32), pads


def _tile(n, want):
    t = min(n, want)
    while n % t:
        t //= 2
    return t


def kernel(x, mem, positions, norm_mix_w, w_in, dn_conv_w, dn_a_log, dn_dt_bias, ret_gn_w, dn_norm_w,
           w_out, norm_xq_w, norm_mem_w, w_xq, w_xkv, w_xo, norm_moe_w, w_group_router, b_group_router,
           w_expert_router, b_expert_router, w_gate, w_up, w_down, norm_final_w):
    B, S, D = x.shape
    M = mem.shape[1]
    T = B * S
    depth = w_in.shape[0]
    n_main = 8 * HEADS * HD
    h = x.reshape(T, D)
    pos_col = positions.astype(f32).reshape(T, 1)
    mem2 = mem.reshape(B * M, D)
    hg = HEADS
    bm = 512
    for l in range(depth):
        w_in_t = jnp.swapaxes(w_in[l], 0, 1)
        w_main, w_ba = _cast_rows_t(w_in_t, n_main, 512)
        E, _, F = w_gate[l].shape
        expert_w = [w_gate[l].reshape(E * D, F), w_up[l].reshape(E * D, F), w_down[l].reshape(E * F, D)]
        proj, ba, wg, wu, wd = _in_proj(h, norm_mix_w[l], w_main, w_ba, expert_w, _tile(T, 1024), 1024)
        wg, wu, wd = wg.reshape(E, D, F), wu.reshape(E, D, F), wd.reshape(E, F, D)
        mix_r = _retention(proj, pos_col, ret_gn_w[l], B, S, hg)
        mix_d = _gdn(proj, ba, dn_conv_w[l], dn_a_log[l], dn_dt_bias[l], dn_norm_w[l], B, S, hg)
        h = _mm_res(mix_r, mix_d, w_out[l].astype(bf16), h, _tile(T, 512), D)
        kv = _norm_mm(mem2, norm_mem_w[l], w_xkv[l].astype(bf16), _tile(B * M, 512), _tile(2 * D, 1024), "xkv")
        w_r = jnp.pad(jnp.concatenate([w_group_router[l], w_expert_router[l]], axis=1),
                      ((0, 0), (0, LANES - N_GROUPS - N_EXPERTS))).astype(bf16)
        b_r = jnp.pad(jnp.concatenate([b_group_router[l], b_expert_router[l]]),
                      (0, LANES - N_GROUPS - N_EXPERTS)).reshape(1, LANES).astype(f32)
        h, hn, rout, rout_t, cnt = _xattn_router(h, norm_xq_w[l], w_xq[l].astype(bf16), kv,
                                                 w_xo[l].astype(bf16), norm_moe_w[l], w_r, b_r, B, S, M,
                                                 _tile(S, 512))
        blk_e, nused, nvalid, dest, pads = _dispatch_plan(rout_t, cnt, bm)
        xs = _dispatch(hn, dest, pads, blk_e.shape[0] * bm, _tile(T, 1024), D // (2 * LANES), bm)
        y = _ffn(xs, wg, wu, wd, blk_e, nused, nvalid, bm)
        if l + 1 < depth:
            raise NotImplementedError("the fused MoE combine applies the final norm: depth 1 only")
        h = _combine(y, h, rout, dest, norm_final_w, _tile(T, 256))
    return h.reshape(B, S, D)
```

```python
import functools
from typing import NamedTuple

import numpy as np
import jax
import jax.numpy as jnp
from jax import lax
from jax.experimental import pallas as pl
from jax.experimental.pallas import tpu as pltpu

f32 = jnp.float32
bf16 = jnp.bfloat16

EPS = 1e-6
HEADS = 8
HD = 128
LANES = 128
XA_HEADS = 4
CONV_W = 4
ROPE_BASE = 10000.0
N_GROUPS = 4
EXP_PER_GROUP = 8
N_EXPERTS = N_GROUPS * EXP_PER_GROUP
RET_CHUNK = 256
DN_CHUNK = 128
VMEM_LIMIT = 56 * 1024 * 1024


def _params(sem, vmem=VMEM_LIMIT):
    return pltpu.CompilerParams(dimension_semantics=sem, vmem_limit_bytes=vmem)


def _dot(a, b):
    return jnp.dot(a.astype(bf16), b.astype(bf16), preferred_element_type=f32)


def _dot_nt(a, b):
    return lax.dot_general(a.astype(bf16), b.astype(bf16), (((1,), (1,)), ((), ())),
                           preferred_element_type=f32)


def _dot_tn(a, b):
    return lax.dot_general(a.astype(bf16), b.astype(bf16), (((0,), (0,)), ((), ())),
                           preferred_element_type=f32)


def _split3(a):
    hi = a.astype(bf16)
    r = a - hi.astype(f32)
    mid = r.astype(bf16)
    lo = (r - mid.astype(f32)).astype(bf16)
    return hi, mid, lo


def _sigmoid(x):
    return 1.0 / (1.0 + jnp.exp(-x))


def _silu(x):
    hx = 0.5 * x
    return hx + hx * jnp.tanh(hx)


def _rms_scale(x):
    return lax.rsqrt(jnp.mean(x * x, axis=-1, keepdims=True) + EPS)


def _cast_t_kernel(w_ref, tail_ref, o_ref, otail_ref):
    o_ref[...] = w_ref[...].T.astype(bf16)

    @pl.when(pl.program_id(0) == 0)
    def _():
        r = tail_ref.shape[0]
        pick = (lax.broadcasted_iota(jnp.int32, (r, LANES), 0)
                == lax.broadcasted_iota(jnp.int32, (r, LANES), 1))
        otail_ref[...] = _dot_tn(tail_ref[...], jnp.where(pick, 1.0, 0.0)).astype(bf16)


def _cast_rows_t(wt, n, tn):
    N, K = wt.shape
    r = N - n
    assert n % r == 0 and r % 8 == 0 and r <= LANES
    return pl.pallas_call(
        _cast_t_kernel,
        grid=(n // tn,),
        in_specs=[pl.BlockSpec((tn, K), lambda j: (j, 0)),
                  pl.BlockSpec((r, K), lambda j: (n // r, 0))],
        out_specs=[pl.BlockSpec((K, tn), lambda j: (0, j)),
                   pl.BlockSpec((K, LANES), lambda j: (0, 0))],
        out_shape=[jax.ShapeDtypeStruct((K, n), bf16), jax.ShapeDtypeStruct((K, LANES), bf16)],
        compiler_params=_params(("arbitrary",)),
        name="cast_w_in",
    )(wt, wt)


def _side_cast_step(step, nsteps, srcs, dsts, stage_in, stage_out, sem_in, sem_out):
    slot = step % 2
    mats = range(len(srcs))

    def in_copy(m, st, sl):
        r = stage_in[m].shape[1]
        rows = pl.ds(pl.multiple_of(st * r, r), r)
        return pltpu.make_async_copy(srcs[m].at[rows], stage_in[m].at[sl], sem_in.at[m, sl])

    def out_copy(m, st, sl):
        r = stage_out[m].shape[1]
        rows = pl.ds(pl.multiple_of(st * r, r), r)
        return pltpu.make_async_copy(stage_out[m].at[sl], dsts[m].at[rows], sem_out.at[m, sl])

    @pl.when(step == 0)
    def _():
        for m in mats:
            in_copy(m, 0, 0).start()

    @pl.when(step + 1 < nsteps)
    def _():
        for m in mats:
            in_copy(m, step + 1, 1 - slot).start()

    for m in mats:
        in_copy(m, step, slot).wait()

    @pl.when(step >= 2)
    def _():
        for m in mats:
            out_copy(m, step - 2, slot).wait()

    def convert():
        for m in mats:
            stage_out[m][slot] = stage_in[m][slot].astype(bf16)

    def finish():
        for m in mats:
            out_copy(m, step, slot).start()

        @pl.when(step == nsteps - 1)
        def _():
            for m in mats:
                if nsteps >= 2:
                    out_copy(m, step - 1, 1 - slot).wait()
                out_copy(m, step, slot).wait()

    return convert, finish


def _side_scratch(side, nsteps):
    rows = [a.shape[0] // nsteps for a in side]
    assert all(a.shape[0] == r * nsteps and r % 16 == 0 for a, r in zip(side, rows))
    return ([pltpu.VMEM((2, r, a.shape[1]), f32) for a, r in zip(side, rows)]
            + [pltpu.VMEM((2, r, a.shape[1]), bf16) for a, r in zip(side, rows)]
            + [pltpu.SemaphoreType.DMA((len(side), 2)), pltpu.SemaphoreType.DMA((len(side), 2))])


def _in_proj_kernel(x_ref, nw_ref, w_ref, wba_ref, o_ref, ba_ref, xn_ref):
    @pl.when(pl.program_id(1) == 0)
    def _():
        x = x_ref[...]
        xn = (x * _rms_scale(x) * nw_ref[...]).astype(bf16)
        xn_ref[...] = xn
        ba_ref[...] = jnp.dot(xn, wba_ref[...], preferred_element_type=f32)

    acc = jnp.dot(xn_ref[...], w_ref[...], preferred_element_type=f32)
    for c in range(o_ref.shape[0]):
        o_ref[c] = acc[:, c * LANES:(c + 1) * LANES].astype(bf16)


def _in_proj(x2, norm_w, w_main, w_ba, tm, tn):
    T, D = x2.shape
    N = w_main.shape[1]
    return pl.pallas_call(
        _in_proj_kernel,
        grid=(T // tm, N // tn),
        in_specs=[
            pl.BlockSpec((tm, D), lambda i, j: (i, 0)),
            pl.BlockSpec((1, D), lambda i, j: (0, 0)),
            pl.BlockSpec((D, tn), lambda i, j: (0, j)),
            pl.BlockSpec((D, LANES), lambda i, j: (0, 0)),
        ],
        out_specs=[
            pl.BlockSpec((tn // LANES, tm, LANES), lambda i, j: (j, i, 0)),
            pl.BlockSpec((tm, LANES), lambda i, j: (i, 0)),
        ],
        out_shape=[
            jax.ShapeDtypeStruct((N // LANES, T, LANES), bf16),
            jax.ShapeDtypeStruct((T, LANES), f32),
        ],
        scratch_shapes=[pltpu.VMEM((tm, D), bf16)],
        compiler_params=_params(("parallel", "arbitrary")),
        name="in_proj",
    )(x2, norm_w.reshape(1, D), w_main, w_ba)


def _retention_kernel(q_ref, k_ref, v_ref, g_ref, pos_ref, inv_ref, sgn_ref, dmask_ref, qdec_ref,
                      kdec_ref, cdec_ref, gnw_ref, o_ref, state_ref):
    @pl.when(pl.program_id(2) == 0)
    def _():
        state_ref[...] = jnp.zeros_like(state_ref)

    C = pos_ref.shape[0]
    lane = lax.broadcasted_iota(jnp.int32, (C // 2, HD), 1)
    low = lane < HD // 2
    ang = jnp.where(low, pos_ref[0:C // 2, :], pos_ref[C // 2:, :]) * inv_ref[...]
    cos_p = jnp.cos(ang)
    sin_p = jnp.sin(ang)

    def spread(t):
        swapped = pltpu.roll(t, HD // 2, 1)
        return jnp.concatenate([jnp.where(low, t, swapped), jnp.where(low, swapped, t)], axis=0)

    cos2 = spread(cos_p)
    sin2 = spread(sin_p) * sgn_ref[...]
    heads = range(q_ref.shape[0])

    def rope(x):
        return x * cos2 + pltpu.roll(x, HD // 2, 1) * sin2

    qr = [rope(q_ref[i].astype(f32)) for i in heads]
    kr = [rope(k_ref[i].astype(f32)) * (HD ** -0.5) for i in heads]
    state = [state_ref[i] for i in heads]
    s = [_dot_nt(qr[i], kr[i]) * dmask_ref[i] for i in heads]
    cross = [_dot(qr[i], state[i]) * qdec_ref[i] for i in heads]
    upd = [_dot_tn(kr[i] * kdec_ref[i], v_ref[i]) for i in heads]
    o = [_dot(s[i], v_ref[i]) + cross[i] for i in heads]
    for i in heads:
        state_ref[i] = state[i] * cdec_ref[i] + upd[i]
        d = o[i] - jnp.mean(o[i], axis=-1, keepdims=True)
        y = d * lax.rsqrt(jnp.mean(d * d, axis=-1, keepdims=True) + EPS) * gnw_ref[i]
        y = y * _silu(g_ref[i].astype(f32))
        o_ref[:, i * HD:(i + 1) * HD] = y.astype(bf16)


def _retention_tables(C):
    h = np.arange(HEADS, dtype=np.float64)
    log_gamma = np.log1p(-np.exp2(-5.0 - h))
    idx = np.arange(C, dtype=np.float64)
    rel = idx[:, None] - idx[None, :]
    dmask = np.where(rel >= 0, np.exp(log_gamma[:, None, None] * np.where(rel >= 0, rel, 0.0)), 0.0)
    qdec = np.exp(log_gamma[:, None] * (idx + 1.0))
    kdec = np.exp(log_gamma[:, None] * (C - 1.0 - idx))
    cdec = np.exp(log_gamma * C)
    rep = lambda a: np.broadcast_to(a[..., None], a.shape + (LANES,))
    return (jnp.asarray(dmask, f32), jnp.asarray(rep(qdec), f32), jnp.asarray(rep(kdec), f32),
            jnp.asarray(np.broadcast_to(cdec[:, None, None], (HEADS, 1, LANES)), f32))


def _retention(proj, pos_col, gn_w, B, S, hg):
    C = RET_CHUNK
    NC = S // C
    T = B * S
    G = HEADS // hg
    half = HD // 2
    inv = ROPE_BASE ** (-np.arange(half, dtype=np.float32) / half)
    inv2 = jnp.asarray(np.concatenate([inv, inv]).reshape(1, HD), f32)
    sgn = jnp.asarray(np.concatenate([-np.ones(half), np.ones(half)]).reshape(1, HD), f32)
    dmask, qdec, kdec, cdec = _retention_tables(C)

    def slab(off):
        return pl.BlockSpec((hg, C, HD), lambda b, g, n: (off // hg + g, b * NC + n, 0))

    def table(shape):
        return pl.BlockSpec((hg,) + shape, lambda b, g, n: (g,) + (0,) * len(shape))

    return pl.pallas_call(
        _retention_kernel,
        grid=(B, G, NC),
        in_specs=[
            slab(0), slab(HEADS), slab(2 * HEADS), slab(3 * HEADS),
            pl.BlockSpec((C, 1), lambda b, g, n: (b * NC + n, 0)),
            pl.BlockSpec((1, HD), lambda b, g, n: (0, 0)),
            pl.BlockSpec((1, HD), lambda b, g, n: (0, 0)),
            table((C, C)), table((C, LANES)), table((C, LANES)), table((1, LANES)), table((1, HD)),
        ],
        out_specs=pl.BlockSpec((C, hg * HD), lambda b, g, n: (b * NC + n, g)),
        out_shape=jax.ShapeDtypeStruct((T, HEADS * HD), bf16),
        scratch_shapes=[pltpu.VMEM((hg, HD, HD), f32)],
        compiler_params=_params(("parallel", "parallel", "arbitrary")),
        name="retention",
    )(proj, proj, proj, proj, pos_col, inv2, sgn, dmask, qdec, kdec, cdec,
      gn_w.reshape(HEADS, 1, HD))


def _gdn_kernel(q_ref, k_ref, v_ref, z_ref, ba_ref, cw_ref, arow_ref, dtrow_ref, nw_ref, *rest, hg, n_side):
    srcs, o_ref, dsts = rest[:n_side], rest[n_side], rest[n_side + 1:2 * n_side + 1]
    state_ref, tail_ref, win_ref = rest[2 * n_side + 1:2 * n_side + 4]
    rest = rest[2 * n_side + 4:]
    stage_in, stage_out, (sem_in, sem_out) = rest[:n_side], rest[n_side:2 * n_side], rest[2 * n_side:]
    step = (pl.program_id(0) * pl.num_programs(1) + pl.program_id(1)) * pl.num_programs(2) + pl.program_id(2)
    nsteps = pl.num_programs(0) * pl.num_programs(1) * pl.num_programs(2)
    convert, finish = _side_cast_step(step, nsteps, srcs, dsts, stage_in, stage_out, sem_in, sem_out)

    C = q_ref.shape[1]
    first = pl.program_id(2) == 0

    @pl.when(first)
    def _():
        state_ref[...] = jnp.zeros_like(state_ref)
        tail_ref[...] = jnp.zeros_like(tail_ref)

    convert()
    row = lax.broadcasted_iota(jnp.int32, (C, C), 0)
    col = lax.broadcasted_iota(jnp.int32, (C, C), 1)
    causal = row >= col
    strict = row > col
    same8 = (row // 8) == (col // 8)
    level_masks = []
    s = 8
    while s < C:
        level_masks.append(((row // (2 * s)) == (col // (2 * s))) & ((row // s) != (col // s)))
        s *= 2
    eye = jnp.where(row == col, 1.0, 0.0).astype(f32)
    tri = jnp.where(causal, 1.0, 0.0).astype(bf16)

    ba = ba_ref[...]
    beta_all = _sigmoid(ba)
    xa = ba + dtrow_ref[...]
    softplus = jnp.maximum(xa, 0.0) + jnp.log(1.0 + jnp.exp(-jnp.abs(xa)))
    glog = -jnp.exp(arow_ref[...]) * softplus
    g_hi, g_mid, g_lo = _split3(glog)
    gcum = (jnp.dot(tri, g_hi, preferred_element_type=f32)
            + jnp.dot(tri, g_mid, preferred_element_type=f32)
            + jnp.dot(tri, g_lo, preferred_element_type=f32))
    gcum_t = gcum.T
    lane = lax.broadcasted_iota(jnp.int32, (C, LANES), 1)
    sub = lax.broadcasted_iota(jnp.int32, (LANES, C), 0)

    def conv_silu(x_ref, kind, i):
        slot = kind * hg + i
        win_ref[slot, 0:8, :] = tail_ref[slot]
        win_ref[slot, 8:8 + C, :] = x_ref[i].astype(f32)
        w = cw_ref[kind, i]
        y = win_ref[slot, 8:8 + C, :] * w[CONV_W - 1:CONV_W, :]
        for t in range(1, CONV_W):
            y = y + win_ref[slot, 8 - t:8 - t + C, :] * w[CONV_W - 1 - t:CONV_W - t, :]
        tail_ref[slot] = win_ref[slot, C:C + 8, :]
        return _silu(y)

    heads = range(hg)
    hd = [pl.program_id(1) * hg + i for i in heads]
    q = [conv_silu(q_ref, 0, i) for i in heads]
    k = [conv_silu(k_ref, 1, i) for i in heads]
    v = [conv_silu(v_ref, 2, i) for i in heads]
    q = [x * lax.rsqrt(jnp.sum(x * x, axis=-1, keepdims=True) + EPS) * (HD ** -0.5) for x in q]
    k = [x * lax.rsqrt(jnp.sum(x * x, axis=-1, keepdims=True) + EPS) for x in k]

    bcol = [jnp.sum(jnp.where(lane == h, beta_all, 0.0), axis=1, keepdims=True) for h in hd]
    gcol = [jnp.sum(jnp.where(lane == h + HEADS, gcum, 0.0), axis=1, keepdims=True) for h in hd]
    grow = [jnp.sum(jnp.where(sub == h + HEADS, gcum_t, 0.0), axis=0, keepdims=True) for h in hd]
    glast = [g[C - 1:C, :] for g in gcol]
    gam = [jnp.where(causal, jnp.exp(jnp.where(causal, gc - gr, 0.0)), 0.0) for gc, gr in zip(gcol, grow)]
    egc = [jnp.exp(g) for g in gcol]

    kk = [_dot_nt(x, x) for x in k]
    qk = [_dot_nt(x, y) for x, y in zip(q, k)]
    a = [jnp.where(strict, m * g * b, 0.0) for m, g, b in zip(kk, gam, bcol)]
    qk = [m * g for m, g in zip(qk, gam)]

    d = [jnp.where(same8, m, 0.0) for m in a]
    d2 = [_dot(m, m) for m in d]
    d3 = [_dot(m, m2) for m, m2 in zip(d, d2)]
    d4 = [_dot(m2, m2) for m2 in d2]
    x = [eye - m + m2 - m3 for m, m2, m3 in zip(d, d2, d3)]
    x = [xi + _dot(xi, m4) for xi, m4 in zip(x, d4)]
    for mask in level_masks:
        t = [_dot(xi, jnp.where(mask, m, 0.0)) for xi, m in zip(x, a)]
        x = [xi - _dot(ti, xi) for xi, ti in zip(x, t)]

    rhs = [jnp.concatenate([vi * b, ki * (b * e)], axis=1) for vi, ki, b, e in zip(v, k, bcol, egc)]
    sol = [_dot(xi, r) for xi, r in zip(x, rhs)]
    q_dec = [qi * e for qi, e in zip(q, egc)]
    k_dec = [ki * jnp.exp(gl - gc) for ki, gl, gc in zip(k, glast, gcol)]

    state = [state_ref[i] for i in heads]
    ws = [_dot(jnp.concatenate([s_[:, HD:], qd], axis=0), st) for s_, qd, st in zip(sol, q_dec, state)]
    v_new = [s_[:, :HD] - w_[:C] for s_, w_ in zip(sol, ws)]
    o = [w_[C:] + _dot(m, vn) for w_, m, vn in zip(ws, qk, v_new)]
    upd = [_dot_tn(kd, vn) for kd, vn in zip(k_dec, v_new)]
    for i in heads:
        state_ref[i] = state[i] * jnp.exp(glast[i]) + upd[i]
        y = o[i] * _rms_scale(o[i]) * nw_ref[...]
        y = y * _silu(z_ref[i].astype(f32))
        o_ref[:, i * HD:(i + 1) * HD] = y.astype(bf16)
    finish()


def _gdn(proj, ba, conv_w, a_log, dt_bias, norm_w, side, B, S, hg):
    C = DN_CHUNK
    NC = S // C
    T = B * S
    G = HEADS // hg
    any_spec = pl.BlockSpec(memory_space=pl.ANY)
    cw = conv_w.reshape(CONV_W, 3, HEADS, HD).transpose(1, 2, 0, 3)
    pad = jnp.zeros((LANES - 2 * HEADS,), f32)
    arow = jnp.concatenate([jnp.zeros((HEADS,), f32), a_log.astype(f32), pad]).reshape(1, LANES)
    dtrow = jnp.concatenate([jnp.zeros((HEADS,), f32), dt_bias.astype(f32), pad]).reshape(1, LANES)

    def slab(off):
        return pl.BlockSpec((hg, C, HD), lambda b, g, n: (off // hg + g, b * NC + n, 0))

    row_spec = pl.BlockSpec((1, LANES), lambda b, g, n: (0, 0))
    return pl.pallas_call(
        functools.partial(_gdn_kernel, hg=hg, n_side=len(side)),
        grid=(B, G, NC),
        in_specs=[
            slab(4 * HEADS), slab(5 * HEADS), slab(6 * HEADS), slab(7 * HEADS),
            pl.BlockSpec((C, LANES), lambda b, g, n: (b * NC + n, 0)),
            pl.BlockSpec((3, hg, CONV_W, HD), lambda b, g, n: (0, g, 0, 0)),
            row_spec, row_spec, row_spec,
        ] + [any_spec] * len(side),
        out_specs=[pl.BlockSpec((C, hg * HD), lambda b, g, n: (b * NC + n, g))] + [any_spec] * len(side),
        out_shape=[jax.ShapeDtypeStruct((T, HEADS * HD), bf16)]
        + [jax.ShapeDtypeStruct(a.shape, bf16) for a in side],
        scratch_shapes=[
            pltpu.VMEM((hg, HD, HD), f32),
            pltpu.VMEM((3 * hg, 8, HD), f32),
            pltpu.VMEM((3 * hg, C + 8, HD), f32),
        ] + _side_scratch(side, B * G * NC),
        compiler_params=_params(("arbitrary", "arbitrary", "arbitrary")),
        name="gdn",
    )(proj, proj, proj, proj, ba, cw, arow, dtrow, norm_w.reshape(1, HD), *side)


def _mm_res_kernel(a1_ref, a2_ref, w_ref, r_ref, o_ref):
    a = jnp.concatenate([a1_ref[...], a2_ref[...]], axis=1)
    o_ref[...] = r_ref[...] + jnp.dot(a, w_ref[...], preferred_element_type=f32)


def _mm_res(a1, a2, w, res, tm, tn):
    T, K1 = a1.shape
    K2 = a2.shape[1]
    N = w.shape[1]
    return pl.pallas_call(
        _mm_res_kernel,
        grid=(T // tm, N // tn),
        in_specs=[
            pl.BlockSpec((tm, K1), lambda i, j: (i, 0)),
            pl.BlockSpec((tm, K2), lambda i, j: (i, 0)),
            pl.BlockSpec((K1 + K2, tn), lambda i, j: (0, j)),
            pl.BlockSpec((tm, tn), lambda i, j: (i, j)),
        ],
        out_specs=pl.BlockSpec((tm, tn), lambda i, j: (i, j)),
        out_shape=jax.ShapeDtypeStruct((T, N), f32),
        compiler_params=_params(("parallel", "parallel")),
        name="out_proj",
    )(a1, a2, w, res)


def _norm_mm_kernel(x_ref, nw_ref, w_ref, o_ref, xn_ref):
    @pl.when(pl.program_id(1) == 0)
    def _():
        x = x_ref[...]
        xn_ref[...] = (x * _rms_scale(x) * nw_ref[...]).astype(bf16)

    o_ref[...] = jnp.dot(xn_ref[...], w_ref[...], preferred_element_type=f32).astype(o_ref.dtype)


def _norm_mm(x2, norm_w, w, tm, tn, name):
    T, D = x2.shape
    N = w.shape[1]
    return pl.pallas_call(
        _norm_mm_kernel,
        grid=(T // tm, N // tn),
        in_specs=[
            pl.BlockSpec((tm, D), lambda i, j: (i, 0)),
            pl.BlockSpec((1, D), lambda i, j: (0, 0)),
            pl.BlockSpec((D, tn), lambda i, j: (0, j)),
        ],
        out_specs=pl.BlockSpec((tm, tn), lambda i, j: (i, j)),
        out_shape=jax.ShapeDtypeStruct((T, N), bf16),
        scratch_shapes=[pltpu.VMEM((tm, D), bf16)],
        compiler_params=_params(("parallel", "arbitrary")),
        name=name,
    )(x2, norm_w.reshape(1, D), w)


def _xattn_router_kernel(h_ref, nq_ref, wq_ref, k_ref, v_ref, wo_ref, nm_ref, wr_ref, b_ref,
                         h2_ref, hn_ref, rt_ref, rtt_ref, cnt_ref, run_ref):
    @pl.when((pl.program_id(0) == 0) & (pl.program_id(1) == 0))
    def _():
        run_ref[...] = jnp.zeros_like(run_ref)

    h1 = h_ref[...]
    D = h1.shape[1]
    dh = D // XA_HEADS
    xn = (h1 * _rms_scale(h1) * nq_ref[...]).astype(bf16)
    q = jnp.dot(xn, wq_ref[...], preferred_element_type=f32).astype(bf16)
    heads = []
    for h in range(XA_HEADS):
        sl = slice(h * dh, (h + 1) * dh)
        s = _dot_nt(q[:, sl], k_ref[:, sl]) * (dh ** -0.5)
        p = jnp.exp(s - jnp.max(s, axis=-1, keepdims=True))
        p = p / jnp.sum(p, axis=-1, keepdims=True)
        heads.append(_dot(p, v_ref[:, sl]).astype(bf16))
    h2 = h1 + jnp.dot(jnp.concatenate(heads, axis=1), wo_ref[...], preferred_element_type=f32)
    h2_ref[...] = h2
    _route(h2, nm_ref, wr_ref, b_ref, hn_ref, rt_ref, rtt_ref, cnt_ref, run_ref)


def _xattn_router(h1, norm_q, w_q, kv, w_o, norm_moe, w_r, bias_row, B, S, M, tm):
    T, D = h1.shape
    nt = S // tm
    g = D // (2 * LANES)
    tok = lambda b, i: (b * nt + i, 0)
    const = lambda b, i: (0, 0)
    resident = dict(pipeline_mode=pl.Buffered(1))
    return pl.pallas_call(
        _xattn_router_kernel,
        grid=(B, nt),
        in_specs=[
            pl.BlockSpec((tm, D), tok),
            pl.BlockSpec((1, D), const),
            pl.BlockSpec((D, D), const, **resident),
            pl.BlockSpec((M, D), lambda b, i: (b, 0)),
            pl.BlockSpec((M, D), lambda b, i: (b, 1)),
            pl.BlockSpec((D, D), const, **resident),
            pl.BlockSpec((1, D), const),
            pl.BlockSpec((D, LANES), const),
            pl.BlockSpec((1, LANES), const),
        ],
        out_specs=[
            pl.BlockSpec((tm, D), tok),
            pl.BlockSpec((tm * g, LANES), tok),
            pl.BlockSpec((tm, LANES), tok),
            pl.BlockSpec((8, tm), lambda b, i: (0, b * nt + i)),
            pl.BlockSpec((1, LANES), const),
        ],
        out_shape=[
            jax.ShapeDtypeStruct((T, D), f32),
            jax.ShapeDtypeStruct((T * g, LANES), jnp.uint32),
            jax.ShapeDtypeStruct((T, LANES), f32),
            jax.ShapeDtypeStruct((8, T), f32),
            jax.ShapeDtypeStruct((1, LANES), f32),
        ],
        scratch_shapes=[pltpu.VMEM((1, LANES), f32)],
        compiler_params=_params(("arbitrary", "arbitrary")),
        name="xattn_router",
    )(h1, norm_q.reshape(1, D), w_q, kv, kv, w_o, norm_moe.reshape(1, D), w_r, bias_row)


def _pack_pair(a, b):
    au = lax.bitcast_convert_type(a.astype(f32), jnp.uint32)
    bu = lax.bitcast_convert_type(b.astype(f32), jnp.uint32)
    return (au >> 16) | bu


def _unpack_pair(u):
    a = lax.bitcast_convert_type(u << 16, f32)
    b = lax.bitcast_convert_type(u & jnp.uint32(0xFFFF0000), f32)
    return a, b


def _store_slabs(ref, words):
    n = words.shape[0]
    g = words.shape[1] // LANES
    for c in range(g):
        ref[pl.ds(c, n, stride=g), :] = words[:, c * LANES:(c + 1) * LANES]


def _load_slabs(ref, n, g):
    return [ref[pl.ds(c, n, stride=g), :] for c in range(g)]


def _route(x, nw_ref, wr_ref, b_ref, hn_ref, o_ref, ot_ref, cnt_ref, run_ref):
    xn = x * _rms_scale(x) * nw_ref[...]
    xhi = xn.astype(bf16)
    logits = jnp.dot(xhi, wr_ref[...], preferred_element_type=f32) + b_ref[...]
    lane = lax.broadcasted_iota(jnp.int32, logits.shape, 1).astype(f32)
    neg = -jnp.inf
    big = float(LANES)

    def first_max(vals):
        m = jnp.max(vals, axis=-1, keepdims=True)
        return m, jnp.min(jnp.where(vals == m, lane, big), axis=-1, keepdims=True)

    is_group = lane < N_GROUPS
    gmax, gsel = first_max(jnp.where(is_group, logits, neg))
    gsum = jnp.sum(jnp.where(is_group, jnp.exp(logits - gmax), 0.0), axis=-1, keepdims=True)
    g_w = 1.0 / gsum
    lo = N_GROUPS + EXP_PER_GROUP * gsel
    in_group = (lane >= lo) & (lane < lo + EXP_PER_GROUP)
    el = jnp.where(in_group, logits, neg)
    l1, i1 = first_max(el)
    l2, i2 = first_max(jnp.where(lane == i1, neg, el))
    esum = jnp.sum(jnp.where(in_group, jnp.exp(logits - l1), 0.0), axis=-1, keepdims=True)
    p1 = 1.0 / esum
    p2 = jnp.exp(l2 - l1) / esum
    gate1 = g_w * p1 / (p1 + p2)
    gate2 = g_w * p2 / (p1 + p2)

    tm = x.shape[0]
    chosen = jnp.where((lane == i1) | (lane == i2), 1.0, 0.0)
    earlier = (lax.broadcasted_iota(jnp.int32, (tm, tm), 0) > lax.broadcasted_iota(jnp.int32, (tm, tm), 1))
    before = _dot(jnp.where(earlier, 1.0, 0.0), chosen) + run_ref[...]
    rank1 = jnp.sum(jnp.where(lane == i1, before, 0.0), axis=-1, keepdims=True)
    rank2 = jnp.sum(jnp.where(lane == i2, before, 0.0), axis=-1, keepdims=True)
    run_ref[...] = run_ref[...] + jnp.sum(chosen, axis=0, keepdims=True)
    cnt_ref[...] = run_ref[...]

    cols = (i1 - N_GROUPS, i2 - N_GROUPS, gate1, gate2, rank1, rank2)
    out = jnp.zeros_like(logits)
    for c, val in enumerate(cols):
        out = jnp.where(lane == c, val, out)
    o_ref[...] = out
    ot_ref[...] = out.T[0:ot_ref.shape[0], :]
    half = x.shape[1] // 2
    _store_slabs(hn_ref, _pack_pair(xhi[:, :half], xhi[:, half:]))


def _dispatch_kernel(dest_ref, pad_lo_ref, pad_n_ref, zcnt_ref, tail_ref, hn_ref, xs_hbm, zeros_ref, sem,
                     zsem, *, tb, g, bm):
    i = pl.program_id(0)
    base = i * tb
    sizes = [1 << b for b in range(bm.bit_length() - 1)]

    def zero_copy(first, p):
        return pltpu.make_async_copy(zeros_ref.at[pl.ds(0, p * g)], xs_hbm.at[pl.ds(first, p * g)], zsem)

    @pl.when(i == 0)
    def _():
        zeros_ref[...] = jnp.zeros_like(zeros_ref)

        def per_expert(e, carry):
            row = pad_lo_ref[e]
            n = pad_n_ref[e]
            for p in sizes:
                hit = (n & p) != 0

                @pl.when(hit)
                def _(row=row, p=p):
                    zero_copy(pl.multiple_of(row * g, g), p).start()

                row = row + jnp.where(hit, p, 0)
            return carry

        lax.fori_loop(0, N_EXPERTS, per_expert, 0)

        def per_tail_chunk(c, carry):
            zero_copy(pl.multiple_of((tail_ref[0] + c * sizes[-1]) * g, g), sizes[-1]).start()
            return carry

        lax.fori_loop(0, tail_ref[1], per_tail_chunk, 0)

    def body(r, carry):
        src = hn_ref.at[pl.ds(pl.multiple_of(r * g, g), g)]
        for kk in range(2):
            d = dest_ref[kk * (tb * pl.num_programs(0)) + base + r]
            pltpu.make_async_copy(src, xs_hbm.at[pl.ds(pl.multiple_of(d * g, g), g)],
                                  sem.at[kk]).start(priority=kk)
        return carry

    lax.fori_loop(0, tb, body, 0, unroll=8)
    for kk in range(2):
        pltpu.make_async_copy(hn_ref, hn_ref, sem.at[kk]).wait()

    @pl.when(i == pl.num_programs(0) - 1)
    def _():
        for b, p in enumerate(sizes):
            def wait_one(_, carry, p=p):
                zero_copy(0, p).wait()
                return carry
            lax.fori_loop(0, zcnt_ref[b], wait_one, 0)


def _dispatch(hn, dest, pads, P, tb, g, bm):
    T = hn.shape[0] // g
    grid_spec = pltpu.PrefetchScalarGridSpec(
        num_scalar_prefetch=5,
        grid=(T // tb,),
        in_specs=[pl.BlockSpec((tb * g, LANES), lambda i, *_: (i, 0))],
        out_specs=pl.BlockSpec(memory_space=pl.ANY),
        scratch_shapes=[pltpu.VMEM((bm // 2 * g, LANES), jnp.uint32),
                        pltpu.SemaphoreType.DMA((2,)), pltpu.SemaphoreType.DMA(())],
    )
    return pl.pallas_call(
        functools.partial(_dispatch_kernel, tb=tb, g=g, bm=bm),
        grid_spec=grid_spec,
        out_shape=jax.ShapeDtypeStruct((P * g, LANES), jnp.uint32),
        compiler_params=_params(("arbitrary",)),
        name="dispatch",
    )(dest, *pads, hn)


def _ffn_kernel(blk_e_ref, nused_ref, nvalid_ref, x_ref, wg_ref, wu_ref, wd_ref, y_ref, *, bm):
    nv = nvalid_ref[pl.program_id(0)]
    half = wg_ref.shape[0] // 2
    g = half // LANES

    def run(rows):
        pairs = [_unpack_pair(w) for w in _load_slabs(x_ref.at[pl.ds(0, rows * g)], rows, g)]
        x = jnp.concatenate([p[0].astype(bf16) for p in pairs] + [p[1].astype(bf16) for p in pairs], axis=1)
        gate = jnp.dot(x, wg_ref[...], preferred_element_type=f32)
        up = jnp.dot(x, wu_ref[...], preferred_element_type=f32)
        hmid = (_silu(gate) * up).astype(bf16)
        for s in range(g):
            ys = jnp.dot(hmid, wd_ref[:, 2 * s * LANES:2 * (s + 1) * LANES], preferred_element_type=f32)
            y_ref[pl.ds(s, rows, stride=g), :] = _pack_pair(ys[:, :LANES].astype(bf16),
                                                            ys[:, LANES:].astype(bf16))

    @pl.when(nv > bm // 2)
    def _():
        run(bm)

    @pl.when((nv > 0) & (nv <= bm // 2))
    def _():
        run(bm // 2)
        y_ref[pl.ds(bm // 2 * g, bm // 2 * g), :] = jnp.zeros((bm // 2 * g, LANES), jnp.uint32)

    @pl.when(nv == 0)
    def _():
        y_ref[...] = jnp.zeros_like(y_ref)


def _ffn(xs, wg, wu, wd, blk_e, nused, nvalid, bm):
    D, F = wg.shape[1:]
    g = D // (2 * LANES)
    NB = blk_e.shape[0]
    grid_spec = pltpu.PrefetchScalarGridSpec(
        num_scalar_prefetch=3,
        grid=(NB,),
        in_specs=[
            pl.BlockSpec((bm * g, LANES), lambda i, be, n, nv: (jnp.minimum(i, n[0] - 1), 0)),
            pl.BlockSpec((None, D, F), lambda i, be, n, nv: (be[i], 0, 0)),
            pl.BlockSpec((None, D, F), lambda i, be, n, nv: (be[i], 0, 0)),
            pl.BlockSpec((None, F, D), lambda i, be, n, nv: (be[i], 0, 0)),
        ],
        out_specs=pl.BlockSpec((bm * g, LANES), lambda i, be, n, nv: (i, 0)),
    )
    return pl.pallas_call(
        functools.partial(_ffn_kernel, bm=bm),
        grid_spec=grid_spec,
        out_shape=jax.ShapeDtypeStruct(xs.shape, jnp.uint32),
        compiler_params=_params(("arbitrary",)),
        name="ffn",
    )(blk_e, nused, nvalid, xs, wg, wu, wd)


def _combine_kernel(dest_ref, y_hbm, h_ref, rt_ref, nw_ref, o_ref, ybuf, sem):
    i = pl.program_id(0)
    n = pl.num_programs(0)
    tm = h_ref.shape[0]
    nslots = ybuf.shape[0]

    D = h_ref.shape[1]
    half = D // 2
    g = half // LANES

    def row_copy(blk, r, kk, slot_):
        d = dest_ref[kk * (tm * n) + blk * tm + r]
        first = r * g if isinstance(r, int) else pl.multiple_of(r * g, g)
        return pltpu.make_async_copy(y_hbm.at[pl.ds(pl.multiple_of(d * g, g), g)],
                                     ybuf.at[slot_, kk, pl.ds(first, g)], sem.at[slot_])

    def wait_slot(slot_):
        pltpu.make_async_copy(ybuf.at[slot_], ybuf.at[slot_], sem.at[slot_]).wait()

    @pl.when(i == 0)
    def _():
        for first in range(nslots - 1):
            def body(r, carry, first=first):
                for kk in range(2):
                    row_copy(jnp.minimum(first, n - 1), r, kk, first).start(priority=kk)
                return carry
            lax.fori_loop(0, tm, body, 0, unroll=8)

    nxt = jnp.minimum(i + nslots - 1, n - 1)
    pieces = 4 * g
    per = tm // pieces

    def step(slot_):
        def issue_batch(p):
            for r in range(p * per, (p + 1) * per):
                for kk in range(2):
                    row_copy(nxt, r, kk, (slot_ + nslots - 1) % nslots).start(priority=kk)

        wait_slot(slot_)
        rt = rt_ref[...]
        g1 = rt[:, 2:3]
        g2 = rt[:, 3:4]
        cols = [slice(c * LANES, (c + 1) * LANES) for c in range(2 * g)]
        hs = []
        for s in range(g):
            a1 = _unpack_pair(ybuf[slot_, 0, pl.ds(s, tm, stride=g), :])
            a2 = _unpack_pair(ybuf[slot_, 1, pl.ds(s, tm, stride=g), :])
            for hw in range(2):
                hs.append(h_ref[:, cols[2 * s + hw]] + g1 * a1[hw] + g2 * a2[hw])
                issue_batch(2 * s + hw)
        ms = sum(jnp.sum(x * x, axis=-1, keepdims=True) for x in hs) / D
        scale = lax.rsqrt(ms + EPS)
        for c in range(2 * g):
            o_ref[:, cols[c]] = hs[c] * scale * nw_ref[:, cols[c]]
            issue_batch(2 * g + c)

    for s in range(nslots):
        @pl.when(i % nslots == s)
        def _(s=s):
            step(s)

    @pl.when(i == n - 1)
    def _():
        for ahead in range(1, nslots):
            wait_slot((i + ahead) % nslots)


def _combine(y, h2, rout, dest, norm_w, tm):
    T, D = h2.shape
    g = D // (2 * LANES)
    grid_spec = pltpu.PrefetchScalarGridSpec(
        num_scalar_prefetch=1,
        grid=(T // tm,),
        in_specs=[
            pl.BlockSpec(memory_space=pl.ANY),
            pl.BlockSpec((tm, D), lambda i, d: (i, 0)),
            pl.BlockSpec((tm, LANES), lambda i, d: (i, 0)),
            pl.BlockSpec((1, D), lambda i, d: (0, 0)),
        ],
        out_specs=pl.BlockSpec((tm, D), lambda i, d: (i, 0)),
        scratch_shapes=[pltpu.VMEM((3, 2, tm * g, LANES), jnp.uint32), pltpu.SemaphoreType.DMA((3,))],
    )
    return pl.pallas_call(
        _combine_kernel,
        grid_spec=grid_spec,
        out_shape=jax.ShapeDtypeStruct((T, D), f32),
        compiler_params=_params(("arbitrary",)),
        name="combine",
    )(dest, y, h2, rout, norm_w.reshape(1, D))


def _dispatch_plan(rout_t, cnt, bm):
    T = rout_t.shape[1]
    NB = 2 * T // bm + N_EXPERTS
    e_kt = rout_t[0:2].astype(jnp.int32)
    rank_kt = rout_t[4:6].astype(jnp.int32)
    counts = cnt[0, N_GROUPS:N_GROUPS + N_EXPERTS].astype(jnp.int32)
    padded = (counts + bm - 1) // bm * bm
    pend = jnp.cumsum(padded)
    pstart = pend - padded
    first_row = jnp.arange(NB, dtype=jnp.int32) * bm
    blk_e = jnp.minimum(jnp.sum((pend[None, :] <= first_row[:, None]).astype(jnp.int32), axis=1),
                        N_EXPERTS - 1)
    nused = (pend[-1] // bm).astype(jnp.int32).reshape(1)
    of_blk = blk_e[:, None] == jnp.arange(N_EXPERTS, dtype=jnp.int32)
    row_end = jnp.sum(jnp.where(of_blk, pstart + counts, 0), axis=1)
    nvalid = jnp.where(first_row < pend[-1], jnp.clip(row_end - first_row, 0, bm), 0).astype(jnp.int32)
    onehot = e_kt[None] == jnp.arange(N_EXPERTS, dtype=jnp.int32)[:, None, None]
    dest = jnp.sum(jnp.where(onehot, pstart[:, None, None], 0), axis=0) + rank_kt
    pad_n = padded - counts
    nbits = bm.bit_length() - 1
    bits = (pad_n[:, None] >> jnp.arange(nbits, dtype=jnp.int32)) & 1
    tail_chunks = 2 * (NB - nused[0])
    zcnt = jnp.sum(bits, axis=0) + jnp.where(jnp.arange(nbits) == nbits - 1, tail_chunks, 0)
    pads = ((pstart + counts).astype(jnp.int32), pad_n.astype(jnp.int32), zcnt.astype(jnp.int32),
            jnp.stack([pend[-1], tail_chunks]).astype(jnp.int32))
    return blk_e, nused, nvalid, dest.reshape(2 * T).astype(jnp.int32), pads


def _tile(n, want):
    t = min(n, want)
    while n % t:
        t //= 2
    return t


class _Tiles(NamedTuple):
    cast_rows: int
    proj_rows: int
    proj_cols: int
    out_rows: int
    mem_rows: int
    mem_cols: int
    attn_rows: int
    moe_rows: int
    dispatch_rows: int
    combine_rows: int


def _tile_plan(T, S, n_mem, D):
    return _Tiles(cast_rows=512, proj_rows=_tile(T, 1024), proj_cols=1024, out_rows=_tile(T, 512),
                  mem_rows=_tile(n_mem, 512), mem_cols=_tile(2 * D, 1024), attn_rows=_tile(S, 512),
                  moe_rows=512, dispatch_rows=_tile(T, 4096), combine_rows=_tile(T, 256))


def kernel(x, mem, positions, norm_mix_w, w_in, dn_conv_w, dn_a_log, dn_dt_bias, ret_gn_w, dn_norm_w,
           w_out, norm_xq_w, norm_mem_w, w_xq, w_xkv, w_xo, norm_moe_w, w_group_router, b_group_router,
           w_expert_router, b_expert_router, w_gate, w_up, w_down, norm_final_w):
    B, S, D = x.shape
    M = mem.shape[1]
    T = B * S
    depth = w_in.shape[0]
    n_main = 8 * HEADS * HD
    h = x.reshape(T, D)
    pos_col = positions.astype(f32).reshape(T, 1)
    mem2 = mem.reshape(B * M, D)
    tiles = _tile_plan(T, S, B * M, D)
    hg = HEADS
    bm = tiles.moe_rows
    for l in range(depth):
        w_in_t = jnp.swapaxes(w_in[l], 0, 1)
        w_main, w_ba = _cast_rows_t(w_in_t, n_main, tiles.cast_rows)
        proj, ba = _in_proj(h, norm_mix_w[l], w_main, w_ba, tiles.proj_rows, tiles.proj_cols)
        mix_r = _retention(proj, pos_col, ret_gn_w[l], B, S, hg)
        E, _, F = w_gate[l].shape
        expert_w = [w_gate[l].reshape(E * D, F), w_up[l].reshape(E * D, F), w_down[l].reshape(E * F, D)]
        mix_d, wg, wu, wd = _gdn(proj, ba, dn_conv_w[l], dn_a_log[l], dn_dt_bias[l], dn_norm_w[l],
                                 expert_w, B, S, hg)
        wg, wu, wd = wg.reshape(E, D, F), wu.reshape(E, D, F), wd.reshape(E, F, D)
        h = _mm_res(mix_r, mix_d, w_out[l].astype(bf16), h, tiles.out_rows, D)
        kv = _norm_mm(mem2, norm_mem_w[l], w_xkv[l].astype(bf16), tiles.mem_rows, tiles.mem_cols, "xkv")
        w_r = jnp.pad(jnp.concatenate([w_group_router[l], w_expert_router[l]], axis=1),
                      ((0, 0), (0, LANES - N_GROUPS - N_EXPERTS))).astype(bf16)
        b_r = jnp.pad(jnp.concatenate([b_group_router[l], b_expert_router[l]]),
                      (0, LANES - N_GROUPS - N_EXPERTS)).reshape(1, LANES).astype(f32)
        h, hn, rout, rout_t, cnt = _xattn_router(h, norm_xq_w[l], w_xq[l].astype(bf16), kv,
                                                 w_xo[l].astype(bf16), norm_moe_w[l], w_r, b_r, B, S, M,
                                                 tiles.attn_rows)
        blk_e, nused, nvalid, dest, pads = _dispatch_plan(rout_t, cnt, bm)
        xs = _dispatch(hn, dest, pads, blk_e.shape[0] * bm, tiles.dispatch_rows, D // (2 * LANES), bm)
        y = _ffn(xs, wg, wu, wd, blk_e, nused, nvalid, bm)
        if l + 1 < depth:
            raise NotImplementedError("the fused MoE combine applies the final norm: depth 1 only")
        h = _combine(y, h, rout, dest, norm_final_w, tiles.combine_rows)
    return h.reshape(B, S, D)
```

```python
import functools
from typing import NamedTuple

import numpy as np
import jax
import jax.numpy as jnp
from jax import lax
from jax.experimental import pallas as pl
from jax.experimental.pallas import tpu as pltpu

f32 = jnp.float32
bf16 = jnp.bfloat16

EPS = 1e-6
HEADS = 8
HD = 128
LANES = 128
XA_HEADS = 4
CONV_W = 4
ROPE_BASE = 10000.0
N_GROUPS = 4
EXP_PER_GROUP = 8
N_EXPERTS = N_GROUPS * EXP_PER_GROUP
RET_CHUNK = 256
DN_CHUNK = 128
VMEM_LIMIT = 56 * 1024 * 1024


def _params(sem, vmem=VMEM_LIMIT):
    return pltpu.CompilerParams(dimension_semantics=sem, vmem_limit_bytes=vmem)


def _dot(a, b):
    return jnp.dot(a.astype(bf16), b.astype(bf16), preferred_element_type=f32)


def _dot_nt(a, b):
    return lax.dot_general(a.astype(bf16), b.astype(bf16), (((1,), (1,)), ((), ())),
                           preferred_element_type=f32)


def _dot_tn(a, b):
    return lax.dot_general(a.astype(bf16), b.astype(bf16), (((0,), (0,)), ((), ())),
                           preferred_element_type=f32)


def _split3(a):
    hi = a.astype(bf16)
    r = a - hi.astype(f32)
    mid = r.astype(bf16)
    lo = (r - mid.astype(f32)).astype(bf16)
    return hi, mid, lo


def _sigmoid(x):
    return 1.0 / (1.0 + jnp.exp(-x))


def _silu(x):
    hx = 0.5 * x
    return hx + hx * jnp.tanh(hx)


def _rms_scale(x):
    return lax.rsqrt(jnp.mean(x * x, axis=-1, keepdims=True) + EPS)


def _cast_t_kernel(w_ref, tail_ref, o_ref, otail_ref):
    o_ref[...] = w_ref[...].T.astype(bf16)

    @pl.when(pl.program_id(0) == 0)
    def _():
        r = tail_ref.shape[0]
        pick = (lax.broadcasted_iota(jnp.int32, (r, LANES), 0)
                == lax.broadcasted_iota(jnp.int32, (r, LANES), 1))
        otail_ref[...] = _dot_tn(tail_ref[...], jnp.where(pick, 1.0, 0.0)).astype(bf16)


def _cast_rows_t(wt, n, tn):
    N, K = wt.shape
    r = N - n
    assert n % r == 0 and r % 8 == 0 and r <= LANES
    return pl.pallas_call(
        _cast_t_kernel,
        grid=(n // tn,),
        in_specs=[pl.BlockSpec((tn, K), lambda j: (j, 0)),
                  pl.BlockSpec((r, K), lambda j: (n // r, 0))],
        out_specs=[pl.BlockSpec((K, tn), lambda j: (0, j)),
                   pl.BlockSpec((K, LANES), lambda j: (0, 0))],
        out_shape=[jax.ShapeDtypeStruct((K, n), bf16), jax.ShapeDtypeStruct((K, LANES), bf16)],
        compiler_params=_params(("arbitrary",)),
        name="cast_w_in",
    )(wt, wt)


def _side_cast_step(step, nsteps, srcs, dsts, stage_in, stage_out, sem_in, sem_out):
    slot = step % 2
    mats = range(len(srcs))

    def in_copy(m, st, sl):
        r = stage_in[m].shape[1]
        rows = pl.ds(pl.multiple_of(st * r, r), r)
        return pltpu.make_async_copy(srcs[m].at[rows], stage_in[m].at[sl], sem_in.at[m, sl])

    def out_copy(m, st, sl):
        r = stage_out[m].shape[1]
        rows = pl.ds(pl.multiple_of(st * r, r), r)
        return pltpu.make_async_copy(stage_out[m].at[sl], dsts[m].at[rows], sem_out.at[m, sl])

    @pl.when(step == 0)
    def _():
        for m in mats:
            in_copy(m, 0, 0).start()

    @pl.when(step + 1 < nsteps)
    def _():
        for m in mats:
            in_copy(m, step + 1, 1 - slot).start()

    for m in mats:
        in_copy(m, step, slot).wait()

    @pl.when(step >= 2)
    def _():
        for m in mats:
            out_copy(m, step - 2, slot).wait()

    def convert():
        for m in mats:
            stage_out[m][slot] = stage_in[m][slot].astype(bf16)

    def finish():
        for m in mats:
            out_copy(m, step, slot).start()

        @pl.when(step == nsteps - 1)
        def _():
            for m in mats:
                if nsteps >= 2:
                    out_copy(m, step - 1, 1 - slot).wait()
                out_copy(m, step, slot).wait()

    return convert, finish


def _side_scratch(side, nsteps):
    rows = [a.shape[0] // nsteps for a in side]
    assert all(a.shape[0] == r * nsteps and r % 16 == 0 for a, r in zip(side, rows))
    return ([pltpu.VMEM((2, r, a.shape[1]), f32) for a, r in zip(side, rows)]
            + [pltpu.VMEM((2, r, a.shape[1]), bf16) for a, r in zip(side, rows)]
            + [pltpu.SemaphoreType.DMA((len(side), 2)), pltpu.SemaphoreType.DMA((len(side), 2))])


def _in_proj_kernel(x_ref, nw_ref, w_ref, wba_ref, o_ref, ba_ref, xn_ref):
    @pl.when(pl.program_id(1) == 0)
    def _():
        x = x_ref[...]
        xn = (x * _rms_scale(x) * nw_ref[...]).astype(bf16)
        xn_ref[...] = xn
        ba_ref[...] = jnp.dot(xn, wba_ref[...], preferred_element_type=f32)

    acc = jnp.dot(xn_ref[...], w_ref[...], preferred_element_type=f32)
    for c in range(o_ref.shape[0]):
        o_ref[c] = acc[:, c * LANES:(c + 1) * LANES].astype(bf16)


def _in_proj(x2, norm_w, w_main, w_ba, tm, tn):
    T, D = x2.shape
    N = w_main.shape[1]
    return pl.pallas_call(
        _in_proj_kernel,
        grid=(T // tm, N // tn),
        in_specs=[
            pl.BlockSpec((tm, D), lambda i, j: (i, 0)),
            pl.BlockSpec((1, D), lambda i, j: (0, 0)),
            pl.BlockSpec((D, tn), lambda i, j: (0, j)),
            pl.BlockSpec((D, LANES), lambda i, j: (0, 0)),
        ],
        out_specs=[
            pl.BlockSpec((tn // LANES, tm, LANES), lambda i, j: (j, i, 0)),
            pl.BlockSpec((tm, LANES), lambda i, j: (i, 0)),
        ],
        out_shape=[
            jax.ShapeDtypeStruct((N // LANES, T, LANES), bf16),
            jax.ShapeDtypeStruct((T, LANES), f32),
        ],
        scratch_shapes=[pltpu.VMEM((tm, D), bf16)],
        compiler_params=_params(("parallel", "arbitrary")),
        name="in_proj",
    )(x2, norm_w.reshape(1, D), w_main, w_ba)


def _retention_kernel(q_ref, k_ref, v_ref, g_ref, pos_ref, inv_ref, sgn_ref, dmask_ref, qdec_ref,
                      kdec_ref, cdec_ref, gnw_ref, o_ref, state_ref):
    @pl.when(pl.program_id(2) == 0)
    def _():
        state_ref[...] = jnp.zeros_like(state_ref)

    C = pos_ref.shape[0]
    lane = lax.broadcasted_iota(jnp.int32, (C // 2, HD), 1)
    low = lane < HD // 2
    ang = jnp.where(low, pos_ref[0:C // 2, :], pos_ref[C // 2:, :]) * inv_ref[...]
    cos_p = jnp.cos(ang)
    sin_p = jnp.sin(ang)

    def spread(t):
        swapped = pltpu.roll(t, HD // 2, 1)
        return jnp.concatenate([jnp.where(low, t, swapped), jnp.where(low, swapped, t)], axis=0)

    cos2 = spread(cos_p)
    sin2 = spread(sin_p) * sgn_ref[...]
    heads = range(q_ref.shape[0])

    def rope(x):
        return x * cos2 + pltpu.roll(x, HD // 2, 1) * sin2

    qr = [rope(q_ref[i].astype(f32)) for i in heads]
    kr = [rope(k_ref[i].astype(f32)) * (HD ** -0.5) for i in heads]
    state = [state_ref[i] for i in heads]
    s = [_dot_nt(qr[i], kr[i]) * dmask_ref[i] for i in heads]
    cross = [_dot(qr[i], state[i]) * qdec_ref[i] for i in heads]
    upd = [_dot_tn(kr[i] * kdec_ref[i], v_ref[i]) for i in heads]
    o = [_dot(s[i], v_ref[i]) + cross[i] for i in heads]
    for i in heads:
        state_ref[i] = state[i] * cdec_ref[i] + upd[i]
        d = o[i] - jnp.mean(o[i], axis=-1, keepdims=True)
        y = d * lax.rsqrt(jnp.mean(d * d, axis=-1, keepdims=True) + EPS) * gnw_ref[i]
        y = y * _silu(g_ref[i].astype(f32))
        o_ref[:, i * HD:(i + 1) * HD] = y.astype(bf16)


def _retention_tables(C):
    h = np.arange(HEADS, dtype=np.float64)
    log_gamma = np.log1p(-np.exp2(-5.0 - h))
    idx = np.arange(C, dtype=np.float64)
    rel = idx[:, None] - idx[None, :]
    dmask = np.where(rel >= 0, np.exp(log_gamma[:, None, None] * np.where(rel >= 0, rel, 0.0)), 0.0)
    qdec = np.exp(log_gamma[:, None] * (idx + 1.0))
    kdec = np.exp(log_gamma[:, None] * (C - 1.0 - idx))
    cdec = np.exp(log_gamma * C)
    rep = lambda a: np.broadcast_to(a[..., None], a.shape + (LANES,))
    return (jnp.asarray(dmask, f32), jnp.asarray(rep(qdec), f32), jnp.asarray(rep(kdec), f32),
            jnp.asarray(np.broadcast_to(cdec[:, None, None], (HEADS, 1, LANES)), f32))


def _retention(proj, pos_col, gn_w, B, S, hg):
    C = RET_CHUNK
    NC = S // C
    T = B * S
    G = HEADS // hg
    half = HD // 2
    inv = ROPE_BASE ** (-np.arange(half, dtype=np.float32) / half)
    inv2 = jnp.asarray(np.concatenate([inv, inv]).reshape(1, HD), f32)
    sgn = jnp.asarray(np.concatenate([-np.ones(half), np.ones(half)]).reshape(1, HD), f32)
    dmask, qdec, kdec, cdec = _retention_tables(C)

    def slab(off):
        return pl.BlockSpec((hg, C, HD), lambda b, g, n: (off // hg + g, b * NC + n, 0))

    def table(shape):
        return pl.BlockSpec((hg,) + shape, lambda b, g, n: (g,) + (0,) * len(shape))

    return pl.pallas_call(
        _retention_kernel,
        grid=(B, G, NC),
        in_specs=[
            slab(0), slab(HEADS), slab(2 * HEADS), slab(3 * HEADS),
            pl.BlockSpec((C, 1), lambda b, g, n: (b * NC + n, 0)),
            pl.BlockSpec((1, HD), lambda b, g, n: (0, 0)),
            pl.BlockSpec((1, HD), lambda b, g, n: (0, 0)),
            table((C, C)), table((C, LANES)), table((C, LANES)), table((1, LANES)), table((1, HD)),
        ],
        out_specs=pl.BlockSpec((C, hg * HD), lambda b, g, n: (b * NC + n, g)),
        out_shape=jax.ShapeDtypeStruct((T, HEADS * HD), bf16),
        scratch_shapes=[pltpu.VMEM((hg, HD, HD), f32)],
        compiler_params=_params(("parallel", "parallel", "arbitrary")),
        name="retention",
    )(proj, proj, proj, proj, pos_col, inv2, sgn, dmask, qdec, kdec, cdec,
      gn_w.reshape(HEADS, 1, HD))


def _gdn_kernel(q_ref, k_ref, v_ref, z_ref, ba_ref, cw_ref, arow_ref, dtrow_ref, nw_ref, *rest, hg, n_side):
    srcs, o_ref, dsts = rest[:n_side], rest[n_side], rest[n_side + 1:2 * n_side + 1]
    state_ref, tail_ref, win_ref = rest[2 * n_side + 1:2 * n_side + 4]
    rest = rest[2 * n_side + 4:]
    stage_in, stage_out, (sem_in, sem_out) = rest[:n_side], rest[n_side:2 * n_side], rest[2 * n_side:]
    step = (pl.program_id(0) * pl.num_programs(1) + pl.program_id(1)) * pl.num_programs(2) + pl.program_id(2)
    nsteps = pl.num_programs(0) * pl.num_programs(1) * pl.num_programs(2)
    convert, finish = _side_cast_step(step, nsteps, srcs, dsts, stage_in, stage_out, sem_in, sem_out)

    C = q_ref.shape[1]
    first = pl.program_id(2) == 0

    @pl.when(first)
    def _():
        state_ref[...] = jnp.zeros_like(state_ref)
        tail_ref[...] = jnp.zeros_like(tail_ref)

    convert()
    row = lax.broadcasted_iota(jnp.int32, (C, C), 0)
    col = lax.broadcasted_iota(jnp.int32, (C, C), 1)
    causal = row >= col
    strict = row > col
    same8 = (row // 8) == (col // 8)
    level_masks = []
    s = 8
    while s < C:
        level_masks.append(((row // (2 * s)) == (col // (2 * s))) & ((row // s) != (col // s)))
        s *= 2
    eye = jnp.where(row == col, 1.0, 0.0).astype(f32)
    tri = jnp.where(causal, 1.0, 0.0).astype(bf16)

    ba = ba_ref[...]
    beta_all = _sigmoid(ba)
    xa = ba + dtrow_ref[...]
    softplus = jnp.maximum(xa, 0.0) + jnp.log(1.0 + jnp.exp(-jnp.abs(xa)))
    glog = -jnp.exp(arow_ref[...]) * softplus
    g_hi, g_mid, g_lo = _split3(glog)
    gcum = (jnp.dot(tri, g_hi, preferred_element_type=f32)
            + jnp.dot(tri, g_mid, preferred_element_type=f32)
            + jnp.dot(tri, g_lo, preferred_element_type=f32))
    gcum_t = gcum.T
    lane = lax.broadcasted_iota(jnp.int32, (C, LANES), 1)
    sub = lax.broadcasted_iota(jnp.int32, (LANES, C), 0)

    def conv_silu(x_ref, kind, i):
        slot = kind * hg + i
        win_ref[slot, 0:8, :] = tail_ref[slot]
        win_ref[slot, 8:8 + C, :] = x_ref[i].astype(f32)
        w = cw_ref[kind, i]
        y = win_ref[slot, 8:8 + C, :] * w[CONV_W - 1:CONV_W, :]
        for t in range(1, CONV_W):
            y = y + win_ref[slot, 8 - t:8 - t + C, :] * w[CONV_W - 1 - t:CONV_W - t, :]
        tail_ref[slot] = win_ref[slot, C:C + 8, :]
        return _silu(y)

    heads = range(hg)
    hd = [pl.program_id(1) * hg + i for i in heads]
    q = [conv_silu(q_ref, 0, i) for i in heads]
    k = [conv_silu(k_ref, 1, i) for i in heads]
    v = [conv_silu(v_ref, 2, i) for i in heads]
    q = [x * lax.rsqrt(jnp.sum(x * x, axis=-1, keepdims=True) + EPS) * (HD ** -0.5) for x in q]
    k = [x * lax.rsqrt(jnp.sum(x * x, axis=-1, keepdims=True) + EPS) for x in k]

    bcol = [jnp.sum(jnp.where(lane == h, beta_all, 0.0), axis=1, keepdims=True) for h in hd]
    gcol = [jnp.sum(jnp.where(lane == h + HEADS, gcum, 0.0), axis=1, keepdims=True) for h in hd]
    grow = [jnp.sum(jnp.where(sub == h + HEADS, gcum_t, 0.0), axis=0, keepdims=True) for h in hd]
    glast = [g[C - 1:C, :] for g in gcol]
    gam = [jnp.where(causal, jnp.exp(jnp.where(causal, gc - gr, 0.0)), 0.0) for gc, gr in zip(gcol, grow)]
    egc = [jnp.exp(g) for g in gcol]

    kk = [_dot_nt(x, x) for x in k]
    qk = [_dot_nt(x, y) for x, y in zip(q, k)]
    a = [jnp.where(strict, m * g * b, 0.0) for m, g, b in zip(kk, gam, bcol)]
    qk = [m * g for m, g in zip(qk, gam)]

    d = [jnp.where(same8, m, 0.0) for m in a]
    d2 = [_dot(m, m) for m in d]
    d3 = [_dot(m, m2) for m, m2 in zip(d, d2)]
    d4 = [_dot(m2, m2) for m2 in d2]
    x = [eye - m + m2 - m3 for m, m2, m3 in zip(d, d2, d3)]
    x = [xi + _dot(xi, m4) for xi, m4 in zip(x, d4)]
    for mask in level_masks:
        t = [_dot(xi, jnp.where(mask, m, 0.0)) for xi, m in zip(x, a)]
        x = [xi - _dot(ti, xi) for xi, ti in zip(x, t)]

    rhs = [jnp.concatenate([vi * b, ki * (b * e)], axis=1) for vi, ki, b, e in zip(v, k, bcol, egc)]
    sol = [_dot(xi, r) for xi, r in zip(x, rhs)]
    q_dec = [qi * e for qi, e in zip(q, egc)]
    k_dec = [ki * jnp.exp(gl - gc) for ki, gl, gc in zip(k, glast, gcol)]

    state = [state_ref[i] for i in heads]
    ws = [_dot(jnp.concatenate([s_[:, HD:], qd], axis=0), st) for s_, qd, st in zip(sol, q_dec, state)]
    v_new = [s_[:, :HD] - w_[:C] for s_, w_ in zip(sol, ws)]
    o = [w_[C:] + _dot(m, vn) for w_, m, vn in zip(ws, qk, v_new)]
    upd = [_dot_tn(kd, vn) for kd, vn in zip(k_dec, v_new)]
    for i in heads:
        state_ref[i] = state[i] * jnp.exp(glast[i]) + upd[i]
        y = o[i] * _rms_scale(o[i]) * nw_ref[...]
        y = y * _silu(z_ref[i].astype(f32))
        o_ref[:, i * HD:(i + 1) * HD] = y.astype(bf16)
    finish()


def _gdn(proj, ba, conv_w, a_log, dt_bias, norm_w, side, B, S, hg):
    C = DN_CHUNK
    NC = S // C
    T = B * S
    G = HEADS // hg
    any_spec = pl.BlockSpec(memory_space=pl.ANY)
    cw = conv_w.reshape(CONV_W, 3, HEADS, HD).transpose(1, 2, 0, 3)
    pad = jnp.zeros((LANES - 2 * HEADS,), f32)
    arow = jnp.concatenate([jnp.zeros((HEADS,), f32), a_log.astype(f32), pad]).reshape(1, LANES)
    dtrow = jnp.concatenate([jnp.zeros((HEADS,), f32), dt_bias.astype(f32), pad]).reshape(1, LANES)

    def slab(off):
        return pl.BlockSpec((hg, C, HD), lambda b, g, n: (off // hg + g, b * NC + n, 0))

    row_spec = pl.BlockSpec((1, LANES), lambda b, g, n: (0, 0))
    return pl.pallas_call(
        functools.partial(_gdn_kernel, hg=hg, n_side=len(side)),
        grid=(B, G, NC),
        in_specs=[
            slab(4 * HEADS), slab(5 * HEADS), slab(6 * HEADS), slab(7 * HEADS),
            pl.BlockSpec((C, LANES), lambda b, g, n: (b * NC + n, 0)),
            pl.BlockSpec((3, hg, CONV_W, HD), lambda b, g, n: (0, g, 0, 0)),
            row_spec, row_spec, row_spec,
        ] + [any_spec] * len(side),
        out_specs=[pl.BlockSpec((C, hg * HD), lambda b, g, n: (b * NC + n, g))] + [any_spec] * len(side),
        out_shape=[jax.ShapeDtypeStruct((T, HEADS * HD), bf16)]
        + [jax.ShapeDtypeStruct(a.shape, bf16) for a in side],
        scratch_shapes=[
            pltpu.VMEM((hg, HD, HD), f32),
            pltpu.VMEM((3 * hg, 8, HD), f32),
            pltpu.VMEM((3 * hg, C + 8, HD), f32),
        ] + _side_scratch(side, B * G * NC),
        compiler_params=_params(("arbitrary", "arbitrary", "arbitrary")),
        name="gdn",
    )(proj, proj, proj, proj, ba, cw, arow, dtrow, norm_w.reshape(1, HD), *side)


def _mm_res_kernel(a1_ref, a2_ref, w_ref, r_ref, o_ref):
    a = jnp.concatenate([a1_ref[...], a2_ref[...]], axis=1)
    o_ref[...] = r_ref[...] + jnp.dot(a, w_ref[...], preferred_element_type=f32)


def _mm_res(a1, a2, w, res, tm, tn):
    T, K1 = a1.shape
    K2 = a2.shape[1]
    N = w.shape[1]
    w_mode = dict(pipeline_mode=pl.Buffered(1)) if tn == N else {}
    return pl.pallas_call(
        _mm_res_kernel,
        grid=(T // tm, N // tn),
        in_specs=[
            pl.BlockSpec((tm, K1), lambda i, j: (i, 0)),
            pl.BlockSpec((tm, K2), lambda i, j: (i, 0)),
            pl.BlockSpec((K1 + K2, tn), lambda i, j: (0, j), **w_mode),
            pl.BlockSpec((tm, tn), lambda i, j: (i, j)),
        ],
        out_specs=pl.BlockSpec((tm, tn), lambda i, j: (i, j)),
        out_shape=jax.ShapeDtypeStruct((T, N), f32),
        compiler_params=_params(("parallel", "parallel")),
        name="out_proj",
    )(a1, a2, w, res)


def _norm_mm_kernel(x_ref, nw_ref, w_ref, o_ref, xn_ref):
    @pl.when(pl.program_id(1) == 0)
    def _():
        x = x_ref[...]
        xn_ref[...] = (x * _rms_scale(x) * nw_ref[...]).astype(bf16)

    o_ref[...] = jnp.dot(xn_ref[...], w_ref[...], preferred_element_type=f32).astype(o_ref.dtype)


def _norm_mm(x2, norm_w, w, tm, tn, name):
    T, D = x2.shape
    N = w.shape[1]
    return pl.pallas_call(
        _norm_mm_kernel,
        grid=(T // tm, N // tn),
        in_specs=[
            pl.BlockSpec((tm, D), lambda i, j: (i, 0)),
            pl.BlockSpec((1, D), lambda i, j: (0, 0)),
            pl.BlockSpec((D, tn), lambda i, j: (0, j)),
        ],
        out_specs=pl.BlockSpec((tm, tn), lambda i, j: (i, j)),
        out_shape=jax.ShapeDtypeStruct((T, N), bf16),
        scratch_shapes=[pltpu.VMEM((tm, D), bf16)],
        compiler_params=_params(("parallel", "arbitrary")),
        name=name,
    )(x2, norm_w.reshape(1, D), w)


def _xattn_router_kernel(h_ref, nq_ref, wq_ref, k_ref, v_ref, wo_ref, nm_ref, wr_ref, b_ref,
                         h2_ref, hn_ref, rt_ref, rtt_ref, cnt_ref, run_ref):
    @pl.when((pl.program_id(0) == 0) & (pl.program_id(1) == 0))
    def _():
        run_ref[...] = jnp.zeros_like(run_ref)

    h1 = h_ref[...]
    D = h1.shape[1]
    dh = D // XA_HEADS
    xn = (h1 * _rms_scale(h1) * nq_ref[...]).astype(bf16)
    q = jnp.dot(xn, wq_ref[...], preferred_element_type=f32).astype(bf16)
    heads = []
    for h in range(XA_HEADS):
        sl = slice(h * dh, (h + 1) * dh)
        s = _dot_nt(q[:, sl], k_ref[:, sl]) * (dh ** -0.5)
        p = jnp.exp(s - jnp.max(s, axis=-1, keepdims=True))
        p = p / jnp.sum(p, axis=-1, keepdims=True)
        heads.append(_dot(p, v_ref[:, sl]).astype(bf16))
    h2 = h1 + jnp.dot(jnp.concatenate(heads, axis=1), wo_ref[...], preferred_element_type=f32)
    h2_ref[...] = h2
    _route(h2, nm_ref, wr_ref, b_ref, hn_ref, rt_ref, rtt_ref, cnt_ref, run_ref)


def _xattn_router(h1, norm_q, w_q, kv, w_o, norm_moe, w_r, bias_row, B, S, M, tm):
    T, D = h1.shape
    nt = S // tm
    g = D // (2 * LANES)
    tok = lambda b, i: (b * nt + i, 0)
    const = lambda b, i: (0, 0)
    resident = dict(pipeline_mode=pl.Buffered(1))
    return pl.pallas_call(
        _xattn_router_kernel,
        grid=(B, nt),
        in_specs=[
            pl.BlockSpec((tm, D), tok),
            pl.BlockSpec((1, D), const),
            pl.BlockSpec((D, D), const, **resident),
            pl.BlockSpec((M, D), lambda b, i: (b, 0)),
            pl.BlockSpec((M, D), lambda b, i: (b, 1)),
            pl.BlockSpec((D, D), const, **resident),
            pl.BlockSpec((1, D), const),
            pl.BlockSpec((D, LANES), const),
            pl.BlockSpec((1, LANES), const),
        ],
        out_specs=[
            pl.BlockSpec((tm, D), tok),
            pl.BlockSpec((tm * g, LANES), tok),
            pl.BlockSpec((tm, LANES), tok),
            pl.BlockSpec((8, tm), lambda b, i: (0, b * nt + i)),
            pl.BlockSpec((1, LANES), const),
        ],
        out_shape=[
            jax.ShapeDtypeStruct((T, D), f32),
            jax.ShapeDtypeStruct((T * g, LANES), jnp.uint32),
            jax.ShapeDtypeStruct((T, LANES), f32),
            jax.ShapeDtypeStruct((8, T), f32),
            jax.ShapeDtypeStruct((1, LANES), f32),
        ],
        scratch_shapes=[pltpu.VMEM((1, LANES), f32)],
        compiler_params=_params(("arbitrary", "arbitrary")),
        name="xattn_router",
    )(h1, norm_q.reshape(1, D), w_q, kv, kv, w_o, norm_moe.reshape(1, D), w_r, bias_row)


def _pack_pair(a, b):
    au = lax.bitcast_convert_type(a.astype(f32), jnp.uint32)
    bu = lax.bitcast_convert_type(b.astype(f32), jnp.uint32)
    return (au >> 16) | bu


def _unpack_pair(u):
    a = lax.bitcast_convert_type(u << 16, f32)
    b = lax.bitcast_convert_type(u & jnp.uint32(0xFFFF0000), f32)
    return a, b


def _store_slabs(ref, words):
    n = words.shape[0]
    g = words.shape[1] // LANES
    for c in range(g):
        ref[pl.ds(c, n, stride=g), :] = words[:, c * LANES:(c + 1) * LANES]


def _load_slabs(ref, n, g):
    return [ref[pl.ds(c, n, stride=g), :] for c in range(g)]


def _route(x, nw_ref, wr_ref, b_ref, hn_ref, o_ref, ot_ref, cnt_ref, run_ref):
    xn = x * _rms_scale(x) * nw_ref[...]
    xhi = xn.astype(bf16)
    logits = jnp.dot(xhi, wr_ref[...], preferred_element_type=f32) + b_ref[...]
    lane = lax.broadcasted_iota(jnp.int32, logits.shape, 1).astype(f32)
    neg = -jnp.inf
    big = float(LANES)

    def first_max(vals):
        m = jnp.max(vals, axis=-1, keepdims=True)
        return m, jnp.min(jnp.where(vals == m, lane, big), axis=-1, keepdims=True)

    is_group = lane < N_GROUPS
    gmax, gsel = first_max(jnp.where(is_group, logits, neg))
    gsum = jnp.sum(jnp.where(is_group, jnp.exp(logits - gmax), 0.0), axis=-1, keepdims=True)
    g_w = 1.0 / gsum
    lo = N_GROUPS + EXP_PER_GROUP * gsel
    in_group = (lane >= lo) & (lane < lo + EXP_PER_GROUP)
    el = jnp.where(in_group, logits, neg)
    l1, i1 = first_max(el)
    l2, i2 = first_max(jnp.where(lane == i1, neg, el))
    esum = jnp.sum(jnp.where(in_group, jnp.exp(logits - l1), 0.0), axis=-1, keepdims=True)
    p1 = 1.0 / esum
    p2 = jnp.exp(l2 - l1) / esum
    gate1 = g_w * p1 / (p1 + p2)
    gate2 = g_w * p2 / (p1 + p2)

    tm = x.shape[0]
    chosen = jnp.where((lane == i1) | (lane == i2), 1.0, 0.0)
    earlier = (lax.broadcasted_iota(jnp.int32, (tm, tm), 0) > lax.broadcasted_iota(jnp.int32, (tm, tm), 1))
    before = _dot(jnp.where(earlier, 1.0, 0.0), chosen) + run_ref[...]
    rank1 = jnp.sum(jnp.where(lane == i1, before, 0.0), axis=-1, keepdims=True)
    rank2 = jnp.sum(jnp.where(lane == i2, before, 0.0), axis=-1, keepdims=True)
    run_ref[...] = run_ref[...] + jnp.sum(chosen, axis=0, keepdims=True)
    cnt_ref[...] = run_ref[...]

    cols = (i1 - N_GROUPS, i2 - N_GROUPS, gate1, gate2, rank1, rank2)
    out = jnp.zeros_like(logits)
    for c, val in enumerate(cols):
        out = jnp.where(lane == c, val, out)
    o_ref[...] = out
    ot_ref[...] = out.T[0:ot_ref.shape[0], :]
    half = x.shape[1] // 2
    _store_slabs(hn_ref, _pack_pair(xhi[:, :half], xhi[:, half:]))


def _dispatch_kernel(dest_ref, pad_lo_ref, pad_n_ref, zcnt_ref, tail_ref, hn_ref, xs_hbm, zeros_ref, sem,
                     zsem, *, tb, g, bm):
    i = pl.program_id(0)
    base = i * tb
    sizes = [1 << b for b in range(bm.bit_length() - 1)]

    def zero_copy(first, p):
        return pltpu.make_async_copy(zeros_ref.at[pl.ds(0, p * g)], xs_hbm.at[pl.ds(first, p * g)], zsem)

    @pl.when(i == 0)
    def _():
        zeros_ref[...] = jnp.zeros_like(zeros_ref)

        def per_expert(e, carry):
            row = pad_lo_ref[e]
            n = pad_n_ref[e]
            for p in sizes:
                hit = (n & p) != 0

                @pl.when(hit)
                def _(row=row, p=p):
                    zero_copy(pl.multiple_of(row * g, g), p).start()

                row = row + jnp.where(hit, p, 0)
            return carry

        lax.fori_loop(0, N_EXPERTS, per_expert, 0)

        def per_tail_chunk(c, carry):
            zero_copy(pl.multiple_of((tail_ref[0] + c * sizes[-1]) * g, g), sizes[-1]).start()
            return carry

        lax.fori_loop(0, tail_ref[1], per_tail_chunk, 0)

    def body(r, carry):
        src = hn_ref.at[pl.ds(pl.multiple_of(r * g, g), g)]
        for kk in range(2):
            d = dest_ref[kk * (tb * pl.num_programs(0)) + base + r]
            pltpu.make_async_copy(src, xs_hbm.at[pl.ds(pl.multiple_of(d * g, g), g)],
                                  sem.at[kk]).start(priority=kk)
        return carry

    lax.fori_loop(0, tb, body, 0, unroll=8)
    for kk in range(2):
        pltpu.make_async_copy(hn_ref, hn_ref, sem.at[kk]).wait()

    @pl.when(i == pl.num_programs(0) - 1)
    def _():
        for b, p in enumerate(sizes):
            def wait_one(_, carry, p=p):
                zero_copy(0, p).wait()
                return carry
            lax.fori_loop(0, zcnt_ref[b], wait_one, 0)


def _dispatch(hn, dest, pads, P, tb, g, bm):
    T = hn.shape[0] // g
    grid_spec = pltpu.PrefetchScalarGridSpec(
        num_scalar_prefetch=5,
        grid=(T // tb,),
        in_specs=[pl.BlockSpec((tb * g, LANES), lambda i, *_: (i, 0))],
        out_specs=pl.BlockSpec(memory_space=pl.ANY),
        scratch_shapes=[pltpu.VMEM((bm // 2 * g, LANES), jnp.uint32),
                        pltpu.SemaphoreType.DMA((2,)), pltpu.SemaphoreType.DMA(())],
    )
    return pl.pallas_call(
        functools.partial(_dispatch_kernel, tb=tb, g=g, bm=bm),
        grid_spec=grid_spec,
        out_shape=jax.ShapeDtypeStruct((P * g, LANES), jnp.uint32),
        compiler_params=_params(("arbitrary",)),
        name="dispatch",
    )(dest, *pads, hn)


def _ffn_kernel(blk_e_ref, nused_ref, nvalid_ref, x_ref, wg_ref, wu_ref, wd_ref, y_ref, *, bm):
    nv = nvalid_ref[pl.program_id(0)]
    half = wg_ref.shape[0] // 2
    g = half // LANES

    def run(rows):
        pairs = [_unpack_pair(w) for w in _load_slabs(x_ref.at[pl.ds(0, rows * g)], rows, g)]
        x = jnp.concatenate([p[0].astype(bf16) for p in pairs] + [p[1].astype(bf16) for p in pairs], axis=1)
        gate = jnp.dot(x, wg_ref[...], preferred_element_type=f32)
        up = jnp.dot(x, wu_ref[...], preferred_element_type=f32)
        hmid = (_silu(gate) * up).astype(bf16)
        for s in range(g):
            ys = jnp.dot(hmid, wd_ref[:, 2 * s * LANES:2 * (s + 1) * LANES], preferred_element_type=f32)
            y_ref[pl.ds(s, rows, stride=g), :] = _pack_pair(ys[:, :LANES].astype(bf16),
                                                            ys[:, LANES:].astype(bf16))

    @pl.when(nv > bm // 2)
    def _():
        run(bm)

    @pl.when((nv > 0) & (nv <= bm // 2))
    def _():
        run(bm // 2)
        y_ref[pl.ds(bm // 2 * g, bm // 2 * g), :] = jnp.zeros((bm // 2 * g, LANES), jnp.uint32)

    @pl.when(nv == 0)
    def _():
        y_ref[...] = jnp.zeros_like(y_ref)


def _ffn(xs, wg, wu, wd, blk_e, nused, nvalid, bm):
    D, F = wg.shape[1:]
    g = D // (2 * LANES)
    NB = blk_e.shape[0]
    grid_spec = pltpu.PrefetchScalarGridSpec(
        num_scalar_prefetch=3,
        grid=(NB,),
        in_specs=[
            pl.BlockSpec((bm * g, LANES), lambda i, be, n, nv: (jnp.minimum(i, n[0] - 1), 0)),
            pl.BlockSpec((None, D, F), lambda i, be, n, nv: (be[i], 0, 0)),
            pl.BlockSpec((None, D, F), lambda i, be, n, nv: (be[i], 0, 0)),
            pl.BlockSpec((None, F, D), lambda i, be, n, nv: (be[i], 0, 0)),
        ],
        out_specs=pl.BlockSpec((bm * g, LANES), lambda i, be, n, nv: (i, 0)),
    )
    return pl.pallas_call(
        functools.partial(_ffn_kernel, bm=bm),
        grid_spec=grid_spec,
        out_shape=jax.ShapeDtypeStruct(xs.shape, jnp.uint32),
        compiler_params=_params(("arbitrary",)),
        name="ffn",
    )(blk_e, nused, nvalid, xs, wg, wu, wd)


def _combine_kernel(dest_ref, y_hbm, h_ref, rt_ref, nw_ref, o_ref, ybuf, sem):
    i = pl.program_id(0)
    n = pl.num_programs(0)
    tm = h_ref.shape[0]
    nslots = ybuf.shape[0]

    D = h_ref.shape[1]
    half = D // 2
    g = half // LANES

    def row_copy(blk, r, kk, slot_):
        d = dest_ref[kk * (tm * n) + blk * tm + r]
        first = r * g if isinstance(r, int) else pl.multiple_of(r * g, g)
        return pltpu.make_async_copy(y_hbm.at[pl.ds(pl.multiple_of(d * g, g), g)],
                                     ybuf.at[slot_, kk, pl.ds(first, g)], sem.at[slot_])

    def wait_slot(slot_):
        pltpu.make_async_copy(ybuf.at[slot_], ybuf.at[slot_], sem.at[slot_]).wait()

    @pl.when(i == 0)
    def _():
        for first in range(nslots - 1):
            def body(r, carry, first=first):
                for kk in range(2):
                    row_copy(jnp.minimum(first, n - 1), r, kk, first).start(priority=kk)
                return carry
            lax.fori_loop(0, tm, body, 0, unroll=8)

    nxt = jnp.minimum(i + nslots - 1, n - 1)
    pieces = 4 * g
    per = tm // pieces

    def step(slot_):
        def issue_batch(p):
            for r in range(p * per, (p + 1) * per):
                for kk in range(2):
                    row_copy(nxt, r, kk, (slot_ + nslots - 1) % nslots).start(priority=kk)

        wait_slot(slot_)
        rt = rt_ref[...]
        g1 = rt[:, 2:3]
        g2 = rt[:, 3:4]
        cols = [slice(c * LANES, (c + 1) * LANES) for c in range(2 * g)]
        hs = []
        for s in range(g):
            a1 = _unpack_pair(ybuf[slot_, 0, pl.ds(s, tm, stride=g), :])
            a2 = _unpack_pair(ybuf[slot_, 1, pl.ds(s, tm, stride=g), :])
            for hw in range(2):
                hs.append(h_ref[:, cols[2 * s + hw]] + g1 * a1[hw] + g2 * a2[hw])
                issue_batch(2 * s + hw)
        ms = sum(jnp.sum(x * x, axis=-1, keepdims=True) for x in hs) / D
        scale = lax.rsqrt(ms + EPS)
        for c in range(2 * g):
            o_ref[:, cols[c]] = hs[c] * scale * nw_ref[:, cols[c]]
            issue_batch(2 * g + c)

    for s in range(nslots):
        @pl.when(i % nslots == s)
        def _(s=s):
            step(s)

    @pl.when(i == n - 1)
    def _():
        for ahead in range(1, nslots):
            wait_slot((i + ahead) % nslots)


def _combine(y, h2, rout, dest, norm_w, tm):
    T, D = h2.shape
    g = D // (2 * LANES)
    grid_spec = pltpu.PrefetchScalarGridSpec(
        num_scalar_prefetch=1,
        grid=(T // tm,),
        in_specs=[
            pl.BlockSpec(memory_space=pl.ANY),
            pl.BlockSpec((tm, D), lambda i, d: (i, 0)),
            pl.BlockSpec((tm, LANES), lambda i, d: (i, 0)),
            pl.BlockSpec((1, D), lambda i, d: (0, 0)),
        ],
        out_specs=pl.BlockSpec((tm, D), lambda i, d: (i, 0)),
        scratch_shapes=[pltpu.VMEM((3, 2, tm * g, LANES), jnp.uint32), pltpu.SemaphoreType.DMA((3,))],
    )
    return pl.pallas_call(
        _combine_kernel,
        grid_spec=grid_spec,
        out_shape=jax.ShapeDtypeStruct((T, D), f32),
        compiler_params=_params(("arbitrary",)),
        name="combine",
    )(dest, y, h2, rout, norm_w.reshape(1, D))


def _dispatch_plan(rout_t, cnt, bm):
    T = rout_t.shape[1]
    NB = 2 * T // bm + N_EXPERTS
    e_kt = rout_t[0:2].astype(jnp.int32)
    rank_kt = rout_t[4:6].astype(jnp.int32)
    counts = cnt[0, N_GROUPS:N_GROUPS + N_EXPERTS].astype(jnp.int32)
    padded = (counts + bm - 1) // bm * bm
    pend = jnp.cumsum(padded)
    pstart = pend - padded
    first_row = jnp.arange(NB, dtype=jnp.int32) * bm
    blk_e = jnp.minimum(jnp.sum((pend[None, :] <= first_row[:, None]).astype(jnp.int32), axis=1),
                        N_EXPERTS - 1)
    nused = (pend[-1] // bm).astype(jnp.int32).reshape(1)
    of_blk = blk_e[:, None] == jnp.arange(N_EXPERTS, dtype=jnp.int32)
    row_end = jnp.sum(jnp.where(of_blk, pstart + counts, 0), axis=1)
    nvalid = jnp.where(first_row < pend[-1], jnp.clip(row_end - first_row, 0, bm), 0).astype(jnp.int32)
    onehot = e_kt[None] == jnp.arange(N_EXPERTS, dtype=jnp.int32)[:, None, None]
    dest = jnp.sum(jnp.where(onehot, pstart[:, None, None], 0), axis=0) + rank_kt
    pad_n = padded - counts
    nbits = bm.bit_length() - 1
    bits = (pad_n[:, None] >> jnp.arange(nbits, dtype=jnp.int32)) & 1
    tail_chunks = 2 * (NB - nused[0])
    zcnt = jnp.sum(bits, axis=0) + jnp.where(jnp.arange(nbits) == nbits - 1, tail_chunks, 0)
    pads = ((pstart + counts).astype(jnp.int32), pad_n.astype(jnp.int32), zcnt.astype(jnp.int32),
            jnp.stack([pend[-1], tail_chunks]).astype(jnp.int32))
    return blk_e, nused, nvalid, dest.reshape(2 * T).astype(jnp.int32), pads


def _tile(n, want):
    t = min(n, want)
    while n % t:
        t //= 2
    return t


class _Tiles(NamedTuple):
    cast_rows: int
    proj_rows: int
    proj_cols: int
    out_rows: int
    mem_rows: int
    mem_cols: int
    attn_rows: int
    moe_rows: int
    dispatch_rows: int
    combine_rows: int


def _tile_plan(T, S, n_mem, D):
    return _Tiles(cast_rows=512, proj_rows=_tile(T, 1024), proj_cols=2048, out_rows=_tile(T, 1024),
                  mem_rows=_tile(n_mem, 512), mem_cols=_tile(2 * D, 1024), attn_rows=_tile(S, 512),
                  moe_rows=512, dispatch_rows=_tile(T, 4096), combine_rows=_tile(T, 256))


def kernel(x, mem, positions, norm_mix_w, w_in, dn_conv_w, dn_a_log, dn_dt_bias, ret_gn_w, dn_norm_w,
           w_out, norm_xq_w, norm_mem_w, w_xq, w_xkv, w_xo, norm_moe_w, w_group_router, b_group_router,
           w_expert_router, b_expert_router, w_gate, w_up, w_down, norm_final_w):
    B, S, D = x.shape
    M = mem.shape[1]
    T = B * S
    depth = w_in.shape[0]
    n_main = 8 * HEADS * HD
    h = x.reshape(T, D)
    pos_col = positions.astype(f32).reshape(T, 1)
    mem2 = mem.reshape(B * M, D)
    tiles = _tile_plan(T, S, B * M, D)
    hg = HEADS
    bm = tiles.moe_rows
    for l in range(depth):
        w_in_t = jnp.swapaxes(w_in[l], 0, 1)
        w_main, w_ba = _cast_rows_t(w_in_t, n_main, tiles.cast_rows)
        proj, ba = _in_proj(h, norm_mix_w[l], w_main, w_ba, tiles.proj_rows, tiles.proj_cols)
        mix_r = _retention(proj, pos_col, ret_gn_w[l], B, S, hg)
        E, _, F = w_gate[l].shape
        expert_w = [w_gate[l].reshape(E * D, F), w_up[l].reshape(E * D, F), w_down[l].reshape(E * F, D)]
        mix_d, wg, wu, wd = _gdn(proj, ba, dn_conv_w[l], dn_a_log[l], dn_dt_bias[l], dn_norm_w[l],
                                 expert_w, B, S, hg)
        wg, wu, wd = wg.reshape(E, D, F), wu.reshape(E, D, F), wd.reshape(E, F, D)
        h = _mm_res(mix_r, mix_d, w_out[l].astype(bf16), h, tiles.out_rows, D)
        kv = _norm_mm(mem2, norm_mem_w[l], w_xkv[l].astype(bf16), tiles.mem_rows, tiles.mem_cols, "xkv")
        w_r = jnp.pad(jnp.concatenate([w_group_router[l], w_expert_router[l]], axis=1),
                      ((0, 0), (0, LANES - N_GROUPS - N_EXPERTS))).astype(bf16)
        b_r = jnp.pad(jnp.concatenate([b_group_router[l], b_expert_router[l]]),
                      (0, LANES - N_GROUPS - N_EXPERTS)).reshape(1, LANES).astype(f32)
        h, hn, rout, rout_t, cnt = _xattn_router(h, norm_xq_w[l], w_xq[l].astype(bf16), kv,
                                                 w_xo[l].astype(bf16), norm_moe_w[l], w_r, b_r, B, S, M,
                                                 tiles.attn_rows)
        blk_e, nused, nvalid, dest, pads = _dispatch_plan(rout_t, cnt, bm)
        xs = _dispatch(hn, dest, pads, blk_e.shape[0] * bm, tiles.dispatch_rows, D // (2 * LANES), bm)
        y = _ffn(xs, wg, wu, wd, blk_e, nused, nvalid, bm)
        if l + 1 < depth:
            raise NotImplementedError("the fused MoE combine applies the final norm: depth 1 only")
        h = _combine(y, h, rout, dest, norm_final_w, tiles.combine_rows)
    return h.reshape(B, S, D)
```

```python
import functools
from typing import NamedTuple

import numpy as np
import jax
import jax.numpy as jnp
from jax import lax
from jax.experimental import pallas as pl
from jax.experimental.pallas import tpu as pltpu

f32 = jnp.float32
bf16 = jnp.bfloat16

EPS = 1e-6
HEADS = 8
HD = 128
LANES = 128
XA_HEADS = 4
CONV_W = 4
ROPE_BASE = 10000.0
N_GROUPS = 4
EXP_PER_GROUP = 8
N_EXPERTS = N_GROUPS * EXP_PER_GROUP
RET_CHUNK = 256
DN_CHUNK = 128
VMEM_LIMIT = 56 * 1024 * 1024


def _params(sem, vmem=VMEM_LIMIT):
    return pltpu.CompilerParams(dimension_semantics=sem, vmem_limit_bytes=vmem)


def _dot(a, b):
    return jnp.dot(a.astype(bf16), b.astype(bf16), preferred_element_type=f32)


def _dot_nt(a, b):
    return lax.dot_general(a.astype(bf16), b.astype(bf16), (((1,), (1,)), ((), ())),
                           preferred_element_type=f32)


def _dot_tn(a, b):
    return lax.dot_general(a.astype(bf16), b.astype(bf16), (((0,), (0,)), ((), ())),
                           preferred_element_type=f32)


def _split3(a):
    hi = a.astype(bf16)
    r = a - hi.astype(f32)
    mid = r.astype(bf16)
    lo = (r - mid.astype(f32)).astype(bf16)
    return hi, mid, lo


def _sigmoid(x):
    return 1.0 / (1.0 + jnp.exp(-x))


def _silu(x):
    hx = 0.5 * x
    return hx + hx * jnp.tanh(hx)


def _rms_scale(x):
    return lax.rsqrt(jnp.mean(x * x, axis=-1, keepdims=True) + EPS)


def _cast_t_kernel(w_ref, tail_ref, o_ref, otail_ref):
    o_ref[...] = w_ref[...].T.astype(bf16)

    @pl.when(pl.program_id(0) == 0)
    def _():
        r = tail_ref.shape[0]
        pick = (lax.broadcasted_iota(jnp.int32, (r, LANES), 0)
                == lax.broadcasted_iota(jnp.int32, (r, LANES), 1))
        otail_ref[...] = _dot_tn(tail_ref[...], jnp.where(pick, 1.0, 0.0)).astype(bf16)


def _cast_rows_t(wt, n, tn):
    N, K = wt.shape
    r = N - n
    assert n % r == 0 and r % 8 == 0 and r <= LANES
    return pl.pallas_call(
        _cast_t_kernel,
        grid=(n // tn,),
        in_specs=[pl.BlockSpec((tn, K), lambda j: (j, 0)),
                  pl.BlockSpec((r, K), lambda j: (n // r, 0))],
        out_specs=[pl.BlockSpec((K, tn), lambda j: (0, j)),
                   pl.BlockSpec((K, LANES), lambda j: (0, 0))],
        out_shape=[jax.ShapeDtypeStruct((K, n), bf16), jax.ShapeDtypeStruct((K, LANES), bf16)],
        compiler_params=_params(("arbitrary",)),
        name="cast_w_in",
    )(wt, wt)


def _side_cast_step(step, nsteps, srcs, dsts, stage_in, stage_out, sem_in, sem_out):
    slot = step % 2
    mats = range(len(srcs))

    def in_copy(m, st, sl):
        r = stage_in[m].shape[1]
        rows = pl.ds(pl.multiple_of(st * r, r), r)
        return pltpu.make_async_copy(srcs[m].at[rows], stage_in[m].at[sl], sem_in.at[m, sl])

    def out_copy(m, st, sl):
        r = stage_out[m].shape[1]
        rows = pl.ds(pl.multiple_of(st * r, r), r)
        return pltpu.make_async_copy(stage_out[m].at[sl], dsts[m].at[rows], sem_out.at[m, sl])

    @pl.when(step == 0)
    def _():
        for m in mats:
            in_copy(m, 0, 0).start()

    @pl.when(step + 1 < nsteps)
    def _():
        for m in mats:
            in_copy(m, step + 1, 1 - slot).start()

    for m in mats:
        in_copy(m, step, slot).wait()

    @pl.when(step >= 2)
    def _():
        for m in mats:
            out_copy(m, step - 2, slot).wait()

    def convert():
        for m in mats:
            stage_out[m][slot] = stage_in[m][slot].astype(bf16)

    def finish():
        for m in mats:
            out_copy(m, step, slot).start()

        @pl.when(step == nsteps - 1)
        def _():
            for m in mats:
                if nsteps >= 2:
                    out_copy(m, step - 1, 1 - slot).wait()
                out_copy(m, step, slot).wait()

    return convert, finish


def _side_scratch(side, nsteps):
    rows = [a.shape[0] // nsteps for a in side]
    assert all(a.shape[0] == r * nsteps and r % 16 == 0 for a, r in zip(side, rows))
    return ([pltpu.VMEM((2, r, a.shape[1]), f32) for a, r in zip(side, rows)]
            + [pltpu.VMEM((2, r, a.shape[1]), bf16) for a, r in zip(side, rows)]
            + [pltpu.SemaphoreType.DMA((len(side), 2)), pltpu.SemaphoreType.DMA((len(side), 2))])


def _in_proj_kernel(x_ref, nw_ref, w_ref, wba_ref, o_ref, ba_ref, xn_ref):
    @pl.when(pl.program_id(1) == 0)
    def _():
        x = x_ref[...]
        xn = (x * _rms_scale(x) * nw_ref[...]).astype(bf16)
        xn_ref[...] = xn
        ba_ref[...] = jnp.dot(xn, wba_ref[...], preferred_element_type=f32)

    acc = jnp.dot(xn_ref[...], w_ref[...], preferred_element_type=f32)
    for c in range(o_ref.shape[0]):
        o_ref[c] = acc[:, c * LANES:(c + 1) * LANES].astype(bf16)


def _in_proj(x2, norm_w, w_main, w_ba, tm, tn):
    T, D = x2.shape
    N = w_main.shape[1]
    return pl.pallas_call(
        _in_proj_kernel,
        grid=(T // tm, N // tn),
        in_specs=[
            pl.BlockSpec((tm, D), lambda i, j: (i, 0)),
            pl.BlockSpec((1, D), lambda i, j: (0, 0)),
            pl.BlockSpec((D, tn), lambda i, j: (0, j)),
            pl.BlockSpec((D, LANES), lambda i, j: (0, 0)),
        ],
        out_specs=[
            pl.BlockSpec((tn // LANES, tm, LANES), lambda i, j: (j, i, 0)),
            pl.BlockSpec((tm, LANES), lambda i, j: (i, 0)),
        ],
        out_shape=[
            jax.ShapeDtypeStruct((N // LANES, T, LANES), bf16),
            jax.ShapeDtypeStruct((T, LANES), f32),
        ],
        scratch_shapes=[pltpu.VMEM((tm, D), bf16)],
        compiler_params=_params(("parallel", "arbitrary")),
        name="in_proj",
    )(x2, norm_w.reshape(1, D), w_main, w_ba)


def _retention_kernel(q_ref, k_ref, v_ref, g_ref, pos_ref, inv_ref, sgn_ref, dmask_ref, qdec_ref,
                      kdec_ref, cdec_ref, gnw_ref, *rest, n_side):
    srcs, o_ref, dsts = rest[:n_side], rest[n_side], rest[n_side + 1:2 * n_side + 1]
    state_ref = rest[2 * n_side + 1]
    rest = rest[2 * n_side + 2:]
    stage_in, stage_out, (sem_in, sem_out) = rest[:n_side], rest[n_side:2 * n_side], rest[2 * n_side:]
    step = (pl.program_id(0) * pl.num_programs(1) + pl.program_id(1)) * pl.num_programs(2) + pl.program_id(2)
    nsteps = pl.num_programs(0) * pl.num_programs(1) * pl.num_programs(2)
    convert, finish = _side_cast_step(step, nsteps, srcs, dsts, stage_in, stage_out, sem_in, sem_out)

    @pl.when(pl.program_id(2) == 0)
    def _():
        state_ref[...] = jnp.zeros_like(state_ref)

    convert()

    C = pos_ref.shape[0]
    lane = lax.broadcasted_iota(jnp.int32, (C // 2, HD), 1)
    low = lane < HD // 2
    ang = jnp.where(low, pos_ref[0:C // 2, :], pos_ref[C // 2:, :]) * inv_ref[...]
    cos_p = jnp.cos(ang)
    sin_p = jnp.sin(ang)

    def spread(t):
        swapped = pltpu.roll(t, HD // 2, 1)
        return jnp.concatenate([jnp.where(low, t, swapped), jnp.where(low, swapped, t)], axis=0)

    cos2 = spread(cos_p)
    sin2 = spread(sin_p) * sgn_ref[...]
    heads = range(q_ref.shape[0])

    def rope(x):
        return x * cos2 + pltpu.roll(x, HD // 2, 1) * sin2

    qr = [rope(q_ref[i].astype(f32)) for i in heads]
    kr = [rope(k_ref[i].astype(f32)) * (HD ** -0.5) for i in heads]
    state = [state_ref[i] for i in heads]
    s = [_dot_nt(qr[i], kr[i]) * dmask_ref[i] for i in heads]
    cross = [_dot(qr[i], state[i]) * qdec_ref[i] for i in heads]
    upd = [_dot_tn(kr[i] * kdec_ref[i], v_ref[i]) for i in heads]
    o = [_dot(s[i], v_ref[i]) + cross[i] for i in heads]
    for i in heads:
        state_ref[i] = state[i] * cdec_ref[i] + upd[i]
        d = o[i] - jnp.mean(o[i], axis=-1, keepdims=True)
        y = d * lax.rsqrt(jnp.mean(d * d, axis=-1, keepdims=True) + EPS) * gnw_ref[i]
        y = y * _silu(g_ref[i].astype(f32))
        o_ref[:, i * HD:(i + 1) * HD] = y.astype(bf16)
    finish()


def _retention_tables(C):
    h = np.arange(HEADS, dtype=np.float64)
    log_gamma = np.log1p(-np.exp2(-5.0 - h))
    idx = np.arange(C, dtype=np.float64)
    rel = idx[:, None] - idx[None, :]
    dmask = np.where(rel >= 0, np.exp(log_gamma[:, None, None] * np.where(rel >= 0, rel, 0.0)), 0.0)
    qdec = np.exp(log_gamma[:, None] * (idx + 1.0))
    kdec = np.exp(log_gamma[:, None] * (C - 1.0 - idx))
    cdec = np.exp(log_gamma * C)
    rep = lambda a: np.broadcast_to(a[..., None], a.shape + (LANES,))
    return (jnp.asarray(dmask, f32), jnp.asarray(rep(qdec), f32), jnp.asarray(rep(kdec), f32),
            jnp.asarray(np.broadcast_to(cdec[:, None, None], (HEADS, 1, LANES)), f32))


def _retention(proj, pos_col, gn_w, side, B, S, hg):
    any_spec = pl.BlockSpec(memory_space=pl.ANY)
    C = RET_CHUNK
    NC = S // C
    T = B * S
    G = HEADS // hg
    half = HD // 2
    inv = ROPE_BASE ** (-np.arange(half, dtype=np.float32) / half)
    inv2 = jnp.asarray(np.concatenate([inv, inv]).reshape(1, HD), f32)
    sgn = jnp.asarray(np.concatenate([-np.ones(half), np.ones(half)]).reshape(1, HD), f32)
    dmask, qdec, kdec, cdec = _retention_tables(C)

    def slab(off):
        return pl.BlockSpec((hg, C, HD), lambda b, g, n: (off // hg + g, b * NC + n, 0))

    def table(shape):
        return pl.BlockSpec((hg,) + shape, lambda b, g, n: (g,) + (0,) * len(shape))

    return pl.pallas_call(
        functools.partial(_retention_kernel, n_side=len(side)),
        grid=(B, G, NC),
        in_specs=[
            slab(0), slab(HEADS), slab(2 * HEADS), slab(3 * HEADS),
            pl.BlockSpec((C, 1), lambda b, g, n: (b * NC + n, 0)),
            pl.BlockSpec((1, HD), lambda b, g, n: (0, 0)),
            pl.BlockSpec((1, HD), lambda b, g, n: (0, 0)),
            table((C, C)), table((C, LANES)), table((C, LANES)), table((1, LANES)), table((1, HD)),
        ] + [any_spec] * len(side),
        out_specs=[pl.BlockSpec((C, hg * HD), lambda b, g, n: (b * NC + n, g))] + [any_spec] * len(side),
        out_shape=[jax.ShapeDtypeStruct((T, HEADS * HD), bf16)]
        + [jax.ShapeDtypeStruct(a.shape, bf16) for a in side],
        scratch_shapes=[pltpu.VMEM((hg, HD, HD), f32)] + _side_scratch(side, B * G * NC),
        compiler_params=_params(("arbitrary", "arbitrary", "arbitrary")),
        name="retention",
    )(proj, proj, proj, proj, pos_col, inv2, sgn, dmask, qdec, kdec, cdec,
      gn_w.reshape(HEADS, 1, HD), *side)


def _gdn_kernel(q_ref, k_ref, v_ref, z_ref, ba_ref, cw_ref, arow_ref, dtrow_ref, nw_ref, *rest, hg, n_side):
    srcs, o_ref, dsts = rest[:n_side], rest[n_side], rest[n_side + 1:2 * n_side + 1]
    state_ref, tail_ref, win_ref = rest[2 * n_side + 1:2 * n_side + 4]
    rest = rest[2 * n_side + 4:]
    stage_in, stage_out, (sem_in, sem_out) = rest[:n_side], rest[n_side:2 * n_side], rest[2 * n_side:]
    step = (pl.program_id(0) * pl.num_programs(1) + pl.program_id(1)) * pl.num_programs(2) + pl.program_id(2)
    nsteps = pl.num_programs(0) * pl.num_programs(1) * pl.num_programs(2)
    convert, finish = _side_cast_step(step, nsteps, srcs, dsts, stage_in, stage_out, sem_in, sem_out)

    C = q_ref.shape[1]
    first = pl.program_id(2) == 0

    @pl.when(first)
    def _():
        state_ref[...] = jnp.zeros_like(state_ref)
        tail_ref[...] = jnp.zeros_like(tail_ref)

    convert()
    row = lax.broadcasted_iota(jnp.int32, (C, C), 0)
    col = lax.broadcasted_iota(jnp.int32, (C, C), 1)
    causal = row >= col
    strict = row > col
    same8 = (row // 8) == (col // 8)
    level_masks = []
    s = 8
    while s < C:
        level_masks.append(((row // (2 * s)) == (col // (2 * s))) & ((row // s) != (col // s)))
        s *= 2
    eye = jnp.where(row == col, 1.0, 0.0).astype(f32)
    tri = jnp.where(causal, 1.0, 0.0).astype(bf16)

    ba = ba_ref[...]
    beta_all = _sigmoid(ba)
    xa = ba + dtrow_ref[...]
    softplus = jnp.maximum(xa, 0.0) + jnp.log(1.0 + jnp.exp(-jnp.abs(xa)))
    glog = -jnp.exp(arow_ref[...]) * softplus
    g_hi, g_mid, g_lo = _split3(glog)
    gcum = (jnp.dot(tri, g_hi, preferred_element_type=f32)
            + jnp.dot(tri, g_mid, preferred_element_type=f32)
            + jnp.dot(tri, g_lo, preferred_element_type=f32))
    gcum_t = gcum.T
    lane = lax.broadcasted_iota(jnp.int32, (C, LANES), 1)
    sub = lax.broadcasted_iota(jnp.int32, (LANES, C), 0)

    def conv_silu(x_ref, kind, i):
        slot = kind * hg + i
        win_ref[slot, 0:8, :] = tail_ref[slot]
        win_ref[slot, 8:8 + C, :] = x_ref[i].astype(f32)
        w = cw_ref[kind, i]
        y = win_ref[slot, 8:8 + C, :] * w[CONV_W - 1:CONV_W, :]
        for t in range(1, CONV_W):
            y = y + win_ref[slot, 8 - t:8 - t + C, :] * w[CONV_W - 1 - t:CONV_W - t, :]
        tail_ref[slot] = win_ref[slot, C:C + 8, :]
        return _silu(y)

    heads = range(hg)
    hd = [pl.program_id(1) * hg + i for i in heads]
    q = [conv_silu(q_ref, 0, i) for i in heads]
    k = [conv_silu(k_ref, 1, i) for i in heads]
    v = [conv_silu(v_ref, 2, i) for i in heads]
    q = [x * lax.rsqrt(jnp.sum(x * x, axis=-1, keepdims=True) + EPS) * (HD ** -0.5) for x in q]
    k = [x * lax.rsqrt(jnp.sum(x * x, axis=-1, keepdims=True) + EPS) for x in k]

    bcol = [jnp.sum(jnp.where(lane == h, beta_all, 0.0), axis=1, keepdims=True) for h in hd]
    gcol = [jnp.sum(jnp.where(lane == h + HEADS, gcum, 0.0), axis=1, keepdims=True) for h in hd]
    grow = [jnp.sum(jnp.where(sub == h + HEADS, gcum_t, 0.0), axis=0, keepdims=True) for h in hd]
    glast = [g[C - 1:C, :] for g in gcol]
    gam = [jnp.where(causal, jnp.exp(jnp.where(causal, gc - gr, 0.0)), 0.0) for gc, gr in zip(gcol, grow)]
    egc = [jnp.exp(g) for g in gcol]

    kk = [_dot_nt(x, x) for x in k]
    qk = [_dot_nt(x, y) for x, y in zip(q, k)]
    a = [jnp.where(strict, m * g * b, 0.0) for m, g, b in zip(kk, gam, bcol)]
    qk = [m * g for m, g in zip(qk, gam)]

    d = [jnp.where(same8, m, 0.0) for m in a]
    d2 = [_dot(m, m) for m in d]
    d3 = [_dot(m, m2) for m, m2 in zip(d, d2)]
    d4 = [_dot(m2, m2) for m2 in d2]
    x = [eye - m + m2 - m3 for m, m2, m3 in zip(d, d2, d3)]
    x = [xi + _dot(xi, m4) for xi, m4 in zip(x, d4)]
    for mask in level_masks:
        t = [_dot(xi, jnp.where(mask, m, 0.0)) for xi, m in zip(x, a)]
        x = [xi - _dot(ti, xi) for xi, ti in zip(x, t)]

    rhs = [jnp.concatenate([vi * b, ki * (b * e)], axis=1) for vi, ki, b, e in zip(v, k, bcol, egc)]
    sol = [_dot(xi, r) for xi, r in zip(x, rhs)]
    q_dec = [qi * e for qi, e in zip(q, egc)]
    k_dec = [ki * jnp.exp(gl - gc) for ki, gl, gc in zip(k, glast, gcol)]

    state = [state_ref[i] for i in heads]
    ws = [_dot(jnp.concatenate([s_[:, HD:], qd], axis=0), st) for s_, qd, st in zip(sol, q_dec, state)]
    v_new = [s_[:, :HD] - w_[:C] for s_, w_ in zip(sol, ws)]
    o = [w_[C:] + _dot(m, vn) for w_, m, vn in zip(ws, qk, v_new)]
    upd = [_dot_tn(kd, vn) for kd, vn in zip(k_dec, v_new)]
    for i in heads:
        state_ref[i] = state[i] * jnp.exp(glast[i]) + upd[i]
        y = o[i] * _rms_scale(o[i]) * nw_ref[...]
        y = y * _silu(z_ref[i].astype(f32))
        o_ref[:, i * HD:(i + 1) * HD] = y.astype(bf16)
    finish()


def _gdn(proj, ba, conv_w, a_log, dt_bias, norm_w, side, B, S, hg):
    C = DN_CHUNK
    NC = S // C
    T = B * S
    G = HEADS // hg
    any_spec = pl.BlockSpec(memory_space=pl.ANY)
    cw = conv_w.reshape(CONV_W, 3, HEADS, HD).transpose(1, 2, 0, 3)
    pad = jnp.zeros((LANES - 2 * HEADS,), f32)
    arow = jnp.concatenate([jnp.zeros((HEADS,), f32), a_log.astype(f32), pad]).reshape(1, LANES)
    dtrow = jnp.concatenate([jnp.zeros((HEADS,), f32), dt_bias.astype(f32), pad]).reshape(1, LANES)

    def slab(off):
        return pl.BlockSpec((hg, C, HD), lambda b, g, n: (off // hg + g, b * NC + n, 0))

    row_spec = pl.BlockSpec((1, LANES), lambda b, g, n: (0, 0))
    return pl.pallas_call(
        functools.partial(_gdn_kernel, hg=hg, n_side=len(side)),
        grid=(B, G, NC),
        in_specs=[
            slab(4 * HEADS), slab(5 * HEADS), slab(6 * HEADS), slab(7 * HEADS),
            pl.BlockSpec((C, LANES), lambda b, g, n: (b * NC + n, 0)),
            pl.BlockSpec((3, hg, CONV_W, HD), lambda b, g, n: (0, g, 0, 0)),
            row_spec, row_spec, row_spec,
        ] + [any_spec] * len(side),
        out_specs=[pl.BlockSpec((C, hg * HD), lambda b, g, n: (b * NC + n, g))] + [any_spec] * len(side),
        out_shape=[jax.ShapeDtypeStruct((T, HEADS * HD), bf16)]
        + [jax.ShapeDtypeStruct(a.shape, bf16) for a in side],
        scratch_shapes=[
            pltpu.VMEM((hg, HD, HD), f32),
            pltpu.VMEM((3 * hg, 8, HD), f32),
            pltpu.VMEM((3 * hg, C + 8, HD), f32),
        ] + _side_scratch(side, B * G * NC),
        compiler_params=_params(("arbitrary", "arbitrary", "arbitrary")),
        name="gdn",
    )(proj, proj, proj, proj, ba, cw, arow, dtrow, norm_w.reshape(1, HD), *side)


def _mm_res_kernel(a1_ref, a2_ref, w_ref, r_ref, o_ref):
    a = jnp.concatenate([a1_ref[...], a2_ref[...]], axis=1)
    o_ref[...] = r_ref[...] + jnp.dot(a, w_ref[...], preferred_element_type=f32)


def _mm_res(a1, a2, w, res, tm, tn):
    T, K1 = a1.shape
    K2 = a2.shape[1]
    N = w.shape[1]
    w_mode = dict(pipeline_mode=pl.Buffered(1)) if tn == N else {}
    return pl.pallas_call(
        _mm_res_kernel,
        grid=(T // tm, N // tn),
        in_specs=[
            pl.BlockSpec((tm, K1), lambda i, j: (i, 0)),
            pl.BlockSpec((tm, K2), lambda i, j: (i, 0)),
            pl.BlockSpec((K1 + K2, tn), lambda i, j: (0, j), **w_mode),
            pl.BlockSpec((tm, tn), lambda i, j: (i, j)),
        ],
        out_specs=pl.BlockSpec((tm, tn), lambda i, j: (i, j)),
        out_shape=jax.ShapeDtypeStruct((T, N), f32),
        compiler_params=_params(("parallel", "parallel")),
        name="out_proj",
    )(a1, a2, w, res)


def _norm_mm_kernel(x_ref, nw_ref, w_ref, o_ref, xn_ref):
    @pl.when(pl.program_id(1) == 0)
    def _():
        x = x_ref[...]
        xn_ref[...] = (x * _rms_scale(x) * nw_ref[...]).astype(bf16)

    o_ref[...] = jnp.dot(xn_ref[...], w_ref[...], preferred_element_type=f32).astype(o_ref.dtype)


def _norm_mm(x2, norm_w, w, tm, tn, name):
    T, D = x2.shape
    N = w.shape[1]
    return pl.pallas_call(
        _norm_mm_kernel,
        grid=(T // tm, N // tn),
        in_specs=[
            pl.BlockSpec((tm, D), lambda i, j: (i, 0)),
            pl.BlockSpec((1, D), lambda i, j: (0, 0)),
            pl.BlockSpec((D, tn), lambda i, j: (0, j)),
        ],
        out_specs=pl.BlockSpec((tm, tn), lambda i, j: (i, j)),
        out_shape=jax.ShapeDtypeStruct((T, N), bf16),
        scratch_shapes=[pltpu.VMEM((tm, D), bf16)],
        compiler_params=_params(("parallel", "arbitrary")),
        name=name,
    )(x2, norm_w.reshape(1, D), w)


def _xattn_router_kernel(h_ref, nq_ref, wq_ref, k_ref, v_ref, wo_ref, nm_ref, wr_ref, b_ref,
                         h2_ref, hn_ref, rt_ref, rtt_ref, cnt_ref, run_ref):
    @pl.when((pl.program_id(0) == 0) & (pl.program_id(1) == 0))
    def _():
        run_ref[...] = jnp.zeros_like(run_ref)

    h1 = h_ref[...]
    D = h1.shape[1]
    dh = D // XA_HEADS
    xn = (h1 * _rms_scale(h1) * nq_ref[...]).astype(bf16)
    q = jnp.dot(xn, wq_ref[...], preferred_element_type=f32).astype(bf16)
    heads = []
    for h in range(XA_HEADS):
        sl = slice(h * dh, (h + 1) * dh)
        s = _dot_nt(q[:, sl], k_ref[:, sl]) * (dh ** -0.5)
        p = jnp.exp(s - jnp.max(s, axis=-1, keepdims=True))
        p = p / jnp.sum(p, axis=-1, keepdims=True)
        heads.append(_dot(p, v_ref[:, sl]).astype(bf16))
    h2 = h1 + jnp.dot(jnp.concatenate(heads, axis=1), wo_ref[...], preferred_element_type=f32)
    h2_ref[...] = h2
    _route(h2, nm_ref, wr_ref, b_ref, hn_ref, rt_ref, rtt_ref, cnt_ref, run_ref)


def _xattn_router(h1, norm_q, w_q, kv, w_o, norm_moe, w_r, bias_row, B, S, M, tm):
    T, D = h1.shape
    nt = S // tm
    g = D // (2 * LANES)
    tok = lambda b, i: (b * nt + i, 0)
    const = lambda b, i: (0, 0)
    resident = dict(pipeline_mode=pl.Buffered(1))
    return pl.pallas_call(
        _xattn_router_kernel,
        grid=(B, nt),
        in_specs=[
            pl.BlockSpec((tm, D), tok),
            pl.BlockSpec((1, D), const),
            pl.BlockSpec((D, D), const, **resident),
            pl.BlockSpec((M, D), lambda b, i: (b, 0)),
            pl.BlockSpec((M, D), lambda b, i: (b, 1)),
            pl.BlockSpec((D, D), const, **resident),
            pl.BlockSpec((1, D), const),
            pl.BlockSpec((D, LANES), const),
            pl.BlockSpec((1, LANES), const),
        ],
        out_specs=[
            pl.BlockSpec((tm, D), tok),
            pl.BlockSpec((tm * g, LANES), tok),
            pl.BlockSpec((tm, LANES), tok),
            pl.BlockSpec((8, tm), lambda b, i: (0, b * nt + i)),
            pl.BlockSpec((1, LANES), const),
        ],
        out_shape=[
            jax.ShapeDtypeStruct((T, D), f32),
            jax.ShapeDtypeStruct((T * g, LANES), jnp.uint32),
            jax.ShapeDtypeStruct((T, LANES), f32),
            jax.ShapeDtypeStruct((8, T), f32),
            jax.ShapeDtypeStruct((1, LANES), f32),
        ],
        scratch_shapes=[pltpu.VMEM((1, LANES), f32)],
        compiler_params=_params(("arbitrary", "arbitrary")),
        name="xattn_router",
    )(h1, norm_q.reshape(1, D), w_q, kv, kv, w_o, norm_moe.reshape(1, D), w_r, bias_row)


def _pack_pair(a, b):
    au = lax.bitcast_convert_type(a.astype(f32), jnp.uint32)
    bu = lax.bitcast_convert_type(b.astype(f32), jnp.uint32)
    return (au >> 16) | bu


def _unpack_pair(u):
    a = lax.bitcast_convert_type(u << 16, f32)
    b = lax.bitcast_convert_type(u & jnp.uint32(0xFFFF0000), f32)
    return a, b


def _store_slabs(ref, words):
    n = words.shape[0]
    g = words.shape[1] // LANES
    for c in range(g):
        ref[pl.ds(c, n, stride=g), :] = words[:, c * LANES:(c + 1) * LANES]


def _load_slabs(ref, n, g):
    return [ref[pl.ds(c, n, stride=g), :] for c in range(g)]


def _route(x, nw_ref, wr_ref, b_ref, hn_ref, o_ref, ot_ref, cnt_ref, run_ref):
    xn = x * _rms_scale(x) * nw_ref[...]
    xhi = xn.astype(bf16)
    logits = jnp.dot(xhi, wr_ref[...], preferred_element_type=f32) + b_ref[...]
    lane = lax.broadcasted_iota(jnp.int32, logits.shape, 1).astype(f32)
    neg = -jnp.inf
    big = float(LANES)

    def first_max(vals):
        m = jnp.max(vals, axis=-1, keepdims=True)
        return m, jnp.min(jnp.where(vals == m, lane, big), axis=-1, keepdims=True)

    is_group = lane < N_GROUPS
    gmax, gsel = first_max(jnp.where(is_group, logits, neg))
    gsum = jnp.sum(jnp.where(is_group, jnp.exp(logits - gmax), 0.0), axis=-1, keepdims=True)
    g_w = 1.0 / gsum
    lo = N_GROUPS + EXP_PER_GROUP * gsel
    in_group = (lane >= lo) & (lane < lo + EXP_PER_GROUP)
    el = jnp.where(in_group, logits, neg)
    l1, i1 = first_max(el)
    l2, i2 = first_max(jnp.where(lane == i1, neg, el))
    esum = jnp.sum(jnp.where(in_group, jnp.exp(logits - l1), 0.0), axis=-1, keepdims=True)
    p1 = 1.0 / esum
    p2 = jnp.exp(l2 - l1) / esum
    gate1 = g_w * p1 / (p1 + p2)
    gate2 = g_w * p2 / (p1 + p2)

    tm = x.shape[0]
    chosen = jnp.where((lane == i1) | (lane == i2), 1.0, 0.0)
    earlier = (lax.broadcasted_iota(jnp.int32, (tm, tm), 0) > lax.broadcasted_iota(jnp.int32, (tm, tm), 1))
    before = _dot(jnp.where(earlier, 1.0, 0.0), chosen) + run_ref[...]
    rank1 = jnp.sum(jnp.where(lane == i1, before, 0.0), axis=-1, keepdims=True)
    rank2 = jnp.sum(jnp.where(lane == i2, before, 0.0), axis=-1, keepdims=True)
    run_ref[...] = run_ref[...] + jnp.sum(chosen, axis=0, keepdims=True)
    cnt_ref[...] = run_ref[...]

    cols = (i1 - N_GROUPS, i2 - N_GROUPS, gate1, gate2, rank1, rank2)
    out = jnp.zeros_like(logits)
    for c, val in enumerate(cols):
        out = jnp.where(lane == c, val, out)
    o_ref[...] = out
    ot_ref[...] = out.T[0:ot_ref.shape[0], :]
    half = x.shape[1] // 2
    _store_slabs(hn_ref, _pack_pair(xhi[:, :half], xhi[:, half:]))


def _dispatch_kernel(dest_ref, pad_lo_ref, pad_n_ref, zcnt_ref, tail_ref, hn_ref, xs_hbm, zeros_ref, sem,
                     zsem, *, tb, g, bm):
    i = pl.program_id(0)
    base = i * tb
    sizes = [1 << b for b in range(bm.bit_length() - 1)]

    def zero_copy(first, p):
        return pltpu.make_async_copy(zeros_ref.at[pl.ds(0, p * g)], xs_hbm.at[pl.ds(first, p * g)], zsem)

    @pl.when(i == 0)
    def _():
        zeros_ref[...] = jnp.zeros_like(zeros_ref)

        def per_expert(e, carry):
            row = pad_lo_ref[e]
            n = pad_n_ref[e]
            for p in sizes:
                hit = (n & p) != 0

                @pl.when(hit)
                def _(row=row, p=p):
                    zero_copy(pl.multiple_of(row * g, g), p).start()

                row = row + jnp.where(hit, p, 0)
            return carry

        lax.fori_loop(0, N_EXPERTS, per_expert, 0)

        def per_tail_chunk(c, carry):
            zero_copy(pl.multiple_of((tail_ref[0] + c * sizes[-1]) * g, g), sizes[-1]).start()
            return carry

        lax.fori_loop(0, tail_ref[1], per_tail_chunk, 0)

    def body(r, carry):
        src = hn_ref.at[pl.ds(pl.multiple_of(r * g, g), g)]
        for kk in range(2):
            d = dest_ref[kk * (tb * pl.num_programs(0)) + base + r]
            pltpu.make_async_copy(src, xs_hbm.at[pl.ds(pl.multiple_of(d * g, g), g)],
                                  sem.at[kk]).start(priority=kk)
        return carry

    lax.fori_loop(0, tb, body, 0, unroll=8)
    for kk in range(2):
        pltpu.make_async_copy(hn_ref, hn_ref, sem.at[kk]).wait()

    @pl.when(i == pl.num_programs(0) - 1)
    def _():
        for b, p in enumerate(sizes):
            def wait_one(_, carry, p=p):
                zero_copy(0, p).wait()
                return carry
            lax.fori_loop(0, zcnt_ref[b], wait_one, 0)


def _dispatch(hn, dest, pads, P, tb, g, bm):
    T = hn.shape[0] // g
    grid_spec = pltpu.PrefetchScalarGridSpec(
        num_scalar_prefetch=5,
        grid=(T // tb,),
        in_specs=[pl.BlockSpec((tb * g, LANES), lambda i, *_: (i, 0))],
        out_specs=pl.BlockSpec(memory_space=pl.ANY),
        scratch_shapes=[pltpu.VMEM((bm // 2 * g, LANES), jnp.uint32),
                        pltpu.SemaphoreType.DMA((2,)), pltpu.SemaphoreType.DMA(())],
    )
    return pl.pallas_call(
        functools.partial(_dispatch_kernel, tb=tb, g=g, bm=bm),
        grid_spec=grid_spec,
        out_shape=jax.ShapeDtypeStruct((P * g, LANES), jnp.uint32),
        compiler_params=_params(("arbitrary",)),
        name="dispatch",
    )(dest, *pads, hn)


def _ffn_kernel(blk_e_ref, nused_ref, nvalid_ref, x_ref, wg_ref, wu_ref, wd_ref, y_ref, *, bm):
    nv = nvalid_ref[pl.program_id(0)]
    half = wg_ref.shape[0] // 2
    g = half // LANES

    def run(rows):
        pairs = [_unpack_pair(w) for w in _load_slabs(x_ref.at[pl.ds(0, rows * g)], rows, g)]
        x = jnp.concatenate([p[0].astype(bf16) for p in pairs] + [p[1].astype(bf16) for p in pairs], axis=1)
        gate = jnp.dot(x, wg_ref[...], preferred_element_type=f32)
        up = jnp.dot(x, wu_ref[...], preferred_element_type=f32)
        hmid = (_silu(gate) * up).astype(bf16)
        for s in range(g):
            ys = jnp.dot(hmid, wd_ref[:, 2 * s * LANES:2 * (s + 1) * LANES], preferred_element_type=f32)
            y_ref[pl.ds(s, rows, stride=g), :] = _pack_pair(ys[:, :LANES].astype(bf16),
                                                            ys[:, LANES:].astype(bf16))

    @pl.when(nv > bm // 2)
    def _():
        run(bm)

    @pl.when((nv > 0) & (nv <= bm // 2))
    def _():
        run(bm // 2)
        y_ref[pl.ds(bm // 2 * g, bm // 2 * g), :] = jnp.zeros((bm // 2 * g, LANES), jnp.uint32)

    @pl.when(nv == 0)
    def _():
        y_ref[...] = jnp.zeros_like(y_ref)


def _ffn(xs, wg, wu, wd, blk_e, nused, nvalid, bm):
    D, F = wg.shape[1:]
    g = D // (2 * LANES)
    NB = blk_e.shape[0]
    grid_spec = pltpu.PrefetchScalarGridSpec(
        num_scalar_prefetch=3,
        grid=(NB,),
        in_specs=[
            pl.BlockSpec((bm * g, LANES), lambda i, be, n, nv: (jnp.minimum(i, n[0] - 1), 0)),
            pl.BlockSpec((None, D, F), lambda i, be, n, nv: (be[i], 0, 0)),
            pl.BlockSpec((None, D, F), lambda i, be, n, nv: (be[i], 0, 0)),
            pl.BlockSpec((None, F, D), lambda i, be, n, nv: (be[i], 0, 0)),
        ],
        out_specs=pl.BlockSpec((bm * g, LANES), lambda i, be, n, nv: (i, 0)),
    )
    return pl.pallas_call(
        functools.partial(_ffn_kernel, bm=bm),
        grid_spec=grid_spec,
        out_shape=jax.ShapeDtypeStruct(xs.shape, jnp.uint32),
        compiler_params=_params(("arbitrary",)),
        name="ffn",
    )(blk_e, nused, nvalid, xs, wg, wu, wd)


def _combine_kernel(dest_ref, y_hbm, h_ref, rt_ref, nw_ref, o_ref, ybuf, sem):
    i = pl.program_id(0)
    n = pl.num_programs(0)
    tm = h_ref.shape[0]
    nslots = ybuf.shape[0]

    D = h_ref.shape[1]
    half = D // 2
    g = half // LANES

    def row_copy(blk, r, kk, slot_):
        d = dest_ref[kk * (tm * n) + blk * tm + r]
        first = r * g if isinstance(r, int) else pl.multiple_of(r * g, g)
        return pltpu.make_async_copy(y_hbm.at[pl.ds(pl.multiple_of(d * g, g), g)],
                                     ybuf.at[slot_, kk, pl.ds(first, g)], sem.at[slot_])

    def wait_slot(slot_):
        pltpu.make_async_copy(ybuf.at[slot_], ybuf.at[slot_], sem.at[slot_]).wait()

    @pl.when(i == 0)
    def _():
        for first in range(nslots - 1):
            def body(r, carry, first=first):
                for kk in range(2):
                    row_copy(jnp.minimum(first, n - 1), r, kk, first).start(priority=kk)
                return carry
            lax.fori_loop(0, tm, body, 0, unroll=8)

    nxt = jnp.minimum(i + nslots - 1, n - 1)
    pieces = 4 * g
    per = tm // pieces

    def step(slot_):
        def issue_batch(p):
            for r in range(p * per, (p + 1) * per):
                for kk in range(2):
                    row_copy(nxt, r, kk, (slot_ + nslots - 1) % nslots).start(priority=kk)

        wait_slot(slot_)
        rt = rt_ref[...]
        g1 = rt[:, 2:3]
        g2 = rt[:, 3:4]
        cols = [slice(c * LANES, (c + 1) * LANES) for c in range(2 * g)]
        hs = []
        for s in range(g):
            a1 = _unpack_pair(ybuf[slot_, 0, pl.ds(s, tm, stride=g), :])
            a2 = _unpack_pair(ybuf[slot_, 1, pl.ds(s, tm, stride=g), :])
            for hw in range(2):
                hs.append(h_ref[:, cols[2 * s + hw]] + g1 * a1[hw] + g2 * a2[hw])
                issue_batch(2 * s + hw)
        ms = sum(jnp.sum(x * x, axis=-1, keepdims=True) for x in hs) / D
        scale = lax.rsqrt(ms + EPS)
        for c in range(2 * g):
            o_ref[:, cols[c]] = hs[c] * scale * nw_ref[:, cols[c]]
            issue_batch(2 * g + c)

    for s in range(nslots):
        @pl.when(i % nslots == s)
        def _(s=s):
            step(s)

    @pl.when(i == n - 1)
    def _():
        for ahead in range(1, nslots):
            wait_slot((i + ahead) % nslots)


def _combine(y, h2, rout, dest, norm_w, tm):
    T, D = h2.shape
    g = D // (2 * LANES)
    grid_spec = pltpu.PrefetchScalarGridSpec(
        num_scalar_prefetch=1,
        grid=(T // tm,),
        in_specs=[
            pl.BlockSpec(memory_space=pl.ANY),
            pl.BlockSpec((tm, D), lambda i, d: (i, 0)),
            pl.BlockSpec((tm, LANES), lambda i, d: (i, 0)),
            pl.BlockSpec((1, D), lambda i, d: (0, 0)),
        ],
        out_specs=pl.BlockSpec((tm, D), lambda i, d: (i, 0)),
        scratch_shapes=[pltpu.VMEM((3, 2, tm * g, LANES), jnp.uint32), pltpu.SemaphoreType.DMA((3,))],
    )
    return pl.pallas_call(
        _combine_kernel,
        grid_spec=grid_spec,
        out_shape=jax.ShapeDtypeStruct((T, D), f32),
        compiler_params=_params(("arbitrary",)),
        name="combine",
    )(dest, y, h2, rout, norm_w.reshape(1, D))


def _dispatch_plan(rout_t, cnt, bm):
    T = rout_t.shape[1]
    NB = 2 * T // bm + N_EXPERTS
    e_kt = rout_t[0:2].astype(jnp.int32)
    rank_kt = rout_t[4:6].astype(jnp.int32)
    counts = cnt[0, N_GROUPS:N_GROUPS + N_EXPERTS].astype(jnp.int32)
    padded = (counts + bm - 1) // bm * bm
    pend = jnp.cumsum(padded)
    pstart = pend - padded
    first_row = jnp.arange(NB, dtype=jnp.int32) * bm
    blk_e = jnp.minimum(jnp.sum((pend[None, :] <= first_row[:, None]).astype(jnp.int32), axis=1),
                        N_EXPERTS - 1)
    nused = (pend[-1] // bm).astype(jnp.int32).reshape(1)
    of_blk = blk_e[:, None] == jnp.arange(N_EXPERTS, dtype=jnp.int32)
    row_end = jnp.sum(jnp.where(of_blk, pstart + counts, 0), axis=1)
    nvalid = jnp.where(first_row < pend[-1], jnp.clip(row_end - first_row, 0, bm), 0).astype(jnp.int32)
    onehot = e_kt[None] == jnp.arange(N_EXPERTS, dtype=jnp.int32)[:, None, None]
    dest = jnp.sum(jnp.where(onehot, pstart[:, None, None], 0), axis=0) + rank_kt
    pad_n = padded - counts
    nbits = bm.bit_length() - 1
    bits = (pad_n[:, None] >> jnp.arange(nbits, dtype=jnp.int32)) & 1
    tail_chunks = 2 * (NB - nused[0])
    zcnt = jnp.sum(bits, axis=0) + jnp.where(jnp.arange(nbits) == nbits - 1, tail_chunks, 0)
    pads = ((pstart + counts).astype(jnp.int32), pad_n.astype(jnp.int32), zcnt.astype(jnp.int32),
            jnp.stack([pend[-1], tail_chunks]).astype(jnp.int32))
    return blk_e, nused, nvalid, dest.reshape(2 * T).astype(jnp.int32), pads


def _tile(n, want):
    t = min(n, want)
    while n % t:
        t //= 2
    return t


class _Tiles(NamedTuple):
    cast_rows: int
    proj_rows: int
    proj_cols: int
    out_rows: int
    mem_rows: int
    mem_cols: int
    attn_rows: int
    moe_rows: int
    dispatch_rows: int
    combine_rows: int


def _tile_plan(T, S, n_mem, D):
    return _Tiles(cast_rows=512, proj_rows=_tile(T, 1024), proj_cols=2048, out_rows=_tile(T, 1024),
                  mem_rows=_tile(n_mem, 512), mem_cols=_tile(2 * D, 1024), attn_rows=_tile(S, 512),
                  moe_rows=512, dispatch_rows=_tile(T, 4096), combine_rows=_tile(T, 256))


def kernel(x, mem, positions, norm_mix_w, w_in, dn_conv_w, dn_a_log, dn_dt_bias, ret_gn_w, dn_norm_w,
           w_out, norm_xq_w, norm_mem_w, w_xq, w_xkv, w_xo, norm_moe_w, w_group_router, b_group_router,
           w_expert_router, b_expert_router, w_gate, w_up, w_down, norm_final_w):
    B, S, D = x.shape
    M = mem.shape[1]
    T = B * S
    depth = w_in.shape[0]
    n_main = 8 * HEADS * HD
    h = x.reshape(T, D)
    pos_col = positions.astype(f32).reshape(T, 1)
    mem2 = mem.reshape(B * M, D)
    tiles = _tile_plan(T, S, B * M, D)
    hg = HEADS
    bm = tiles.moe_rows
    for l in range(depth):
        w_in_t = jnp.swapaxes(w_in[l], 0, 1)
        w_main, w_ba = _cast_rows_t(w_in_t, n_main, tiles.cast_rows)
        proj, ba = _in_proj(h, norm_mix_w[l], w_main, w_ba, tiles.proj_rows, tiles.proj_cols)
        mix_r, w_out_b, w_xq_b, w_xkv_b, w_xo_b = _retention(
            proj, pos_col, ret_gn_w[l], [w_out[l], w_xq[l], w_xkv[l], w_xo[l]], B, S, hg)
        E, _, F = w_gate[l].shape
        expert_w = [w_gate[l].reshape(E * D, F), w_up[l].reshape(E * D, F), w_down[l].reshape(E * F, D)]
        mix_d, wg, wu, wd = _gdn(proj, ba, dn_conv_w[l], dn_a_log[l], dn_dt_bias[l], dn_norm_w[l],
                                 expert_w, B, S, hg)
        wg, wu, wd = wg.reshape(E, D, F), wu.reshape(E, D, F), wd.reshape(E, F, D)
        h = _mm_res(mix_r, mix_d, w_out_b, h, tiles.out_rows, D)
        kv = _norm_mm(mem2, norm_mem_w[l], w_xkv_b, tiles.mem_rows, tiles.mem_cols, "xkv")
        w_r = jnp.pad(jnp.concatenate([w_group_router[l], w_expert_router[l]], axis=1),
                      ((0, 0), (0, LANES - N_GROUPS - N_EXPERTS))).astype(bf16)
        b_r = jnp.pad(jnp.concatenate([b_group_router[l], b_expert_router[l]]),
                      (0, LANES - N_GROUPS - N_EXPERTS)).reshape(1, LANES).astype(f32)
        h, hn, rout, rout_t, cnt = _xattn_router(h, norm_xq_w[l], w_xq_b, kv,
                                                 w_xo_b, norm_moe_w[l], w_r, b_r, B, S, M,
                                                 tiles.attn_rows)
        blk_e, nused, nvalid, dest, pads = _dispatch_plan(rout_t, cnt, bm)
        xs = _dispatch(hn, dest, pads, blk_e.shape[0] * bm, tiles.dispatch_rows, D // (2 * LANES), bm)
        y = _ffn(xs, wg, wu, wd, blk_e, nused, nvalid, bm)
        if l + 1 < depth:
            raise NotImplementedError("the fused MoE combine applies the final norm: depth 1 only")
        h = _combine(y, h, rout, dest, norm_final_w, tiles.combine_rows)
    return h.reshape(B, S, D)
```

```python
import functools
from typing import NamedTuple

import numpy as np
import jax
import jax.numpy as jnp
from jax import lax
from jax.experimental import pallas as pl
from jax.experimental.pallas import tpu as pltpu

f32 = jnp.float32
bf16 = jnp.bfloat16

EPS = 1e-6
HEADS = 8
HD = 128
LANES = 128
XA_HEADS = 4
CONV_W = 4
ROPE_BASE = 10000.0
N_GROUPS = 4
EXP_PER_GROUP = 8
N_EXPERTS = N_GROUPS * EXP_PER_GROUP
RET_CHUNK = 256
DN_CHUNK = 128
VMEM_LIMIT = 56 * 1024 * 1024


def _params(sem, vmem=VMEM_LIMIT):
    return pltpu.CompilerParams(dimension_semantics=sem, vmem_limit_bytes=vmem)


def _dot(a, b):
    return jnp.dot(a.astype(bf16), b.astype(bf16), preferred_element_type=f32)


def _dot_nt(a, b):
    return lax.dot_general(a.astype(bf16), b.astype(bf16), (((1,), (1,)), ((), ())),
                           preferred_element_type=f32)


def _dot_tn(a, b):
    return lax.dot_general(a.astype(bf16), b.astype(bf16), (((0,), (0,)), ((), ())),
                           preferred_element_type=f32)


def _split3(a):
    hi = a.astype(bf16)
    r = a - hi.astype(f32)
    mid = r.astype(bf16)
    lo = (r - mid.astype(f32)).astype(bf16)
    return hi, mid, lo


def _sigmoid(x):
    return 1.0 / (1.0 + jnp.exp(-x))


def _silu(x):
    hx = 0.5 * x
    return hx + hx * jnp.tanh(hx)


def _rms_scale(x):
    return lax.rsqrt(jnp.mean(x * x, axis=-1, keepdims=True) + EPS)


def _cast_t_kernel(w_ref, tail_ref, o_ref, otail_ref):
    o_ref[...] = w_ref[...].T.astype(bf16)

    @pl.when(pl.program_id(0) == 0)
    def _():
        r = tail_ref.shape[0]
        pick = (lax.broadcasted_iota(jnp.int32, (r, LANES), 0)
                == lax.broadcasted_iota(jnp.int32, (r, LANES), 1))
        otail_ref[...] = _dot_tn(tail_ref[...], jnp.where(pick, 1.0, 0.0)).astype(bf16)


def _cast_rows_t(wt, n, tn):
    N, K = wt.shape
    r = N - n
    assert n % r == 0 and r % 8 == 0 and r <= LANES
    return pl.pallas_call(
        _cast_t_kernel,
        grid=(n // tn,),
        in_specs=[pl.BlockSpec((tn, K), lambda j: (j, 0)),
                  pl.BlockSpec((r, K), lambda j: (n // r, 0))],
        out_specs=[pl.BlockSpec((K, tn), lambda j: (0, j)),
                   pl.BlockSpec((K, LANES), lambda j: (0, 0))],
        out_shape=[jax.ShapeDtypeStruct((K, n), bf16), jax.ShapeDtypeStruct((K, LANES), bf16)],
        compiler_params=_params(("arbitrary",)),
        name="cast_w_in",
    )(wt, wt)


def _side_cast_step(step, nsteps, srcs, dsts, stage_in, stage_out, sem_in, sem_out):
    slot = step % 2
    mats = range(len(srcs))

    def in_copy(m, st, sl):
        r = stage_in[m].shape[1]
        rows = pl.ds(pl.multiple_of(st * r, r), r)
        return pltpu.make_async_copy(srcs[m].at[rows], stage_in[m].at[sl], sem_in.at[m, sl])

    def out_copy(m, st, sl):
        r = stage_out[m].shape[1]
        rows = pl.ds(pl.multiple_of(st * r, r), r)
        return pltpu.make_async_copy(stage_out[m].at[sl], dsts[m].at[rows], sem_out.at[m, sl])

    @pl.when(step == 0)
    def _():
        for m in mats:
            in_copy(m, 0, 0).start()

    @pl.when(step + 1 < nsteps)
    def _():
        for m in mats:
            in_copy(m, step + 1, 1 - slot).start()

    for m in mats:
        in_copy(m, step, slot).wait()

    @pl.when(step >= 2)
    def _():
        for m in mats:
            out_copy(m, step - 2, slot).wait()

    def convert():
        for m in mats:
            stage_out[m][slot] = stage_in[m][slot].astype(bf16)

    def finish():
        for m in mats:
            out_copy(m, step, slot).start()

        @pl.when(step == nsteps - 1)
        def _():
            for m in mats:
                if nsteps >= 2:
                    out_copy(m, step - 1, 1 - slot).wait()
                out_copy(m, step, slot).wait()

    return convert, finish


def _side_scratch(side, nsteps):
    rows = [a.shape[0] // nsteps for a in side]
    assert all(a.shape[0] == r * nsteps and r % 16 == 0 for a, r in zip(side, rows))
    return ([pltpu.VMEM((2, r, a.shape[1]), f32) for a, r in zip(side, rows)]
            + [pltpu.VMEM((2, r, a.shape[1]), bf16) for a, r in zip(side, rows)]
            + [pltpu.SemaphoreType.DMA((len(side), 2)), pltpu.SemaphoreType.DMA((len(side), 2))])


def _in_proj_kernel(x_ref, nw_ref, w_ref, wba_ref, o_ref, ba_ref, xn_ref):
    @pl.when(pl.program_id(1) == 0)
    def _():
        x = x_ref[...]
        xn = (x * _rms_scale(x) * nw_ref[...]).astype(bf16)
        xn_ref[...] = xn
        ba_ref[...] = jnp.dot(xn, wba_ref[...], preferred_element_type=f32)

    acc = jnp.dot(xn_ref[...], w_ref[...], preferred_element_type=f32)
    for c in range(o_ref.shape[0]):
        o_ref[c] = acc[:, c * LANES:(c + 1) * LANES].astype(bf16)


def _in_proj(x2, norm_w, w_main, w_ba, tm, tn):
    T, D = x2.shape
    N = w_main.shape[1]
    return pl.pallas_call(
        _in_proj_kernel,
        grid=(T // tm, N // tn),
        in_specs=[
            pl.BlockSpec((tm, D), lambda i, j: (i, 0)),
            pl.BlockSpec((1, D), lambda i, j: (0, 0)),
            pl.BlockSpec((D, tn), lambda i, j: (0, j)),
            pl.BlockSpec((D, LANES), lambda i, j: (0, 0)),
        ],
        out_specs=[
            pl.BlockSpec((tn // LANES, tm, LANES), lambda i, j: (j, i, 0)),
            pl.BlockSpec((tm, LANES), lambda i, j: (i, 0)),
        ],
        out_shape=[
            jax.ShapeDtypeStruct((N // LANES, T, LANES), bf16),
            jax.ShapeDtypeStruct((T, LANES), f32),
        ],
        scratch_shapes=[pltpu.VMEM((tm, D), bf16)],
        compiler_params=_params(("parallel", "arbitrary")),
        name="in_proj",
    )(x2, norm_w.reshape(1, D), w_main, w_ba)


def _retention_kernel(q_ref, k_ref, v_ref, g_ref, pos_ref, inv_ref, sgn_ref, dmask_ref, qdec_ref,
                      kdec_ref, cdec_ref, gnw_ref, *rest, n_side):
    srcs, o_ref, dsts = rest[:n_side], rest[n_side], rest[n_side + 1:2 * n_side + 1]
    state_ref = rest[2 * n_side + 1]
    rest = rest[2 * n_side + 2:]
    stage_in, stage_out, (sem_in, sem_out) = rest[:n_side], rest[n_side:2 * n_side], rest[2 * n_side:]
    step = (pl.program_id(0) * pl.num_programs(1) + pl.program_id(1)) * pl.num_programs(2) + pl.program_id(2)
    nsteps = pl.num_programs(0) * pl.num_programs(1) * pl.num_programs(2)
    convert, finish = _side_cast_step(step, nsteps, srcs, dsts, stage_in, stage_out, sem_in, sem_out)

    @pl.when(pl.program_id(2) == 0)
    def _():
        state_ref[...] = jnp.zeros_like(state_ref)

    convert()

    C = pos_ref.shape[0]
    lane = lax.broadcasted_iota(jnp.int32, (C // 2, HD), 1)
    low = lane < HD // 2
    ang = jnp.where(low, pos_ref[0:C // 2, :], pos_ref[C // 2:, :]) * inv_ref[...]
    cos_p = jnp.cos(ang)
    sin_p = jnp.sin(ang)

    def spread(t):
        swapped = pltpu.roll(t, HD // 2, 1)
        return jnp.concatenate([jnp.where(low, t, swapped), jnp.where(low, swapped, t)], axis=0)

    cos2 = spread(cos_p)
    sin2 = spread(sin_p) * sgn_ref[...]
    heads = range(q_ref.shape[0])

    def rope(x):
        return x * cos2 + pltpu.roll(x, HD // 2, 1) * sin2

    qr = [rope(q_ref[i].astype(f32)) for i in heads]
    kr = [rope(k_ref[i].astype(f32)) * (HD ** -0.5) for i in heads]
    state = [state_ref[i] for i in heads]
    s = [_dot_nt(qr[i], kr[i]) * dmask_ref[i] for i in heads]
    cross = [_dot(qr[i], state[i]) * qdec_ref[i] for i in heads]
    upd = [_dot_tn(kr[i] * kdec_ref[i], v_ref[i]) for i in heads]
    o = [_dot(s[i], v_ref[i]) + cross[i] for i in heads]
    for i in heads:
        state_ref[i] = state[i] * cdec_ref[i] + upd[i]
        d = o[i] - jnp.mean(o[i], axis=-1, keepdims=True)
        y = d * lax.rsqrt(jnp.mean(d * d, axis=-1, keepdims=True) + EPS) * gnw_ref[i]
        y = y * _silu(g_ref[i].astype(f32))
        o_ref[:, i * HD:(i + 1) * HD] = y.astype(bf16)
    finish()


def _retention_tables(C):
    h = np.arange(HEADS, dtype=np.float64)
    log_gamma = np.log1p(-np.exp2(-5.0 - h))
    idx = np.arange(C, dtype=np.float64)
    rel = idx[:, None] - idx[None, :]
    dmask = np.where(rel >= 0, np.exp(log_gamma[:, None, None] * np.where(rel >= 0, rel, 0.0)), 0.0)
    qdec = np.exp(log_gamma[:, None] * (idx + 1.0))
    kdec = np.exp(log_gamma[:, None] * (C - 1.0 - idx))
    cdec = np.exp(log_gamma * C)
    rep = lambda a: np.broadcast_to(a[..., None], a.shape + (LANES,))
    return (jnp.asarray(dmask, f32), jnp.asarray(rep(qdec), f32), jnp.asarray(rep(kdec), f32),
            jnp.asarray(np.broadcast_to(cdec[:, None, None], (HEADS, 1, LANES)), f32))


def _retention(proj, pos_col, gn_w, side, B, S, hg):
    any_spec = pl.BlockSpec(memory_space=pl.ANY)
    C = RET_CHUNK
    NC = S // C
    T = B * S
    G = HEADS // hg
    half = HD // 2
    inv = ROPE_BASE ** (-np.arange(half, dtype=np.float32) / half)
    inv2 = jnp.asarray(np.concatenate([inv, inv]).reshape(1, HD), f32)
    sgn = jnp.asarray(np.concatenate([-np.ones(half), np.ones(half)]).reshape(1, HD), f32)
    dmask, qdec, kdec, cdec = _retention_tables(C)

    def slab(off):
        return pl.BlockSpec((hg, C, HD), lambda b, g, n: (off // hg + g, b * NC + n, 0))

    def table(shape):
        return pl.BlockSpec((hg,) + shape, lambda b, g, n: (g,) + (0,) * len(shape))

    return pl.pallas_call(
        functools.partial(_retention_kernel, n_side=len(side)),
        grid=(B, G, NC),
        in_specs=[
            slab(0), slab(HEADS), slab(2 * HEADS), slab(3 * HEADS),
            pl.BlockSpec((C, 1), lambda b, g, n: (b * NC + n, 0)),
            pl.BlockSpec((1, HD), lambda b, g, n: (0, 0)),
            pl.BlockSpec((1, HD), lambda b, g, n: (0, 0)),
            table((C, C)), table((C, LANES)), table((C, LANES)), table((1, LANES)), table((1, HD)),
        ] + [any_spec] * len(side),
        out_specs=[pl.BlockSpec((C, hg * HD), lambda b, g, n: (b * NC + n, g))] + [any_spec] * len(side),
        out_shape=[jax.ShapeDtypeStruct((T, HEADS * HD), bf16)]
        + [jax.ShapeDtypeStruct(a.shape, bf16) for a in side],
        scratch_shapes=[pltpu.VMEM((hg, HD, HD), f32)] + _side_scratch(side, B * G * NC),
        compiler_params=_params(("arbitrary", "arbitrary", "arbitrary")),
        name="retention",
    )(proj, proj, proj, proj, pos_col, inv2, sgn, dmask, qdec, kdec, cdec,
      gn_w.reshape(HEADS, 1, HD), *side)


def _gdn_kernel(q_ref, k_ref, v_ref, z_ref, ba_ref, cw_ref, arow_ref, dtrow_ref, nw_ref, *rest, hg, n_side):
    srcs, o_ref, dsts = rest[:n_side], rest[n_side], rest[n_side + 1:2 * n_side + 1]
    state_ref, tail_ref, win_ref = rest[2 * n_side + 1:2 * n_side + 4]
    rest = rest[2 * n_side + 4:]
    stage_in, stage_out, (sem_in, sem_out) = rest[:n_side], rest[n_side:2 * n_side], rest[2 * n_side:]
    step = (pl.program_id(0) * pl.num_programs(1) + pl.program_id(1)) * pl.num_programs(2) + pl.program_id(2)
    nsteps = pl.num_programs(0) * pl.num_programs(1) * pl.num_programs(2)
    convert, finish = _side_cast_step(step, nsteps, srcs, dsts, stage_in, stage_out, sem_in, sem_out)

    C = DN_CHUNK
    first = pl.program_id(2) == 0

    @pl.when(first)
    def _():
        state_ref[...] = jnp.zeros_like(state_ref)
        tail_ref[...] = jnp.zeros_like(tail_ref)

    convert()
    row = lax.broadcasted_iota(jnp.int32, (C, C), 0)
    col = lax.broadcasted_iota(jnp.int32, (C, C), 1)
    causal = row >= col
    strict = row > col
    same8 = (row // 8) == (col // 8)
    level_masks = []
    s = 8
    while s < C:
        level_masks.append(((row // (2 * s)) == (col // (2 * s))) & ((row // s) != (col // s)))
        s *= 2
    eye = jnp.where(row == col, 1.0, 0.0).astype(f32)
    tri = jnp.where(causal, 1.0, 0.0).astype(bf16)

    def run_chunk(q_ref, k_ref, v_ref, z_ref, ba_ref, o_ref):
        ba = ba_ref[...]
        beta_all = _sigmoid(ba)
        xa = ba + dtrow_ref[...]
        softplus = jnp.maximum(xa, 0.0) + jnp.log(1.0 + jnp.exp(-jnp.abs(xa)))
        glog = -jnp.exp(arow_ref[...]) * softplus
        g_hi, g_mid, g_lo = _split3(glog)
        gcum = (jnp.dot(tri, g_hi, preferred_element_type=f32)
                + jnp.dot(tri, g_mid, preferred_element_type=f32)
                + jnp.dot(tri, g_lo, preferred_element_type=f32))
        gcum_t = gcum.T
        lane = lax.broadcasted_iota(jnp.int32, (C, LANES), 1)
        sub = lax.broadcasted_iota(jnp.int32, (LANES, C), 0)

        def conv_silu(x_ref, kind, i):
            slot = kind * hg + i
            win_ref[slot, 0:8, :] = tail_ref[slot]
            win_ref[slot, 8:8 + C, :] = x_ref[i].astype(f32)
            w = cw_ref[kind, i]
            y = win_ref[slot, 8:8 + C, :] * w[CONV_W - 1:CONV_W, :]
            for t in range(1, CONV_W):
                y = y + win_ref[slot, 8 - t:8 - t + C, :] * w[CONV_W - 1 - t:CONV_W - t, :]
            tail_ref[slot] = win_ref[slot, C:C + 8, :]
            return _silu(y)

        heads = range(hg)
        hd = [pl.program_id(1) * hg + i for i in heads]
        q = [conv_silu(q_ref, 0, i) for i in heads]
        k = [conv_silu(k_ref, 1, i) for i in heads]
        v = [conv_silu(v_ref, 2, i) for i in heads]
        q = [x * lax.rsqrt(jnp.sum(x * x, axis=-1, keepdims=True) + EPS) * (HD ** -0.5) for x in q]
        k = [x * lax.rsqrt(jnp.sum(x * x, axis=-1, keepdims=True) + EPS) for x in k]

        bcol = [jnp.sum(jnp.where(lane == h, beta_all, 0.0), axis=1, keepdims=True) for h in hd]
        gcol = [jnp.sum(jnp.where(lane == h + HEADS, gcum, 0.0), axis=1, keepdims=True) for h in hd]
        grow = [jnp.sum(jnp.where(sub == h + HEADS, gcum_t, 0.0), axis=0, keepdims=True) for h in hd]
        glast = [g[C - 1:C, :] for g in gcol]
        gam = [jnp.where(causal, jnp.exp(jnp.where(causal, gc - gr, 0.0)), 0.0) for gc, gr in zip(gcol, grow)]
        egc = [jnp.exp(g) for g in gcol]

        kk = [_dot_nt(x, x) for x in k]
        qk = [_dot_nt(x, y) for x, y in zip(q, k)]
        a = [jnp.where(strict, m * g * b, 0.0) for m, g, b in zip(kk, gam, bcol)]
        qk = [m * g for m, g in zip(qk, gam)]

        d = [jnp.where(same8, m, 0.0) for m in a]
        d2 = [_dot(m, m) for m in d]
        d3 = [_dot(m, m2) for m, m2 in zip(d, d2)]
        d4 = [_dot(m2, m2) for m2 in d2]
        x = [eye - m + m2 - m3 for m, m2, m3 in zip(d, d2, d3)]
        x = [xi + _dot(xi, m4) for xi, m4 in zip(x, d4)]
        for mask in level_masks:
            t = [_dot(xi, jnp.where(mask, m, 0.0)) for xi, m in zip(x, a)]
            x = [xi - _dot(ti, xi) for xi, ti in zip(x, t)]

        rhs = [jnp.concatenate([vi * b, ki * (b * e)], axis=1) for vi, ki, b, e in zip(v, k, bcol, egc)]
        sol = [_dot(xi, r) for xi, r in zip(x, rhs)]
        q_dec = [qi * e for qi, e in zip(q, egc)]
        k_dec = [ki * jnp.exp(gl - gc) for ki, gl, gc in zip(k, glast, gcol)]

        state = [state_ref[i] for i in heads]
        ws = [_dot(jnp.concatenate([s_[:, HD:], qd], axis=0), st) for s_, qd, st in zip(sol, q_dec, state)]
        v_new = [s_[:, :HD] - w_[:C] for s_, w_ in zip(sol, ws)]
        o = [w_[C:] + _dot(m, vn) for w_, m, vn in zip(ws, qk, v_new)]
        upd = [_dot_tn(kd, vn) for kd, vn in zip(k_dec, v_new)]
        for i in heads:
            state_ref[i] = state[i] * jnp.exp(glast[i]) + upd[i]
            y = o[i] * _rms_scale(o[i]) * nw_ref[...]
            y = y * _silu(z_ref[i].astype(f32))
            o_ref[:, i * HD:(i + 1) * HD] = y.astype(bf16)

    for c in range(q_ref.shape[1] // C):
        rows = pl.ds(c * C, C)
        run_chunk(q_ref.at[:, rows, :], k_ref.at[:, rows, :], v_ref.at[:, rows, :], z_ref.at[:, rows, :],
                  ba_ref.at[rows, :], o_ref.at[rows, :])
    finish()


def _gdn(proj, ba, conv_w, a_log, dt_bias, norm_w, side, B, S, hg, sub):
    C = sub * DN_CHUNK
    NC = S // C
    T = B * S
    G = HEADS // hg
    any_spec = pl.BlockSpec(memory_space=pl.ANY)
    cw = conv_w.reshape(CONV_W, 3, HEADS, HD).transpose(1, 2, 0, 3)
    pad = jnp.zeros((LANES - 2 * HEADS,), f32)
    arow = jnp.concatenate([jnp.zeros((HEADS,), f32), a_log.astype(f32), pad]).reshape(1, LANES)
    dtrow = jnp.concatenate([jnp.zeros((HEADS,), f32), dt_bias.astype(f32), pad]).reshape(1, LANES)

    def slab(off):
        return pl.BlockSpec((hg, C, HD), lambda b, g, n: (off // hg + g, b * NC + n, 0))

    row_spec = pl.BlockSpec((1, LANES), lambda b, g, n: (0, 0))
    return pl.pallas_call(
        functools.partial(_gdn_kernel, hg=hg, n_side=len(side)),
        grid=(B, G, NC),
        in_specs=[
            slab(4 * HEADS), slab(5 * HEADS), slab(6 * HEADS), slab(7 * HEADS),
            pl.BlockSpec((C, LANES), lambda b, g, n: (b * NC + n, 0)),
            pl.BlockSpec((3, hg, CONV_W, HD), lambda b, g, n: (0, g, 0, 0)),
            row_spec, row_spec, row_spec,
        ] + [any_spec] * len(side),
        out_specs=[pl.BlockSpec((C, hg * HD), lambda b, g, n: (b * NC + n, g))] + [any_spec] * len(side),
        out_shape=[jax.ShapeDtypeStruct((T, HEADS * HD), bf16)]
        + [jax.ShapeDtypeStruct(a.shape, bf16) for a in side],
        scratch_shapes=[
            pltpu.VMEM((hg, HD, HD), f32),
            pltpu.VMEM((3 * hg, 8, HD), f32),
            pltpu.VMEM((3 * hg, DN_CHUNK + 8, HD), f32),
        ] + _side_scratch(side, B * G * NC),
        compiler_params=_params(("arbitrary", "arbitrary", "arbitrary")),
        name="gdn",
    )(proj, proj, proj, proj, ba, cw, arow, dtrow, norm_w.reshape(1, HD), *side)


def _mm_res_kernel(a1_ref, a2_ref, w_ref, r_ref, o_ref):
    a = jnp.concatenate([a1_ref[...], a2_ref[...]], axis=1)
    o_ref[...] = r_ref[...] + jnp.dot(a, w_ref[...], preferred_element_type=f32)


def _mm_res(a1, a2, w, res, tm, tn):
    T, K1 = a1.shape
    K2 = a2.shape[1]
    N = w.shape[1]
    w_mode = dict(pipeline_mode=pl.Buffered(1)) if tn == N else {}
    return pl.pallas_call(
        _mm_res_kernel,
        grid=(T // tm, N // tn),
        in_specs=[
            pl.BlockSpec((tm, K1), lambda i, j: (i, 0)),
            pl.BlockSpec((tm, K2), lambda i, j: (i, 0)),
            pl.BlockSpec((K1 + K2, tn), lambda i, j: (0, j), **w_mode),
            pl.BlockSpec((tm, tn), lambda i, j: (i, j)),
        ],
        out_specs=pl.BlockSpec((tm, tn), lambda i, j: (i, j)),
        out_shape=jax.ShapeDtypeStruct((T, N), f32),
        compiler_params=_params(("parallel", "parallel")),
        name="out_proj",
    )(a1, a2, w, res)


def _norm_mm_kernel(x_ref, nw_ref, w_ref, o_ref, xn_ref):
    @pl.when(pl.program_id(1) == 0)
    def _():
        x = x_ref[...]
        xn_ref[...] = (x * _rms_scale(x) * nw_ref[...]).astype(bf16)

    o_ref[...] = jnp.dot(xn_ref[...], w_ref[...], preferred_element_type=f32).astype(o_ref.dtype)


def _norm_mm(x2, norm_w, w, tm, tn, name):
    T, D = x2.shape
    N = w.shape[1]
    return pl.pallas_call(
        _norm_mm_kernel,
        grid=(T // tm, N // tn),
        in_specs=[
            pl.BlockSpec((tm, D), lambda i, j: (i, 0)),
            pl.BlockSpec((1, D), lambda i, j: (0, 0)),
            pl.BlockSpec((D, tn), lambda i, j: (0, j)),
        ],
        out_specs=pl.BlockSpec((tm, tn), lambda i, j: (i, j)),
        out_shape=jax.ShapeDtypeStruct((T, N), bf16),
        scratch_shapes=[pltpu.VMEM((tm, D), bf16)],
        compiler_params=_params(("parallel", "arbitrary")),
        name=name,
    )(x2, norm_w.reshape(1, D), w)


def _xattn_router_kernel(h_ref, nq_ref, wq_ref, k_ref, v_ref, wo_ref, nm_ref, wr_ref, b_ref,
                         h2_ref, hn_ref, rt_ref, rtt_ref, cnt_ref, run_ref):
    @pl.when((pl.program_id(0) == 0) & (pl.program_id(1) == 0))
    def _():
        run_ref[...] = jnp.zeros_like(run_ref)

    h1 = h_ref[...]
    D = h1.shape[1]
    dh = D // XA_HEADS
    xn = (h1 * _rms_scale(h1) * nq_ref[...]).astype(bf16)
    q = jnp.dot(xn, wq_ref[...], preferred_element_type=f32).astype(bf16)
    heads = []
    for h in range(XA_HEADS):
        sl = slice(h * dh, (h + 1) * dh)
        s = _dot_nt(q[:, sl], k_ref[:, sl]) * (dh ** -0.5)
        p = jnp.exp(s - jnp.max(s, axis=-1, keepdims=True))
        p = p / jnp.sum(p, axis=-1, keepdims=True)
        heads.append(_dot(p, v_ref[:, sl]).astype(bf16))
    h2 = h1 + jnp.dot(jnp.concatenate(heads, axis=1), wo_ref[...], preferred_element_type=f32)
    h2_ref[...] = h2
    _route(h2, nm_ref, wr_ref, b_ref, hn_ref, rt_ref, rtt_ref, cnt_ref, run_ref)


def _xattn_router(h1, norm_q, w_q, kv, w_o, norm_moe, w_r, bias_row, B, S, M, tm):
    T, D = h1.shape
    nt = S // tm
    g = D // (2 * LANES)
    tok = lambda b, i: (b * nt + i, 0)
    const = lambda b, i: (0, 0)
    resident = dict(pipeline_mode=pl.Buffered(1))
    return pl.pallas_call(
        _xattn_router_kernel,
        grid=(B, nt),
        in_specs=[
            pl.BlockSpec((tm, D), tok),
            pl.BlockSpec((1, D), const),
            pl.BlockSpec((D, D), const, **resident),
            pl.BlockSpec((M, D), lambda b, i: (b, 0)),
            pl.BlockSpec((M, D), lambda b, i: (b, 1)),
            pl.BlockSpec((D, D), const, **resident),
            pl.BlockSpec((1, D), const),
            pl.BlockSpec((D, LANES), const),
            pl.BlockSpec((1, LANES), const),
        ],
        out_specs=[
            pl.BlockSpec((tm, D), tok),
            pl.BlockSpec((tm * g, LANES), tok),
            pl.BlockSpec((tm, LANES), tok),
            pl.BlockSpec((8, tm), lambda b, i: (0, b * nt + i)),
            pl.BlockSpec((1, LANES), const),
        ],
        out_shape=[
            jax.ShapeDtypeStruct((T, D), f32),
            jax.ShapeDtypeStruct((T * g, LANES), jnp.uint32),
            jax.ShapeDtypeStruct((T, LANES), f32),
            jax.ShapeDtypeStruct((8, T), f32),
            jax.ShapeDtypeStruct((1, LANES), f32),
        ],
        scratch_shapes=[pltpu.VMEM((1, LANES), f32)],
        compiler_params=_params(("arbitrary", "arbitrary")),
        name="xattn_router",
    )(h1, norm_q.reshape(1, D), w_q, kv, kv, w_o, norm_moe.reshape(1, D), w_r, bias_row)


def _pack_pair(a, b):
    au = lax.bitcast_convert_type(a.astype(f32), jnp.uint32)
    bu = lax.bitcast_convert_type(b.astype(f32), jnp.uint32)
    return (au >> 16) | bu


def _unpack_pair(u):
    a = lax.bitcast_convert_type(u << 16, f32)
    b = lax.bitcast_convert_type(u & jnp.uint32(0xFFFF0000), f32)
    return a, b


def _store_slabs(ref, words):
    n = words.shape[0]
    g = words.shape[1] // LANES
    for c in range(g):
        ref[pl.ds(c, n, stride=g), :] = words[:, c * LANES:(c + 1) * LANES]


def _load_slabs(ref, n, g):
    return [ref[pl.ds(c, n, stride=g), :] for c in range(g)]


def _route(x, nw_ref, wr_ref, b_ref, hn_ref, o_ref, ot_ref, cnt_ref, run_ref):
    xn = x * _rms_scale(x) * nw_ref[...]
    xhi = xn.astype(bf16)
    logits = jnp.dot(xhi, wr_ref[...], preferred_element_type=f32) + b_ref[...]
    lane = lax.broadcasted_iota(jnp.int32, logits.shape, 1).astype(f32)
    neg = -jnp.inf
    big = float(LANES)

    def first_max(vals):
        m = jnp.max(vals, axis=-1, keepdims=True)
        return m, jnp.min(jnp.where(vals == m, lane, big), axis=-1, keepdims=True)

    is_group = lane < N_GROUPS
    gmax, gsel = first_max(jnp.where(is_group, logits, neg))
    gsum = jnp.sum(jnp.where(is_group, jnp.exp(logits - gmax), 0.0), axis=-1, keepdims=True)
    g_w = 1.0 / gsum
    lo = N_GROUPS + EXP_PER_GROUP * gsel
    in_group = (lane >= lo) & (lane < lo + EXP_PER_GROUP)
    el = jnp.where(in_group, logits, neg)
    l1, i1 = first_max(el)
    l2, i2 = first_max(jnp.where(lane == i1, neg, el))
    esum = jnp.sum(jnp.where(in_group, jnp.exp(logits - l1), 0.0), axis=-1, keepdims=True)
    p1 = 1.0 / esum
    p2 = jnp.exp(l2 - l1) / esum
    gate1 = g_w * p1 / (p1 + p2)
    gate2 = g_w * p2 / (p1 + p2)

    tm = x.shape[0]
    chosen = jnp.where((lane == i1) | (lane == i2), 1.0, 0.0)
    earlier = (lax.broadcasted_iota(jnp.int32, (tm, tm), 0) > lax.broadcasted_iota(jnp.int32, (tm, tm), 1))
    before = _dot(jnp.where(earlier, 1.0, 0.0), chosen) + run_ref[...]
    rank1 = jnp.sum(jnp.where(lane == i1, before, 0.0), axis=-1, keepdims=True)
    rank2 = jnp.sum(jnp.where(lane == i2, before, 0.0), axis=-1, keepdims=True)
    run_ref[...] = run_ref[...] + jnp.sum(chosen, axis=0, keepdims=True)
    cnt_ref[...] = run_ref[...]

    cols = (i1 - N_GROUPS, i2 - N_GROUPS, gate1, gate2, rank1, rank2)
    out = jnp.zeros_like(logits)
    for c, val in enumerate(cols):
        out = jnp.where(lane == c, val, out)
    o_ref[...] = out
    ot_ref[...] = out.T[0:ot_ref.shape[0], :]
    half = x.shape[1] // 2
    _store_slabs(hn_ref, _pack_pair(xhi[:, :half], xhi[:, half:]))


def _dispatch_kernel(dest_ref, pad_lo_ref, pad_n_ref, zcnt_ref, tail_ref, hn_ref, xs_hbm, zeros_ref, sem,
                     zsem, *, tb, g, bm):
    i = pl.program_id(0)
    base = i * tb
    sizes = [1 << b for b in range(bm.bit_length() - 1)]

    def zero_copy(first, p):
        return pltpu.make_async_copy(zeros_ref.at[pl.ds(0, p * g)], xs_hbm.at[pl.ds(first, p * g)], zsem)

    @pl.when(i == 0)
    def _():
        zeros_ref[...] = jnp.zeros_like(zeros_ref)

        def per_expert(e, carry):
            row = pad_lo_ref[e]
            n = pad_n_ref[e]
            for p in sizes:
                hit = (n & p) != 0

                @pl.when(hit)
                def _(row=row, p=p):
                    zero_copy(pl.multiple_of(row * g, g), p).start()

                row = row + jnp.where(hit, p, 0)
            return carry

        lax.fori_loop(0, N_EXPERTS, per_expert, 0)

        def per_tail_chunk(c, carry):
            zero_copy(pl.multiple_of((tail_ref[0] + c * sizes[-1]) * g, g), sizes[-1]).start()
            return carry

        lax.fori_loop(0, tail_ref[1], per_tail_chunk, 0)

    def body(r, carry):
        src = hn_ref.at[pl.ds(pl.multiple_of(r * g, g), g)]
        for kk in range(2):
            d = dest_ref[kk * (tb * pl.num_programs(0)) + base + r]
            pltpu.make_async_copy(src, xs_hbm.at[pl.ds(pl.multiple_of(d * g, g), g)],
                                  sem.at[kk]).start(priority=kk)
        return carry

    lax.fori_loop(0, tb, body, 0, unroll=8)
    for kk in range(2):
        pltpu.make_async_copy(hn_ref, hn_ref, sem.at[kk]).wait()

    @pl.when(i == pl.num_programs(0) - 1)
    def _():
        for b, p in enumerate(sizes):
            def wait_one(_, carry, p=p):
                zero_copy(0, p).wait()
                return carry
            lax.fori_loop(0, zcnt_ref[b], wait_one, 0)


def _dispatch(hn, dest, pads, P, tb, g, bm):
    T = hn.shape[0] // g
    grid_spec = pltpu.PrefetchScalarGridSpec(
        num_scalar_prefetch=5,
        grid=(T // tb,),
        in_specs=[pl.BlockSpec((tb * g, LANES), lambda i, *_: (i, 0))],
        out_specs=pl.BlockSpec(memory_space=pl.ANY),
        scratch_shapes=[pltpu.VMEM((bm // 2 * g, LANES), jnp.uint32),
                        pltpu.SemaphoreType.DMA((2,)), pltpu.SemaphoreType.DMA(())],
    )
    return pl.pallas_call(
        functools.partial(_dispatch_kernel, tb=tb, g=g, bm=bm),
        grid_spec=grid_spec,
        out_shape=jax.ShapeDtypeStruct((P * g, LANES), jnp.uint32),
        compiler_params=_params(("arbitrary",)),
        name="dispatch",
    )(dest, *pads, hn)


def _ffn_kernel(blk_e_ref, nused_ref, nvalid_ref, x_ref, wg_ref, wu_ref, wd_ref, y_ref, *, bm):
    nv = nvalid_ref[pl.program_id(0)]
    half = wg_ref.shape[0] // 2
    g = half // LANES

    def run(rows):
        pairs = [_unpack_pair(w) for w in _load_slabs(x_ref.at[pl.ds(0, rows * g)], rows, g)]
        x = jnp.concatenate([p[0].astype(bf16) for p in pairs] + [p[1].astype(bf16) for p in pairs], axis=1)
        gate = jnp.dot(x, wg_ref[...], preferred_element_type=f32)
        up = jnp.dot(x, wu_ref[...], preferred_element_type=f32)
        hmid = (_silu(gate) * up).astype(bf16)
        for s in range(g):
            ys = jnp.dot(hmid, wd_ref[:, 2 * s * LANES:2 * (s + 1) * LANES], preferred_element_type=f32)
            y_ref[pl.ds(s, rows, stride=g), :] = _pack_pair(ys[:, :LANES].astype(bf16),
                                                            ys[:, LANES:].astype(bf16))

    @pl.when(nv > bm // 2)
    def _():
        run(bm)

    @pl.when((nv > 0) & (nv <= bm // 2))
    def _():
        run(bm // 2)
        y_ref[pl.ds(bm // 2 * g, bm // 2 * g), :] = jnp.zeros((bm // 2 * g, LANES), jnp.uint32)

    @pl.when(nv == 0)
    def _():
        y_ref[...] = jnp.zeros_like(y_ref)


def _ffn(xs, wg, wu, wd, blk_e, nused, nvalid, bm):
    D, F = wg.shape[1:]
    g = D // (2 * LANES)
    NB = blk_e.shape[0]
    grid_spec = pltpu.PrefetchScalarGridSpec(
        num_scalar_prefetch=3,
        grid=(NB,),
        in_specs=[
            pl.BlockSpec((bm * g, LANES), lambda i, be, n, nv: (jnp.minimum(i, n[0] - 1), 0)),
            pl.BlockSpec((None, D, F), lambda i, be, n, nv: (be[i], 0, 0)),
            pl.BlockSpec((None, D, F), lambda i, be, n, nv: (be[i], 0, 0)),
            pl.BlockSpec((None, F, D), lambda i, be, n, nv: (be[i], 0, 0)),
        ],
        out_specs=pl.BlockSpec((bm * g, LANES), lambda i, be, n, nv: (i, 0)),
    )
    return pl.pallas_call(
        functools.partial(_ffn_kernel, bm=bm),
        grid_spec=grid_spec,
        out_shape=jax.ShapeDtypeStruct(xs.shape, jnp.uint32),
        compiler_params=_params(("arbitrary",)),
        name="ffn",
    )(blk_e, nused, nvalid, xs, wg, wu, wd)


def _combine_kernel(dest_ref, y_hbm, h_ref, rt_ref, nw_ref, o_ref, ybuf, sem):
    i = pl.program_id(0)
    n = pl.num_programs(0)
    tm = h_ref.shape[0]
    nslots = ybuf.shape[0]

    D = h_ref.shape[1]
    half = D // 2
    g = half // LANES

    def row_copy(blk, r, kk, slot_):
        d = dest_ref[kk * (tm * n) + blk * tm + r]
        first = r * g if isinstance(r, int) else pl.multiple_of(r * g, g)
        return pltpu.make_async_copy(y_hbm.at[pl.ds(pl.multiple_of(d * g, g), g)],
                                     ybuf.at[slot_, kk, pl.ds(first, g)], sem.at[slot_])

    def wait_slot(slot_):
        pltpu.make_async_copy(ybuf.at[slot_], ybuf.at[slot_], sem.at[slot_]).wait()

    @pl.when(i == 0)
    def _():
        for first in range(nslots - 1):
            def body(r, carry, first=first):
                for kk in range(2):
                    row_copy(jnp.minimum(first, n - 1), r, kk, first).start(priority=kk)
                return carry
            lax.fori_loop(0, tm, body, 0, unroll=8)

    nxt = jnp.minimum(i + nslots - 1, n - 1)
    pieces = 4 * g
    per = tm // pieces

    def step(slot_):
        def issue_batch(p):
            for r in range(p * per, (p + 1) * per):
                for kk in range(2):
                    row_copy(nxt, r, kk, (slot_ + nslots - 1) % nslots).start(priority=kk)

        wait_slot(slot_)
        rt = rt_ref[...]
        g1 = rt[:, 2:3]
        g2 = rt[:, 3:4]
        cols = [slice(c * LANES, (c + 1) * LANES) for c in range(2 * g)]
        hs = []
        for s in range(g):
            a1 = _unpack_pair(ybuf[slot_, 0, pl.ds(s, tm, stride=g), :])
            a2 = _unpack_pair(ybuf[slot_, 1, pl.ds(s, tm, stride=g), :])
            for hw in range(2):
                hs.append(h_ref[:, cols[2 * s + hw]] + g1 * a1[hw] + g2 * a2[hw])
                issue_batch(2 * s + hw)
        ms = sum(jnp.sum(x * x, axis=-1, keepdims=True) for x in hs) / D
        scale = lax.rsqrt(ms + EPS)
        for c in range(2 * g):
            o_ref[:, cols[c]] = hs[c] * scale * nw_ref[:, cols[c]]
            issue_batch(2 * g + c)

    for s in range(nslots):
        @pl.when(i % nslots == s)
        def _(s=s):
            step(s)

    @pl.when(i == n - 1)
    def _():
        for ahead in range(1, nslots):
            wait_slot((i + ahead) % nslots)


def _combine(y, h2, rout, dest, norm_w, tm):
    T, D = h2.shape
    g = D // (2 * LANES)
    grid_spec = pltpu.PrefetchScalarGridSpec(
        num_scalar_prefetch=1,
        grid=(T // tm,),
        in_specs=[
            pl.BlockSpec(memory_space=pl.ANY),
            pl.BlockSpec((tm, D), lambda i, d: (i, 0)),
            pl.BlockSpec((tm, LANES), lambda i, d: (i, 0)),
            pl.BlockSpec((1, D), lambda i, d: (0, 0)),
        ],
        out_specs=pl.BlockSpec((tm, D), lambda i, d: (i, 0)),
        scratch_shapes=[pltpu.VMEM((3, 2, tm * g, LANES), jnp.uint32), pltpu.SemaphoreType.DMA((3,))],
    )
    return pl.pallas_call(
        _combine_kernel,
        grid_spec=grid_spec,
        out_shape=jax.ShapeDtypeStruct((T, D), f32),
        compiler_params=_params(("arbitrary",)),
        name="combine",
    )(dest, y, h2, rout, norm_w.reshape(1, D))


def _dispatch_plan(rout_t, cnt, bm):
    T = rout_t.shape[1]
    NB = 2 * T // bm + N_EXPERTS
    e_kt = rout_t[0:2].astype(jnp.int32)
    rank_kt = rout_t[4:6].astype(jnp.int32)
    counts = cnt[0, N_GROUPS:N_GROUPS + N_EXPERTS].astype(jnp.int32)
    padded = (counts + bm - 1) // bm * bm
    pend = jnp.cumsum(padded)
    pstart = pend - padded
    first_row = jnp.arange(NB, dtype=jnp.int32) * bm
    blk_e = jnp.minimum(jnp.sum((pend[None, :] <= first_row[:, None]).astype(jnp.int32), axis=1),
                        N_EXPERTS - 1)
    nused = (pend[-1] // bm).astype(jnp.int32).reshape(1)
    of_blk = blk_e[:, None] == jnp.arange(N_EXPERTS, dtype=jnp.int32)
    row_end = jnp.sum(jnp.where(of_blk, pstart + counts, 0), axis=1)
    nvalid = jnp.where(first_row < pend[-1], jnp.clip(row_end - first_row, 0, bm), 0).astype(jnp.int32)
    onehot = e_kt[None] == jnp.arange(N_EXPERTS, dtype=jnp.int32)[:, None, None]
    dest = jnp.sum(jnp.where(onehot, pstart[:, None, None], 0), axis=0) + rank_kt
    pad_n = padded - counts
    nbits = bm.bit_length() - 1
    bits = (pad_n[:, None] >> jnp.arange(nbits, dtype=jnp.int32)) & 1
    tail_chunks = 2 * (NB - nused[0])
    zcnt = jnp.sum(bits, axis=0) + jnp.where(jnp.arange(nbits) == nbits - 1, tail_chunks, 0)
    pads = ((pstart + counts).astype(jnp.int32), pad_n.astype(jnp.int32), zcnt.astype(jnp.int32),
            jnp.stack([pend[-1], tail_chunks]).astype(jnp.int32))
    return blk_e, nused, nvalid, dest.reshape(2 * T).astype(jnp.int32), pads


def _tile(n, want):
    t = min(n, want)
    while n % t:
        t //= 2
    return t


class _Tiles(NamedTuple):
    cast_rows: int
    proj_rows: int
    proj_cols: int
    out_rows: int
    mem_rows: int
    mem_cols: int
    attn_rows: int
    dn_chunks: int
    moe_rows: int
    dispatch_rows: int
    combine_rows: int


def _tile_plan(T, S, n_mem, D):
    return _Tiles(cast_rows=512, proj_rows=_tile(T, 1024), proj_cols=2048, out_rows=_tile(T, 1024),
                  mem_rows=_tile(n_mem, 512), mem_cols=_tile(2 * D, 1024), attn_rows=_tile(S, 512),
                  dn_chunks=2 if S % (2 * DN_CHUNK) == 0 else 1,
                  moe_rows=512, dispatch_rows=_tile(T, 4096), combine_rows=_tile(T, 256))


def kernel(x, mem, positions, norm_mix_w, w_in, dn_conv_w, dn_a_log, dn_dt_bias, ret_gn_w, dn_norm_w,
           w_out, norm_xq_w, norm_mem_w, w_xq, w_xkv, w_xo, norm_moe_w, w_group_router, b_group_router,
           w_expert_router, b_expert_router, w_gate, w_up, w_down, norm_final_w):
    B, S, D = x.shape
    M = mem.shape[1]
    T = B * S
    depth = w_in.shape[0]
    n_main = 8 * HEADS * HD
    h = x.reshape(T, D)
    pos_col = positions.astype(f32).reshape(T, 1)
    mem2 = mem.reshape(B * M, D)
    tiles = _tile_plan(T, S, B * M, D)
    hg = HEADS
    bm = tiles.moe_rows
    for l in range(depth):
        w_in_t = jnp.swapaxes(w_in[l], 0, 1)
        w_main, w_ba = _cast_rows_t(w_in_t, n_main, tiles.cast_rows)
        proj, ba = _in_proj(h, norm_mix_w[l], w_main, w_ba, tiles.proj_rows, tiles.proj_cols)
        mix_r, w_out_b, w_xq_b, w_xkv_b, w_xo_b = _retention(
            proj, pos_col, ret_gn_w[l], [w_out[l], w_xq[l], w_xkv[l], w_xo[l]], B, S, hg)
        E, _, F = w_gate[l].shape
        expert_w = [w_gate[l].reshape(E * D, F), w_up[l].reshape(E * D, F), w_down[l].reshape(E * F, D)]
        mix_d, wg, wu, wd = _gdn(proj, ba, dn_conv_w[l], dn_a_log[l], dn_dt_bias[l], dn_norm_w[l],
                                 expert_w, B, S, hg, tiles.dn_chunks)
        wg, wu, wd = wg.reshape(E, D, F), wu.reshape(E, D, F), wd.reshape(E, F, D)
        h = _mm_res(mix_r, mix_d, w_out_b, h, tiles.out_rows, D)
        kv = _norm_mm(mem2, norm_mem_w[l], w_xkv_b, tiles.mem_rows, tiles.mem_cols, "xkv")
        w_r = jnp.pad(jnp.concatenate([w_group_router[l], w_expert_router[l]], axis=1),
                      ((0, 0), (0, LANES - N_GROUPS - N_EXPERTS))).astype(bf16)
        b_r = jnp.pad(jnp.concatenate([b_group_router[l], b_expert_router[l]]),
                      (0, LANES - N_GROUPS - N_EXPERTS)).reshape(1, LANES).astype(f32)
        h, hn, rout, rout_t, cnt = _xattn_router(h, norm_xq_w[l], w_xq_b, kv,
                                                 w_xo_b, norm_moe_w[l], w_r, b_r, B, S, M,
                                                 tiles.attn_rows)
        blk_e, nused, nvalid, dest, pads = _dispatch_plan(rout_t, cnt, bm)
        xs = _dispatch(hn, dest, pads, blk_e.shape[0] * bm, tiles.dispatch_rows, D // (2 * LANES), bm)
        y = _ffn(xs, wg, wu, wd, blk_e, nused, nvalid, bm)
        if l + 1 < depth:
            raise NotImplementedError("the fused MoE combine applies the final norm: depth 1 only")
        h = _combine(y, h, rout, dest, norm_final_w, tiles.combine_rows)
    return h.reshape(B, S, D)
```
